```python
import jax
import jax.numpy as jnp
from jax import lax
import numpy as np

D_MODEL = 1024
BATCH = 16
SEQ = 4096
DEPTH = 2

GRID_W = 64
CTX_LEN = 256
HEAD_DIM = 64
FOURIER_WIDTH = D_MODEL // 2
FOURIER_GROUPS = 4
FOURIER_GROUP_DIM = FOURIER_WIDTH // FOURIER_GROUPS
SWA_Q_HEADS = (D_MODEL // 2) // HEAD_DIM
SWA_KV_HEADS = SWA_Q_HEADS // 4
SWA_WINDOW = 128
SWA_BLOCK = 128
NA_HEADS = D_MODEL // HEAD_DIM
NA_ROWS_MAX = 8
NA_COLS = 16
N_EXPERTS = 16
EXPERT_FF = D_MODEL
EC_CAPACITY_FACTOR = 2
ROPE_BASE = 10000.0
EPS = 1e-6
N_EVEN = (DEPTH + 1) // 2
N_ODD = DEPTH // 2
EVEN_IN_WIDTH = FOURIER_WIDTH + (SWA_Q_HEADS + 2 * SWA_KV_HEADS) * HEAD_DIM
EVEN_OUT_WIDTH = FOURIER_WIDTH + SWA_Q_HEADS * HEAD_DIM
ODD_WIDTH = NA_HEADS * HEAD_DIM

kernel_name = 'hybrid_fourier_swa_natten_ecmoe_dit'


def rms_norm(x, gain):
    xf = x.astype(jnp.float32)
    y = xf * lax.rsqrt(jnp.mean(xf * xf, axis=-1, keepdims=True) + EPS)
    return (y * gain.astype(jnp.float32)).astype(x.dtype)


def modulate(h, shift, scale):
    return h * (1 + scale) + shift


def axial_rope(n_tokens):
    t = jnp.arange(n_tokens)
    row = (t // GRID_W).astype(jnp.float32)
    col = (t % GRID_W).astype(jnp.float32)
    n_freq = HEAD_DIM // 4
    inv_freq = jnp.power(ROPE_BASE, -jnp.arange(n_freq, dtype=jnp.float32) / n_freq)
    ang = jnp.concatenate([row[:, None] * inv_freq, col[:, None] * inv_freq], axis=-1)
    return jnp.cos(ang), jnp.sin(ang)


def apply_rope(x, cos, sin):
    xf = x.astype(jnp.float32)
    x1, x2 = jnp.split(xf, 2, axis=-1)
    c = cos[None, :, None, :]
    s = sin[None, :, None, :]
    return jnp.concatenate([x1 * c - x2 * s, x1 * s + x2 * c], axis=-1).astype(x.dtype)


def fourier_mix(u):
    y = jnp.fft.fft2(u.astype(jnp.float32), axes=(1, 3), norm='ortho')
    return jnp.real(y).astype(u.dtype)


def ctx_attention(q, k, v, sink):
    n_k = k.shape[1]
    s = jnp.einsum('bqhgd,bkhd->bhgqk', q, k).astype(jnp.float32) * (HEAD_DIM ** -0.5)
    if sink is not None:
        sk = jnp.broadcast_to(sink.astype(jnp.float32)[None, :, :, None, None], s.shape[:-1] + (1,))
        s = jnp.concatenate([s, sk], axis=-1)
    p = jax.nn.softmax(s, axis=-1)[..., :n_k].astype(v.dtype)
    return jnp.einsum('bhgqk,bkhd->bqhgd', p, v)


def swa_latent(q, k, v, k_ctx, v_ctx, sink):
    bsz, n, n_kv, grp, dh = q.shape
    n_ctx = k_ctx.shape[1]
    n_blk = n // SWA_BLOCK
    span = SWA_BLOCK + 2 * SWA_WINDOW
    pad = ((0, 0), (SWA_WINDOW, SWA_WINDOW), (0, 0), (0, 0))
    k_pad = jnp.pad(k, pad)
    v_pad = jnp.pad(v, pad)
    scale = HEAD_DIM ** -0.5
    sink_logit = jnp.broadcast_to(sink.astype(jnp.float32)[None, :, :, None, None],
                                  (bsz, n_kv, grp, SWA_BLOCK, 1))

    def block(i):
        q0 = i * SWA_BLOCK
        qb = lax.dynamic_slice_in_dim(q, q0, SWA_BLOCK, axis=1)
        kb = lax.dynamic_slice_in_dim(k_pad, q0, span, axis=1)
        vb = lax.dynamic_slice_in_dim(v_pad, q0, span, axis=1)
        s_ctx = jnp.einsum('bqhgd,bkhd->bhgqk', qb, k_ctx).astype(jnp.float32) * scale
        s_loc = jnp.einsum('bqhgd,bkhd->bhgqk', qb, kb).astype(jnp.float32) * scale
        q_pos = q0 + jnp.arange(SWA_BLOCK)
        k_pos = q0 - SWA_WINDOW + jnp.arange(span)
        valid = ((jnp.abs(q_pos[:, None] - k_pos[None, :]) <= SWA_WINDOW)
                 & (k_pos >= 0)[None, :] & (k_pos < n)[None, :])
        s_loc = jnp.where(valid, s_loc, -jnp.inf)
        p = jax.nn.softmax(jnp.concatenate([s_ctx, s_loc, sink_logit], axis=-1), axis=-1)
        p_ctx = p[..., :n_ctx].astype(v.dtype)
        p_loc = p[..., n_ctx:n_ctx + span].astype(v.dtype)
        return (jnp.einsum('bhgqk,bkhd->bqhgd', p_ctx, v_ctx)
                + jnp.einsum('bhgqk,bkhd->bqhgd', p_loc, vb))

    out = lax.map(block, jnp.arange(n_blk))
    return jnp.moveaxis(out, 0, 1).reshape(bsz, n, n_kv, grp, dh)


def na_latent(q, k, v, k_ctx, v_ctx, rpb):
    bsz, n, nh, dh = q.shape
    n_ctx = k_ctx.shape[1]
    rows = n // GRID_W
    kh = min(NA_ROWS_MAX, rows)
    kw = NA_COLS
    scale = HEAD_DIM ** -0.5
    q_g = q.reshape(bsz, rows, GRID_W, nh, dh)
    k_g = k.reshape(bsz, rows, GRID_W, nh, dh)
    v_g = v.reshape(bsz, rows, GRID_W, nh, dh)
    col_q = jnp.arange(GRID_W)
    col_k = jnp.arange(GRID_W)
    c_start = jnp.clip(col_q - kw // 2, 0, GRID_W - kw)
    col_valid = (col_k[None, :] >= c_start[:, None]) & (col_k[None, :] < c_start[:, None] + kw)
    dc_idx = jnp.clip(col_k[None, :] - col_q[:, None] + NA_COLS - 1, 0, 2 * NA_COLS - 2)
    rpb_f = rpb.astype(jnp.float32)

    def row_block(r):
        r_start = jnp.clip(r - kh // 2, 0, rows - kh)
        qb = lax.dynamic_index_in_dim(q_g, r, axis=1, keepdims=False)
        kb = lax.dynamic_slice_in_dim(k_g, r_start, kh, axis=1)
        vb = lax.dynamic_slice_in_dim(v_g, r_start, kh, axis=1)
        dr_idx = r_start + jnp.arange(kh) - r + NA_ROWS_MAX - 1
        bias = jnp.take(rpb_f, dr_idx, axis=1)[:, :, dc_idx]
        bias = jnp.transpose(bias, (0, 2, 1, 3))
        s_loc = jnp.einsum('bqhd,brkhd->bhqrk', qb, kb).astype(jnp.float32) * scale + bias
        s_loc = jnp.where(col_valid[:, None, :], s_loc, -jnp.inf).reshape(bsz, nh, GRID_W, kh * GRID_W)
        s_ctx = jnp.einsum('bqhd,bkhd->bhqk', qb, k_ctx).astype(jnp.float32) * scale
        p = jax.nn.softmax(jnp.concatenate([s_ctx, s_loc], axis=-1), axis=-1).astype(v.dtype)
        p_loc = p[..., n_ctx:].reshape(bsz, nh, GRID_W, kh, GRID_W)
        return (jnp.einsum('bhqk,bkhd->bqhd', p[..., :n_ctx], v_ctx)
                + jnp.einsum('bhqrk,brkhd->bqhd', p_loc, vb))

    out = lax.map(row_block, jnp.arange(rows))
    return jnp.moveaxis(out, 0, 1).reshape(bsz, n, nh, dh)


def expert_choice_ffn(h, w_router, w_gate, w_up, w_down):
    bsz, n, _ = h.shape
    cap = EC_CAPACITY_FACTOR * n // N_EXPERTS
    aff = jax.nn.softmax(jnp.einsum('bnd,de->bne', h, w_router).astype(jnp.float32), axis=-1)
    gate, idx = lax.top_k(jnp.swapaxes(aff, 1, 2), cap)
    b_idx = jnp.arange(bsz)[:, None, None]
    xs = h[b_idx, idx]
    a = jnp.einsum('becd,edf->becf', xs, w_gate)
    u = jnp.einsum('becd,edf->becf', xs, w_up)
    y = jnp.einsum('becf,efd->becd', jax.nn.silu(a) * u, w_down) * gate[..., None].astype(h.dtype)
    return jnp.zeros_like(h).at[b_idx, idx].add(y)


def split_heads(t, nh):
    return t.reshape(t.shape[0], t.shape[1], nh, HEAD_DIM)


def even_mixer(h_x, h_c, w_in, w_out, q_gain, k_gain, sink, cos, sin, ctx_out):
    bsz, n, _ = h_x.shape
    n_ctx = h_c.shape[1]
    dq = SWA_Q_HEADS * HEAD_DIM
    dkv = SWA_KV_HEADS * HEAD_DIM
    grp = SWA_Q_HEADS // SWA_KV_HEADS
    cuts = [FOURIER_WIDTH, FOURIER_WIDTH + dq, FOURIER_WIDTH + dq + dkv]
    sink_g = sink.reshape(SWA_KV_HEADS, grp)
    f_x, q_x, k_x, v_x = jnp.split(h_x @ w_in, cuts, axis=-1)
    q_x = apply_rope(rms_norm(split_heads(q_x, SWA_Q_HEADS), q_gain), cos, sin)
    k_x = apply_rope(rms_norm(split_heads(k_x, SWA_KV_HEADS), k_gain), cos, sin)
    v_x = split_heads(v_x, SWA_KV_HEADS)
    if ctx_out:
        f_c, q_c, k_c, v_c = jnp.split(h_c @ w_in, cuts, axis=-1)
    else:
        k_c, v_c = jnp.split(h_c @ w_in[:, FOURIER_WIDTH + dq:], [dkv], axis=-1)
    k_c = rms_norm(split_heads(k_c, SWA_KV_HEADS), k_gain)
    v_c = split_heads(v_c, SWA_KV_HEADS)
    a_x = swa_latent(q_x.reshape(bsz, n, SWA_KV_HEADS, grp, HEAD_DIM), k_x, v_x, k_c, v_c, sink_g)
    four_x = fourier_mix(f_x.reshape(bsz, n, FOURIER_GROUPS, FOURIER_GROUP_DIM))
    y_x = jnp.concatenate([four_x.reshape(bsz, n, FOURIER_WIDTH), a_x.reshape(bsz, n, dq)], axis=-1) @ w_out
    if not ctx_out:
        return y_x, None
    q_c = rms_norm(split_heads(q_c, SWA_Q_HEADS), q_gain).reshape(bsz, n_ctx, SWA_KV_HEADS, grp, HEAD_DIM)
    a_c = ctx_attention(q_c, k_c, v_c, sink_g)
    four_c = fourier_mix(f_c.reshape(bsz, n_ctx, FOURIER_GROUPS, FOURIER_GROUP_DIM))
    y_c = jnp.concatenate([four_c.reshape(bsz, n_ctx, FOURIER_WIDTH), a_c.reshape(bsz, n_ctx, dq)], axis=-1) @ w_out
    return y_x, y_c


def odd_mixer(h_x, h_c, w_in, w_out, q_gain, k_gain, rpb, ctx_out):
    bsz, n, _ = h_x.shape
    n_ctx = h_c.shape[1]
    q_x, k_x, v_x = jnp.split(h_x @ w_in, [ODD_WIDTH, 2 * ODD_WIDTH], axis=-1)
    q_x = rms_norm(split_heads(q_x, NA_HEADS), q_gain)
    k_x = rms_norm(split_heads(k_x, NA_HEADS), k_gain)
    v_x = split_heads(v_x, NA_HEADS)
    if ctx_out:
        q_c, k_c, v_c = jnp.split(h_c @ w_in, [ODD_WIDTH, 2 * ODD_WIDTH], axis=-1)
    else:
        k_c, v_c = jnp.split(h_c @ w_in[:, ODD_WIDTH:], [ODD_WIDTH], axis=-1)
    k_c = rms_norm(split_heads(k_c, NA_HEADS), k_gain)
    v_c = split_heads(v_c, NA_HEADS)
    a_x = na_latent(q_x, k_x, v_x, k_c, v_c, rpb)
    y_x = a_x.reshape(bsz, n, ODD_WIDTH) @ w_out
    if not ctx_out:
        return y_x, None
    q_c = rms_norm(split_heads(q_c, NA_HEADS), q_gain).reshape(bsz, n_ctx, NA_HEADS, 1, HEAD_DIM)
    a_c = ctx_attention(q_c, k_c, v_c, None)
    y_c = a_c.reshape(bsz, n_ctx, ODD_WIDTH) @ w_out
    return y_x, y_c


def setup_inputs(seed: int = 0) -> dict:
    key = jax.random.key(seed)
    ks = jax.random.split(key, 24)
    D = D_MODEL

    def nrm(k, shape, s):
        return jax.random.normal(k, shape, jnp.float32) * s

    return {
        'x': nrm(ks[0], (BATCH, SEQ, D), 1.0),
        'c': nrm(ks[1], (BATCH, D), 1.0),
        'ctx': nrm(ks[2], (BATCH, CTX_LEN, D), 1.0),
        'c_ctx': nrm(ks[3], (D,), 1.0),
        'ada_w': nrm(ks[4], (DEPTH, D, 6 * D), 0.5 * D ** -0.5),
        'ada_b': nrm(ks[5], (DEPTH, 6 * D), 0.02),
        'norm1_g': 1.0 + nrm(ks[6], (DEPTH, D), 0.05),
        'norm2_g': 1.0 + nrm(ks[7], (DEPTH, D), 0.05),
        'router_w': nrm(ks[8], (DEPTH, D, N_EXPERTS), D ** -0.5),
        'exp_w_gate': nrm(ks[9], (DEPTH, N_EXPERTS, D, EXPERT_FF), D ** -0.5),
        'exp_w_up': nrm(ks[10], (DEPTH, N_EXPERTS, D, EXPERT_FF), D ** -0.5),
        'exp_w_down': nrm(ks[11], (DEPTH, N_EXPERTS, EXPERT_FF, D), EXPERT_FF ** -0.5),
        'ev_w_in': nrm(ks[12], (N_EVEN, D, EVEN_IN_WIDTH), D ** -0.5),
        'ev_w_out': nrm(ks[13], (N_EVEN, EVEN_OUT_WIDTH, D), EVEN_OUT_WIDTH ** -0.5),
        'ev_q_gain': 1.0 + nrm(ks[14], (N_EVEN, HEAD_DIM), 0.05),
        'ev_k_gain': 1.0 + nrm(ks[15], (N_EVEN, HEAD_DIM), 0.05),
        'ev_sink': nrm(ks[16], (N_EVEN, SWA_Q_HEADS), 1.0),
        'od_w_in': nrm(ks[17], (N_ODD, D, 3 * ODD_WIDTH), D ** -0.5),
        'od_w_out': nrm(ks[18], (N_ODD, ODD_WIDTH, D), ODD_WIDTH ** -0.5),
        'od_q_gain': 1.0 + nrm(ks[19], (N_ODD, HEAD_DIM), 0.05),
        'od_k_gain': 1.0 + nrm(ks[20], (N_ODD, HEAD_DIM), 0.05),
        'od_rpb': nrm(ks[21], (N_ODD, NA_HEADS, 2 * NA_ROWS_MAX - 1, 2 * NA_COLS - 1), 0.1),
    }


def reference(x, c, ctx, c_ctx, ada_w, ada_b, norm1_g, norm2_g, router_w, exp_w_gate, exp_w_up,
              exp_w_down, ev_w_in, ev_w_out, ev_q_gain, ev_k_gain, ev_sink, od_w_in, od_w_out,
              od_q_gain, od_k_gain, od_rpb):
    n = x.shape[1]
    cos, sin = axial_rope(n)
    for layer in range(DEPTH):
        last = layer == DEPTH - 1
        j = layer // 2
        mod_x = jnp.split(jax.nn.silu(c) @ ada_w[layer] + ada_b[layer], 6, axis=-1)
        mod_c = jnp.split(jax.nn.silu(c_ctx) @ ada_w[layer] + ada_b[layer], 6, axis=-1)
        sh1, sc1, g1, sh2, sc2, g2 = [m[:, None, :] for m in mod_x]
        csh1, csc1, cg1, csh2, csc2, cg2 = mod_c
        h_x = modulate(rms_norm(x, norm1_g[layer]), sh1, sc1)
        h_c = modulate(rms_norm(ctx, norm1_g[layer]), csh1, csc1)
        if layer % 2 == 0:
            y_x, y_c = even_mixer(h_x, h_c, ev_w_in[j], ev_w_out[j], ev_q_gain[j], ev_k_gain[j],
                                  ev_sink[j], cos, sin, not last)
        else:
            y_x, y_c = odd_mixer(h_x, h_c, od_w_in[j], od_w_out[j], od_q_gain[j], od_k_gain[j],
                                 od_rpb[j], not last)
        x = x + g1 * y_x
        h_x = modulate(rms_norm(x, norm2_g[layer]), sh2, sc2)
        x = x + g2 * expert_choice_ffn(h_x, router_w[layer], exp_w_gate[layer], exp_w_up[layer],
                                       exp_w_down[layer])
        if not last:
            ctx = ctx + cg1 * y_c
            h_c = modulate(rms_norm(ctx, norm2_g[layer]), csh2, csc2)
            ctx = ctx + cg2 * expert_choice_ffn(h_c, router_w[layer], exp_w_gate[layer], exp_w_up[layer],
                                                exp_w_down[layer])
    return x
```

```python
import functools
import math

import numpy as np
import jax
import jax.numpy as jnp
from jax import lax
from jax.experimental import pallas as pl
from jax.experimental.pallas import tpu as pltpu

GRID_W = 64
HEAD_DIM = 64
FOURIER_GROUPS = 4
SWA_WINDOW = 128
SWA_BLOCK = 128
NA_ROWS_MAX = 8
NA_COLS = 16
N_EXPERTS = 16
EC_CAPACITY_FACTOR = 2
ROPE_BASE = 10000.0
EPS = 1e-6

LANES = 128
NEG = -1e30
NA_QROWS = 4
NA_KROWS = 12
DFT_SUB = 64
VMEM_LIMIT = 60 * 1024 * 1024

F32 = jnp.float32
BF16 = jnp.bfloat16


def _cparams(n_axes, vmem=None):
    return pltpu.CompilerParams(dimension_semantics=("arbitrary",) * n_axes, vmem_limit_bytes=vmem)


def _dot(a, b):
    return jnp.dot(a, b, preferred_element_type=F32)


def _dot_t(a, b):
    return lax.dot_general(a, b, (((1,), (1,)), ((), ())), preferred_element_type=F32)


def _split(a):
    hi = a.astype(BF16)
    lo = (a - hi.astype(F32)).astype(BF16)
    return hi, lo


def _dot3(a, b):
    ah, al = _split(a)
    bh, bl = _split(b)
    return _dot(ah, bh) + _dot(al, bh) + _dot(ah, bl)


def _silu(a):
    return a / (1.0 + jnp.exp(-a))


def _rms_mod(x, gain, shift, scale):
    ms = jnp.mean(x * x, axis=-1, keepdims=True)
    y = x * lax.rsqrt(ms + EPS) * gain
    return y * (1.0 + scale) + shift


def _mod_kernel(cs_ref, w_ref, b_ref, o_ref):
    o_ref[0] = _dot3(_silu(cs_ref[...]), w_ref[0]) + b_ref[0]


def _mod_call(cs, ada_w, ada_b):
    depth, d, d6 = ada_w.shape
    r = cs.shape[0]
    tn = 1024
    return pl.pallas_call(
        _mod_kernel,
        grid=(depth, d6 // tn),
        in_specs=[
            pl.BlockSpec((r, d), lambda l, j: (0, 0)),
            pl.BlockSpec((1, d, tn), lambda l, j: (l, 0, j)),
            pl.BlockSpec((1, 1, tn), lambda l, j: (l, 0, j)),
        ],
        out_specs=pl.BlockSpec((1, r, tn), lambda l, j: (l, 0, j)),
        out_shape=jax.ShapeDtypeStruct((depth, r, d6), F32),
        compiler_params=_cparams(2),
        name="mod",
    )(cs, ada_w, ada_b.reshape(depth, 1, d6))


def _head_norm(t, gain, lane):
    t2 = t * t
    s_lo = jnp.sum(jnp.where(lane < HEAD_DIM, t2, 0.0), axis=-1, keepdims=True)
    s_all = jnp.sum(t2, axis=-1, keepdims=True)
    ms = jnp.where(lane < HEAD_DIM, s_lo, s_all - s_lo) * (1.0 / HEAD_DIM)
    return t * lax.rsqrt(ms + EPS) * gain


def _rope(t, cos_t, sin_t, lane):
    partner = jnp.where((lane % HEAD_DIM) < HEAD_DIM // 2, pltpu.roll(t, LANES - HEAD_DIM // 2, 1),
                        pltpu.roll(t, HEAD_DIM // 2, 1))
    return t * cos_t + partner * sin_t


def _dup_halves(t, lane):
    sw = pltpu.roll(t, HEAD_DIM, 1)
    return jnp.where(lane < HEAD_DIM, t, sw), jnp.where(lane < HEAD_DIM, sw, t)


def _premix_even_kernel(x_ref, mod_ref, g_ref, w_ref, cs_ref, cos_ref, sin_ref, qg_ref, kg_ref,
                        ab_ref, q_ref, kd_ref, vd_ref, *, rope):
    tm = x_ref.shape[1]
    m = mod_ref[0]
    hb = _rms_mod(x_ref[0], g_ref[...], m[0:1], m[1:2]).astype(BF16)
    lane = lax.broadcasted_iota(jnp.int32, (tm, LANES), 1)
    fw = FOURIER_GROUPS * LANES
    pf = _dot(hb, w_ref[:, 0:fw])
    csb = cs_ref[...].astype(BF16)
    for g in range(FOURIER_GROUPS):
        ab = _dot(pf[:, LANES * g:LANES * (g + 1)].astype(BF16), csb)
        ab_ref[0, :, LANES * g:LANES * (g + 1)] = ab[:, :LANES].astype(BF16)
        ab_ref[0, :, fw + LANES * g:fw + LANES * (g + 1)] = ab[:, LANES:].astype(BF16)
    qw = q_ref.shape[2]
    pq = _dot(hb, w_ref[:, fw:fw + qw])
    if rope:
        cos_t = cos_ref[...]
        sin_t = sin_ref[...]
    for c in range(qw // LANES):
        t = _head_norm(pq[:, LANES * c:LANES * (c + 1)], qg_ref[...], lane)
        if rope:
            t = _rope(t, cos_t, sin_t, lane)
        q_ref[0, :, LANES * c:LANES * (c + 1)] = t.astype(BF16)
    pk = _dot(hb, w_ref[:, fw + qw:fw + qw + 2 * LANES])
    k = _head_norm(pk[:, :LANES], kg_ref[...], lane)
    if rope:
        k = _rope(k, cos_t, sin_t, lane)
    k0, k1 = _dup_halves(k, lane)
    kd_ref[0, :, :LANES] = k0.astype(BF16)
    kd_ref[0, :, LANES:] = k1.astype(BF16)
    v0, v1 = _dup_halves(pk[:, LANES:], lane)
    vd_ref[0, :, :LANES] = v0.astype(BF16)
    vd_ref[0, :, LANES:] = v1.astype(BF16)


def _premix_even_call(x, mod, gain, w_bf, cs, cos_t, sin_t, qg, kg, *, rope, tm):
    b, n, d = x.shape
    mb = mod.shape[0]
    wtot = w_bf.shape[1]
    fw = FOURIER_GROUPS * LANES
    qw = wtot - fw - 2 * LANES
    mod_map = (lambda i, j: (i, 0, 0)) if mb > 1 else (lambda i, j: (0, 0, 0))
    tok = lambda i, j: (i, j, 0)
    const2 = lambda i, j: (0, 0)
    return pl.pallas_call(
        functools.partial(_premix_even_kernel, rope=rope),
        grid=(b, n // tm),
        in_specs=[
            pl.BlockSpec((1, tm, d), tok),
            pl.BlockSpec((1, 6, d), mod_map),
            pl.BlockSpec((1, d), const2),
            pl.BlockSpec((d, wtot), const2),
            pl.BlockSpec((LANES, 2 * LANES), const2),
            pl.BlockSpec((tm, LANES), lambda i, j: (j, 0)),
            pl.BlockSpec((tm, LANES), lambda i, j: (j, 0)),
            pl.BlockSpec((1, LANES), const2),
            pl.BlockSpec((1, LANES), const2),
        ],
        out_specs=[
            pl.BlockSpec((1, tm, 2 * fw), tok),
            pl.BlockSpec((1, tm, qw), tok),
            pl.BlockSpec((1, tm, 2 * LANES), tok),
            pl.BlockSpec((1, tm, 2 * LANES), tok),
        ],
        out_shape=[
            jax.ShapeDtypeStruct((b, n, 2 * fw), BF16),
            jax.ShapeDtypeStruct((b, n, qw), BF16),
            jax.ShapeDtypeStruct((b, n, 2 * LANES), BF16),
            jax.ShapeDtypeStruct((b, n, 2 * LANES), BF16),
        ],
        compiler_params=_cparams(2, VMEM_LIMIT),
        name="premix_even",
    )(x, mod, gain, w_bf, cs, cos_t, sin_t, qg, kg)


def _premix_odd_kernel(x_ref, mod_ref, g_ref, w_ref, qg_ref, kg_ref, q_ref, k_ref, v_ref):
    tm = x_ref.shape[1]
    m = mod_ref[0]
    hb = _rms_mod(x_ref[0], g_ref[...], m[0:1], m[1:2]).astype(BF16)
    lane = lax.broadcasted_iota(jnp.int32, (tm, LANES), 1)
    wq = q_ref.shape[2]
    chunk = 4 * LANES
    for c0 in range(0, 3 * wq, chunk):
        p = _dot(hb, w_ref[:, c0:c0 + chunk])
        for cc in range(chunk // LANES):
            col = c0 + cc * LANES
            t = p[:, cc * LANES:(cc + 1) * LANES]
            if col < wq:
                q_ref[0, :, col:col + LANES] = _head_norm(t, qg_ref[...], lane).astype(BF16)
            elif col < 2 * wq:
                k_ref[0, :, col - wq:col - wq + LANES] = _head_norm(t, kg_ref[...], lane).astype(BF16)
            else:
                v_ref[0, :, col - 2 * wq:col - 2 * wq + LANES] = t.astype(BF16)


def _premix_odd_call(x, mod, gain, w_bf, qg, kg, *, tm):
    b, n, d = x.shape
    mb = mod.shape[0]
    wq = w_bf.shape[1] // 3
    mod_map = (lambda i, j: (i, 0, 0)) if mb > 1 else (lambda i, j: (0, 0, 0))
    tok = lambda i, j: (i, j, 0)
    const2 = lambda i, j: (0, 0)
    return pl.pallas_call(
        _premix_odd_kernel,
        grid=(b, n // tm),
        in_specs=[
            pl.BlockSpec((1, tm, d), tok),
            pl.BlockSpec((1, 6, d), mod_map),
            pl.BlockSpec((1, d), const2),
            pl.BlockSpec((d, 3 * wq), const2),
            pl.BlockSpec((1, LANES), const2),
            pl.BlockSpec((1, LANES), const2),
        ],
        out_specs=[pl.BlockSpec((1, tm, wq), tok)] * 3,
        out_shape=[jax.ShapeDtypeStruct((b, n, wq), BF16)] * 3,
        compiler_params=_cparams(2, VMEM_LIMIT),
        name="premix_odd",
    )(x, mod, gain, w_bf, qg, kg)


def _softmax_pv(scores, values, sink=None):
    mx = None
    for s in scores:
        ms = jnp.max(s, axis=-1, keepdims=True)
        mx = ms if mx is None else jnp.maximum(mx, ms)
    if sink is not None:
        mx = jnp.maximum(mx, sink)
    den = None
    out = None
    for s, v in zip(scores, values):
        p = jnp.exp(s - mx)
        ds = jnp.sum(p, axis=-1, keepdims=True)
        den = ds if den is None else den + ds
        o = _dot(p.astype(BF16), v)
        out = o if out is None else out + o
    if sink is not None:
        den = den + jnp.exp(sink - mx)
    return out / den


def _mask_half(qc, lane, half):
    q32 = qc.astype(F32)
    keep = (lane < HEAD_DIM) if half == 0 else (lane >= HEAD_DIM)
    return jnp.where(keep, q32, 0.0).astype(BF16)


def _swa_kernel(sink_ref, q_ref, km_ref, k0_ref, kp_ref, vm_ref, v0_ref, vp_ref, kc_ref, vc_ref, o_ref, *, n_blk):
    i = pl.program_id(1)
    tq = q_ref.shape[1]
    scale = HEAD_DIM ** -0.5
    lane = lax.broadcasted_iota(jnp.int32, (tq, LANES), 1)
    iq = lax.broadcasted_iota(jnp.int32, (tq, SWA_BLOCK), 0)
    jk = lax.broadcasted_iota(jnp.int32, (tq, SWA_BLOCK), 1)
    mask_m = jnp.logical_and(jk >= iq, i > 0)
    mask_p = jnp.logical_and(jk <= iq, i < n_blk - 1)
    n_kv = kc_ref.shape[2] // LANES
    cols_per_kv = q_ref.shape[2] // LANES // n_kv
    for h in range(n_kv):
        hs = slice(LANES * h, LANES * (h + 1))
        kc, vc = kc_ref[0, :, hs], vc_ref[0, :, hs]
        km, k0, kp = km_ref[0, :, hs], k0_ref[0, :, hs], kp_ref[0, :, hs]
        vm, v0, vp = vm_ref[0, :, hs], v0_ref[0, :, hs], vp_ref[0, :, hs]
        for cc in range(cols_per_kv):
            c = cols_per_kv * h + cc
            qc = q_ref[0, :, LANES * c:LANES * (c + 1)]
            outs = []
            for half in range(2):
                qa = _mask_half(qc, lane, half)
                s_c = _dot_t(qa, kc) * scale
                s_m = jnp.where(mask_m, _dot_t(qa, km) * scale, NEG)
                s_0 = _dot_t(qa, k0) * scale
                s_p = jnp.where(mask_p, _dot_t(qa, kp) * scale, NEG)
                outs.append(_softmax_pv([s_c, s_m, s_0, s_p], [vc, vm, v0, vp], sink_ref[2 * c + half]))
            o_ref[0, :, LANES * c:LANES * (c + 1)] = jnp.where(lane < HEAD_DIM, outs[0], outs[1]).astype(BF16)


def _swa_call(sink, q, kd, vd, kdc, vdc):
    b, n, qw = q.shape
    nc = kdc.shape[1]
    kw = kd.shape[2]
    n_blk = n // SWA_BLOCK
    cur = lambda i, j: (i, j, 0)
    prev = lambda i, j: (i, jnp.maximum(j - 1, 0), 0)
    nxt = lambda i, j: (i, jnp.minimum(j + 1, n_blk - 1), 0)
    ctxm = lambda i, j: (i, 0, 0)
    kspec = lambda im: pl.BlockSpec((1, SWA_BLOCK, kw), im)
    return pl.pallas_call(
        functools.partial(_swa_kernel, n_blk=n_blk),
        grid=(b, n_blk),
        in_specs=[
            pl.BlockSpec(memory_space=pltpu.SMEM),
            pl.BlockSpec((1, SWA_BLOCK, qw), cur),
            kspec(prev), kspec(cur), kspec(nxt),
            kspec(prev), kspec(cur), kspec(nxt),
            pl.BlockSpec((1, nc, kw), ctxm),
            pl.BlockSpec((1, nc, kw), ctxm),
        ],
        out_specs=pl.BlockSpec((1, SWA_BLOCK, qw), cur),
        out_shape=jax.ShapeDtypeStruct((b, n, qw), BF16),
        compiler_params=_cparams(2, VMEM_LIMIT),
        name="swa",
    )(sink, q, kd, kd, kd, vd, vd, vd, kdc, vdc)


def _ctx_attn_kernel(sink_ref, q_ref, kc_ref, vc_ref, o_ref):
    tq = q_ref.shape[1]
    scale = HEAD_DIM ** -0.5
    lane = lax.broadcasted_iota(jnp.int32, (tq, LANES), 1)
    n_kv = kc_ref.shape[2] // LANES
    cols_per_kv = q_ref.shape[2] // LANES // n_kv
    for h in range(n_kv):
        hs = slice(LANES * h, LANES * (h + 1))
        kc, vc = kc_ref[0, :, hs], vc_ref[0, :, hs]
        for cc in range(cols_per_kv):
            c = cols_per_kv * h + cc
            qc = q_ref[0, :, LANES * c:LANES * (c + 1)]
            outs = []
            for half in range(2):
                qa = _mask_half(qc, lane, half)
                outs.append(_softmax_pv([_dot_t(qa, kc) * scale], [vc], sink_ref[2 * c + half]))
            o_ref[0, :, LANES * c:LANES * (c + 1)] = jnp.where(lane < HEAD_DIM, outs[0], outs[1]).astype(BF16)


def _ctx_attn_call(sink, q, kd, vd):
    b, nc, qw = q.shape
    kw = kd.shape[2]
    m = lambda i: (i, 0, 0)
    return pl.pallas_call(
        _ctx_attn_kernel,
        grid=(b,),
        in_specs=[
            pl.BlockSpec(memory_space=pltpu.SMEM),
            pl.BlockSpec((1, nc, qw), m),
            pl.BlockSpec((1, nc, kw), m),
            pl.BlockSpec((1, nc, kw), m),
        ],
        out_specs=pl.BlockSpec((1, nc, qw), m),
        out_shape=jax.ShapeDtypeStruct((b, nc, qw), BF16),
        compiler_params=_cparams(1, VMEM_LIMIT),
        name="ctx_attn",
    )(sink, q, kd, vd)


def _na_kernel(q_ref, k0_ref, k1_ref, k2_ref, v0_ref, v1_ref, v2_ref, kc_ref, vc_ref, tb_ref, o_ref, *, n_j, rows):
    j = pl.program_id(1)
    tq = q_ref.shape[1]
    tk = 3 * k0_ref.shape[1]
    scale = HEAD_DIM ** -0.5
    w0 = NA_QROWS * jnp.clip(j - 1, 0, n_j - 3)
    lane = lax.broadcasted_iota(jnp.int32, (tq, LANES), 1)
    q_row = lax.broadcasted_iota(jnp.int32, (tq, tk), 0) // GRID_W
    k_row = lax.broadcasted_iota(jnp.int32, (tq, tk), 1) // GRID_W
    lo = jnp.zeros((tq, tk), jnp.int32)
    for rr in range(NA_QROWS):
        r = NA_QROWS * j + rr
        lo_rr = jnp.clip(r - NA_ROWS_MAX // 2, 0, rows - NA_ROWS_MAX) - w0
        lo = jnp.where(q_row == rr, lo_rr, lo)
    row_ok = jnp.logical_and(k_row >= lo, k_row < lo + NA_ROWS_MAX)
    delta = w0 - NA_QROWS * j
    n_dr = 2 * NA_ROWS_MAX - 1
    for c in range(q_ref.shape[2] // LANES):
        cs = slice(LANES * c, LANES * (c + 1))
        qc = q_ref[0, :, cs]
        kl = jnp.concatenate([k0_ref[0, :, cs], k1_ref[0, :, cs], k2_ref[0, :, cs]], axis=0)
        vl = jnp.concatenate([v0_ref[0, :, cs], v1_ref[0, :, cs], v2_ref[0, :, cs]], axis=0)
        kc, vc = kc_ref[0, :, cs], vc_ref[0, :, cs]
        outs = []
        for half in range(2):
            h = 2 * c + half
            qa = _mask_half(qc, lane, half)
            s_c = _dot_t(qa, kc) * scale
            bias_rows = []
            for rr in range(NA_QROWS):
                pieces = []
                for mm in range(NA_KROWS // 2):
                    dr = delta + 2 * mm - rr + NA_ROWS_MAX - 1
                    pieces.append(tb_ref[h, jnp.clip(dr + 1, 0, n_dr)])
                bias_rows.append(jnp.concatenate(pieces, axis=1))
            bias = jnp.concatenate(bias_rows, axis=0)
            s_l = jnp.where(row_ok, _dot_t(qa, kl) * scale + bias, NEG)
            outs.append(_softmax_pv([s_c, s_l], [vc, vl]))
        o_ref[0, :, cs] = jnp.where(lane < HEAD_DIM, outs[0], outs[1]).astype(BF16)


def _na_call(q, k, v, kc, vc, tb):
    b, n, w = q.shape
    nc = kc.shape[1]
    rows = n // GRID_W
    n_j = rows // NA_QROWS
    tq = NA_QROWS * GRID_W
    cur = lambda i, j: (i, j, 0)
    kmap = lambda d: (lambda i, j: (i, jnp.clip(j - 1, 0, n_j - 3) + d, 0))
    ctxm = lambda i, j: (i, 0, 0)
    kspec = lambda d: pl.BlockSpec((1, tq, w), kmap(d))
    return pl.pallas_call(
        functools.partial(_na_kernel, n_j=n_j, rows=rows),
        grid=(b, n_j),
        in_specs=[
            pl.BlockSpec((1, tq, w), cur),
            kspec(0), kspec(1), kspec(2),
            kspec(0), kspec(1), kspec(2),
            pl.BlockSpec((1, nc, w), ctxm),
            pl.BlockSpec((1, nc, w), ctxm),
            pl.BlockSpec(tb.shape, lambda i, j: (0, 0, 0, 0)),
        ],
        out_specs=pl.BlockSpec((1, tq, w), cur),
        out_shape=jax.ShapeDtypeStruct((b, n, w), BF16),
        compiler_params=_cparams(2, VMEM_LIMIT),
        name="na",
    )(q, k, k, k, v, v, v, kc, vc, tb)


def _na_bias_table(rpb):
    col_q = np.arange(GRID_W)[:, None]
    col_k = np.arange(GRID_W)[None, :]
    c_start = np.clip(col_q - NA_COLS // 2, 0, GRID_W - NA_COLS)
    col_valid = (col_k >= c_start) & (col_k < c_start + NA_COLS)
    dc_idx = np.clip(col_k - col_q + NA_COLS - 1, 0, 2 * NA_COLS - 2)
    t = jnp.where(col_valid[None, None], rpb.astype(F32)[:, :, dc_idx], NEG)
    pad = jnp.full_like(t[:, :1], NEG)
    t = jnp.concatenate([pad, t, pad], axis=1)
    return jnp.concatenate([t[:, :-1], t[:, 1:]], axis=-1)


def _fourier_kernel(ab_ref, ca_ref, sa_ref, cb_ref, sb_ref, o_ref, cm_ref, sm_ref):
    @pl.when(pl.program_id(1) == 0)
    def _():
        cb = cb_ref[...]
        sb = sb_ref[...]
        for a in range(ca_ref.shape[0]):
            ca = ca_ref[a:a + 1, :]
            sa = sa_ref[a:a + 1, :]
            rs = slice(DFT_SUB * a, DFT_SUB * (a + 1))
            cm_ref[rs, :] = (ca * cb - sa * sb).astype(BF16)
            sm_ref[rs, :] = (sa * cb + ca * sb).astype(BF16)

    fw = o_ref.shape[2]
    y = _dot(cm_ref[...], ab_ref[0, :, :fw]) - _dot(sm_ref[...], ab_ref[0, :, fw:])
    o_ref[0] = y.astype(BF16)


def _fourier_call(ab, ca, sa, cb, sb, *, tm):
    b, n, w2 = ab.shape
    fw = w2 // 2
    na = tm // DFT_SUB
    return pl.pallas_call(
        _fourier_kernel,
        grid=(n // tm, b),
        in_specs=[
            pl.BlockSpec((1, n, w2), lambda i, j: (j, 0, 0)),
            pl.BlockSpec((na, n), lambda i, j: (i, 0)),
            pl.BlockSpec((na, n), lambda i, j: (i, 0)),
            pl.BlockSpec((DFT_SUB, n), lambda i, j: (0, 0)),
            pl.BlockSpec((DFT_SUB, n), lambda i, j: (0, 0)),
        ],
        out_specs=pl.BlockSpec((1, tm, fw), lambda i, j: (j, i, 0)),
        out_shape=jax.ShapeDtypeStruct((b, n, fw), BF16),
        scratch_shapes=[pltpu.VMEM((tm, n), BF16), pltpu.VMEM((tm, n), BF16)],
        compiler_params=_cparams(2, VMEM_LIMIT),
        name="fourier",
    )(ab, ca, sa, cb, sb)


def _dft_tables(n):
    k = jnp.arange(n, dtype=jnp.int32)
    a = jnp.arange(n // DFT_SUB, dtype=jnp.int32)
    bb = jnp.arange(DFT_SUB, dtype=jnp.int32)
    ang_a = ((DFT_SUB * a[:, None] * k[None, :]) % n).astype(F32) * (2.0 * math.pi / n)
    ang_b = ((bb[:, None] * k[None, :]) % n).astype(F32) * (2.0 * math.pi / n)
    s = float(n) ** -0.5
    return jnp.cos(ang_a), jnp.sin(ang_a), jnp.cos(ang_b) * s, jnp.sin(ang_b) * s


def _channel_dft():
    c = jnp.arange(LANES, dtype=jnp.int32)
    ang = ((c[:, None] * c[None, :]) % LANES).astype(F32) * (2.0 * math.pi / LANES)
    s = float(LANES) ** -0.5
    return jnp.concatenate([jnp.cos(ang) * s, jnp.sin(ang) * s], axis=1)


def _postmix_kernel(*refs, n_in):
    a_refs = refs[:n_in]
    w_refs = refs[n_in:2 * n_in]
    x_ref, mod_ref, g_ref, rwh_ref, rwl_ref, x1_ref, h3_ref, aff_ref = refs[2 * n_in:]
    tm, d = x_ref.shape[1], x_ref.shape[2]
    y = None
    for a_ref, w_ref in zip(a_refs, w_refs):
        t = _dot(a_ref[0], w_ref[...])
        y = t if y is None else y + t
    m = mod_ref[0]
    x1 = x_ref[0] + m[2:3] * y
    x1_ref[0] = x1
    h2 = _rms_mod(x1, g_ref[...], m[3:4], m[4:5])
    hh, hl = _split(h2)
    logits = _dot(hh, rwh_ref[...]) + _dot(hl, rwh_ref[...]) + _dot(hh, rwl_ref[...])
    lt = logits.T[:N_EXPERTS, :]
    ex = jnp.exp(lt - jnp.max(lt, axis=0, keepdims=True))
    aff_ref[0] = ex / jnp.sum(ex, axis=0, keepdims=True)
    bits = pltpu.bitcast(hh.astype(F32), jnp.uint32)
    packed = (bits[:, d // 2:] & jnp.uint32(0xFFFF0000)) | (bits[:, :d // 2] >> 16)
    p = d // 2 // LANES
    for jj in range(p):
        h3_ref[0, pl.ds(jj, tm, stride=p), :] = packed[:, LANES * jj:LANES * (jj + 1)]


def _postmix_call(a_list, w_list, x, mod, gain, rwh, rwl, *, tm):
    b, n, d = x.shape
    mb = mod.shape[0]
    n_in = len(a_list)
    p = d // 2 // LANES
    mod_map = (lambda i, j: (i, 0, 0)) if mb > 1 else (lambda i, j: (0, 0, 0))
    tok = lambda i, j: (i, j, 0)
    const2 = lambda i, j: (0, 0)
    in_specs = [pl.BlockSpec((1, tm, a.shape[2]), tok) for a in a_list]
    in_specs += [pl.BlockSpec(w.shape, const2) for w in w_list]
    in_specs += [
        pl.BlockSpec((1, tm, d), tok),
        pl.BlockSpec((1, 6, d), mod_map),
        pl.BlockSpec((1, d), const2),
        pl.BlockSpec((d, LANES), const2),
        pl.BlockSpec((d, LANES), const2),
    ]
    return pl.pallas_call(
        functools.partial(_postmix_kernel, n_in=n_in),
        grid=(b, n // tm),
        in_specs=in_specs,
        out_specs=[
            pl.BlockSpec((1, tm, d), tok),
            pl.BlockSpec((1, tm * p, LANES), tok),
            pl.BlockSpec((1, N_EXPERTS, tm), lambda i, j: (i, 0, j)),
        ],
        out_shape=[
            jax.ShapeDtypeStruct((b, n, d), F32),
            jax.ShapeDtypeStruct((b, n * p, LANES), jnp.uint32),
            jax.ShapeDtypeStruct((b, N_EXPERTS, n), F32),
        ],
        compiler_params=_cparams(2, VMEM_LIMIT),
        name="postmix",
    )(*a_list, *w_list, x, mod, gain, rwh, rwl)


def _prefix_incl(ones, tri):
    carry = jnp.zeros((ones.shape[0], 1), F32)
    outs = []
    for c in range(ones.shape[1] // LANES):
        blk = ones[:, LANES * c:LANES * (c + 1)]
        outs.append(_dot(blk.astype(BF16), tri) + carry)
        carry = carry + jnp.sum(blk, axis=1, keepdims=True)
    return jnp.concatenate(outs, axis=1)


def _topk_kernel(aff_ref, tri_ref, o_ref, q_scr, *, cap):
    e = pl.program_id(1)
    n = aff_ref.shape[2]

    @pl.when(e == 0)
    def _():
        bits = pltpu.bitcast(aff_ref[0], jnp.int32)
        capf = jnp.float32(cap)

        def body(it, lo):
            t = lo | jnp.left_shift(jnp.int32(1), 30 - it)
            cnt = jnp.sum(jnp.where(bits >= t, 1.0, 0.0), axis=1, keepdims=True)
            return jnp.where(cnt >= capf, t, lo)

        thr = lax.fori_loop(0, 31, body, jnp.zeros((N_EXPERTS, 1), jnp.int32))
        gt = jnp.where(bits > thr, 1.0, 0.0)
        eq = jnp.where(bits == thr, 1.0, 0.0)
        need = capf - jnp.sum(gt, axis=1, keepdims=True)
        tri = tri_ref[...]
        eq_before = _prefix_incl(eq, tri) - eq
        sel = gt + eq * jnp.where(eq_before < need, 1.0, 0.0)
        q_scr[...] = sel * _prefix_incl(sel, tri)

    slot = (lax.broadcasted_iota(jnp.int32, (cap, n), 0) + 1).astype(F32)
    onehot = jnp.where(q_scr[pl.ds(e, 1), :] == slot, 1.0, 0.0).astype(BF16)
    arow = aff_ref[0, pl.ds(e, 1), :]
    g0 = arow.astype(BF16).astype(F32)
    g1 = (arow - g0).astype(BF16).astype(F32)
    g2 = arow - g0 - g1
    tok = lax.broadcasted_iota(jnp.int32, (1, n), 1)
    t_hi = (tok >> 6).astype(F32)
    t_lo = (tok & 63).astype(F32)
    ri = lax.broadcasted_iota(jnp.int32, (8, n), 0)
    vals = jnp.where(ri == 0, t_hi, jnp.where(ri == 1, t_lo, jnp.where(ri == 2, g0, jnp.where(ri == 3, g1,
                     jnp.where(ri == 4, g2, 0.0)))))
    o_ref[0, 0] = _dot_t(vals.astype(BF16), onehot)


def _topk_call(aff_t, tri, *, cap):
    b, e, n = aff_t.shape
    res = pl.pallas_call(
        functools.partial(_topk_kernel, cap=cap),
        grid=(b, e),
        in_specs=[
            pl.BlockSpec((1, e, n), lambda i, j: (i, 0, 0)),
            pl.BlockSpec((LANES, LANES), lambda i, j: (0, 0)),
        ],
        out_specs=pl.BlockSpec((1, 1, 8, cap), lambda i, j: (i, j, 0, 0)),
        out_shape=jax.ShapeDtypeStruct((b, e, 8, cap), F32),
        scratch_shapes=[pltpu.VMEM((e, n), F32)],
        compiler_params=_cparams(2, VMEM_LIMIT),
        name="topk",
    )(aff_t, tri)
    idx = (res[:, :, 0] * 64.0 + res[:, :, 1]).astype(jnp.int32)
    gate = res[:, :, 2] + res[:, :, 3] + res[:, :, 4]
    return idx, gate


def _moe_kernel(idx_ref, h3_ref, gate_ref, wg_ref, wu_ref, wd_ref, o_ref, xs3, y3, *, cap, p, c8):
    e = pl.program_id(1)

    @pl.when(e == 0)
    def _():
        o_ref[...] = jnp.zeros(o_ref.shape, o_ref.dtype)

    unroll = 8

    def gather(i, carry):
        for u in range(unroll):
            s = i * unroll + u
            t = idx_ref[0, 0, s]
            xs3[pl.ds(pl.multiple_of(s * p, p), p), :] = h3_ref[0, pl.ds(pl.multiple_of(t * p, p), p), :]
        return carry

    lax.fori_loop(0, cap // unroll, gather, 0)
    lo, hi = [], []
    for j in range(p):
        w = xs3[pl.ds(j, cap, stride=p), :]
        lo.append(pltpu.bitcast(w << 16, F32).astype(BF16))
        hi.append(pltpu.bitcast(w & jnp.uint32(0xFFFF0000), F32).astype(BF16))
    xs = jnp.concatenate(lo + hi, axis=1)
    a = _dot(xs, wg_ref[0])
    u_ = _dot(xs, wu_ref[0])
    act = (_silu(a) * u_).astype(BF16)
    y = _dot(act, wd_ref[0]) * gate_ref[0]
    for j in range(c8):
        y3[pl.ds(j, cap, stride=c8), :] = y[:, LANES * j:LANES * (j + 1)]

    group = 4

    def scatter(i, carry):
        sums = []
        for u in range(group):
            s = i * group + u
            t = idx_ref[0, 0, s]
            rows = pl.ds(pl.multiple_of(t * c8, c8), c8)
            sums.append((rows, o_ref[0, rows, :] + y3[pl.ds(pl.multiple_of(s * c8, c8), c8), :]))
        for rows, val in sums:
            o_ref[0, rows, :] = val
        return carry

    lax.fori_loop(0, cap // group, scatter, 0)


def _moe_call(idx, h3, gate, wg, wu, wd):
    bm, ne, cap = idx.shape
    d = wg.shape[1]
    p = d // 2 // LANES
    c8 = d // LANES
    n = h3.shape[1] // p
    idx3 = idx.reshape(bm * ne, 1, cap)
    gate3 = gate.reshape(bm * ne, cap, 1)
    wmap = lambda i, j: (j, 0, 0)
    return pl.pallas_call(
        functools.partial(_moe_kernel, cap=cap, p=p, c8=c8),
        grid=(bm, ne),
        in_specs=[
            pl.BlockSpec((1, 1, cap), lambda i, j: (i * ne + j, 0, 0), memory_space=pltpu.SMEM),
            pl.BlockSpec((1, n * p, LANES), lambda i, j: (i, 0, 0), pipeline_mode=pl.Buffered(1)),
            pl.BlockSpec((1, cap, 1), lambda i, j: (i * ne + j, 0, 0)),
            pl.BlockSpec((1, d, wg.shape[2]), wmap),
            pl.BlockSpec((1, d, wu.shape[2]), wmap),
            pl.BlockSpec((1, wd.shape[1], d), wmap),
        ],
        out_specs=pl.BlockSpec((1, n * c8, LANES), lambda i, j: (i, 0, 0), pipeline_mode=pl.Buffered(1)),
        out_shape=jax.ShapeDtypeStruct((bm, n * c8, LANES), F32),
        scratch_shapes=[pltpu.VMEM((cap * p, LANES), jnp.uint32), pltpu.VMEM((cap * c8, LANES), F32)],
        compiler_params=_cparams(2, VMEM_LIMIT),
        name="moe",
    )(idx3, h3, gate3, wg, wu, wd)


def _resid_kernel(x_ref, moe_ref, mod_ref, o_ref):
    tm, d = x_ref.shape[1], x_ref.shape[2]
    c8 = d // LANES
    g2 = mod_ref[0][5:6]
    for j in range(c8):
        cs = slice(LANES * j, LANES * (j + 1))
        o_ref[0, :, cs] = x_ref[0, :, cs] + g2[:, cs] * moe_ref[0, pl.ds(j, tm, stride=c8), :]


def _resid_call(x1, moe3, mod, *, tm):
    b, n, d = x1.shape
    mb = mod.shape[0]
    c8 = d // LANES
    mod_map = (lambda i, j: (i, 0, 0)) if mb > 1 else (lambda i, j: (0, 0, 0))
    tok = lambda i, j: (i, j, 0)
    return pl.pallas_call(
        _resid_kernel,
        grid=(b, n // tm),
        in_specs=[
            pl.BlockSpec((1, tm, d), tok),
            pl.BlockSpec((1, tm * c8, LANES), tok),
            pl.BlockSpec((1, 6, d), mod_map),
        ],
        out_specs=pl.BlockSpec((1, tm, d), tok),
        out_shape=jax.ShapeDtypeStruct((b, n, d), F32),
        compiler_params=_cparams(2, VMEM_LIMIT),
        name="resid",
    )(x1, moe3, mod)


def _rope_tables(n):
    t = jnp.arange(n)
    row = (t // GRID_W).astype(F32)
    col = (t % GRID_W).astype(F32)
    n_freq = HEAD_DIM // 4
    inv_freq = jnp.power(ROPE_BASE, -jnp.arange(n_freq, dtype=F32) / n_freq)
    ang = jnp.concatenate([row[:, None] * inv_freq, col[:, None] * inv_freq], axis=-1)
    cos, sin = jnp.cos(ang), jnp.sin(ang)
    cos_t = jnp.concatenate([cos, cos, cos, cos], axis=-1)
    sin_t = jnp.concatenate([-sin, sin, -sin, sin], axis=-1)
    return cos_t, sin_t


def _gain128(g):
    return jnp.concatenate([g, g]).reshape(1, LANES).astype(F32)


def _channel_mix(h_pack, x1, mod, aff_t, tri, weights, *, flatten, tm):
    wg, wu, wd = weights
    b, n, d = x1.shape
    cap = EC_CAPACITY_FACTOR * n // N_EXPERTS
    idx, gate = _topk_call(aff_t, tri, cap=cap)
    if flatten:
        idx = (idx + (jnp.arange(b, dtype=jnp.int32) * n)[:, None, None]).transpose(1, 0, 2).reshape(1, N_EXPERTS, b * cap)
        gate = gate.transpose(1, 0, 2).reshape(1, N_EXPERTS, b * cap)
        h_pack = h_pack.reshape(1, -1, LANES)
    moe3 = _moe_call(idx, h_pack, gate, wg, wu, wd)
    moe3 = moe3.reshape(b, -1, LANES)
    return _resid_call(x1, moe3, mod, tm=tm)


def kernel(x, c, ctx, c_ctx, ada_w, ada_b, norm1_g, norm2_g, router_w, exp_w_gate, exp_w_up, exp_w_down, ev_w_in,
           ev_w_out, ev_q_gain, ev_k_gain, ev_sink, od_w_in, od_w_out, od_q_gain, od_k_gain, od_rpb):
    b, n, d = x.shape
    nc = ctx.shape[1]
    tm = min(512, n)
    tmc = min(512, nc)

    rows = -(-(b + 1) // 8) * 8
    cs = jnp.concatenate([c, c_ctx[None], jnp.zeros((rows - b - 1, d), F32)], axis=0)
    mods = _mod_call(cs, ada_w, ada_b).reshape(ada_w.shape[0], rows, 6, d)

    cos_t, sin_t = _rope_tables(n)
    cs_dft = _channel_dft()
    tri = jnp.asarray(np.triu(np.ones((LANES, LANES), np.float32)), BF16)
    fw = FOURIER_GROUPS * LANES

    def router_split(l):
        rw = jnp.pad(router_w[l], ((0, 0), (0, LANES - N_EXPERTS)))
        rh = rw.astype(BF16)
        return rh, (rw - rh.astype(F32)).astype(BF16)

    def experts(l):
        return exp_w_gate[l].astype(BF16), exp_w_up[l].astype(BF16), exp_w_down[l].astype(BF16)

    mod_x, mod_c = mods[0, :b], mods[0, b:b + 1]
    g1 = norm1_g[0].reshape(1, d)
    g2 = norm2_g[0].reshape(1, d)
    w_in = ev_w_in[0].astype(BF16)
    w_out = ev_w_out[0].astype(BF16)
    qg, kg = _gain128(ev_q_gain[0]), _gain128(ev_k_gain[0])
    sink = ev_sink[0].astype(F32)
    rwh, rwl = router_split(0)
    wts = experts(0)

    ab_x, q_x, kd_x, vd_x = _premix_even_call(x, mod_x, g1, w_in, cs_dft, cos_t, sin_t, qg, kg, rope=True, tm=tm)
    ab_c, q_c, kd_c, vd_c = _premix_even_call(ctx, mod_c, g1, w_in, cs_dft, cos_t[:nc], sin_t[:nc], qg, kg,
                                              rope=False, tm=tmc)
    a_x = _swa_call(sink, q_x, kd_x, vd_x, kd_c, vd_c)
    a_c = _ctx_attn_call(sink, q_c, kd_c, vd_c)
    four_x = _fourier_call(ab_x, *_dft_tables(n), tm=tm)
    four_c = _fourier_call(ab_c, *_dft_tables(nc), tm=tmc)
    w_out_parts = [w_out[:fw], w_out[fw:]]
    x1, h3_x, aff_x = _postmix_call([four_x, a_x], w_out_parts, x, mod_x, g2, rwh, rwl, tm=tm)
    c1, h3_c, aff_c = _postmix_call([four_c, a_c], w_out_parts, ctx, mod_c, g2, rwh, rwl, tm=tmc)
    x = _channel_mix(h3_x, x1, mod_x, aff_x, tri, wts, flatten=False, tm=tm)
    ctx = _channel_mix(h3_c, c1, mod_c, aff_c, tri, wts, flatten=True, tm=tmc)

    mod_x, mod_c = mods[1, :b], mods[1, b:b + 1]
    g1 = norm1_g[1].reshape(1, d)
    g2 = norm2_g[1].reshape(1, d)
    w_in = od_w_in[0].astype(BF16)
    w_out = od_w_out[0].astype(BF16)
    qg, kg = _gain128(od_q_gain[0]), _gain128(od_k_gain[0])
    rwh, rwl = router_split(1)
    wts = experts(1)

    q_x, k_x, v_x = _premix_odd_call(x, mod_x, g1, w_in, qg, kg, tm=tm)
    _, k_c, v_c = _premix_odd_call(ctx, mod_c, g1, w_in, qg, kg, tm=tmc)
    a_x = _na_call(q_x, k_x, v_x, k_c, v_c, _na_bias_table(od_rpb[0]))
    x1, h3_x, aff_x = _postmix_call([a_x], [w_out], x, mod_x, g2, rwh, rwl, tm=tm)
    return _channel_mix(h3_x, x1, mod_x, aff_x, tri, wts, flatten=False, tm=tm)
```

```python
import functools
import math

import numpy as np
import jax
import jax.numpy as jnp
from jax import lax
from jax.experimental import pallas as pl
from jax.experimental.pallas import tpu as pltpu

GRID_W = 64
HEAD_DIM = 64
FOURIER_GROUPS = 4
SWA_WINDOW = 128
SWA_BLOCK = 128
NA_ROWS_MAX = 8
NA_COLS = 16
N_EXPERTS = 16
EC_CAPACITY_FACTOR = 2
ROPE_BASE = 10000.0
EPS = 1e-6

LANES = 128
NEG = -1e30
NA_QROWS = 4
NA_KROWS = 12
DFT_SUB = 64
MOE_SUB = 128
ONES_ROWS = 16
LOG2E = math.log2(math.e)
VMEM_LIMIT = 60 * 1024 * 1024

F32 = jnp.float32
BF16 = jnp.bfloat16


def _cparams(n_axes, vmem=None):
    return pltpu.CompilerParams(dimension_semantics=("arbitrary",) * n_axes, vmem_limit_bytes=vmem)


def _dot(a, b):
    return jnp.dot(a, b, preferred_element_type=F32)


def _dot_t(a, b):
    return lax.dot_general(a, b, (((1,), (1,)), ((), ())), preferred_element_type=F32)


def _split(a):
    hi = a.astype(BF16)
    lo = (a - hi.astype(F32)).astype(BF16)
    return hi, lo


def _dot3(a, b):
    ah, al = _split(a)
    bh, bl = _split(b)
    return _dot(ah, bh) + _dot(al, bh) + _dot(ah, bl)


def _silu(a):
    return a / (1.0 + jnp.exp(-a))


def _rms_mod(x, gain, shift, scale):
    ms = jnp.mean(x * x, axis=-1, keepdims=True)
    y = x * lax.rsqrt(ms + EPS) * gain
    return y * (1.0 + scale) + shift


def _mod_kernel(cs_ref, w_ref, b_ref, o_ref):
    o_ref[0] = _dot3(_silu(cs_ref[...]), w_ref[0]) + b_ref[0]


def _mod_call(cs, ada_w, ada_b):
    depth, d, d6 = ada_w.shape
    r = cs.shape[0]
    tn = 1024
    return pl.pallas_call(
        _mod_kernel,
        grid=(depth, d6 // tn),
        in_specs=[
            pl.BlockSpec((r, d), lambda l, j: (0, 0)),
            pl.BlockSpec((1, d, tn), lambda l, j: (l, 0, j)),
            pl.BlockSpec((1, 1, tn), lambda l, j: (l, 0, j)),
        ],
        out_specs=pl.BlockSpec((1, r, tn), lambda l, j: (l, 0, j)),
        out_shape=jax.ShapeDtypeStruct((depth, r, d6), F32),
        compiler_params=_cparams(2),
        name="mod",
    )(cs, ada_w, ada_b.reshape(depth, 1, d6))


def _head_norm(t, gain, lane):
    t2 = t * t
    s_lo = jnp.sum(jnp.where(lane < HEAD_DIM, t2, 0.0), axis=-1, keepdims=True)
    s_all = jnp.sum(t2, axis=-1, keepdims=True)
    ms = jnp.where(lane < HEAD_DIM, s_lo, s_all - s_lo) * (1.0 / HEAD_DIM)
    return t * lax.rsqrt(ms + EPS) * gain


def _rope(t, cos_t, sin_t, lane):
    partner = jnp.where((lane % HEAD_DIM) < HEAD_DIM // 2, pltpu.roll(t, LANES - HEAD_DIM // 2, 1),
                        pltpu.roll(t, HEAD_DIM // 2, 1))
    return t * cos_t + partner * sin_t


def _dup_halves(t, lane):
    sw = pltpu.roll(t, HEAD_DIM, 1)
    return jnp.where(lane < HEAD_DIM, t, sw), jnp.where(lane < HEAD_DIM, sw, t)


def _premix_even_kernel(x_ref, mod_ref, g_ref, w_ref, cs_ref, cos_ref, sin_ref, qg_ref, kg_ref,
                        ab_ref, q_ref, kd_ref, vt_ref, *, rope):
    tm = x_ref.shape[1]
    m = mod_ref[0]
    hb = _rms_mod(x_ref[0], g_ref[...], m[0:1], m[1:2]).astype(BF16)
    lane = lax.broadcasted_iota(jnp.int32, (tm, LANES), 1)
    fw = FOURIER_GROUPS * LANES
    pf = _dot(hb, w_ref[:, 0:fw])
    csb = cs_ref[...].astype(BF16)
    for g in range(FOURIER_GROUPS):
        ab = _dot(pf[:, LANES * g:LANES * (g + 1)].astype(BF16), csb)
        ab_ref[0, :, LANES * g:LANES * (g + 1)] = ab[:, :LANES].astype(BF16)
        ab_ref[0, :, fw + LANES * g:fw + LANES * (g + 1)] = ab[:, LANES:].astype(BF16)
    qw = q_ref.shape[2]
    pq = _dot(hb, w_ref[:, fw:fw + qw])
    if rope:
        cos_t = cos_ref[...]
        sin_t = sin_ref[...]
    for c in range(qw // LANES):
        t = _head_norm(pq[:, LANES * c:LANES * (c + 1)], qg_ref[...], lane)
        if rope:
            t = _rope(t, cos_t, sin_t, lane)
        q_ref[0, :, LANES * c:LANES * (c + 1)] = t.astype(BF16)
    pk = _dot(hb, w_ref[:, fw + qw:fw + qw + 2 * LANES])
    k = _head_norm(pk[:, :LANES], kg_ref[...], lane)
    if rope:
        k = _rope(k, cos_t, sin_t, lane)
    k0, k1 = _dup_halves(k, lane)
    kd_ref[0, :, :LANES] = k0.astype(BF16)
    kd_ref[0, :, LANES:] = k1.astype(BF16)
    vt_ref[0] = pk[:, LANES:].T.astype(BF16)


def _premix_even_call(x, mod, gain, w_bf, cs, cos_t, sin_t, qg, kg, *, rope, tm):
    b, n, d = x.shape
    mb = mod.shape[0]
    wtot = w_bf.shape[1]
    fw = FOURIER_GROUPS * LANES
    qw = wtot - fw - 2 * LANES
    mod_map = (lambda i, j: (i, 0, 0)) if mb > 1 else (lambda i, j: (0, 0, 0))
    tok = lambda i, j: (i, j, 0)
    const2 = lambda i, j: (0, 0)
    return pl.pallas_call(
        functools.partial(_premix_even_kernel, rope=rope),
        grid=(b, n // tm),
        in_specs=[
            pl.BlockSpec((1, tm, d), tok),
            pl.BlockSpec((1, 6, d), mod_map),
            pl.BlockSpec((1, d), const2),
            pl.BlockSpec((d, wtot), const2),
            pl.BlockSpec((LANES, 2 * LANES), const2),
            pl.BlockSpec((tm, LANES), lambda i, j: (j, 0)),
            pl.BlockSpec((tm, LANES), lambda i, j: (j, 0)),
            pl.BlockSpec((1, LANES), const2),
            pl.BlockSpec((1, LANES), const2),
        ],
        out_specs=[
            pl.BlockSpec((1, tm, 2 * fw), tok),
            pl.BlockSpec((1, tm, qw), tok),
            pl.BlockSpec((1, tm, 2 * LANES), tok),
            pl.BlockSpec((1, LANES, tm), lambda i, j: (i, 0, j)),
        ],
        out_shape=[
            jax.ShapeDtypeStruct((b, n, 2 * fw), BF16),
            jax.ShapeDtypeStruct((b, n, qw), BF16),
            jax.ShapeDtypeStruct((b, n, 2 * LANES), BF16),
            jax.ShapeDtypeStruct((b, LANES, n), BF16),
        ],
        compiler_params=_cparams(2, VMEM_LIMIT),
        name="premix_even",
    )(x, mod, gain, w_bf, cs, cos_t, sin_t, qg, kg)


def _premix_odd_kernel(x_ref, mod_ref, g_ref, w_ref, qg_ref, kg_ref, q_ref, k_ref, v_ref):
    tm = x_ref.shape[1]
    m = mod_ref[0]
    hb = _rms_mod(x_ref[0], g_ref[...], m[0:1], m[1:2]).astype(BF16)
    lane = lax.broadcasted_iota(jnp.int32, (tm, LANES), 1)
    wq = q_ref.shape[2]
    chunk = 4 * LANES
    for c0 in range(0, 3 * wq, chunk):
        p = _dot(hb, w_ref[:, c0:c0 + chunk])
        for cc in range(chunk // LANES):
            col = c0 + cc * LANES
            t = p[:, cc * LANES:(cc + 1) * LANES]
            if col < wq:
                q_ref[0, :, col:col + LANES] = _head_norm(t, qg_ref[...], lane).astype(BF16)
            elif col < 2 * wq:
                k_ref[0, :, col - wq:col - wq + LANES] = _head_norm(t, kg_ref[...], lane).astype(BF16)
            else:
                v_ref[0, col - 2 * wq:col - 2 * wq + LANES, :] = t.T.astype(BF16)


def _premix_odd_call(x, mod, gain, w_bf, qg, kg, *, tm):
    b, n, d = x.shape
    mb = mod.shape[0]
    wq = w_bf.shape[1] // 3
    mod_map = (lambda i, j: (i, 0, 0)) if mb > 1 else (lambda i, j: (0, 0, 0))
    tok = lambda i, j: (i, j, 0)
    const2 = lambda i, j: (0, 0)
    return pl.pallas_call(
        _premix_odd_kernel,
        grid=(b, n // tm),
        in_specs=[
            pl.BlockSpec((1, tm, d), tok),
            pl.BlockSpec((1, 6, d), mod_map),
            pl.BlockSpec((1, d), const2),
            pl.BlockSpec((d, 3 * wq), const2),
            pl.BlockSpec((1, LANES), const2),
            pl.BlockSpec((1, LANES), const2),
        ],
        out_specs=[pl.BlockSpec((1, tm, wq), tok)] * 2 + [pl.BlockSpec((1, wq, tm), lambda i, j: (i, 0, j))],
        out_shape=[jax.ShapeDtypeStruct((b, n, wq), BF16)] * 2 + [jax.ShapeDtypeStruct((b, wq, n), BF16)],
        compiler_params=_cparams(2, VMEM_LIMIT),
        name="premix_odd",
    )(x, mod, gain, w_bf, qg, kg)


def _col_reduce(x, op, slab=64):
    r = x.shape[0]
    if r > slab and r % slab == 0:
        x = op(x.reshape(r // slab, slab, x.shape[1]), axis=0)
    return op(x, axis=0, keepdims=True)


def _softmax_pv_t(s_t, v_t, sink=None):
    mx = _col_reduce(s_t, jnp.max)
    if sink is not None:
        mx = jnp.maximum(mx, sink)
    p = jnp.exp2(s_t - mx).astype(BF16)
    nd = v_t.shape[0]
    o = _dot(jnp.concatenate([v_t, jnp.ones((ONES_ROWS, v_t.shape[1]), BF16)], axis=0), p)
    den = o[nd:nd + 1]
    if sink is not None:
        den = den + jnp.exp2(sink - mx)
    return o[:nd] / den


def _mask_half(qc, lane, half):
    q32 = qc.astype(F32) * (HEAD_DIM ** -0.5 * LOG2E)
    keep = (lane < HEAD_DIM) if half == 0 else (lane >= HEAD_DIM)
    return jnp.where(keep, q32, 0.0).astype(BF16)


def _gqa_group(sink_ref, q_ref, o_ref, h, kd, v_t, masks, lane, r0=0, tq=None):
    tq = q_ref.shape[1] if tq is None else tq
    rs = slice(r0, r0 + tq)
    n_kv = v_t.shape[0] // HEAD_DIM
    cols_per_kv = q_ref.shape[2] // LANES // n_kv
    heads = 2 * cols_per_kv
    q_rows = []
    for cc in range(cols_per_kv):
        c = cols_per_kv * h + cc
        for half in range(2):
            q_rows.append(_mask_half(q_ref[0, rs, LANES * c:LANES * (c + 1)], lane, half))
    qm = jnp.concatenate(q_rows, axis=0)
    s = _dot_t(kd, qm)
    pieces, k0 = [], 0
    for rows, mask in masks:
        blk = s[k0:k0 + rows]
        pieces.append(blk if mask is None else jnp.where(mask, blk, NEG))
        k0 += rows
    s_t = jnp.concatenate(pieces, axis=0) if len(pieces) > 1 else pieces[0]
    head_of_lane = lax.broadcasted_iota(jnp.int32, (1, heads * tq), 1) // tq
    sink = jnp.zeros((1, heads * tq), F32)
    for a in range(heads):
        sink = jnp.where(head_of_lane == a, sink_ref[heads * h + a] * LOG2E, sink)
    o_t = _softmax_pv_t(s_t, v_t, sink)[HEAD_DIM * h:HEAD_DIM * (h + 1)]
    for cc in range(cols_per_kv):
        c = cols_per_kv * h + cc
        tile = jnp.concatenate([o_t[:, tq * (2 * cc):tq * (2 * cc + 1)], o_t[:, tq * (2 * cc + 1):tq * (2 * cc + 2)]],
                               axis=0)
        o_ref[0, rs, LANES * c:LANES * (c + 1)] = tile.T.astype(BF16)


def _swa_kernel(sink_ref, q_ref, km_ref, k0_ref, kp_ref, vm_ref, v0_ref, vp_ref, kc_ref, vc_ref, o_ref, *, n_pair):
    g = pl.program_id(1)
    tq = SWA_BLOCK
    nc = kc_ref.shape[1]
    n_kv = kc_ref.shape[2] // LANES
    heads = q_ref.shape[2] // HEAD_DIM // n_kv
    lane = lax.broadcasted_iota(jnp.int32, (tq, LANES), 1)
    jk = lax.broadcasted_iota(jnp.int32, (SWA_BLOCK, heads * tq), 0)
    iq = lax.broadcasted_iota(jnp.int32, (SWA_BLOCK, heads * tq), 1) % tq
    below, above = jk >= iq, jk <= iq
    prev_masks = [jnp.logical_and(below, g > 0), below]
    next_masks = [above, jnp.logical_and(above, g < n_pair - 1)]
    for sub in range(2):
        masks = [(nc, None), (SWA_BLOCK, prev_masks[sub]), (SWA_BLOCK, None), (SWA_BLOCK, next_masks[sub])]
        lo, hi = slice(0, SWA_BLOCK), slice(SWA_BLOCK, 2 * SWA_BLOCK)
        if sub == 0:
            v_loc = [vm_ref[0], v0_ref[0, :, lo], v0_ref[0, :, hi]]
        else:
            v_loc = [v0_ref[0, :, lo], v0_ref[0, :, hi], vp_ref[0]]
        v_t = jnp.concatenate([vc_ref[0]] + v_loc, axis=1)
        for h in range(n_kv):
            hs = slice(LANES * h, LANES * (h + 1))
            if sub == 0:
                k_loc = [km_ref[0, :, hs], k0_ref[0, lo, hs], k0_ref[0, hi, hs]]
            else:
                k_loc = [k0_ref[0, lo, hs], k0_ref[0, hi, hs], kp_ref[0, :, hs]]
            kd = jnp.concatenate([kc_ref[0, :, hs]] + k_loc, axis=0)
            _gqa_group(sink_ref, q_ref, o_ref, h, kd, v_t, masks, lane, r0=SWA_BLOCK * sub, tq=tq)


def _swa_call(sink, q, kd, vt, kdc, vtc):
    b, n, qw = q.shape
    nc = kdc.shape[1]
    kw = kd.shape[2]
    vw = vt.shape[1]
    n_blk = n // SWA_BLOCK
    n_pair = n_blk // 2
    prev = lambda j: jnp.maximum(2 * j - 1, 0)
    nxt = lambda j: jnp.minimum(2 * j + 2, n_blk - 1)
    kspec = lambda f: pl.BlockSpec((1, SWA_BLOCK, kw), lambda i, j: (i, f(j), 0))
    vspec = lambda f: pl.BlockSpec((1, vw, SWA_BLOCK), lambda i, j: (i, 0, f(j)))
    return pl.pallas_call(
        functools.partial(_swa_kernel, n_pair=n_pair),
        grid=(b, n_pair),
        in_specs=[
            pl.BlockSpec(memory_space=pltpu.SMEM),
            pl.BlockSpec((1, 2 * SWA_BLOCK, qw), lambda i, j: (i, j, 0)),
            kspec(prev), pl.BlockSpec((1, 2 * SWA_BLOCK, kw), lambda i, j: (i, j, 0)), kspec(nxt),
            vspec(prev), pl.BlockSpec((1, vw, 2 * SWA_BLOCK), lambda i, j: (i, 0, j)), vspec(nxt),
            pl.BlockSpec((1, nc, kw), lambda i, j: (i, 0, 0)),
            pl.BlockSpec((1, vw, nc), lambda i, j: (i, 0, 0)),
        ],
        out_specs=pl.BlockSpec((1, 2 * SWA_BLOCK, qw), lambda i, j: (i, j, 0)),
        out_shape=jax.ShapeDtypeStruct((b, n, qw), BF16),
        compiler_params=_cparams(2, VMEM_LIMIT),
        name="swa",
    )(sink, q, kd, kd, kd, vt, vt, vt, kdc, vtc)


def _ctx_attn_kernel(sink_ref, q_ref, kc_ref, vc_ref, o_ref):
    tq = q_ref.shape[1]
    lane = lax.broadcasted_iota(jnp.int32, (tq, LANES), 1)
    for h in range(kc_ref.shape[2] // LANES):
        _gqa_group(sink_ref, q_ref, o_ref, h, kc_ref[0, :, LANES * h:LANES * (h + 1)], vc_ref[0],
                   [(kc_ref.shape[1], None)], lane)


def _ctx_attn_call(sink, q, kd, vt):
    b, nc, qw = q.shape
    kw = kd.shape[2]
    vw = vt.shape[1]
    m = lambda i: (i, 0, 0)
    return pl.pallas_call(
        _ctx_attn_kernel,
        grid=(b,),
        in_specs=[
            pl.BlockSpec(memory_space=pltpu.SMEM),
            pl.BlockSpec((1, nc, qw), m),
            pl.BlockSpec((1, nc, kw), m),
            pl.BlockSpec((1, vw, nc), m),
        ],
        out_specs=pl.BlockSpec((1, nc, qw), m),
        out_shape=jax.ShapeDtypeStruct((b, nc, qw), BF16),
        compiler_params=_cparams(1, VMEM_LIMIT),
        name="ctx_attn",
    )(sink, q, kd, vt)


def _na_kernel(q_ref, k0_ref, k1_ref, k2_ref, v0_ref, v1_ref, v2_ref, kc_ref, vc_ref, tb_ref, o_ref, *, n_j, rows):
    j = pl.program_id(1)
    tq = q_ref.shape[1]
    tk = 3 * k0_ref.shape[1]
    nc = kc_ref.shape[1]
    w0 = NA_QROWS * jnp.clip(j - 1, 0, n_j - 3)
    lane = lax.broadcasted_iota(jnp.int32, (tq, LANES), 1)
    k_row = lax.broadcasted_iota(jnp.int32, (tk, 2 * tq), 0) // GRID_W
    q_row = (lax.broadcasted_iota(jnp.int32, (tk, 2 * tq), 1) % tq) // GRID_W
    lo = jnp.zeros((tk, 2 * tq), jnp.int32)
    for rr in range(NA_QROWS):
        r = NA_QROWS * j + rr
        lo_rr = jnp.clip(r - NA_ROWS_MAX // 2, 0, rows - NA_ROWS_MAX) - w0
        lo = jnp.where(q_row == rr, lo_rr, lo)
    row_bias = jnp.where(jnp.logical_and(k_row >= lo, k_row < lo + NA_ROWS_MAX), 0.0, NEG)
    delta = w0 - NA_QROWS * j
    n_dr = 2 * NA_ROWS_MAX - 1
    for c in range(q_ref.shape[2] // LANES):
        cs = slice(LANES * c, LANES * (c + 1))
        qc = q_ref[0, :, cs]
        kl = jnp.concatenate([kc_ref[0, :, cs], k0_ref[0, :, cs], k1_ref[0, :, cs], k2_ref[0, :, cs]], axis=0)
        v_t = jnp.concatenate([vc_ref[0, cs, :], v0_ref[0, cs, :], v1_ref[0, cs, :], v2_ref[0, cs, :]], axis=1)
        qm = jnp.concatenate([_mask_half(qc, lane, 0), _mask_half(qc, lane, 1)], axis=0)
        s = _dot_t(kl, qm)
        bias_rows = []
        for kr in range(NA_KROWS):
            pieces = []
            for half in range(2):
                for t in range(NA_QROWS // 2):
                    dr = delta + kr - 2 * t + NA_ROWS_MAX - 1
                    pieces.append(tb_ref[2 * c + half, jnp.clip(dr, 0, n_dr)])
            bias_rows.append(jnp.concatenate(pieces, axis=1))
        bias = jnp.concatenate(bias_rows, axis=0)
        s_t = jnp.concatenate([s[:nc], s[nc:] + bias + row_bias], axis=0)
        o_t = _softmax_pv_t(s_t, v_t)
        tile = jnp.concatenate([o_t[:HEAD_DIM, :tq], o_t[HEAD_DIM:, tq:]], axis=0)
        o_ref[0, :, cs] = tile.T.astype(BF16)


def _na_call(q, k, vt, kc, vtc, tb):
    b, n, w = q.shape
    nc = kc.shape[1]
    rows = n // GRID_W
    n_j = rows // NA_QROWS
    tq = NA_QROWS * GRID_W
    cur = lambda i, j: (i, j, 0)
    first = lambda j: jnp.clip(j - 1, 0, n_j - 3)
    kspec = lambda d: pl.BlockSpec((1, tq, w), lambda i, j: (i, first(j) + d, 0))
    vspec = lambda d: pl.BlockSpec((1, w, tq), lambda i, j: (i, 0, first(j) + d))
    return pl.pallas_call(
        functools.partial(_na_kernel, n_j=n_j, rows=rows),
        grid=(b, n_j),
        in_specs=[
            pl.BlockSpec((1, tq, w), cur),
            kspec(0), kspec(1), kspec(2),
            vspec(0), vspec(1), vspec(2),
            pl.BlockSpec((1, nc, w), lambda i, j: (i, 0, 0)),
            pl.BlockSpec((1, w, nc), lambda i, j: (i, 0, 0)),
            pl.BlockSpec(tb.shape, lambda i, j: (0, 0, 0, 0)),
        ],
        out_specs=pl.BlockSpec((1, tq, w), cur),
        out_shape=jax.ShapeDtypeStruct((b, n, w), BF16),
        compiler_params=_cparams(2, VMEM_LIMIT),
        name="na",
    )(q, k, k, k, vt, vt, vt, kc, vtc, tb)


def _na_bias_table(rpb):
    col_q = np.arange(GRID_W)[None, :]
    col_k = np.arange(GRID_W)[:, None]
    c_start = np.clip(col_q - NA_COLS // 2, 0, GRID_W - NA_COLS)
    col_valid = (col_k >= c_start) & (col_k < c_start + NA_COLS)
    dc_idx = np.clip(col_k - col_q + NA_COLS - 1, 0, 2 * NA_COLS - 2)
    t = jnp.where(col_valid[None, None], (rpb.astype(F32) * LOG2E)[:, :, dc_idx], NEG)
    pad = jnp.full_like(t[:, :1], NEG)
    t = jnp.concatenate([pad, t, pad], axis=1)
    return jnp.concatenate([t[:, 1:], t[:, :-1]], axis=-1)


def _fourier_kernel(ab_ref, ca_ref, sa_ref, cb_ref, sb_ref, o_ref, cm_ref, sm_ref):
    @pl.when(pl.program_id(1) == 0)
    def _():
        cb = cb_ref[...]
        sb = sb_ref[...]
        for a in range(ca_ref.shape[0]):
            ca = ca_ref[a:a + 1, :]
            sa = sa_ref[a:a + 1, :]
            rs = slice(DFT_SUB * a, DFT_SUB * (a + 1))
            cm_ref[rs, :] = (ca * cb - sa * sb).astype(BF16)
            sm_ref[rs, :] = (sa * cb + ca * sb).astype(BF16)

    fw = o_ref.shape[2]
    y = _dot(cm_ref[...], ab_ref[0, :, :fw]) - _dot(sm_ref[...], ab_ref[0, :, fw:])
    o_ref[0] = y.astype(BF16)


def _fourier_call(ab, ca, sa, cb, sb, *, tm):
    b, n, w2 = ab.shape
    fw = w2 // 2
    na = tm // DFT_SUB
    return pl.pallas_call(
        _fourier_kernel,
        grid=(n // tm, b),
        in_specs=[
            pl.BlockSpec((1, n, w2), lambda i, j: (j, 0, 0)),
            pl.BlockSpec((na, n), lambda i, j: (i, 0)),
            pl.BlockSpec((na, n), lambda i, j: (i, 0)),
            pl.BlockSpec((DFT_SUB, n), lambda i, j: (0, 0)),
            pl.BlockSpec((DFT_SUB, n), lambda i, j: (0, 0)),
        ],
        out_specs=pl.BlockSpec((1, tm, fw), lambda i, j: (j, i, 0)),
        out_shape=jax.ShapeDtypeStruct((b, n, fw), BF16),
        scratch_shapes=[pltpu.VMEM((tm, n), BF16), pltpu.VMEM((tm, n), BF16)],
        compiler_params=_cparams(2, VMEM_LIMIT),
        name="fourier",
    )(ab, ca, sa, cb, sb)


def _dft_tables(n):
    k = jnp.arange(n, dtype=jnp.int32)
    a = jnp.arange(n // DFT_SUB, dtype=jnp.int32)
    bb = jnp.arange(DFT_SUB, dtype=jnp.int32)
    ang_a = ((DFT_SUB * a[:, None] * k[None, :]) % n).astype(F32) * (2.0 * math.pi / n)
    ang_b = ((bb[:, None] * k[None, :]) % n).astype(F32) * (2.0 * math.pi / n)
    s = float(n) ** -0.5
    return jnp.cos(ang_a), jnp.sin(ang_a), jnp.cos(ang_b) * s, jnp.sin(ang_b) * s


def _channel_dft():
    c = jnp.arange(LANES, dtype=jnp.int32)
    ang = ((c[:, None] * c[None, :]) % LANES).astype(F32) * (2.0 * math.pi / LANES)
    s = float(LANES) ** -0.5
    return jnp.concatenate([jnp.cos(ang) * s, jnp.sin(ang) * s], axis=1)


def _postmix_kernel(*refs, n_in):
    a_refs = refs[:n_in]
    w_refs = refs[n_in:2 * n_in]
    x_ref, mod_ref, g_ref, rwh_ref, rwl_ref, x1_ref, h3_ref, aff_ref = refs[2 * n_in:]
    tm, d = x_ref.shape[1], x_ref.shape[2]
    y = None
    for a_ref, w_ref in zip(a_refs, w_refs):
        t = _dot(a_ref[0], w_ref[...])
        y = t if y is None else y + t
    m = mod_ref[0]
    x1 = x_ref[0] + m[2:3] * y
    x1_ref[0] = x1
    h2 = _rms_mod(x1, g_ref[...], m[3:4], m[4:5])
    hh, hl = _split(h2)
    logits = _dot(hh, rwh_ref[...]) + _dot(hl, rwh_ref[...]) + _dot(hh, rwl_ref[...])
    lt = logits.T[:N_EXPERTS, :]
    ex = jnp.exp(lt - jnp.max(lt, axis=0, keepdims=True))
    aff_ref[0] = ex / jnp.sum(ex, axis=0, keepdims=True)
    bits = pltpu.bitcast(hh.astype(F32), jnp.uint32)
    packed = (bits[:, d // 2:] & jnp.uint32(0xFFFF0000)) | (bits[:, :d // 2] >> 16)
    p = d // 2 // LANES
    for jj in range(p):
        h3_ref[0, pl.ds(jj, tm, stride=p), :] = packed[:, LANES * jj:LANES * (jj + 1)]


def _postmix_call(a_list, w_list, x, mod, gain, rwh, rwl, *, tm):
    b, n, d = x.shape
    mb = mod.shape[0]
    n_in = len(a_list)
    p = d // 2 // LANES
    mod_map = (lambda i, j: (i, 0, 0)) if mb > 1 else (lambda i, j: (0, 0, 0))
    tok = lambda i, j: (i, j, 0)
    const2 = lambda i, j: (0, 0)
    in_specs = [pl.BlockSpec((1, tm, a.shape[2]), tok) for a in a_list]
    in_specs += [pl.BlockSpec(w.shape, const2) for w in w_list]
    in_specs += [
        pl.BlockSpec((1, tm, d), tok),
        pl.BlockSpec((1, 6, d), mod_map),
        pl.BlockSpec((1, d), const2),
        pl.BlockSpec((d, LANES), const2),
        pl.BlockSpec((d, LANES), const2),
    ]
    return pl.pallas_call(
        functools.partial(_postmix_kernel, n_in=n_in),
        grid=(b, n // tm),
        in_specs=in_specs,
        out_specs=[
            pl.BlockSpec((1, tm, d), tok),
            pl.BlockSpec((1, tm * p, LANES), tok),
            pl.BlockSpec((1, N_EXPERTS, tm), lambda i, j: (i, 0, j)),
        ],
        out_shape=[
            jax.ShapeDtypeStruct((b, n, d), F32),
            jax.ShapeDtypeStruct((b, n * p, LANES), jnp.uint32),
            jax.ShapeDtypeStruct((b, N_EXPERTS, n), F32),
        ],
        compiler_params=_cparams(2, VMEM_LIMIT),
        name="postmix",
    )(*a_list, *w_list, x, mod, gain, rwh, rwl)


def _prefix_incl(ones, tri):
    carry = jnp.zeros((ones.shape[0], 1), F32)
    outs = []
    for c in range(ones.shape[1] // LANES):
        blk = ones[:, LANES * c:LANES * (c + 1)]
        outs.append(_dot(blk.astype(BF16), tri) + carry)
        carry = carry + jnp.sum(blk, axis=1, keepdims=True)
    return jnp.concatenate(outs, axis=1)


def _topk_kernel(aff_ref, tri_ref, o_ref, q_scr, *, cap):
    e = pl.program_id(1)
    n = aff_ref.shape[2]

    @pl.when(e == 0)
    def _():
        bits = pltpu.bitcast(aff_ref[0], jnp.int32)
        capf = jnp.float32(cap)

        def body(it, lo):
            t = lo | jnp.left_shift(jnp.int32(1), 30 - it)
            cnt = jnp.sum(jnp.where(bits >= t, 1.0, 0.0), axis=1, keepdims=True)
            return jnp.where(cnt >= capf, t, lo)

        thr = lax.fori_loop(0, 31, body, jnp.zeros((N_EXPERTS, 1), jnp.int32))
        gt = jnp.where(bits > thr, 1.0, 0.0)
        eq = jnp.where(bits == thr, 1.0, 0.0)
        need = capf - jnp.sum(gt, axis=1, keepdims=True)
        tri = tri_ref[...]
        eq_before = _prefix_incl(eq, tri) - eq
        sel = gt + eq * jnp.where(eq_before < need, 1.0, 0.0)
        q_scr[...] = sel * _prefix_incl(sel, tri)

    slot = (lax.broadcasted_iota(jnp.int32, (cap, n), 0) + 1).astype(F32)
    onehot = jnp.where(q_scr[pl.ds(e, 1), :] == slot, 1.0, 0.0).astype(BF16)
    arow = aff_ref[0, pl.ds(e, 1), :]
    g0 = arow.astype(BF16).astype(F32)
    g1 = (arow - g0).astype(BF16).astype(F32)
    g2 = arow - g0 - g1
    tok = lax.broadcasted_iota(jnp.int32, (1, n), 1)
    t_hi = (tok >> 6).astype(F32)
    t_lo = (tok & 63).astype(F32)
    ri = lax.broadcasted_iota(jnp.int32, (8, n), 0)
    vals = jnp.where(ri == 0, t_hi, jnp.where(ri == 1, t_lo, jnp.where(ri == 2, g0, jnp.where(ri == 3, g1,
                     jnp.where(ri == 4, g2, 0.0)))))
    o_ref[0, 0] = _dot_t(vals.astype(BF16), onehot)


def _topk_call(aff_t, tri, *, cap):
    b, e, n = aff_t.shape
    res = pl.pallas_call(
        functools.partial(_topk_kernel, cap=cap),
        grid=(b, e),
        in_specs=[
            pl.BlockSpec((1, e, n), lambda i, j: (i, 0, 0)),
            pl.BlockSpec((LANES, LANES), lambda i, j: (0, 0)),
        ],
        out_specs=pl.BlockSpec((1, 1, 8, cap), lambda i, j: (i, j, 0, 0)),
        out_shape=jax.ShapeDtypeStruct((b, e, 8, cap), F32),
        scratch_shapes=[pltpu.VMEM((e, n), F32)],
        compiler_params=_cparams(2, VMEM_LIMIT),
        name="topk",
    )(aff_t, tri)
    idx = (res[:, :, 0] * 64.0 + res[:, :, 1]).astype(jnp.int32)
    gate = res[:, :, 2] + res[:, :, 3] + res[:, :, 4]
    return idx, gate


def _moe_kernel(idx_ref, h3_ref, gate_ref, wg_ref, wu_ref, wd_ref, o_ref, xs3, y3, *, cap, p, c8, sub):
    e = pl.program_id(1)

    @pl.when(e == 0)
    def _():
        o_ref[...] = jnp.zeros(o_ref.shape, o_ref.dtype)

    def gather(k):
        for r in range(sub):
            s = k * sub + r
            t = idx_ref[0, 0, s]
            xs3[pl.ds(s * p, p), :] = h3_ref[0, pl.ds(pl.multiple_of(t * p, p), p), :]

    def expert(k):
        base = k * sub
        lo, hi = [], []
        for j in range(p):
            w = xs3[pl.ds(base * p + j, sub, stride=p), :]
            lo.append(pltpu.bitcast(w << 16, F32).astype(BF16))
            hi.append(pltpu.bitcast(w & jnp.uint32(0xFFFF0000), F32).astype(BF16))
        xs = jnp.concatenate(lo + hi, axis=1)
        a = _dot(xs, wg_ref[0])
        u_ = _dot(xs, wu_ref[0])
        act = (_silu(a) * u_).astype(BF16)
        y = _dot(act, wd_ref[0]) * gate_ref[0, base:base + sub, :]
        for j in range(c8):
            y3[pl.ds(base * c8 + j, sub, stride=c8), :] = y[:, LANES * j:LANES * (j + 1)]

    group = 4

    def scatter(k):
        for g0 in range(0, sub, group):
            sums = []
            for u in range(group):
                s = k * sub + g0 + u
                t = idx_ref[0, 0, s]
                rows = pl.ds(pl.multiple_of(t * c8, c8), c8)
                sums.append((rows, o_ref[0, rows, :] + y3[pl.ds(s * c8, c8), :]))
            for rows, val in sums:
                o_ref[0, rows, :] = val

    n_sub = cap // sub
    gather(0)
    for k in range(n_sub):
        if k + 1 < n_sub:
            gather(k + 1)
        expert(k)
        if k > 0:
            scatter(k - 1)
    scatter(n_sub - 1)


def _moe_call(idx, h3, gate, wg, wu, wd):
    bm, ne, cap = idx.shape
    d = wg.shape[1]
    p = d // 2 // LANES
    c8 = d // LANES
    n = h3.shape[1] // p
    idx3 = idx.reshape(bm * ne, 1, cap)
    gate3 = gate.reshape(bm * ne, cap, 1)
    wmap = lambda i, j: (j, 0, 0)
    return pl.pallas_call(
        functools.partial(_moe_kernel, cap=cap, p=p, c8=c8, sub=min(MOE_SUB, cap)),
        grid=(bm, ne),
        in_specs=[
            pl.BlockSpec((1, 1, cap), lambda i, j: (i * ne + j, 0, 0), memory_space=pltpu.SMEM),
            pl.BlockSpec((1, n * p, LANES), lambda i, j: (i, 0, 0), pipeline_mode=pl.Buffered(1)),
            pl.BlockSpec((1, cap, 1), lambda i, j: (i * ne + j, 0, 0)),
            pl.BlockSpec((1, d, wg.shape[2]), wmap),
            pl.BlockSpec((1, d, wu.shape[2]), wmap),
            pl.BlockSpec((1, wd.shape[1], d), wmap),
        ],
        out_specs=pl.BlockSpec((1, n * c8, LANES), lambda i, j: (i, 0, 0), pipeline_mode=pl.Buffered(1)),
        out_shape=jax.ShapeDtypeStruct((bm, n * c8, LANES), F32),
        scratch_shapes=[pltpu.VMEM((cap * p, LANES), jnp.uint32), pltpu.VMEM((cap * c8, LANES), F32)],
        compiler_params=_cparams(2, VMEM_LIMIT),
        name="moe",
    )(idx3, h3, gate3, wg, wu, wd)


def _resid_kernel(x_ref, moe_ref, mod_ref, o_ref):
    tm, d = x_ref.shape[1], x_ref.shape[2]
    c8 = d // LANES
    g2 = mod_ref[0][5:6]
    for j in range(c8):
        cs = slice(LANES * j, LANES * (j + 1))
        o_ref[0, :, cs] = x_ref[0, :, cs] + g2[:, cs] * moe_ref[0, pl.ds(j, tm, stride=c8), :]


def _resid_call(x1, moe3, mod, *, tm):
    b, n, d = x1.shape
    mb = mod.shape[0]
    c8 = d // LANES
    mod_map = (lambda i, j: (i, 0, 0)) if mb > 1 else (lambda i, j: (0, 0, 0))
    tok = lambda i, j: (i, j, 0)
    return pl.pallas_call(
        _resid_kernel,
        grid=(b, n // tm),
        in_specs=[
            pl.BlockSpec((1, tm, d), tok),
            pl.BlockSpec((1, tm * c8, LANES), tok),
            pl.BlockSpec((1, 6, d), mod_map),
        ],
        out_specs=pl.BlockSpec((1, tm, d), tok),
        out_shape=jax.ShapeDtypeStruct((b, n, d), F32),
        compiler_params=_cparams(2, VMEM_LIMIT),
        name="resid",
    )(x1, moe3, mod)


def _rope_tables(n):
    t = jnp.arange(n)
    row = (t // GRID_W).astype(F32)
    col = (t % GRID_W).astype(F32)
    n_freq = HEAD_DIM // 4
    inv_freq = jnp.power(ROPE_BASE, -jnp.arange(n_freq, dtype=F32) / n_freq)
    ang = jnp.concatenate([row[:, None] * inv_freq, col[:, None] * inv_freq], axis=-1)
    cos, sin = jnp.cos(ang), jnp.sin(ang)
    cos_t = jnp.concatenate([cos, cos, cos, cos], axis=-1)
    sin_t = jnp.concatenate([-sin, sin, -sin, sin], axis=-1)
    return cos_t, sin_t


def _gain128(g):
    return jnp.concatenate([g, g]).reshape(1, LANES).astype(F32)


def _channel_mix(h_pack, x1, mod, aff_t, tri, weights, *, flatten, tm):
    wg, wu, wd = weights
    b, n, d = x1.shape
    cap = EC_CAPACITY_FACTOR * n // N_EXPERTS
    idx, gate = _topk_call(aff_t, tri, cap=cap)
    if flatten:
        idx = (idx + (jnp.arange(b, dtype=jnp.int32) * n)[:, None, None]).transpose(1, 0, 2).reshape(1, N_EXPERTS, b * cap)
        gate = gate.transpose(1, 0, 2).reshape(1, N_EXPERTS, b * cap)
        h_pack = h_pack.reshape(1, -1, LANES)
    moe3 = _moe_call(idx, h_pack, gate, wg, wu, wd)
    moe3 = moe3.reshape(b, -1, LANES)
    return _resid_call(x1, moe3, mod, tm=tm)


def kernel(x, c, ctx, c_ctx, ada_w, ada_b, norm1_g, norm2_g, router_w, exp_w_gate, exp_w_up, exp_w_down, ev_w_in,
           ev_w_out, ev_q_gain, ev_k_gain, ev_sink, od_w_in, od_w_out, od_q_gain, od_k_gain, od_rpb):
    b, n, d = x.shape
    nc = ctx.shape[1]
    tm = min(512, n)
    tmc = min(512, nc)

    rows = -(-(b + 1) // 8) * 8
    cs = jnp.concatenate([c, c_ctx[None], jnp.zeros((rows - b - 1, d), F32)], axis=0)
    mods = _mod_call(cs, ada_w, ada_b).reshape(ada_w.shape[0], rows, 6, d)

    cos_t, sin_t = _rope_tables(n)
    cs_dft = _channel_dft()
    tri = jnp.asarray(np.triu(np.ones((LANES, LANES), np.float32)), BF16)
    fw = FOURIER_GROUPS * LANES

    def router_split(l):
        rw = jnp.pad(router_w[l], ((0, 0), (0, LANES - N_EXPERTS)))
        rh = rw.astype(BF16)
        return rh, (rw - rh.astype(F32)).astype(BF16)

    def experts(l):
        return exp_w_gate[l].astype(BF16), exp_w_up[l].astype(BF16), exp_w_down[l].astype(BF16)

    mod_x, mod_c = mods[0, :b], mods[0, b:b + 1]
    g1 = norm1_g[0].reshape(1, d)
    g2 = norm2_g[0].reshape(1, d)
    w_in = ev_w_in[0].astype(BF16)
    w_out = ev_w_out[0].astype(BF16)
    qg, kg = _gain128(ev_q_gain[0]), _gain128(ev_k_gain[0])
    sink = ev_sink[0].astype(F32)
    rwh, rwl = router_split(0)
    wts = experts(0)

    ab_x, q_x, kd_x, vd_x = _premix_even_call(x, mod_x, g1, w_in, cs_dft, cos_t, sin_t, qg, kg, rope=True, tm=tm)
    ab_c, q_c, kd_c, vd_c = _premix_even_call(ctx, mod_c, g1, w_in, cs_dft, cos_t[:nc], sin_t[:nc], qg, kg,
                                              rope=False, tm=tmc)
    a_x = _swa_call(sink, q_x, kd_x, vd_x, kd_c, vd_c)
    a_c = _ctx_attn_call(sink, q_c, kd_c, vd_c)
    four_x = _fourier_call(ab_x, *_dft_tables(n), tm=tm)
    four_c = _fourier_call(ab_c, *_dft_tables(nc), tm=tmc)
    w_out_parts = [w_out[:fw], w_out[fw:]]
    x1, h3_x, aff_x = _postmix_call([four_x, a_x], w_out_parts, x, mod_x, g2, rwh, rwl, tm=tm)
    c1, h3_c, aff_c = _postmix_call([four_c, a_c], w_out_parts, ctx, mod_c, g2, rwh, rwl, tm=tmc)
    x = _channel_mix(h3_x, x1, mod_x, aff_x, tri, wts, flatten=False, tm=tm)
    ctx = _channel_mix(h3_c, c1, mod_c, aff_c, tri, wts, flatten=True, tm=tmc)

    mod_x, mod_c = mods[1, :b], mods[1, b:b + 1]
    g1 = norm1_g[1].reshape(1, d)
    g2 = norm2_g[1].reshape(1, d)
    w_in = od_w_in[0].astype(BF16)
    w_out = od_w_out[0].astype(BF16)
    qg, kg = _gain128(od_q_gain[0]), _gain128(od_k_gain[0])
    rwh, rwl = router_split(1)
    wts = experts(1)

    q_x, k_x, v_x = _premix_odd_call(x, mod_x, g1, w_in, qg, kg, tm=tm)
    _, k_c, v_c = _premix_odd_call(ctx, mod_c, g1, w_in, qg, kg, tm=tmc)
    a_x = _na_call(q_x, k_x, v_x, k_c, v_c, _na_bias_table(od_rpb[0]))
    x1, h3_x, aff_x = _postmix_call([a_x], [w_out], x, mod_x, g2, rwh, rwl, tm=tm)
    return _channel_mix(h3_x, x1, mod_x, aff_x, tri, wts, flatten=False, tm=tm)
```

```python
import functools
import math

import numpy as np
import jax
import jax.numpy as jnp
from jax import lax
from jax.experimental import pallas as pl
from jax.experimental.pallas import tpu as pltpu

GRID_W = 64
HEAD_DIM = 64
FOURIER_GROUPS = 4
SWA_WINDOW = 128
SWA_BLOCK = 128
NA_ROWS_MAX = 8
NA_COLS = 16
N_EXPERTS = 16
EC_CAPACITY_FACTOR = 2
ROPE_BASE = 10000.0
EPS = 1e-6

LANES = 128
NEG = -1e30
NA_QROWS = 4
NA_KROWS = 12
DFT_SUB = 64
MOE_SUB = 128
ROW_SUB = 256
ONES_ROWS = 16
LOG2E = math.log2(math.e)
VMEM_LIMIT = 60 * 1024 * 1024

F32 = jnp.float32
BF16 = jnp.bfloat16


def _cparams(n_axes, vmem=None):
    return pltpu.CompilerParams(dimension_semantics=("arbitrary",) * n_axes, vmem_limit_bytes=vmem)


def _dot(a, b):
    return jnp.dot(a, b, preferred_element_type=F32)


def _dot_t(a, b):
    return lax.dot_general(a, b, (((1,), (1,)), ((), ())), preferred_element_type=F32)


def _split(a):
    hi = a.astype(BF16)
    lo = (a - hi.astype(F32)).astype(BF16)
    return hi, lo


def _dot3(a, b):
    ah, al = _split(a)
    bh, bl = _split(b)
    return _dot(ah, bh) + _dot(al, bh) + _dot(ah, bl)


def _silu(a):
    return a / (1.0 + jnp.exp(-a))


def _rms_mod(x, gain, shift, scale):
    ms = jnp.mean(x * x, axis=-1, keepdims=True)
    y = x * lax.rsqrt(ms + EPS) * gain
    return y * (1.0 + scale) + shift


def _mod_kernel(cs_ref, w_ref, b_ref, o_ref):
    o_ref[0] = _dot3(_silu(cs_ref[...]), w_ref[0]) + b_ref[0]


def _mod_call(cs, ada_w, ada_b):
    depth, d, d6 = ada_w.shape
    r = cs.shape[0]
    tn = 1024
    return pl.pallas_call(
        _mod_kernel,
        grid=(depth, d6 // tn),
        in_specs=[
            pl.BlockSpec((r, d), lambda l, j: (0, 0)),
            pl.BlockSpec((1, d, tn), lambda l, j: (l, 0, j)),
            pl.BlockSpec((1, 1, tn), lambda l, j: (l, 0, j)),
        ],
        out_specs=pl.BlockSpec((1, r, tn), lambda l, j: (l, 0, j)),
        out_shape=jax.ShapeDtypeStruct((depth, r, d6), F32),
        compiler_params=_cparams(2),
        name="mod",
    )(cs, ada_w, ada_b.reshape(depth, 1, d6))


def _head_norm(t, gain, lane):
    t2 = t * t
    s_lo = jnp.sum(jnp.where(lane < HEAD_DIM, t2, 0.0), axis=-1, keepdims=True)
    s_all = jnp.sum(t2, axis=-1, keepdims=True)
    ms = jnp.where(lane < HEAD_DIM, s_lo, s_all - s_lo) * (1.0 / HEAD_DIM)
    return t * lax.rsqrt(ms + EPS) * gain


def _rope(t, cos_t, sin_t, lane):
    partner = jnp.where((lane % HEAD_DIM) < HEAD_DIM // 2, pltpu.roll(t, LANES - HEAD_DIM // 2, 1),
                        pltpu.roll(t, HEAD_DIM // 2, 1))
    return t * cos_t + partner * sin_t


def _dup_halves(t, lane):
    sw = pltpu.roll(t, HEAD_DIM, 1)
    return jnp.where(lane < HEAD_DIM, t, sw), jnp.where(lane < HEAD_DIM, sw, t)


def _premix_even_kernel(x_ref, mod_ref, g_ref, w_ref, cs_ref, cos_ref, sin_ref, qg_ref, kg_ref,
                        ab_ref, q_ref, kd_ref, vt_ref, *, rope):
    tm = x_ref.shape[1]
    sub = min(ROW_SUB, tm)
    m = mod_ref[0]
    lane = lax.broadcasted_iota(jnp.int32, (sub, LANES), 1)
    fw = FOURIER_GROUPS * LANES
    qw = q_ref.shape[2]
    csb = cs_ref[...].astype(BF16)
    for r0 in range(0, tm, sub):
        rs = slice(r0, r0 + sub)
        hb = _rms_mod(x_ref[0, rs], g_ref[...], m[0:1], m[1:2]).astype(BF16)
        pf = _dot(hb, w_ref[:, 0:fw])
        for g in range(FOURIER_GROUPS):
            ab = _dot(pf[:, LANES * g:LANES * (g + 1)].astype(BF16), csb)
            ab_ref[0, rs, LANES * g:LANES * (g + 1)] = ab[:, :LANES].astype(BF16)
            ab_ref[0, rs, fw + LANES * g:fw + LANES * (g + 1)] = ab[:, LANES:].astype(BF16)
        pq = _dot(hb, w_ref[:, fw:fw + qw])
        if rope:
            cos_t = cos_ref[rs]
            sin_t = sin_ref[rs]
        for c in range(qw // LANES):
            t = _head_norm(pq[:, LANES * c:LANES * (c + 1)], qg_ref[...], lane)
            if rope:
                t = _rope(t, cos_t, sin_t, lane)
            q_ref[0, rs, LANES * c:LANES * (c + 1)] = t.astype(BF16)
        pk = _dot(hb, w_ref[:, fw + qw:fw + qw + 2 * LANES])
        k = _head_norm(pk[:, :LANES], kg_ref[...], lane)
        if rope:
            k = _rope(k, cos_t, sin_t, lane)
        k0, k1 = _dup_halves(k, lane)
        kd_ref[0, rs, :LANES] = k0.astype(BF16)
        kd_ref[0, rs, LANES:] = k1.astype(BF16)
        vt_ref[0, :, rs] = pk[:, LANES:].T.astype(BF16)


def _premix_even_call(x, mod, gain, w_bf, cs, cos_t, sin_t, qg, kg, *, rope, tm):
    b, n, d = x.shape
    mb = mod.shape[0]
    wtot = w_bf.shape[1]
    fw = FOURIER_GROUPS * LANES
    qw = wtot - fw - 2 * LANES
    mod_map = (lambda i, j: (i, 0, 0)) if mb > 1 else (lambda i, j: (0, 0, 0))
    tok = lambda i, j: (i, j, 0)
    const2 = lambda i, j: (0, 0)
    return pl.pallas_call(
        functools.partial(_premix_even_kernel, rope=rope),
        grid=(b, n // tm),
        in_specs=[
            pl.BlockSpec((1, tm, d), tok),
            pl.BlockSpec((1, 6, d), mod_map),
            pl.BlockSpec((1, d), const2),
            pl.BlockSpec((d, wtot), const2),
            pl.BlockSpec((LANES, 2 * LANES), const2),
            pl.BlockSpec((tm, LANES), lambda i, j: (j, 0)),
            pl.BlockSpec((tm, LANES), lambda i, j: (j, 0)),
            pl.BlockSpec((1, LANES), const2),
            pl.BlockSpec((1, LANES), const2),
        ],
        out_specs=[
            pl.BlockSpec((1, tm, 2 * fw), tok),
            pl.BlockSpec((1, tm, qw), tok),
            pl.BlockSpec((1, tm, 2 * LANES), tok),
            pl.BlockSpec((1, LANES, tm), lambda i, j: (i, 0, j)),
        ],
        out_shape=[
            jax.ShapeDtypeStruct((b, n, 2 * fw), BF16),
            jax.ShapeDtypeStruct((b, n, qw), BF16),
            jax.ShapeDtypeStruct((b, n, 2 * LANES), BF16),
            jax.ShapeDtypeStruct((b, LANES, n), BF16),
        ],
        compiler_params=_cparams(2, VMEM_LIMIT),
        name="premix_even",
    )(x, mod, gain, w_bf, cs, cos_t, sin_t, qg, kg)


def _gated_moe_residual(x_ref, moe_ref, gate_row, r0, sub):
    c8 = x_ref.shape[2] // LANES
    cols = []
    for j in range(c8):
        cs = slice(LANES * j, LANES * (j + 1))
        cols.append(x_ref[0, r0:r0 + sub, cs] + gate_row[:, cs] * moe_ref[0, pl.ds(r0 * c8 + j, sub, stride=c8), :])
    return jnp.concatenate(cols, axis=1)


def _premix_odd_kernel(x_ref, moe_ref, modp_ref, mod_ref, g_ref, w_ref, qg_ref, kg_ref, q_ref, k_ref, v_ref,
                       *xo_ref):
    tm = x_ref.shape[1]
    sub = min(ROW_SUB, tm)
    m = mod_ref[0]
    gate_prev = modp_ref[0][5:6]
    lane = lax.broadcasted_iota(jnp.int32, (sub, LANES), 1)
    wq = q_ref.shape[2]
    chunk = 4 * LANES
    for r0 in range(0, tm, sub):
        rs = slice(r0, r0 + sub)
        x = _gated_moe_residual(x_ref, moe_ref, gate_prev, r0, sub)
        if xo_ref:
            xo_ref[0][0, rs] = x
        hb = _rms_mod(x, g_ref[...], m[0:1], m[1:2]).astype(BF16)
        for c0 in range(0, 3 * wq, chunk):
            p = _dot(hb, w_ref[:, c0:c0 + chunk])
            for cc in range(chunk // LANES):
                col = c0 + cc * LANES
                t = p[:, cc * LANES:(cc + 1) * LANES]
                if col < wq:
                    q_ref[0, rs, col:col + LANES] = _head_norm(t, qg_ref[...], lane).astype(BF16)
                elif col < 2 * wq:
                    k_ref[0, rs, col - wq:col - wq + LANES] = _head_norm(t, kg_ref[...], lane).astype(BF16)
                else:
                    v_ref[0, col - 2 * wq:col - 2 * wq + LANES, rs] = t.T.astype(BF16)


def _premix_odd_call(x1, moe3, mod_prev, mod, gain, w_bf, qg, kg, *, emit_x, tm):
    b, n, d = x1.shape
    mb = mod.shape[0]
    wq = w_bf.shape[1] // 3
    c8 = d // LANES
    mod_map = (lambda i, j: (i, 0, 0)) if mb > 1 else (lambda i, j: (0, 0, 0))
    tok = lambda i, j: (i, j, 0)
    const2 = lambda i, j: (0, 0)
    out_specs = [pl.BlockSpec((1, tm, wq), tok)] * 2 + [pl.BlockSpec((1, wq, tm), lambda i, j: (i, 0, j))]
    out_shape = [jax.ShapeDtypeStruct((b, n, wq), BF16)] * 2 + [jax.ShapeDtypeStruct((b, wq, n), BF16)]
    if emit_x:
        out_specs.append(pl.BlockSpec((1, tm, d), tok))
        out_shape.append(jax.ShapeDtypeStruct((b, n, d), F32))
    return pl.pallas_call(
        _premix_odd_kernel,
        grid=(b, n // tm),
        in_specs=[
            pl.BlockSpec((1, tm, d), tok),
            pl.BlockSpec((1, tm * c8, LANES), tok),
            pl.BlockSpec((1, 6, d), mod_map),
            pl.BlockSpec((1, 6, d), mod_map),
            pl.BlockSpec((1, d), const2),
            pl.BlockSpec((d, 3 * wq), const2),
            pl.BlockSpec((1, LANES), const2),
            pl.BlockSpec((1, LANES), const2),
        ],
        out_specs=out_specs,
        out_shape=out_shape,
        compiler_params=_cparams(2, VMEM_LIMIT),
        name="premix_odd",
    )(x1, moe3, mod_prev, mod, gain, w_bf, qg, kg)


def _col_reduce(x, op, slab=64):
    r = x.shape[0]
    if r > slab and r % slab == 0:
        x = op(x.reshape(r // slab, slab, x.shape[1]), axis=0)
    return op(x, axis=0, keepdims=True)


def _softmax_pv_t(s_t, v_t, sink=None):
    mx = _col_reduce(s_t, jnp.max)
    if sink is not None:
        mx = jnp.maximum(mx, sink)
    p = jnp.exp2(s_t - mx).astype(BF16)
    nd = v_t.shape[0]
    o = _dot(jnp.concatenate([v_t, jnp.ones((ONES_ROWS, v_t.shape[1]), BF16)], axis=0), p)
    den = o[nd:nd + 1]
    if sink is not None:
        den = den + jnp.exp2(sink - mx)
    return o[:nd] / den


def _mask_half(qc, lane, half):
    q32 = qc.astype(F32) * (HEAD_DIM ** -0.5 * LOG2E)
    keep = (lane < HEAD_DIM) if half == 0 else (lane >= HEAD_DIM)
    return jnp.where(keep, q32, 0.0).astype(BF16)


def _gqa_group(sink_ref, q_ref, o_ref, h, kd, v_t, masks, lane, r0=0, tq=None):
    tq = q_ref.shape[1] if tq is None else tq
    rs = slice(r0, r0 + tq)
    n_kv = v_t.shape[0] // HEAD_DIM
    cols_per_kv = q_ref.shape[2] // LANES // n_kv
    heads = 2 * cols_per_kv
    q_rows = []
    for cc in range(cols_per_kv):
        c = cols_per_kv * h + cc
        for half in range(2):
            q_rows.append(_mask_half(q_ref[0, rs, LANES * c:LANES * (c + 1)], lane, half))
    qm = jnp.concatenate(q_rows, axis=0)
    s = _dot_t(kd, qm)
    pieces, k0 = [], 0
    for rows, mask in masks:
        blk = s[k0:k0 + rows]
        pieces.append(blk if mask is None else jnp.where(mask, blk, NEG))
        k0 += rows
    s_t = jnp.concatenate(pieces, axis=0) if len(pieces) > 1 else pieces[0]
    head_of_lane = lax.broadcasted_iota(jnp.int32, (1, heads * tq), 1) // tq
    sink = jnp.zeros((1, heads * tq), F32)
    for a in range(heads):
        sink = jnp.where(head_of_lane == a, sink_ref[heads * h + a] * LOG2E, sink)
    o_t = _softmax_pv_t(s_t, v_t, sink)[HEAD_DIM * h:HEAD_DIM * (h + 1)]
    for cc in range(cols_per_kv):
        c = cols_per_kv * h + cc
        tile = jnp.concatenate([o_t[:, tq * (2 * cc):tq * (2 * cc + 1)], o_t[:, tq * (2 * cc + 1):tq * (2 * cc + 2)]],
                               axis=0)
        o_ref[0, rs, LANES * c:LANES * (c + 1)] = tile.T.astype(BF16)


def _swa_kernel(sink_ref, q_ref, km_ref, k0_ref, kp_ref, vm_ref, v0_ref, vp_ref, kc_ref, vc_ref, o_ref, *, n_pair):
    g = pl.program_id(1)
    tq = SWA_BLOCK
    nc = kc_ref.shape[1]
    n_kv = kc_ref.shape[2] // LANES
    heads = q_ref.shape[2] // HEAD_DIM // n_kv
    lane = lax.broadcasted_iota(jnp.int32, (tq, LANES), 1)
    jk = lax.broadcasted_iota(jnp.int32, (SWA_BLOCK, heads * tq), 0)
    iq = lax.broadcasted_iota(jnp.int32, (SWA_BLOCK, heads * tq), 1) % tq
    below, above = jk >= iq, jk <= iq
    prev_masks = [jnp.logical_and(below, g > 0), below]
    next_masks = [above, jnp.logical_and(above, g < n_pair - 1)]
    for sub in range(2):
        masks = [(nc, None), (SWA_BLOCK, prev_masks[sub]), (SWA_BLOCK, None), (SWA_BLOCK, next_masks[sub])]
        lo, hi = slice(0, SWA_BLOCK), slice(SWA_BLOCK, 2 * SWA_BLOCK)
        if sub == 0:
            v_loc = [vm_ref[0], v0_ref[0, :, lo], v0_ref[0, :, hi]]
        else:
            v_loc = [v0_ref[0, :, lo], v0_ref[0, :, hi], vp_ref[0]]
        v_t = jnp.concatenate([vc_ref[0]] + v_loc, axis=1)
        for h in range(n_kv):
            hs = slice(LANES * h, LANES * (h + 1))
            if sub == 0:
                k_loc = [km_ref[0, :, hs], k0_ref[0, lo, hs], k0_ref[0, hi, hs]]
            else:
                k_loc = [k0_ref[0, lo, hs], k0_ref[0, hi, hs], kp_ref[0, :, hs]]
            kd = jnp.concatenate([kc_ref[0, :, hs]] + k_loc, axis=0)
            _gqa_group(sink_ref, q_ref, o_ref, h, kd, v_t, masks, lane, r0=SWA_BLOCK * sub, tq=tq)


def _swa_call(sink, q, kd, vt, kdc, vtc):
    b, n, qw = q.shape
    nc = kdc.shape[1]
    kw = kd.shape[2]
    vw = vt.shape[1]
    n_blk = n // SWA_BLOCK
    n_pair = n_blk // 2
    prev = lambda j: jnp.maximum(2 * j - 1, 0)
    nxt = lambda j: jnp.minimum(2 * j + 2, n_blk - 1)
    kspec = lambda f: pl.BlockSpec((1, SWA_BLOCK, kw), lambda i, j: (i, f(j), 0))
    vspec = lambda f: pl.BlockSpec((1, vw, SWA_BLOCK), lambda i, j: (i, 0, f(j)))
    return pl.pallas_call(
        functools.partial(_swa_kernel, n_pair=n_pair),
        grid=(b, n_pair),
        in_specs=[
            pl.BlockSpec(memory_space=pltpu.SMEM),
            pl.BlockSpec((1, 2 * SWA_BLOCK, qw), lambda i, j: (i, j, 0)),
            kspec(prev), pl.BlockSpec((1, 2 * SWA_BLOCK, kw), lambda i, j: (i, j, 0)), kspec(nxt),
            vspec(prev), pl.BlockSpec((1, vw, 2 * SWA_BLOCK), lambda i, j: (i, 0, j)), vspec(nxt),
            pl.BlockSpec((1, nc, kw), lambda i, j: (i, 0, 0)),
            pl.BlockSpec((1, vw, nc), lambda i, j: (i, 0, 0)),
        ],
        out_specs=pl.BlockSpec((1, 2 * SWA_BLOCK, qw), lambda i, j: (i, j, 0)),
        out_shape=jax.ShapeDtypeStruct((b, n, qw), BF16),
        compiler_params=_cparams(2, VMEM_LIMIT),
        name="swa",
    )(sink, q, kd, kd, kd, vt, vt, vt, kdc, vtc)


def _ctx_attn_kernel(sink_ref, q_ref, kc_ref, vc_ref, o_ref):
    tq = q_ref.shape[1]
    lane = lax.broadcasted_iota(jnp.int32, (tq, LANES), 1)
    for h in range(kc_ref.shape[2] // LANES):
        _gqa_group(sink_ref, q_ref, o_ref, h, kc_ref[0, :, LANES * h:LANES * (h + 1)], vc_ref[0],
                   [(kc_ref.shape[1], None)], lane)


def _ctx_attn_call(sink, q, kd, vt):
    b, nc, qw = q.shape
    kw = kd.shape[2]
    vw = vt.shape[1]
    m = lambda i: (i, 0, 0)
    return pl.pallas_call(
        _ctx_attn_kernel,
        grid=(b,),
        in_specs=[
            pl.BlockSpec(memory_space=pltpu.SMEM),
            pl.BlockSpec((1, nc, qw), m),
            pl.BlockSpec((1, nc, kw), m),
            pl.BlockSpec((1, vw, nc), m),
        ],
        out_specs=pl.BlockSpec((1, nc, qw), m),
        out_shape=jax.ShapeDtypeStruct((b, nc, qw), BF16),
        compiler_params=_cparams(1, VMEM_LIMIT),
        name="ctx_attn",
    )(sink, q, kd, vt)


def _na_kernel(q_ref, k0_ref, k1_ref, k2_ref, v0_ref, v1_ref, v2_ref, kc_ref, vc_ref, tb_ref, o_ref, *, n_j, rows):
    j = pl.program_id(1)
    tq = q_ref.shape[1]
    tk = 3 * k0_ref.shape[1]
    nc = kc_ref.shape[1]
    w0 = NA_QROWS * jnp.clip(j - 1, 0, n_j - 3)
    lane = lax.broadcasted_iota(jnp.int32, (tq, LANES), 1)
    k_row = lax.broadcasted_iota(jnp.int32, (tk, 2 * tq), 0) // GRID_W
    q_row = (lax.broadcasted_iota(jnp.int32, (tk, 2 * tq), 1) % tq) // GRID_W
    lo = jnp.zeros((tk, 2 * tq), jnp.int32)
    for rr in range(NA_QROWS):
        r = NA_QROWS * j + rr
        lo_rr = jnp.clip(r - NA_ROWS_MAX // 2, 0, rows - NA_ROWS_MAX) - w0
        lo = jnp.where(q_row == rr, lo_rr, lo)
    row_bias = jnp.where(jnp.logical_and(k_row >= lo, k_row < lo + NA_ROWS_MAX), 0.0, NEG)
    delta = w0 - NA_QROWS * j
    n_dr = 2 * NA_ROWS_MAX - 1
    for c in range(q_ref.shape[2] // LANES):
        cs = slice(LANES * c, LANES * (c + 1))
        qc = q_ref[0, :, cs]
        kl = jnp.concatenate([kc_ref[0, :, cs], k0_ref[0, :, cs], k1_ref[0, :, cs], k2_ref[0, :, cs]], axis=0)
        v_t = jnp.concatenate([vc_ref[0, cs, :], v0_ref[0, cs, :], v1_ref[0, cs, :], v2_ref[0, cs, :]], axis=1)
        qm = jnp.concatenate([_mask_half(qc, lane, 0), _mask_half(qc, lane, 1)], axis=0)
        s = _dot_t(kl, qm)
        bias_rows = []
        for kr in range(NA_KROWS):
            pieces = []
            for half in range(2):
                for t in range(NA_QROWS // 2):
                    dr = delta + kr - 2 * t + NA_ROWS_MAX - 1
                    pieces.append(tb_ref[2 * c + half, jnp.clip(dr, 0, n_dr)])
            bias_rows.append(jnp.concatenate(pieces, axis=1))
        bias = jnp.concatenate(bias_rows, axis=0)
        s_t = jnp.concatenate([s[:nc], s[nc:] + bias + row_bias], axis=0)
        o_t = _softmax_pv_t(s_t, v_t)
        tile = jnp.concatenate([o_t[:HEAD_DIM, :tq], o_t[HEAD_DIM:, tq:]], axis=0)
        o_ref[0, :, cs] = tile.T.astype(BF16)


def _na_call(q, k, vt, kc, vtc, tb):
    b, n, w = q.shape
    nc = kc.shape[1]
    rows = n // GRID_W
    n_j = rows // NA_QROWS
    tq = NA_QROWS * GRID_W
    cur = lambda i, j: (i, j, 0)
    first = lambda j: jnp.clip(j - 1, 0, n_j - 3)
    kspec = lambda d: pl.BlockSpec((1, tq, w), lambda i, j: (i, first(j) + d, 0))
    vspec = lambda d: pl.BlockSpec((1, w, tq), lambda i, j: (i, 0, first(j) + d))
    return pl.pallas_call(
        functools.partial(_na_kernel, n_j=n_j, rows=rows),
        grid=(b, n_j),
        in_specs=[
            pl.BlockSpec((1, tq, w), cur),
            kspec(0), kspec(1), kspec(2),
            vspec(0), vspec(1), vspec(2),
            pl.BlockSpec((1, nc, w), lambda i, j: (i, 0, 0)),
            pl.BlockSpec((1, w, nc), lambda i, j: (i, 0, 0)),
            pl.BlockSpec(tb.shape, lambda i, j: (0, 0, 0, 0)),
        ],
        out_specs=pl.BlockSpec((1, tq, w), cur),
        out_shape=jax.ShapeDtypeStruct((b, n, w), BF16),
        compiler_params=_cparams(2, VMEM_LIMIT),
        name="na",
    )(q, k, k, k, vt, vt, vt, kc, vtc, tb)


def _na_bias_table(rpb):
    col_q = np.arange(GRID_W)[None, :]
    col_k = np.arange(GRID_W)[:, None]
    c_start = np.clip(col_q - NA_COLS // 2, 0, GRID_W - NA_COLS)
    col_valid = (col_k >= c_start) & (col_k < c_start + NA_COLS)
    dc_idx = np.clip(col_k - col_q + NA_COLS - 1, 0, 2 * NA_COLS - 2)
    t = jnp.where(col_valid[None, None], (rpb.astype(F32) * LOG2E)[:, :, dc_idx], NEG)
    pad = jnp.full_like(t[:, :1], NEG)
    t = jnp.concatenate([pad, t, pad], axis=1)
    return jnp.concatenate([t[:, 1:], t[:, :-1]], axis=-1)


def _fourier_kernel(ab_ref, ca_ref, sa_ref, cb_ref, sb_ref, o_ref, cm_ref, sm_ref):
    @pl.when(pl.program_id(1) == 0)
    def _():
        cb = cb_ref[...]
        sb = sb_ref[...]
        for a in range(ca_ref.shape[0]):
            ca = ca_ref[a:a + 1, :]
            sa = sa_ref[a:a + 1, :]
            rs = slice(DFT_SUB * a, DFT_SUB * (a + 1))
            cm_ref[rs, :] = (ca * cb - sa * sb).astype(BF16)
            sm_ref[rs, :] = (sa * cb + ca * sb).astype(BF16)

    fw = o_ref.shape[2]
    y = _dot(cm_ref[...], ab_ref[0, :, :fw]) - _dot(sm_ref[...], ab_ref[0, :, fw:])
    o_ref[0] = y.astype(BF16)


def _fourier_call(ab, ca, sa, cb, sb, *, tm):
    b, n, w2 = ab.shape
    fw = w2 // 2
    na = tm // DFT_SUB
    return pl.pallas_call(
        _fourier_kernel,
        grid=(n // tm, b),
        in_specs=[
            pl.BlockSpec((1, n, w2), lambda i, j: (j, 0, 0)),
            pl.BlockSpec((na, n), lambda i, j: (i, 0)),
            pl.BlockSpec((na, n), lambda i, j: (i, 0)),
            pl.BlockSpec((DFT_SUB, n), lambda i, j: (0, 0)),
            pl.BlockSpec((DFT_SUB, n), lambda i, j: (0, 0)),
        ],
        out_specs=pl.BlockSpec((1, tm, fw), lambda i, j: (j, i, 0)),
        out_shape=jax.ShapeDtypeStruct((b, n, fw), BF16),
        scratch_shapes=[pltpu.VMEM((tm, n), BF16), pltpu.VMEM((tm, n), BF16)],
        compiler_params=_cparams(2, VMEM_LIMIT),
        name="fourier",
    )(ab, ca, sa, cb, sb)


def _dft_tables(n):
    k = jnp.arange(n, dtype=jnp.int32)
    a = jnp.arange(n // DFT_SUB, dtype=jnp.int32)
    bb = jnp.arange(DFT_SUB, dtype=jnp.int32)
    ang_a = ((DFT_SUB * a[:, None] * k[None, :]) % n).astype(F32) * (2.0 * math.pi / n)
    ang_b = ((bb[:, None] * k[None, :]) % n).astype(F32) * (2.0 * math.pi / n)
    s = float(n) ** -0.5
    return jnp.cos(ang_a), jnp.sin(ang_a), jnp.cos(ang_b) * s, jnp.sin(ang_b) * s


def _channel_dft():
    c = jnp.arange(LANES, dtype=jnp.int32)
    ang = ((c[:, None] * c[None, :]) % LANES).astype(F32) * (2.0 * math.pi / LANES)
    s = float(LANES) ** -0.5
    return jnp.concatenate([jnp.cos(ang) * s, jnp.sin(ang) * s], axis=1)


def _postmix_kernel(*refs, n_in):
    a_refs = refs[:n_in]
    w_refs = refs[n_in:2 * n_in]
    x_ref, mod_ref, g_ref, rw_ref, x1_ref, h3_ref, aff_ref = refs[2 * n_in:]
    tm, d = x_ref.shape[1], x_ref.shape[2]
    sub = min(ROW_SUB, tm)
    p = d // 2 // LANES
    m = mod_ref[0]
    for r0 in range(0, tm, sub):
        rs = slice(r0, r0 + sub)
        y = None
        for a_ref, w_ref in zip(a_refs, w_refs):
            t = _dot(a_ref[0, rs], w_ref[...])
            y = t if y is None else y + t
        x1 = x_ref[0, rs] + m[2:3] * y
        x1_ref[0, rs] = x1
        h2 = _rms_mod(x1, g_ref[...], m[3:4], m[4:5])
        hh, hl = _split(h2)
        parts = _dot_t(rw_ref[...], jnp.concatenate([hh, hl], axis=1))
        lt = parts[:N_EXPERTS] + parts[N_EXPERTS:2 * N_EXPERTS]
        ex = jnp.exp(lt - jnp.max(lt, axis=0, keepdims=True))
        aff_ref[0, :, rs] = ex / jnp.sum(ex, axis=0, keepdims=True)
        bits = pltpu.bitcast(hh.astype(F32), jnp.uint32)
        packed = (bits[:, d // 2:] & jnp.uint32(0xFFFF0000)) | (bits[:, :d // 2] >> 16)
        for jj in range(p):
            h3_ref[0, pl.ds(r0 * p + jj, sub, stride=p), :] = packed[:, LANES * jj:LANES * (jj + 1)]


def _postmix_call(a_list, w_list, x, mod, gain, rw, *, tm):
    b, n, d = x.shape
    mb = mod.shape[0]
    n_in = len(a_list)
    p = d // 2 // LANES
    mod_map = (lambda i, j: (i, 0, 0)) if mb > 1 else (lambda i, j: (0, 0, 0))
    tok = lambda i, j: (i, j, 0)
    const2 = lambda i, j: (0, 0)
    in_specs = [pl.BlockSpec((1, tm, a.shape[2]), tok) for a in a_list]
    in_specs += [pl.BlockSpec(w.shape, const2) for w in w_list]
    in_specs += [
        pl.BlockSpec((1, tm, d), tok),
        pl.BlockSpec((1, 6, d), mod_map),
        pl.BlockSpec((1, d), const2),
        pl.BlockSpec((LANES, 2 * d), const2),
    ]
    return pl.pallas_call(
        functools.partial(_postmix_kernel, n_in=n_in),
        grid=(b, n // tm),
        in_specs=in_specs,
        out_specs=[
            pl.BlockSpec((1, tm, d), tok),
            pl.BlockSpec((1, tm * p, LANES), tok),
            pl.BlockSpec((1, N_EXPERTS, tm), lambda i, j: (i, 0, j)),
        ],
        out_shape=[
            jax.ShapeDtypeStruct((b, n, d), F32),
            jax.ShapeDtypeStruct((b, n * p, LANES), jnp.uint32),
            jax.ShapeDtypeStruct((b, N_EXPERTS, n), F32),
        ],
        compiler_params=_cparams(2, VMEM_LIMIT),
        name="postmix",
    )(*a_list, *w_list, x, mod, gain, rw)


def _prefix_incl(ones, tri):
    carry = jnp.zeros((ones.shape[0], 1), F32)
    outs = []
    for c in range(ones.shape[1] // LANES):
        blk = ones[:, LANES * c:LANES * (c + 1)]
        outs.append(_dot(blk.astype(BF16), tri) + carry)
        carry = carry + jnp.sum(blk, axis=1, keepdims=True)
    return jnp.concatenate(outs, axis=1)


def _topk_kernel(aff_ref, tri_ref, idx_ref, gate_ref, *, cap):
    n = aff_ref.shape[2]
    aff = aff_ref[0]
    bits = pltpu.bitcast(aff, jnp.int32)
    capf = jnp.float32(cap)

    def body(it, lo):
        t = lo | jnp.left_shift(jnp.int32(1), 30 - it)
        cnt = jnp.sum(jnp.where(bits >= t, 1.0, 0.0), axis=1, keepdims=True)
        return jnp.where(cnt >= capf, t, lo)

    thr = lax.fori_loop(0, 31, body, jnp.zeros((N_EXPERTS, 1), jnp.int32))
    gt = jnp.where(bits > thr, 1.0, 0.0)
    eq = jnp.where(bits == thr, 1.0, 0.0)
    need = capf - jnp.sum(gt, axis=1, keepdims=True)
    tri = tri_ref[...]
    eq_before = _prefix_incl(eq, tri) - eq
    sel = gt + eq * jnp.where(eq_before < need, 1.0, 0.0)
    slot = _prefix_incl(sel, tri).astype(jnp.int32) - 1
    tok = lax.broadcasted_iota(jnp.int32, (N_EXPERTS, n), 1)
    disp = jnp.where(sel > 0.0, tok - slot, -1)
    gate = aff
    for bit in range((n - 1).bit_length()):
        step = 1 << bit
        disp_in = pltpu.roll(disp, n - step, 1)
        gate_in = pltpu.roll(gate, n - step, 1)
        move_in = jnp.logical_and(disp_in >= 0, ((disp_in >> bit) & 1) == 1)
        stay = jnp.logical_and(disp >= 0, ((disp >> bit) & 1) == 0)
        gate = jnp.where(move_in, gate_in, gate)
        disp = jnp.where(move_in, disp_in, jnp.where(stay, disp, -1))
    idx_ref[0] = (tok + disp)[:, :cap]
    gate_ref[0] = gate[:, :cap]


def _topk_call(aff_t, tri, *, cap):
    b, e, n = aff_t.shape
    return pl.pallas_call(
        functools.partial(_topk_kernel, cap=cap),
        grid=(b,),
        in_specs=[
            pl.BlockSpec((1, e, n), lambda i: (i, 0, 0)),
            pl.BlockSpec((LANES, LANES), lambda i: (0, 0)),
        ],
        out_specs=[pl.BlockSpec((1, e, cap), lambda i: (i, 0, 0))] * 2,
        out_shape=[jax.ShapeDtypeStruct((b, e, cap), jnp.int32), jax.ShapeDtypeStruct((b, e, cap), F32)],
        compiler_params=_cparams(1, VMEM_LIMIT),
        name="topk",
    )(aff_t, tri)


def _moe_kernel(idx_ref, h3_ref, gate_ref, wg_ref, wu_ref, wd_ref, o_ref, xs3, y3, *, cap, p, c8, sub):
    e = pl.program_id(1)

    @pl.when(e == 0)
    def _():
        o_ref[...] = jnp.zeros(o_ref.shape, o_ref.dtype)

    def gather(k):
        for r in range(sub):
            s = k * sub + r
            t = idx_ref[0, 0, s]
            xs3[pl.ds(s * p, p), :] = h3_ref[0, pl.ds(pl.multiple_of(t * p, p), p), :]

    def expert(k):
        base = k * sub
        lo, hi = [], []
        for j in range(p):
            w = xs3[pl.ds(base * p + j, sub, stride=p), :]
            lo.append(pltpu.bitcast(w << 16, F32).astype(BF16))
            hi.append(pltpu.bitcast(w & jnp.uint32(0xFFFF0000), F32).astype(BF16))
        xs = jnp.concatenate(lo + hi, axis=1)
        a = _dot(xs, wg_ref[0])
        u_ = _dot(xs, wu_ref[0])
        act = (_silu(a) * u_).astype(BF16)
        y = _dot(act, wd_ref[0])
        for j in range(c8):
            y3[pl.ds(base * c8 + j, sub, stride=c8), :] = y[:, LANES * j:LANES * (j + 1)]

    group = 4

    def scatter(k):
        for g0 in range(0, sub, group):
            sums = []
            for u in range(group):
                s = k * sub + g0 + u
                t = idx_ref[0, 0, s]
                rows = pl.ds(pl.multiple_of(t * c8, c8), c8)
                sums.append((rows, o_ref[0, rows, :] + y3[pl.ds(s * c8, c8), :] * gate_ref[0, 0, s]))
            for rows, val in sums:
                o_ref[0, rows, :] = val

    n_sub = cap // sub
    gather(0)
    for k in range(n_sub):
        if k + 1 < n_sub:
            gather(k + 1)
        expert(k)
        if k > 0:
            scatter(k - 1)
    scatter(n_sub - 1)


def _moe_call(idx, h3, gate, wg, wu, wd):
    bm, ne, cap = idx.shape
    d = wg.shape[1]
    p = d // 2 // LANES
    c8 = d // LANES
    n = h3.shape[1] // p
    idx3 = idx.reshape(bm * ne, 1, cap)
    gate3 = gate.reshape(bm * ne, 1, cap)
    wmap = lambda i, j: (j, 0, 0)
    return pl.pallas_call(
        functools.partial(_moe_kernel, cap=cap, p=p, c8=c8, sub=min(MOE_SUB, cap)),
        grid=(bm, ne),
        in_specs=[
            pl.BlockSpec((1, 1, cap), lambda i, j: (i * ne + j, 0, 0), memory_space=pltpu.SMEM),
            pl.BlockSpec((1, n * p, LANES), lambda i, j: (i, 0, 0), pipeline_mode=pl.Buffered(1)),
            pl.BlockSpec((1, 1, cap), lambda i, j: (i * ne + j, 0, 0), memory_space=pltpu.SMEM),
            pl.BlockSpec((1, d, wg.shape[2]), wmap),
            pl.BlockSpec((1, d, wu.shape[2]), wmap),
            pl.BlockSpec((1, wd.shape[1], d), wmap),
        ],
        out_specs=pl.BlockSpec((1, n * c8, LANES), lambda i, j: (i, 0, 0), pipeline_mode=pl.Buffered(1)),
        out_shape=jax.ShapeDtypeStruct((bm, n * c8, LANES), F32),
        scratch_shapes=[pltpu.VMEM((cap * p, LANES), jnp.uint32), pltpu.VMEM((cap * c8, LANES), F32)],
        compiler_params=_cparams(2, VMEM_LIMIT),
        name="moe",
    )(idx3, h3, gate3, wg, wu, wd)


def _resid_kernel(x_ref, moe_ref, mod_ref, o_ref):
    o_ref[0] = _gated_moe_residual(x_ref, moe_ref, mod_ref[0][5:6], 0, x_ref.shape[1])


def _resid_call(x1, moe3, mod, *, tm):
    b, n, d = x1.shape
    mb = mod.shape[0]
    c8 = d // LANES
    mod_map = (lambda i, j: (i, 0, 0)) if mb > 1 else (lambda i, j: (0, 0, 0))
    tok = lambda i, j: (i, j, 0)
    return pl.pallas_call(
        _resid_kernel,
        grid=(b, n // tm),
        in_specs=[
            pl.BlockSpec((1, tm, d), tok),
            pl.BlockSpec((1, tm * c8, LANES), tok),
            pl.BlockSpec((1, 6, d), mod_map),
        ],
        out_specs=pl.BlockSpec((1, tm, d), tok),
        out_shape=jax.ShapeDtypeStruct((b, n, d), F32),
        compiler_params=_cparams(2, VMEM_LIMIT),
        name="resid",
    )(x1, moe3, mod)


def _rope_tables(n):
    t = jnp.arange(n)
    row = (t // GRID_W).astype(F32)
    col = (t % GRID_W).astype(F32)
    n_freq = HEAD_DIM // 4
    inv_freq = jnp.power(ROPE_BASE, -jnp.arange(n_freq, dtype=F32) / n_freq)
    ang = jnp.concatenate([row[:, None] * inv_freq, col[:, None] * inv_freq], axis=-1)
    cos, sin = jnp.cos(ang), jnp.sin(ang)
    cos_t = jnp.concatenate([cos, cos, cos, cos], axis=-1)
    sin_t = jnp.concatenate([-sin, sin, -sin, sin], axis=-1)
    return cos_t, sin_t


def _gain128(g):
    return jnp.concatenate([g, g]).reshape(1, LANES).astype(F32)


def _channel_mix(h_pack, aff_t, tri, weights, *, flatten):
    wg, wu, wd = weights
    b, _, n = aff_t.shape
    cap = EC_CAPACITY_FACTOR * n // N_EXPERTS
    idx, gate = _topk_call(aff_t, tri, cap=cap)
    if flatten:
        idx = (idx + (jnp.arange(b, dtype=jnp.int32) * n)[:, None, None]).transpose(1, 0, 2).reshape(1, N_EXPERTS, b * cap)
        gate = gate.transpose(1, 0, 2).reshape(1, N_EXPERTS, b * cap)
        h_pack = h_pack.reshape(1, -1, LANES)
    return _moe_call(idx, h_pack, gate, wg, wu, wd).reshape(b, -1, LANES)


def kernel(x, c, ctx, c_ctx, ada_w, ada_b, norm1_g, norm2_g, router_w, exp_w_gate, exp_w_up, exp_w_down, ev_w_in,
           ev_w_out, ev_q_gain, ev_k_gain, ev_sink, od_w_in, od_w_out, od_q_gain, od_k_gain, od_rpb):
    b, n, d = x.shape
    nc = ctx.shape[1]
    tm = min(512, n)
    tmc = min(512, nc)

    rows = -(-(b + 1) // 8) * 8
    cs = jnp.concatenate([c, c_ctx[None], jnp.zeros((rows - b - 1, d), F32)], axis=0)
    mods = _mod_call(cs, ada_w, ada_b).reshape(ada_w.shape[0], rows, 6, d)

    cos_t, sin_t = _rope_tables(n)
    cs_dft = _channel_dft()
    tri = jnp.asarray(np.triu(np.ones((LANES, LANES), np.float32)), BF16)
    fw = FOURIER_GROUPS * LANES

    def router_split(l):
        rw = router_w[l].T
        rh = rw.astype(BF16)
        rl = (rw - rh.astype(F32)).astype(BF16)
        left = jnp.pad(jnp.concatenate([rh, rl], axis=0), ((0, LANES - 2 * N_EXPERTS), (0, 0)))
        right = jnp.pad(rh, ((0, LANES - N_EXPERTS), (0, 0)))
        return jnp.concatenate([left, right], axis=1)

    def experts(l):
        return exp_w_gate[l].astype(BF16), exp_w_up[l].astype(BF16), exp_w_down[l].astype(BF16)

    mod_x, mod_c = mods[0, :b], mods[0, b:b + 1]
    g1 = norm1_g[0].reshape(1, d)
    g2 = norm2_g[0].reshape(1, d)
    w_in = ev_w_in[0].astype(BF16)
    w_out = ev_w_out[0].astype(BF16)
    qg, kg = _gain128(ev_q_gain[0]), _gain128(ev_k_gain[0])
    sink = ev_sink[0].astype(F32)
    rw = router_split(0)
    wts = experts(0)

    ab_x, q_x, kd_x, vd_x = _premix_even_call(x, mod_x, g1, w_in, cs_dft, cos_t, sin_t, qg, kg, rope=True, tm=tm)
    ab_c, q_c, kd_c, vd_c = _premix_even_call(ctx, mod_c, g1, w_in, cs_dft, cos_t[:nc], sin_t[:nc], qg, kg,
                                              rope=False, tm=tmc)
    a_x = _swa_call(sink, q_x, kd_x, vd_x, kd_c, vd_c)
    a_c = _ctx_attn_call(sink, q_c, kd_c, vd_c)
    four_x = _fourier_call(ab_x, *_dft_tables(n), tm=tm)
    four_c = _fourier_call(ab_c, *_dft_tables(nc), tm=tmc)
    w_out_parts = [w_out[:fw], w_out[fw:]]
    x1, h3_x, aff_x = _postmix_call([four_x, a_x], w_out_parts, x, mod_x, g2, rw, tm=tm)
    c1, h3_c, aff_c = _postmix_call([four_c, a_c], w_out_parts, ctx, mod_c, g2, rw, tm=tmc)
    moe_x = _channel_mix(h3_x, aff_x, tri, wts, flatten=False)
    moe_c = _channel_mix(h3_c, aff_c, tri, wts, flatten=True)

    mod_x0, mod_c0 = mod_x, mod_c
    mod_x, mod_c = mods[1, :b], mods[1, b:b + 1]
    g1 = norm1_g[1].reshape(1, d)
    g2 = norm2_g[1].reshape(1, d)
    w_in = od_w_in[0].astype(BF16)
    w_out = od_w_out[0].astype(BF16)
    qg, kg = _gain128(od_q_gain[0]), _gain128(od_k_gain[0])
    rw = router_split(1)
    wts = experts(1)

    q_x, k_x, v_x, x = _premix_odd_call(x1, moe_x, mod_x0, mod_x, g1, w_in, qg, kg, emit_x=True, tm=tm)
    _, k_c, v_c = _premix_odd_call(c1, moe_c, mod_c0, mod_c, g1, w_in, qg, kg, emit_x=False, tm=tmc)
    a_x = _na_call(q_x, k_x, v_x, k_c, v_c, _na_bias_table(od_rpb[0]))
    x1, h3_x, aff_x = _postmix_call([a_x], [w_out], x, mod_x, g2, rw, tm=tm)
    return _resid_call(x1, _channel_mix(h3_x, aff_x, tri, wts, flatten=False), mod_x, tm=tm)
```

```python
import functools
import math

import numpy as np
import jax
import jax.numpy as jnp
from jax import lax
from jax.experimental import pallas as pl
from jax.experimental.pallas import tpu as pltpu

GRID_W = 64
HEAD_DIM = 64
FOURIER_GROUPS = 4
SWA_WINDOW = 128
SWA_BLOCK = 128
NA_ROWS_MAX = 8
NA_COLS = 16
N_EXPERTS = 16
EC_CAPACITY_FACTOR = 2
ROPE_BASE = 10000.0
EPS = 1e-6

LANES = 128
NEG = -1e30
NA_QROWS = 4
NA_KROWS = 12
DFT_SUB = 64
MOE_SUB = 128
ROW_SUB = 256
ONES_ROWS = 16
LOG2E = math.log2(math.e)
VMEM_LIMIT = 60 * 1024 * 1024

F32 = jnp.float32
BF16 = jnp.bfloat16


def _cparams(n_axes, vmem=None):
    return pltpu.CompilerParams(dimension_semantics=("arbitrary",) * n_axes, vmem_limit_bytes=vmem)


def _dot(a, b):
    return jnp.dot(a, b, preferred_element_type=F32)


def _dot_t(a, b):
    return lax.dot_general(a, b, (((1,), (1,)), ((), ())), preferred_element_type=F32)


def _split(a):
    hi = a.astype(BF16)
    lo = (a - hi.astype(F32)).astype(BF16)
    return hi, lo


def _dot3(a, b):
    ah, al = _split(a)
    bh, bl = _split(b)
    return _dot(ah, bh) + _dot(al, bh) + _dot(ah, bl)


def _silu(a):
    return a / (1.0 + jnp.exp(-a))


def _rms_mod(x, gain, shift, scale):
    ms = jnp.mean(x * x, axis=-1, keepdims=True)
    y = x * lax.rsqrt(ms + EPS) * gain
    return y * (1.0 + scale) + shift


def _mod_kernel(cs_ref, w_ref, b_ref, o_ref):
    o_ref[0] = _dot3(_silu(cs_ref[...]), w_ref[0]) + b_ref[0]


def _mod_call(cs, ada_w, ada_b):
    depth, d, d6 = ada_w.shape
    r = cs.shape[0]
    tn = 1024
    return pl.pallas_call(
        _mod_kernel,
        grid=(depth, d6 // tn),
        in_specs=[
            pl.BlockSpec((r, d), lambda l, j: (0, 0)),
            pl.BlockSpec((1, d, tn), lambda l, j: (l, 0, j)),
            pl.BlockSpec((1, 1, tn), lambda l, j: (l, 0, j)),
        ],
        out_specs=pl.BlockSpec((1, r, tn), lambda l, j: (l, 0, j)),
        out_shape=jax.ShapeDtypeStruct((depth, r, d6), F32),
        compiler_params=_cparams(2),
        name="mod",
    )(cs, ada_w, ada_b.reshape(depth, 1, d6))


def _head_norm(t, gain, lane):
    t2 = t * t
    s_lo = jnp.sum(jnp.where(lane < HEAD_DIM, t2, 0.0), axis=-1, keepdims=True)
    s_all = jnp.sum(t2, axis=-1, keepdims=True)
    ms = jnp.where(lane < HEAD_DIM, s_lo, s_all - s_lo) * (1.0 / HEAD_DIM)
    return t * lax.rsqrt(ms + EPS) * gain


def _rope(t, cos_t, sin_t, lane):
    partner = jnp.where((lane % HEAD_DIM) < HEAD_DIM // 2, pltpu.roll(t, LANES - HEAD_DIM // 2, 1),
                        pltpu.roll(t, HEAD_DIM // 2, 1))
    return t * cos_t + partner * sin_t


def _dup_halves(t, lane):
    sw = pltpu.roll(t, HEAD_DIM, 1)
    return jnp.where(lane < HEAD_DIM, t, sw), jnp.where(lane < HEAD_DIM, sw, t)


def _premix_even_kernel(x_ref, mod_ref, g_ref, w_ref, cs_ref, cos_ref, sin_ref, qg_ref, kg_ref,
                        ab_ref, q_ref, kd_ref, vt_ref, *, rope):
    tm = x_ref.shape[1]
    sub = min(ROW_SUB, tm)
    m = mod_ref[0]
    lane = lax.broadcasted_iota(jnp.int32, (sub, LANES), 1)
    fw = FOURIER_GROUPS * LANES
    qw = q_ref.shape[2]
    csb = cs_ref[...].astype(BF16)
    for r0 in range(0, tm, sub):
        rs = slice(r0, r0 + sub)
        hb = _rms_mod(x_ref[0, rs], g_ref[...], m[0:1], m[1:2]).astype(BF16)
        pf = _dot(hb, w_ref[:, 0:fw])
        for g in range(FOURIER_GROUPS):
            ab = _dot(pf[:, LANES * g:LANES * (g + 1)].astype(BF16), csb)
            ab_ref[0, rs, LANES * g:LANES * (g + 1)] = ab[:, :LANES].astype(BF16)
            ab_ref[0, rs, fw + LANES * g:fw + LANES * (g + 1)] = ab[:, LANES:].astype(BF16)
        pq = _dot(hb, w_ref[:, fw:fw + qw])
        if rope:
            cos_t = cos_ref[rs]
            sin_t = sin_ref[rs]
        for c in range(qw // LANES):
            t = _head_norm(pq[:, LANES * c:LANES * (c + 1)], qg_ref[...], lane)
            if rope:
                t = _rope(t, cos_t, sin_t, lane)
            q_ref[0, rs, LANES * c:LANES * (c + 1)] = t.astype(BF16)
        pk = _dot(hb, w_ref[:, fw + qw:fw + qw + 2 * LANES])
        k = _head_norm(pk[:, :LANES], kg_ref[...], lane)
        if rope:
            k = _rope(k, cos_t, sin_t, lane)
        k0, k1 = _dup_halves(k, lane)
        kd_ref[0, rs, :LANES] = k0.astype(BF16)
        kd_ref[0, rs, LANES:] = k1.astype(BF16)
        vt_ref[0, :, rs] = pk[:, LANES:].T.astype(BF16)


def _premix_even_call(x, mod, gain, w_bf, cs, cos_t, sin_t, qg, kg, *, rope, tm):
    b, n, d = x.shape
    mb = mod.shape[0]
    wtot = w_bf.shape[1]
    fw = FOURIER_GROUPS * LANES
    qw = wtot - fw - 2 * LANES
    mod_map = (lambda i, j: (i, 0, 0)) if mb > 1 else (lambda i, j: (0, 0, 0))
    tok = lambda i, j: (i, j, 0)
    const2 = lambda i, j: (0, 0)
    return pl.pallas_call(
        functools.partial(_premix_even_kernel, rope=rope),
        grid=(b, n // tm),
        in_specs=[
            pl.BlockSpec((1, tm, d), tok),
            pl.BlockSpec((1, 6, d), mod_map),
            pl.BlockSpec((1, d), const2),
            pl.BlockSpec((d, wtot), const2),
            pl.BlockSpec((LANES, 2 * LANES), const2),
            pl.BlockSpec((tm, LANES), lambda i, j: (j, 0)),
            pl.BlockSpec((tm, LANES), lambda i, j: (j, 0)),
            pl.BlockSpec((1, LANES), const2),
            pl.BlockSpec((1, LANES), const2),
        ],
        out_specs=[
            pl.BlockSpec((1, tm, 2 * fw), tok),
            pl.BlockSpec((1, tm, qw), tok),
            pl.BlockSpec((1, tm, 2 * LANES), tok),
            pl.BlockSpec((1, LANES, tm), lambda i, j: (i, 0, j)),
        ],
        out_shape=[
            jax.ShapeDtypeStruct((b, n, 2 * fw), BF16),
            jax.ShapeDtypeStruct((b, n, qw), BF16),
            jax.ShapeDtypeStruct((b, n, 2 * LANES), BF16),
            jax.ShapeDtypeStruct((b, LANES, n), BF16),
        ],
        compiler_params=_cparams(2, VMEM_LIMIT),
        name="premix_even",
    )(x, mod, gain, w_bf, cs, cos_t, sin_t, qg, kg)


def _gated_moe_residual(x_ref, moe_ref, gate_row, r0, sub):
    c8 = x_ref.shape[2] // LANES
    cols = []
    for j in range(c8):
        cs = slice(LANES * j, LANES * (j + 1))
        cols.append(x_ref[0, r0:r0 + sub, cs] + gate_row[:, cs] * moe_ref[0, pl.ds(r0 * c8 + j, sub, stride=c8), :])
    return jnp.concatenate(cols, axis=1)


def _premix_odd_kernel(x_ref, moe_ref, modp_ref, mod_ref, g_ref, w_ref, qg_ref, kg_ref, q_ref, k_ref, v_ref,
                       *xo_ref):
    tm = x_ref.shape[1]
    sub = min(ROW_SUB, tm)
    m = mod_ref[0]
    gate_prev = modp_ref[0][5:6]
    lane = lax.broadcasted_iota(jnp.int32, (sub, LANES), 1)
    wq = q_ref.shape[2]
    chunk = 4 * LANES
    for r0 in range(0, tm, sub):
        rs = slice(r0, r0 + sub)
        x = _gated_moe_residual(x_ref, moe_ref, gate_prev, r0, sub)
        if xo_ref:
            xo_ref[0][0, rs] = x
        hb = _rms_mod(x, g_ref[...], m[0:1], m[1:2]).astype(BF16)
        for c0 in range(0, 3 * wq, chunk):
            p = _dot(hb, w_ref[:, c0:c0 + chunk])
            for cc in range(chunk // LANES):
                col = c0 + cc * LANES
                t = p[:, cc * LANES:(cc + 1) * LANES]
                if col < wq:
                    q_ref[0, rs, col:col + LANES] = _head_norm(t, qg_ref[...], lane).astype(BF16)
                elif col < 2 * wq:
                    k_ref[0, rs, col - wq:col - wq + LANES] = _head_norm(t, kg_ref[...], lane).astype(BF16)
                else:
                    v_ref[0, col - 2 * wq:col - 2 * wq + LANES, rs] = t.T.astype(BF16)


def _premix_odd_call(x1, moe3, mod_prev, mod, gain, w_bf, qg, kg, *, emit_x, tm):
    b, n, d = x1.shape
    mb = mod.shape[0]
    wq = w_bf.shape[1] // 3
    c8 = d // LANES
    mod_map = (lambda i, j: (i, 0, 0)) if mb > 1 else (lambda i, j: (0, 0, 0))
    tok = lambda i, j: (i, j, 0)
    const2 = lambda i, j: (0, 0)
    out_specs = [pl.BlockSpec((1, tm, wq), tok)] * 2 + [pl.BlockSpec((1, wq, tm), lambda i, j: (i, 0, j))]
    out_shape = [jax.ShapeDtypeStruct((b, n, wq), BF16)] * 2 + [jax.ShapeDtypeStruct((b, wq, n), BF16)]
    if emit_x:
        out_specs.append(pl.BlockSpec((1, tm, d), tok))
        out_shape.append(jax.ShapeDtypeStruct((b, n, d), F32))
    return pl.pallas_call(
        _premix_odd_kernel,
        grid=(b, n // tm),
        in_specs=[
            pl.BlockSpec((1, tm, d), tok),
            pl.BlockSpec((1, tm * c8, LANES), tok),
            pl.BlockSpec((1, 6, d), mod_map),
            pl.BlockSpec((1, 6, d), mod_map),
            pl.BlockSpec((1, d), const2),
            pl.BlockSpec((d, 3 * wq), const2),
            pl.BlockSpec((1, LANES), const2),
            pl.BlockSpec((1, LANES), const2),
        ],
        out_specs=out_specs,
        out_shape=out_shape,
        compiler_params=_cparams(2, VMEM_LIMIT),
        name="premix_odd",
    )(x1, moe3, mod_prev, mod, gain, w_bf, qg, kg)


def _col_reduce(x, op, slab=64):
    r = x.shape[0]
    if r > slab and r % slab == 0:
        x = op(x.reshape(r // slab, slab, x.shape[1]), axis=0)
    return op(x, axis=0, keepdims=True)


def _softmax_pv_t(s_t, v_t, sink=None):
    mx = _col_reduce(s_t, jnp.max)
    if sink is not None:
        mx = jnp.maximum(mx, sink)
    p = jnp.exp2((s_t - mx).astype(BF16))
    nd = v_t.shape[0]
    o = _dot(jnp.concatenate([v_t, jnp.ones((ONES_ROWS, v_t.shape[1]), BF16)], axis=0), p)
    den = o[nd:nd + 1]
    if sink is not None:
        den = den + jnp.exp2(sink - mx)
    return o[:nd] / den


def _mask_half(qc, lane, half):
    q32 = qc.astype(F32) * (HEAD_DIM ** -0.5 * LOG2E)
    keep = (lane < HEAD_DIM) if half == 0 else (lane >= HEAD_DIM)
    return jnp.where(keep, q32, 0.0).astype(BF16)


def _gqa_scores(q_ref, h, kd, n_kv, masks, lane, r0, tq):
    rs = slice(r0, r0 + tq)
    cols_per_kv = q_ref.shape[2] // LANES // n_kv
    q_rows = []
    for cc in range(cols_per_kv):
        c = cols_per_kv * h + cc
        for half in range(2):
            q_rows.append(_mask_half(q_ref[0, rs, LANES * c:LANES * (c + 1)], lane, half))
    qm = jnp.concatenate(q_rows, axis=0)
    s = _dot_t(kd, qm)
    pieces, k0 = [], 0
    for rows, mask in masks:
        blk = s[k0:k0 + rows]
        pieces.append(blk if mask is None else jnp.where(mask, blk, NEG))
        k0 += rows
    return jnp.concatenate(pieces, axis=0) if len(pieces) > 1 else pieces[0]


def _gqa_finish(sink_ref, o_ref, h, s_t, v_t, r0, tq):
    rs = slice(r0, r0 + tq)
    n_kv = v_t.shape[0] // HEAD_DIM
    cols_per_kv = o_ref.shape[2] // LANES // n_kv
    heads = 2 * cols_per_kv
    head_of_lane = lax.broadcasted_iota(jnp.int32, (1, heads * tq), 1) // tq
    sink = jnp.zeros((1, heads * tq), F32)
    for a in range(heads):
        sink = jnp.where(head_of_lane == a, sink_ref[heads * h + a] * LOG2E, sink)
    o_t = _softmax_pv_t(s_t, v_t, sink)[HEAD_DIM * h:HEAD_DIM * (h + 1)]
    for cc in range(cols_per_kv):
        c = cols_per_kv * h + cc
        tile = jnp.concatenate([o_t[:, tq * (2 * cc):tq * (2 * cc + 1)], o_t[:, tq * (2 * cc + 1):tq * (2 * cc + 2)]],
                               axis=0)
        o_ref[0, rs, LANES * c:LANES * (c + 1)] = tile.T.astype(BF16)


def _swa_kernel(sink_ref, q_ref, km_ref, k0_ref, kp_ref, vm_ref, v0_ref, vp_ref, kc_ref, vc_ref, o_ref, *, n_pair):
    g = pl.program_id(1)
    tq = SWA_BLOCK
    nc = kc_ref.shape[1]
    n_kv = kc_ref.shape[2] // LANES
    heads = q_ref.shape[2] // HEAD_DIM // n_kv
    lane = lax.broadcasted_iota(jnp.int32, (tq, LANES), 1)
    jk = lax.broadcasted_iota(jnp.int32, (SWA_BLOCK, heads * tq), 0)
    iq = lax.broadcasted_iota(jnp.int32, (SWA_BLOCK, heads * tq), 1) % tq
    below, above = jk >= iq, jk <= iq
    prev_masks = [jnp.logical_and(below, g > 0), below]
    next_masks = [above, jnp.logical_and(above, g < n_pair - 1)]
    lo, hi = slice(0, SWA_BLOCK), slice(SWA_BLOCK, 2 * SWA_BLOCK)
    groups = []
    for sub in range(2):
        masks = [(nc, None), (SWA_BLOCK, prev_masks[sub]), (SWA_BLOCK, None), (SWA_BLOCK, next_masks[sub])]
        for h in range(n_kv):
            hs = slice(LANES * h, LANES * (h + 1))
            if sub == 0:
                k_loc = [km_ref[0, :, hs], k0_ref[0, lo, hs], k0_ref[0, hi, hs]]
            else:
                k_loc = [k0_ref[0, lo, hs], k0_ref[0, hi, hs], kp_ref[0, :, hs]]
            kd = jnp.concatenate([kc_ref[0, :, hs]] + k_loc, axis=0)
            groups.append((sub, h, _gqa_scores(q_ref, h, kd, n_kv, masks, lane, SWA_BLOCK * sub, tq)))
    for sub, h, s_t in groups:
        if sub == 0:
            v_loc = [vm_ref[0], v0_ref[0, :, lo], v0_ref[0, :, hi]]
        else:
            v_loc = [v0_ref[0, :, lo], v0_ref[0, :, hi], vp_ref[0]]
        v_t = jnp.concatenate([vc_ref[0]] + v_loc, axis=1)
        _gqa_finish(sink_ref, o_ref, h, s_t, v_t, SWA_BLOCK * sub, tq)


def _swa_call(sink, q, kd, vt, kdc, vtc):
    b, n, qw = q.shape
    nc = kdc.shape[1]
    kw = kd.shape[2]
    vw = vt.shape[1]
    n_blk = n // SWA_BLOCK
    n_pair = n_blk // 2
    prev = lambda j: jnp.maximum(2 * j - 1, 0)
    nxt = lambda j: jnp.minimum(2 * j + 2, n_blk - 1)
    kspec = lambda f: pl.BlockSpec((1, SWA_BLOCK, kw), lambda i, j: (i, f(j), 0))
    vspec = lambda f: pl.BlockSpec((1, vw, SWA_BLOCK), lambda i, j: (i, 0, f(j)))
    return pl.pallas_call(
        functools.partial(_swa_kernel, n_pair=n_pair),
        grid=(b, n_pair),
        in_specs=[
            pl.BlockSpec(memory_space=pltpu.SMEM),
            pl.BlockSpec((1, 2 * SWA_BLOCK, qw), lambda i, j: (i, j, 0)),
            kspec(prev), pl.BlockSpec((1, 2 * SWA_BLOCK, kw), lambda i, j: (i, j, 0)), kspec(nxt),
            vspec(prev), pl.BlockSpec((1, vw, 2 * SWA_BLOCK), lambda i, j: (i, 0, j)), vspec(nxt),
            pl.BlockSpec((1, nc, kw), lambda i, j: (i, 0, 0)),
            pl.BlockSpec((1, vw, nc), lambda i, j: (i, 0, 0)),
        ],
        out_specs=pl.BlockSpec((1, 2 * SWA_BLOCK, qw), lambda i, j: (i, j, 0)),
        out_shape=jax.ShapeDtypeStruct((b, n, qw), BF16),
        compiler_params=_cparams(2, VMEM_LIMIT),
        name="swa",
    )(sink, q, kd, kd, kd, vt, vt, vt, kdc, vtc)


def _ctx_attn_kernel(sink_ref, q_ref, kc_ref, vc_ref, o_ref):
    tq = q_ref.shape[1]
    lane = lax.broadcasted_iota(jnp.int32, (tq, LANES), 1)
    n_kv = kc_ref.shape[2] // LANES
    scores = [_gqa_scores(q_ref, h, kc_ref[0, :, LANES * h:LANES * (h + 1)], n_kv, [(kc_ref.shape[1], None)], lane,
                          0, tq) for h in range(n_kv)]
    for h, s_t in enumerate(scores):
        _gqa_finish(sink_ref, o_ref, h, s_t, vc_ref[0], 0, tq)


def _ctx_attn_call(sink, q, kd, vt):
    b, nc, qw = q.shape
    kw = kd.shape[2]
    vw = vt.shape[1]
    m = lambda i: (i, 0, 0)
    return pl.pallas_call(
        _ctx_attn_kernel,
        grid=(b,),
        in_specs=[
            pl.BlockSpec(memory_space=pltpu.SMEM),
            pl.BlockSpec((1, nc, qw), m),
            pl.BlockSpec((1, nc, kw), m),
            pl.BlockSpec((1, vw, nc), m),
        ],
        out_specs=pl.BlockSpec((1, nc, qw), m),
        out_shape=jax.ShapeDtypeStruct((b, nc, qw), BF16),
        compiler_params=_cparams(1, VMEM_LIMIT),
        name="ctx_attn",
    )(sink, q, kd, vt)


def _na_kernel(q_ref, k0_ref, k1_ref, k2_ref, v0_ref, v1_ref, v2_ref, kc_ref, vc_ref, tb_ref, o_ref, *, n_j, rows):
    j = pl.program_id(1)
    tq = q_ref.shape[1]
    tk = 3 * k0_ref.shape[1]
    nc = kc_ref.shape[1]
    w0 = NA_QROWS * jnp.clip(j - 1, 0, n_j - 3)
    lane = lax.broadcasted_iota(jnp.int32, (tq, LANES), 1)
    k_row = (lax.broadcasted_iota(jnp.int32, (nc + tk, LANES), 0) - nc) // GRID_W
    k_lane = lax.broadcasted_iota(jnp.int32, (nc + tk, LANES), 1)
    lo = jnp.zeros((nc + tk, LANES), jnp.int32)
    for rr in range(NA_QROWS):
        r = NA_QROWS * j + rr
        lo_rr = jnp.clip(r - NA_ROWS_MAX // 2, 0, rows - NA_ROWS_MAX) - w0
        lo = jnp.where(k_lane == rr, lo_rr, lo)
    outside = jnp.logical_or(k_row < lo, k_row >= lo + NA_ROWS_MAX)
    is_local = lax.broadcasted_iota(jnp.int32, (nc + tk, LANES), 0) >= nc
    k_extra = jnp.where(jnp.logical_and(jnp.logical_and(is_local, k_lane < NA_QROWS), outside), NEG, 0.0).astype(BF16)
    q_grid_row = (lax.broadcasted_iota(jnp.int32, (2 * tq, LANES), 0) % tq) // GRID_W
    q_extra = jnp.where(lax.broadcasted_iota(jnp.int32, (2 * tq, LANES), 1) == q_grid_row, 1.0, 0.0).astype(BF16)
    delta = w0 - NA_QROWS * j
    n_dr = 2 * NA_ROWS_MAX - 1
    def scores(c):
        cs = slice(LANES * c, LANES * (c + 1))
        qc = q_ref[0, :, cs]
        kl = jnp.concatenate([kc_ref[0, :, cs], k0_ref[0, :, cs], k1_ref[0, :, cs], k2_ref[0, :, cs]], axis=0)
        qm = jnp.concatenate([_mask_half(qc, lane, 0), _mask_half(qc, lane, 1)], axis=0)
        return _dot_t(jnp.concatenate([kl, k_extra], axis=1), jnp.concatenate([qm, q_extra], axis=1))

    n_cols = q_ref.shape[2] // LANES
    ahead = 1
    pending = [scores(c) for c in range(min(ahead, n_cols))]
    for c in range(n_cols):
        cs = slice(LANES * c, LANES * (c + 1))
        s = pending.pop(0)
        if c + ahead < n_cols:
            pending.append(scores(c + ahead))
        v_t = jnp.concatenate([vc_ref[0, cs, :], v0_ref[0, cs, :], v1_ref[0, cs, :], v2_ref[0, cs, :]], axis=1)
        bias_rows = []
        for kr in range(NA_KROWS):
            pieces = []
            for half in range(2):
                for t in range(NA_QROWS // 2):
                    dr = delta + kr - 2 * t + NA_ROWS_MAX - 1
                    pieces.append(tb_ref[2 * c + half, jnp.clip(dr, 0, n_dr)])
            bias_rows.append(jnp.concatenate(pieces, axis=1))
        bias = jnp.concatenate(bias_rows, axis=0)
        s_t = jnp.concatenate([s[:nc], s[nc:] + bias], axis=0)
        o_t = _softmax_pv_t(s_t, v_t)
        tile = jnp.concatenate([o_t[:HEAD_DIM, :tq], o_t[HEAD_DIM:, tq:]], axis=0)
        o_ref[0, :, cs] = tile.T.astype(BF16)


def _na_call(q, k, vt, kc, vtc, tb):
    b, n, w = q.shape
    nc = kc.shape[1]
    rows = n // GRID_W
    n_j = rows // NA_QROWS
    tq = NA_QROWS * GRID_W
    cur = lambda i, j: (i, j, 0)
    first = lambda j: jnp.clip(j - 1, 0, n_j - 3)
    kspec = lambda d: pl.BlockSpec((1, tq, w), lambda i, j: (i, first(j) + d, 0))
    vspec = lambda d: pl.BlockSpec((1, w, tq), lambda i, j: (i, 0, first(j) + d))
    return pl.pallas_call(
        functools.partial(_na_kernel, n_j=n_j, rows=rows),
        grid=(b, n_j),
        in_specs=[
            pl.BlockSpec((1, tq, w), cur),
            kspec(0), kspec(1), kspec(2),
            vspec(0), vspec(1), vspec(2),
            pl.BlockSpec((1, nc, w), lambda i, j: (i, 0, 0)),
            pl.BlockSpec((1, w, nc), lambda i, j: (i, 0, 0)),
            pl.BlockSpec(tb.shape, lambda i, j: (0, 0, 0, 0)),
        ],
        out_specs=pl.BlockSpec((1, tq, w), cur),
        out_shape=jax.ShapeDtypeStruct((b, n, w), BF16),
        compiler_params=_cparams(2, VMEM_LIMIT),
        name="na",
    )(q, k, k, k, vt, vt, vt, kc, vtc, tb)


def _na_bias_table(rpb):
    col_q = np.arange(GRID_W)[None, :]
    col_k = np.arange(GRID_W)[:, None]
    c_start = np.clip(col_q - NA_COLS // 2, 0, GRID_W - NA_COLS)
    col_valid = (col_k >= c_start) & (col_k < c_start + NA_COLS)
    dc_idx = np.clip(col_k - col_q + NA_COLS - 1, 0, 2 * NA_COLS - 2)
    t = jnp.where(col_valid[None, None], (rpb.astype(F32) * LOG2E)[:, :, dc_idx], NEG)
    pad = jnp.full_like(t[:, :1], NEG)
    t = jnp.concatenate([pad, t, pad], axis=1)
    return jnp.concatenate([t[:, 1:], t[:, :-1]], axis=-1)


def _fourier_kernel(ab_ref, ca_ref, sa_ref, cb_ref, sb_ref, o_ref, cm_ref, sm_ref):
    @pl.when(pl.program_id(1) == 0)
    def _():
        cb = cb_ref[...]
        sb = sb_ref[...]
        for a in range(ca_ref.shape[0]):
            ca = ca_ref[a:a + 1, :]
            sa = sa_ref[a:a + 1, :]
            rs = slice(DFT_SUB * a, DFT_SUB * (a + 1))
            cm_ref[rs, :] = (ca * cb - sa * sb).astype(BF16)
            sm_ref[rs, :] = (sa * cb + ca * sb).astype(BF16)

    fw = o_ref.shape[2]
    y = _dot(cm_ref[...], ab_ref[0, :, :fw]) - _dot(sm_ref[...], ab_ref[0, :, fw:])
    o_ref[0] = y.astype(BF16)


def _fourier_call(ab, ca, sa, cb, sb, *, tm):
    b, n, w2 = ab.shape
    fw = w2 // 2
    na = tm // DFT_SUB
    return pl.pallas_call(
        _fourier_kernel,
        grid=(n // tm, b),
        in_specs=[
            pl.BlockSpec((1, n, w2), lambda i, j: (j, 0, 0)),
            pl.BlockSpec((na, n), lambda i, j: (i, 0)),
            pl.BlockSpec((na, n), lambda i, j: (i, 0)),
            pl.BlockSpec((DFT_SUB, n), lambda i, j: (0, 0)),
            pl.BlockSpec((DFT_SUB, n), lambda i, j: (0, 0)),
        ],
        out_specs=pl.BlockSpec((1, tm, fw), lambda i, j: (j, i, 0)),
        out_shape=jax.ShapeDtypeStruct((b, n, fw), BF16),
        scratch_shapes=[pltpu.VMEM((tm, n), BF16), pltpu.VMEM((tm, n), BF16)],
        compiler_params=_cparams(2, VMEM_LIMIT),
        name="fourier",
    )(ab, ca, sa, cb, sb)


def _dft_tables(n):
    k = jnp.arange(n, dtype=jnp.int32)
    a = jnp.arange(n // DFT_SUB, dtype=jnp.int32)
    bb = jnp.arange(DFT_SUB, dtype=jnp.int32)
    ang_a = ((DFT_SUB * a[:, None] * k[None, :]) % n).astype(F32) * (2.0 * math.pi / n)
    ang_b = ((bb[:, None] * k[None, :]) % n).astype(F32) * (2.0 * math.pi / n)
    s = float(n) ** -0.5
    return jnp.cos(ang_a), jnp.sin(ang_a), jnp.cos(ang_b) * s, jnp.sin(ang_b) * s


def _channel_dft():
    c = jnp.arange(LANES, dtype=jnp.int32)
    ang = ((c[:, None] * c[None, :]) % LANES).astype(F32) * (2.0 * math.pi / LANES)
    s = float(LANES) ** -0.5
    return jnp.concatenate([jnp.cos(ang) * s, jnp.sin(ang) * s], axis=1)


def _postmix_kernel(*refs, n_in):
    a_refs = refs[:n_in]
    w_refs = refs[n_in:2 * n_in]
    x_ref, mod_ref, g_ref, rw_ref, x1_ref, h3_ref, aff_ref = refs[2 * n_in:]
    tm, d = x_ref.shape[1], x_ref.shape[2]
    sub = min(ROW_SUB, tm)
    p = d // 2 // LANES
    m = mod_ref[0]
    for r0 in range(0, tm, sub):
        rs = slice(r0, r0 + sub)
        y = None
        for a_ref, w_ref in zip(a_refs, w_refs):
            t = _dot(a_ref[0, rs], w_ref[...])
            y = t if y is None else y + t
        x1 = x_ref[0, rs] + m[2:3] * y
        x1_ref[0, rs] = x1
        h2 = _rms_mod(x1, g_ref[...], m[3:4], m[4:5])
        hh, hl = _split(h2)
        parts = _dot_t(rw_ref[...], jnp.concatenate([hh, hl], axis=1))
        lt = parts[:N_EXPERTS] + parts[N_EXPERTS:2 * N_EXPERTS]
        ex = jnp.exp(lt - jnp.max(lt, axis=0, keepdims=True))
        aff_ref[0, :, rs] = ex / jnp.sum(ex, axis=0, keepdims=True)
        bits = pltpu.bitcast(hh.astype(F32), jnp.uint32)
        packed = (bits[:, d // 2:] & jnp.uint32(0xFFFF0000)) | (bits[:, :d // 2] >> 16)
        for jj in range(p):
            h3_ref[0, pl.ds(r0 * p + jj, sub, stride=p), :] = packed[:, LANES * jj:LANES * (jj + 1)]


def _postmix_call(a_list, w_list, x, mod, gain, rw, *, tm):
    b, n, d = x.shape
    mb = mod.shape[0]
    n_in = len(a_list)
    p = d // 2 // LANES
    mod_map = (lambda i, j: (i, 0, 0)) if mb > 1 else (lambda i, j: (0, 0, 0))
    tok = lambda i, j: (i, j, 0)
    const2 = lambda i, j: (0, 0)
    in_specs = [pl.BlockSpec((1, tm, a.shape[2]), tok) for a in a_list]
    in_specs += [pl.BlockSpec(w.shape, const2) for w in w_list]
    in_specs += [
        pl.BlockSpec((1, tm, d), tok),
        pl.BlockSpec((1, 6, d), mod_map),
        pl.BlockSpec((1, d), const2),
        pl.BlockSpec((LANES, 2 * d), const2),
    ]
    return pl.pallas_call(
        functools.partial(_postmix_kernel, n_in=n_in),
        grid=(b, n // tm),
        in_specs=in_specs,
        out_specs=[
            pl.BlockSpec((1, tm, d), tok),
            pl.BlockSpec((1, tm * p, LANES), tok),
            pl.BlockSpec((1, N_EXPERTS, tm), lambda i, j: (i, 0, j)),
        ],
        out_shape=[
            jax.ShapeDtypeStruct((b, n, d), F32),
            jax.ShapeDtypeStruct((b, n * p, LANES), jnp.uint32),
            jax.ShapeDtypeStruct((b, N_EXPERTS, n), F32),
        ],
        compiler_params=_cparams(2, VMEM_LIMIT),
        name="postmix",
    )(*a_list, *w_list, x, mod, gain, rw)


def _prefix_incl(ones, tri):
    carry = jnp.zeros((ones.shape[0], 1), F32)
    outs = []
    for c in range(ones.shape[1] // LANES):
        blk = ones[:, LANES * c:LANES * (c + 1)]
        outs.append(_dot(blk.astype(BF16), tri) + carry)
        carry = carry + jnp.sum(blk, axis=1, keepdims=True)
    return jnp.concatenate(outs, axis=1)


def _topk_kernel(aff_ref, tri_ref, idx_ref, gate_ref, *, cap):
    n = aff_ref.shape[2]
    aff = aff_ref[0]
    bits = pltpu.bitcast(aff, jnp.int32)
    capf = jnp.float32(cap)

    def body(it, lo):
        t = lo | jnp.left_shift(jnp.int32(1), 30 - it)
        cnt = jnp.sum(jnp.where(bits >= t, 1.0, 0.0), axis=1, keepdims=True)
        return jnp.where(cnt >= capf, t, lo)

    thr = lax.fori_loop(0, 31, body, jnp.zeros((N_EXPERTS, 1), jnp.int32))
    gt = jnp.where(bits > thr, 1.0, 0.0)
    eq = jnp.where(bits == thr, 1.0, 0.0)
    need = capf - jnp.sum(gt, axis=1, keepdims=True)
    tri = tri_ref[...]
    eq_before = _prefix_incl(eq, tri) - eq
    sel = gt + eq * jnp.where(eq_before < need, 1.0, 0.0)
    slot = _prefix_incl(sel, tri).astype(jnp.int32) - 1
    tok = lax.broadcasted_iota(jnp.int32, (N_EXPERTS, n), 1)
    disp = jnp.where(sel > 0.0, tok - slot, -1)
    gate = aff
    for bit in range((n - 1).bit_length()):
        step = 1 << bit
        disp_in = pltpu.roll(disp, n - step, 1)
        gate_in = pltpu.roll(gate, n - step, 1)
        move_in = jnp.logical_and(disp_in >= 0, ((disp_in >> bit) & 1) == 1)
        stay = jnp.logical_and(disp >= 0, ((disp >> bit) & 1) == 0)
        gate = jnp.where(move_in, gate_in, gate)
        disp = jnp.where(move_in, disp_in, jnp.where(stay, disp, -1))
    idx_ref[0] = (tok + disp)[:, :cap]
    gate_ref[0] = gate[:, :cap]


def _topk_call(aff_t, tri, *, cap):
    b, e, n = aff_t.shape
    return pl.pallas_call(
        functools.partial(_topk_kernel, cap=cap),
        grid=(b,),
        in_specs=[
            pl.BlockSpec((1, e, n), lambda i: (i, 0, 0)),
            pl.BlockSpec((LANES, LANES), lambda i: (0, 0)),
        ],
        out_specs=[pl.BlockSpec((1, e, cap), lambda i: (i, 0, 0))] * 2,
        out_shape=[jax.ShapeDtypeStruct((b, e, cap), jnp.int32), jax.ShapeDtypeStruct((b, e, cap), F32)],
        compiler_params=_cparams(1, VMEM_LIMIT),
        name="topk",
    )(aff_t, tri)


def _moe_kernel(idx_ref, h3_ref, gate_ref, wg_ref, wu_ref, wd_ref, o_ref, xs3, y3, *, cap, p, c8, sub):
    e = pl.program_id(1)

    @pl.when(e == 0)
    def _():
        o_ref[...] = jnp.zeros(o_ref.shape, o_ref.dtype)

    def gather(k):
        for r in range(sub):
            s = k * sub + r
            t = idx_ref[0, 0, s]
            xs3[pl.ds(s * p, p), :] = h3_ref[0, pl.ds(pl.multiple_of(t * p, p), p), :]

    def expert(k):
        base = k * sub
        lo, hi = [], []
        for j in range(p):
            w = xs3[pl.ds(base * p + j, sub, stride=p), :]
            lo.append(pltpu.bitcast(w << 16, F32).astype(BF16))
            hi.append(pltpu.bitcast(w & jnp.uint32(0xFFFF0000), F32).astype(BF16))
        xs = jnp.concatenate(lo + hi, axis=1)
        a = _dot(xs, wg_ref[0])
        u_ = _dot(xs, wu_ref[0])
        act = (_silu(a) * u_).astype(BF16)
        y = _dot(act, wd_ref[0])
        for j in range(c8):
            y3[pl.ds(base * c8 + j, sub, stride=c8), :] = y[:, LANES * j:LANES * (j + 1)]

    group = 4

    def scatter(k):
        for g0 in range(0, sub, group):
            sums = []
            for u in range(group):
                s = k * sub + g0 + u
                t = idx_ref[0, 0, s]
                rows = pl.ds(pl.multiple_of(t * c8, c8), c8)
                sums.append((rows, o_ref[0, rows, :] + y3[pl.ds(s * c8, c8), :] * gate_ref[0, 0, s]))
            for rows, val in sums:
                o_ref[0, rows, :] = val

    n_sub = cap // sub
    gather(0)
    for k in range(n_sub):
        if k + 1 < n_sub:
            gather(k + 1)
        expert(k)
        if k > 0:
            scatter(k - 1)
    scatter(n_sub - 1)


def _moe_call(idx, h3, gate, wg, wu, wd):
    bm, ne, cap = idx.shape
    d = wg.shape[1]
    p = d // 2 // LANES
    c8 = d // LANES
    n = h3.shape[1] // p
    idx3 = idx.reshape(bm * ne, 1, cap)
    gate3 = gate.reshape(bm * ne, 1, cap)
    wmap = lambda i, j: (j, 0, 0)
    return pl.pallas_call(
        functools.partial(_moe_kernel, cap=cap, p=p, c8=c8, sub=min(MOE_SUB, cap)),
        grid=(bm, ne),
        in_specs=[
            pl.BlockSpec((1, 1, cap), lambda i, j: (i * ne + j, 0, 0), memory_space=pltpu.SMEM),
            pl.BlockSpec((1, n * p, LANES), lambda i, j: (i, 0, 0), pipeline_mode=pl.Buffered(1)),
            pl.BlockSpec((1, 1, cap), lambda i, j: (i * ne + j, 0, 0), memory_space=pltpu.SMEM),
            pl.BlockSpec((1, d, wg.shape[2]), wmap),
            pl.BlockSpec((1, d, wu.shape[2]), wmap),
            pl.BlockSpec((1, wd.shape[1], d), wmap),
        ],
        out_specs=pl.BlockSpec((1, n * c8, LANES), lambda i, j: (i, 0, 0)),
        out_shape=jax.ShapeDtypeStruct((bm, n * c8, LANES), F32),
        scratch_shapes=[pltpu.VMEM((cap * p, LANES), jnp.uint32), pltpu.VMEM((cap * c8, LANES), F32)],
        compiler_params=_cparams(2, VMEM_LIMIT),
        name="moe",
    )(idx3, h3, gate3, wg, wu, wd)


def _resid_kernel(x_ref, moe_ref, mod_ref, o_ref):
    o_ref[0] = _gated_moe_residual(x_ref, moe_ref, mod_ref[0][5:6], 0, x_ref.shape[1])


def _resid_call(x1, moe3, mod, *, tm):
    b, n, d = x1.shape
    mb = mod.shape[0]
    c8 = d // LANES
    mod_map = (lambda i, j: (i, 0, 0)) if mb > 1 else (lambda i, j: (0, 0, 0))
    tok = lambda i, j: (i, j, 0)
    return pl.pallas_call(
        _resid_kernel,
        grid=(b, n // tm),
        in_specs=[
            pl.BlockSpec((1, tm, d), tok),
            pl.BlockSpec((1, tm * c8, LANES), tok),
            pl.BlockSpec((1, 6, d), mod_map),
        ],
        out_specs=pl.BlockSpec((1, tm, d), tok),
        out_shape=jax.ShapeDtypeStruct((b, n, d), F32),
        compiler_params=_cparams(2, VMEM_LIMIT),
        name="resid",
    )(x1, moe3, mod)


def _rope_tables(n):
    t = jnp.arange(n)
    row = (t // GRID_W).astype(F32)
    col = (t % GRID_W).astype(F32)
    n_freq = HEAD_DIM // 4
    inv_freq = jnp.power(ROPE_BASE, -jnp.arange(n_freq, dtype=F32) / n_freq)
    ang = jnp.concatenate([row[:, None] * inv_freq, col[:, None] * inv_freq], axis=-1)
    cos, sin = jnp.cos(ang), jnp.sin(ang)
    cos_t = jnp.concatenate([cos, cos, cos, cos], axis=-1)
    sin_t = jnp.concatenate([-sin, sin, -sin, sin], axis=-1)
    return cos_t, sin_t


def _gain128(g):
    return jnp.concatenate([g, g]).reshape(1, LANES).astype(F32)


def _channel_mix(h_pack, aff_t, tri, weights, *, flatten):
    wg, wu, wd = weights
    b, _, n = aff_t.shape
    cap = EC_CAPACITY_FACTOR * n // N_EXPERTS
    idx, gate = _topk_call(aff_t, tri, cap=cap)
    if flatten:
        idx = (idx + (jnp.arange(b, dtype=jnp.int32) * n)[:, None, None]).transpose(1, 0, 2).reshape(1, N_EXPERTS, b * cap)
        gate = gate.transpose(1, 0, 2).reshape(1, N_EXPERTS, b * cap)
        h_pack = h_pack.reshape(1, -1, LANES)
    return _moe_call(idx, h_pack, gate, wg, wu, wd).reshape(b, -1, LANES)


def kernel(x, c, ctx, c_ctx, ada_w, ada_b, norm1_g, norm2_g, router_w, exp_w_gate, exp_w_up, exp_w_down, ev_w_in,
           ev_w_out, ev_q_gain, ev_k_gain, ev_sink, od_w_in, od_w_out, od_q_gain, od_k_gain, od_rpb):
    b, n, d = x.shape
    nc = ctx.shape[1]
    tm = min(512, n)
    tmc = min(512, nc)

    rows = -(-(b + 1) // 8) * 8
    cs = jnp.concatenate([c, c_ctx[None], jnp.zeros((rows - b - 1, d), F32)], axis=0)
    mods = _mod_call(cs, ada_w, ada_b).reshape(ada_w.shape[0], rows, 6, d)

    cos_t, sin_t = _rope_tables(n)
    cs_dft = _channel_dft()
    tri = jnp.asarray(np.triu(np.ones((LANES, LANES), np.float32)), BF16)
    fw = FOURIER_GROUPS * LANES

    def router_split(l):
        rw = router_w[l].T
        rh = rw.astype(BF16)
        rl = (rw - rh.astype(F32)).astype(BF16)
        left = jnp.pad(jnp.concatenate([rh, rl], axis=0), ((0, LANES - 2 * N_EXPERTS), (0, 0)))
        right = jnp.pad(rh, ((0, LANES - N_EXPERTS), (0, 0)))
        return jnp.concatenate([left, right], axis=1)

    def experts(l):
        return exp_w_gate[l].astype(BF16), exp_w_up[l].astype(BF16), exp_w_down[l].astype(BF16)

    mod_x, mod_c = mods[0, :b], mods[0, b:b + 1]
    g1 = norm1_g[0].reshape(1, d)
    g2 = norm2_g[0].reshape(1, d)
    w_in = ev_w_in[0].astype(BF16)
    w_out = ev_w_out[0].astype(BF16)
    qg, kg = _gain128(ev_q_gain[0]), _gain128(ev_k_gain[0])
    sink = ev_sink[0].astype(F32)
    rw = router_split(0)
    wts = experts(0)

    ab_x, q_x, kd_x, vd_x = _premix_even_call(x, mod_x, g1, w_in, cs_dft, cos_t, sin_t, qg, kg, rope=True, tm=tm)
    ab_c, q_c, kd_c, vd_c = _premix_even_call(ctx, mod_c, g1, w_in, cs_dft, cos_t[:nc], sin_t[:nc], qg, kg,
                                              rope=False, tm=tmc)
    a_x = _swa_call(sink, q_x, kd_x, vd_x, kd_c, vd_c)
    a_c = _ctx_attn_call(sink, q_c, kd_c, vd_c)
    four_x = _fourier_call(ab_x, *_dft_tables(n), tm=tm)
    four_c = _fourier_call(ab_c, *_dft_tables(nc), tm=tmc)
    w_out_parts = [w_out[:fw], w_out[fw:]]
    x1, h3_x, aff_x = _postmix_call([four_x, a_x], w_out_parts, x, mod_x, g2, rw, tm=tm)
    c1, h3_c, aff_c = _postmix_call([four_c, a_c], w_out_parts, ctx, mod_c, g2, rw, tm=tmc)
    moe_x = _channel_mix(h3_x, aff_x, tri, wts, flatten=False)
    moe_c = _channel_mix(h3_c, aff_c, tri, wts, flatten=True)

    mod_x0, mod_c0 = mod_x, mod_c
    mod_x, mod_c = mods[1, :b], mods[1, b:b + 1]
    g1 = norm1_g[1].reshape(1, d)
    g2 = norm2_g[1].reshape(1, d)
    w_in = od_w_in[0].astype(BF16)
    w_out = od_w_out[0].astype(BF16)
    qg, kg = _gain128(od_q_gain[0]), _gain128(od_k_gain[0])
    rw = router_split(1)
    wts = experts(1)

    q_x, k_x, v_x, x = _premix_odd_call(x1, moe_x, mod_x0, mod_x, g1, w_in, qg, kg, emit_x=True, tm=tm)
    _, k_c, v_c = _premix_odd_call(c1, moe_c, mod_c0, mod_c, g1, w_in, qg, kg, emit_x=False, tm=tmc)
    a_x = _na_call(q_x, k_x, v_x, k_c, v_c, _na_bias_table(od_rpb[0]))
    x1, h3_x, aff_x = _postmix_call([a_x], [w_out], x, mod_x, g2, rw, tm=tm)
    return _resid_call(x1, _channel_mix(h3_x, aff_x, tri, wts, flatten=False), mod_x, tm=tm)
```

```python
import functools
import math

import numpy as np
import jax
import jax.numpy as jnp
from jax import lax
from jax.experimental import pallas as pl
from jax.experimental.pallas import tpu as pltpu

GRID_W = 64
HEAD_DIM = 64
FOURIER_GROUPS = 4
SWA_WINDOW = 128
SWA_BLOCK = 128
NA_ROWS_MAX = 8
NA_COLS = 16
N_EXPERTS = 16
EC_CAPACITY_FACTOR = 2
ROPE_BASE = 10000.0
EPS = 1e-6

LANES = 128
NEG = -1e30
NA_QROWS = 4
NA_KROWS = 12
DFT_SUB = 64
MOE_SUB = 128
ROW_SUB = 256
ONES_ROWS = 16
LOG2E = math.log2(math.e)
VMEM_LIMIT = 60 * 1024 * 1024

F32 = jnp.float32
BF16 = jnp.bfloat16


def _cparams(n_axes, vmem=None):
    return pltpu.CompilerParams(dimension_semantics=("arbitrary",) * n_axes, vmem_limit_bytes=vmem)


def _dot(a, b):
    return jnp.dot(a, b, preferred_element_type=F32)


def _dot_t(a, b):
    return lax.dot_general(a, b, (((1,), (1,)), ((), ())), preferred_element_type=F32)


def _split(a):
    hi = a.astype(BF16)
    lo = (a - hi.astype(F32)).astype(BF16)
    return hi, lo


def _dot3(a, b):
    ah, al = _split(a)
    bh, bl = _split(b)
    return _dot(ah, bh) + _dot(al, bh) + _dot(ah, bl)


def _silu(a):
    return a / (1.0 + jnp.exp(-a))


def _rms_mod(x, gain, shift, scale):
    ms = jnp.mean(x * x, axis=-1, keepdims=True)
    y = x * lax.rsqrt(ms + EPS) * gain
    return y * (1.0 + scale) + shift


def _mod_kernel(cs_ref, w_ref, b_ref, o_ref):
    o_ref[0] = _dot3(_silu(cs_ref[...]), w_ref[0]) + b_ref[0]


def _mod_call(cs, ada_w, ada_b):
    depth, d, d6 = ada_w.shape
    r = cs.shape[0]
    tn = 1024
    return pl.pallas_call(
        _mod_kernel,
        grid=(depth, d6 // tn),
        in_specs=[
            pl.BlockSpec((r, d), lambda l, j: (0, 0)),
            pl.BlockSpec((1, d, tn), lambda l, j: (l, 0, j)),
            pl.BlockSpec((1, 1, tn), lambda l, j: (l, 0, j)),
        ],
        out_specs=pl.BlockSpec((1, r, tn), lambda l, j: (l, 0, j)),
        out_shape=jax.ShapeDtypeStruct((depth, r, d6), F32),
        compiler_params=_cparams(2),
        name="mod",
    )(cs, ada_w, ada_b.reshape(depth, 1, d6))


def _head_norm(t, gain, lane):
    t2 = t * t
    s_lo = jnp.sum(jnp.where(lane < HEAD_DIM, t2, 0.0), axis=-1, keepdims=True)
    s_all = jnp.sum(t2, axis=-1, keepdims=True)
    ms = jnp.where(lane < HEAD_DIM, s_lo, s_all - s_lo) * (1.0 / HEAD_DIM)
    return t * lax.rsqrt(ms + EPS) * gain


def _rope(t, cos_t, sin_t, lane):
    partner = jnp.where((lane % HEAD_DIM) < HEAD_DIM // 2, pltpu.roll(t, LANES - HEAD_DIM // 2, 1),
                        pltpu.roll(t, HEAD_DIM // 2, 1))
    return t * cos_t + partner * sin_t


def _dup_halves(t, lane):
    sw = pltpu.roll(t, HEAD_DIM, 1)
    return jnp.where(lane < HEAD_DIM, t, sw), jnp.where(lane < HEAD_DIM, sw, t)


def _premix_even_kernel(x_ref, mod_ref, g_ref, w_ref, cs_ref, cos_ref, sin_ref, qg_ref, kg_ref,
                        ab_ref, q_ref, kd_ref, vt_ref, *, rope):
    tm = x_ref.shape[1]
    sub = min(ROW_SUB, tm)
    m = mod_ref[0]
    lane = lax.broadcasted_iota(jnp.int32, (sub, LANES), 1)
    fw = FOURIER_GROUPS * LANES
    qw = q_ref.shape[2]
    csb = cs_ref[...].astype(BF16)
    for r0 in range(0, tm, sub):
        rs = slice(r0, r0 + sub)
        hb = _rms_mod(x_ref[0, rs], g_ref[...], m[0:1], m[1:2]).astype(BF16)
        pf = _dot(hb, w_ref[:, 0:fw])
        for g in range(FOURIER_GROUPS):
            ab = _dot(pf[:, LANES * g:LANES * (g + 1)].astype(BF16), csb)
            ab_ref[0, rs, LANES * g:LANES * (g + 1)] = ab[:, :LANES].astype(BF16)
            ab_ref[0, rs, fw + LANES * g:fw + LANES * (g + 1)] = ab[:, LANES:].astype(BF16)
        pq = _dot(hb, w_ref[:, fw:fw + qw])
        if rope:
            cos_t = cos_ref[rs]
            sin_t = sin_ref[rs]
        for c in range(qw // LANES):
            t = _head_norm(pq[:, LANES * c:LANES * (c + 1)], qg_ref[...], lane)
            if rope:
                t = _rope(t, cos_t, sin_t, lane)
            q_ref[0, rs, LANES * c:LANES * (c + 1)] = t.astype(BF16)
        pk = _dot(hb, w_ref[:, fw + qw:fw + qw + 2 * LANES])
        k = _head_norm(pk[:, :LANES], kg_ref[...], lane)
        if rope:
            k = _rope(k, cos_t, sin_t, lane)
        k0, k1 = _dup_halves(k, lane)
        kd_ref[0, rs, :LANES] = k0.astype(BF16)
        kd_ref[0, rs, LANES:] = k1.astype(BF16)
        vt_ref[0, :, rs] = pk[:, LANES:].T.astype(BF16)


def _premix_even_call(x, mod, gain, w_bf, cs, cos_t, sin_t, qg, kg, *, rope, tm):
    b, n, d = x.shape
    mb = mod.shape[0]
    wtot = w_bf.shape[1]
    fw = FOURIER_GROUPS * LANES
    qw = wtot - fw - 2 * LANES
    mod_map = (lambda i, j: (i, 0, 0)) if mb > 1 else (lambda i, j: (0, 0, 0))
    tok = lambda i, j: (i, j, 0)
    const2 = lambda i, j: (0, 0)
    return pl.pallas_call(
        functools.partial(_premix_even_kernel, rope=rope),
        grid=(b, n // tm),
        in_specs=[
            pl.BlockSpec((1, tm, d), tok),
            pl.BlockSpec((1, 6, d), mod_map),
            pl.BlockSpec((1, d), const2),
            pl.BlockSpec((d, wtot), const2),
            pl.BlockSpec((LANES, 2 * LANES), const2),
            pl.BlockSpec((tm, LANES), lambda i, j: (j, 0)),
            pl.BlockSpec((tm, LANES), lambda i, j: (j, 0)),
            pl.BlockSpec((1, LANES), const2),
            pl.BlockSpec((1, LANES), const2),
        ],
        out_specs=[
            pl.BlockSpec((1, tm, 2 * fw), tok),
            pl.BlockSpec((1, tm, qw), tok),
            pl.BlockSpec((1, tm, 2 * LANES), tok),
            pl.BlockSpec((1, LANES, tm), lambda i, j: (i, 0, j)),
        ],
        out_shape=[
            jax.ShapeDtypeStruct((b, n, 2 * fw), BF16),
            jax.ShapeDtypeStruct((b, n, qw), BF16),
            jax.ShapeDtypeStruct((b, n, 2 * LANES), BF16),
            jax.ShapeDtypeStruct((b, LANES, n), BF16),
        ],
        compiler_params=_cparams(2, VMEM_LIMIT),
        name="premix_even",
    )(x, mod, gain, w_bf, cs, cos_t, sin_t, qg, kg)


def _gated_moe_residual(x_ref, moe_ref, gate_row, r0, sub):
    c8 = x_ref.shape[2] // LANES
    cols = []
    for j in range(c8):
        cs = slice(LANES * j, LANES * (j + 1))
        cols.append(x_ref[0, r0:r0 + sub, cs] + gate_row[:, cs] * moe_ref[0, pl.ds(r0 * c8 + j, sub, stride=c8), :])
    return jnp.concatenate(cols, axis=1)


def _premix_odd_kernel(x_ref, moe_ref, modp_ref, mod_ref, g_ref, w_ref, qg_ref, kg_ref, q_ref, k_ref, v_ref,
                       *xo_ref):
    tm = x_ref.shape[1]
    sub = min(ROW_SUB, tm)
    m = mod_ref[0]
    gate_prev = modp_ref[0][5:6]
    lane = lax.broadcasted_iota(jnp.int32, (sub, LANES), 1)
    wq = q_ref.shape[2]
    chunk = 4 * LANES
    for r0 in range(0, tm, sub):
        rs = slice(r0, r0 + sub)
        x = _gated_moe_residual(x_ref, moe_ref, gate_prev, r0, sub)
        if xo_ref:
            xo_ref[0][0, rs] = x
        hb = _rms_mod(x, g_ref[...], m[0:1], m[1:2]).astype(BF16)
        for c0 in range(0, 3 * wq, chunk):
            p = _dot(hb, w_ref[:, c0:c0 + chunk])
            for cc in range(chunk // LANES):
                col = c0 + cc * LANES
                t = p[:, cc * LANES:(cc + 1) * LANES]
                if col < wq:
                    q_ref[0, rs, col:col + LANES] = _head_norm(t, qg_ref[...], lane).astype(BF16)
                elif col < 2 * wq:
                    k_ref[0, rs, col - wq:col - wq + LANES] = _head_norm(t, kg_ref[...], lane).astype(BF16)
                else:
                    v_ref[0, col - 2 * wq:col - 2 * wq + LANES, rs] = t.T.astype(BF16)


def _premix_odd_call(x1, moe3, mod_prev, mod, gain, w_bf, qg, kg, *, emit_x, tm):
    b, n, d = x1.shape
    mb = mod.shape[0]
    wq = w_bf.shape[1] // 3
    c8 = d // LANES
    mod_map = (lambda i, j: (i, 0, 0)) if mb > 1 else (lambda i, j: (0, 0, 0))
    tok = lambda i, j: (i, j, 0)
    const2 = lambda i, j: (0, 0)
    out_specs = [pl.BlockSpec((1, tm, wq), tok)] * 2 + [pl.BlockSpec((1, wq, tm), lambda i, j: (i, 0, j))]
    out_shape = [jax.ShapeDtypeStruct((b, n, wq), BF16)] * 2 + [jax.ShapeDtypeStruct((b, wq, n), BF16)]
    if emit_x:
        out_specs.append(pl.BlockSpec((1, tm, d), tok))
        out_shape.append(jax.ShapeDtypeStruct((b, n, d), F32))
    return pl.pallas_call(
        _premix_odd_kernel,
        grid=(b, n // tm),
        in_specs=[
            pl.BlockSpec((1, tm, d), tok),
            pl.BlockSpec((1, tm * c8, LANES), tok),
            pl.BlockSpec((1, 6, d), mod_map),
            pl.BlockSpec((1, 6, d), mod_map),
            pl.BlockSpec((1, d), const2),
            pl.BlockSpec((d, 3 * wq), const2),
            pl.BlockSpec((1, LANES), const2),
            pl.BlockSpec((1, LANES), const2),
        ],
        out_specs=out_specs,
        out_shape=out_shape,
        compiler_params=_cparams(2, VMEM_LIMIT),
        name="premix_odd",
    )(x1, moe3, mod_prev, mod, gain, w_bf, qg, kg)


def _col_reduce(x, op, slab=64):
    r = x.shape[0]
    if r > slab and r % slab == 0:
        x = op(x.reshape(r // slab, slab, x.shape[1]), axis=0)
    return op(x, axis=0, keepdims=True)


def _softmax_pv_t(s_t, v_t, sink=None):
    mx = _col_reduce(s_t, jnp.max)
    if sink is not None:
        mx = jnp.maximum(mx, sink)
    p = jnp.exp2((s_t - mx).astype(BF16))
    nd = v_t.shape[0]
    o = _dot(jnp.concatenate([v_t, jnp.ones((ONES_ROWS, v_t.shape[1]), BF16)], axis=0), p)
    den = o[nd:nd + 1]
    if sink is not None:
        den = den + jnp.exp2(sink - mx)
    return o[:nd] / den


def _mask_half(qc, lane, half):
    q32 = qc.astype(F32) * (HEAD_DIM ** -0.5 * LOG2E)
    keep = (lane < HEAD_DIM) if half == 0 else (lane >= HEAD_DIM)
    return jnp.where(keep, q32, 0.0).astype(BF16)


def _gqa_scores(q_ref, h, kd, n_kv, masks, lane, r0, tq):
    rs = slice(r0, r0 + tq)
    cols_per_kv = q_ref.shape[2] // LANES // n_kv
    q_rows = []
    for cc in range(cols_per_kv):
        c = cols_per_kv * h + cc
        for half in range(2):
            q_rows.append(_mask_half(q_ref[0, rs, LANES * c:LANES * (c + 1)], lane, half))
    qm = jnp.concatenate(q_rows, axis=0)
    s = _dot_t(kd, qm)
    pieces, k0 = [], 0
    for rows, mask in masks:
        blk = s[k0:k0 + rows]
        pieces.append(blk if mask is None else jnp.where(mask, blk, NEG))
        k0 += rows
    return jnp.concatenate(pieces, axis=0) if len(pieces) > 1 else pieces[0]


def _gqa_finish(sink_ref, o_ref, h, s_t, v_t, r0, tq):
    rs = slice(r0, r0 + tq)
    n_kv = v_t.shape[0] // HEAD_DIM
    cols_per_kv = o_ref.shape[2] // LANES // n_kv
    heads = 2 * cols_per_kv
    head_of_lane = lax.broadcasted_iota(jnp.int32, (1, heads * tq), 1) // tq
    sink = jnp.zeros((1, heads * tq), F32)
    for a in range(heads):
        sink = jnp.where(head_of_lane == a, sink_ref[heads * h + a] * LOG2E, sink)
    o_t = _softmax_pv_t(s_t, v_t, sink)[HEAD_DIM * h:HEAD_DIM * (h + 1)]
    for cc in range(cols_per_kv):
        c = cols_per_kv * h + cc
        tile = jnp.concatenate([o_t[:, tq * (2 * cc):tq * (2 * cc + 1)], o_t[:, tq * (2 * cc + 1):tq * (2 * cc + 2)]],
                               axis=0)
        o_ref[0, rs, LANES * c:LANES * (c + 1)] = tile.T.astype(BF16)


def _swa_kernel(sink_ref, q_ref, km_ref, k0_ref, kp_ref, vm_ref, v0_ref, vp_ref, kc_ref, vc_ref, o_ref, *, n_pair):
    g = pl.program_id(1)
    tq = SWA_BLOCK
    nc = kc_ref.shape[1]
    n_kv = kc_ref.shape[2] // LANES
    heads = q_ref.shape[2] // HEAD_DIM // n_kv
    lane = lax.broadcasted_iota(jnp.int32, (tq, LANES), 1)
    jk = lax.broadcasted_iota(jnp.int32, (SWA_BLOCK, heads * tq), 0)
    iq = lax.broadcasted_iota(jnp.int32, (SWA_BLOCK, heads * tq), 1) % tq
    below, above = jk >= iq, jk <= iq
    prev_masks = [jnp.logical_and(below, g > 0), below]
    next_masks = [above, jnp.logical_and(above, g < n_pair - 1)]
    lo, hi = slice(0, SWA_BLOCK), slice(SWA_BLOCK, 2 * SWA_BLOCK)
    groups = []
    for sub in range(2):
        masks = [(nc, None), (SWA_BLOCK, prev_masks[sub]), (SWA_BLOCK, None), (SWA_BLOCK, next_masks[sub])]
        for h in range(n_kv):
            hs = slice(LANES * h, LANES * (h + 1))
            if sub == 0:
                k_loc = [km_ref[0, :, hs], k0_ref[0, lo, hs], k0_ref[0, hi, hs]]
            else:
                k_loc = [k0_ref[0, lo, hs], k0_ref[0, hi, hs], kp_ref[0, :, hs]]
            kd = jnp.concatenate([kc_ref[0, :, hs]] + k_loc, axis=0)
            groups.append((sub, h, _gqa_scores(q_ref, h, kd, n_kv, masks, lane, SWA_BLOCK * sub, tq)))
    for sub, h, s_t in groups:
        if sub == 0:
            v_loc = [vm_ref[0], v0_ref[0, :, lo], v0_ref[0, :, hi]]
        else:
            v_loc = [v0_ref[0, :, lo], v0_ref[0, :, hi], vp_ref[0]]
        v_t = jnp.concatenate([vc_ref[0]] + v_loc, axis=1)
        _gqa_finish(sink_ref, o_ref, h, s_t, v_t, SWA_BLOCK * sub, tq)


def _swa_call(sink, q, kd, vt, kdc, vtc):
    b, n, qw = q.shape
    nc = kdc.shape[1]
    kw = kd.shape[2]
    vw = vt.shape[1]
    n_blk = n // SWA_BLOCK
    n_pair = n_blk // 2
    prev = lambda j: jnp.maximum(2 * j - 1, 0)
    nxt = lambda j: jnp.minimum(2 * j + 2, n_blk - 1)
    kspec = lambda f: pl.BlockSpec((1, SWA_BLOCK, kw), lambda i, j: (i, f(j), 0))
    vspec = lambda f: pl.BlockSpec((1, vw, SWA_BLOCK), lambda i, j: (i, 0, f(j)))
    return pl.pallas_call(
        functools.partial(_swa_kernel, n_pair=n_pair),
        grid=(b, n_pair),
        in_specs=[
            pl.BlockSpec(memory_space=pltpu.SMEM),
            pl.BlockSpec((1, 2 * SWA_BLOCK, qw), lambda i, j: (i, j, 0)),
            kspec(prev), pl.BlockSpec((1, 2 * SWA_BLOCK, kw), lambda i, j: (i, j, 0)), kspec(nxt),
            vspec(prev), pl.BlockSpec((1, vw, 2 * SWA_BLOCK), lambda i, j: (i, 0, j)), vspec(nxt),
            pl.BlockSpec((1, nc, kw), lambda i, j: (i, 0, 0)),
            pl.BlockSpec((1, vw, nc), lambda i, j: (i, 0, 0)),
        ],
        out_specs=pl.BlockSpec((1, 2 * SWA_BLOCK, qw), lambda i, j: (i, j, 0)),
        out_shape=jax.ShapeDtypeStruct((b, n, qw), BF16),
        compiler_params=_cparams(2, VMEM_LIMIT),
        name="swa",
    )(sink, q, kd, kd, kd, vt, vt, vt, kdc, vtc)


def _ctx_attn_kernel(sink_ref, q_ref, kc_ref, vc_ref, o_ref):
    tq = q_ref.shape[1]
    lane = lax.broadcasted_iota(jnp.int32, (tq, LANES), 1)
    n_kv = kc_ref.shape[2] // LANES
    scores = [_gqa_scores(q_ref, h, kc_ref[0, :, LANES * h:LANES * (h + 1)], n_kv, [(kc_ref.shape[1], None)], lane,
                          0, tq) for h in range(n_kv)]
    for h, s_t in enumerate(scores):
        _gqa_finish(sink_ref, o_ref, h, s_t, vc_ref[0], 0, tq)


def _ctx_attn_call(sink, q, kd, vt):
    b, nc, qw = q.shape
    kw = kd.shape[2]
    vw = vt.shape[1]
    m = lambda i: (i, 0, 0)
    return pl.pallas_call(
        _ctx_attn_kernel,
        grid=(b,),
        in_specs=[
            pl.BlockSpec(memory_space=pltpu.SMEM),
            pl.BlockSpec((1, nc, qw), m),
            pl.BlockSpec((1, nc, kw), m),
            pl.BlockSpec((1, vw, nc), m),
        ],
        out_specs=pl.BlockSpec((1, nc, qw), m),
        out_shape=jax.ShapeDtypeStruct((b, nc, qw), BF16),
        compiler_params=_cparams(1, VMEM_LIMIT),
        name="ctx_attn",
    )(sink, q, kd, vt)


def _na_kernel(q_ref, k0_ref, k1_ref, k2_ref, v0_ref, v1_ref, v2_ref, kc_ref, vc_ref, tb_ref, o_ref, *, n_j, rows):
    j = pl.program_id(1)
    tq = q_ref.shape[1]
    tk = 3 * k0_ref.shape[1]
    nc = kc_ref.shape[1]
    w0 = NA_QROWS * jnp.clip(j - 1, 0, n_j - 3)
    lane = lax.broadcasted_iota(jnp.int32, (tq, LANES), 1)
    k_row = (lax.broadcasted_iota(jnp.int32, (nc + tk, LANES), 0) - nc) // GRID_W
    k_lane = lax.broadcasted_iota(jnp.int32, (nc + tk, LANES), 1)
    lo = jnp.zeros((nc + tk, LANES), jnp.int32)
    for rr in range(NA_QROWS):
        r = NA_QROWS * j + rr
        lo_rr = jnp.clip(r - NA_ROWS_MAX // 2, 0, rows - NA_ROWS_MAX) - w0
        lo = jnp.where(k_lane == rr, lo_rr, lo)
    outside = jnp.logical_or(k_row < lo, k_row >= lo + NA_ROWS_MAX)
    is_local = lax.broadcasted_iota(jnp.int32, (nc + tk, LANES), 0) >= nc
    k_extra = jnp.where(jnp.logical_and(jnp.logical_and(is_local, k_lane < NA_QROWS), outside), NEG, 0.0).astype(BF16)
    q_grid_row = (lax.broadcasted_iota(jnp.int32, (2 * tq, LANES), 0) % tq) // GRID_W
    q_extra = jnp.where(lax.broadcasted_iota(jnp.int32, (2 * tq, LANES), 1) == q_grid_row, 1.0, 0.0).astype(BF16)
    delta = w0 - NA_QROWS * j
    n_dr = 2 * NA_ROWS_MAX - 1
    def scores(c):
        cs = slice(LANES * c, LANES * (c + 1))
        qc = q_ref[0, :, cs]
        kl = jnp.concatenate([kc_ref[0, :, cs], k0_ref[0, :, cs], k1_ref[0, :, cs], k2_ref[0, :, cs]], axis=0)
        qm = jnp.concatenate([_mask_half(qc, lane, 0), _mask_half(qc, lane, 1)], axis=0)
        return _dot_t(jnp.concatenate([kl, k_extra], axis=1), jnp.concatenate([qm, q_extra], axis=1))

    n_cols = q_ref.shape[2] // LANES
    ahead = 1
    pending = [scores(c) for c in range(min(ahead, n_cols))]
    for c in range(n_cols):
        cs = slice(LANES * c, LANES * (c + 1))
        s = pending.pop(0)
        if c + ahead < n_cols:
            pending.append(scores(c + ahead))
        v_t = jnp.concatenate([vc_ref[0, cs, :], v0_ref[0, cs, :], v1_ref[0, cs, :], v2_ref[0, cs, :]], axis=1)
        bias_rows = []
        for kr in range(NA_KROWS):
            pieces = []
            for half in range(2):
                for t in range(NA_QROWS // 2):
                    dr = delta + kr - 2 * t + NA_ROWS_MAX - 1
                    pieces.append(tb_ref[2 * c + half, jnp.clip(dr, 0, n_dr)])
            bias_rows.append(jnp.concatenate(pieces, axis=1))
        bias = jnp.concatenate(bias_rows, axis=0)
        s_t = jnp.concatenate([s[:nc], s[nc:] + bias], axis=0)
        o_t = _softmax_pv_t(s_t, v_t)
        tile = jnp.concatenate([o_t[:HEAD_DIM, :tq], o_t[HEAD_DIM:, tq:]], axis=0)
        o_ref[0, :, cs] = tile.T.astype(BF16)


def _na_call(q, k, vt, kc, vtc, tb):
    b, n, w = q.shape
    nc = kc.shape[1]
    rows = n // GRID_W
    n_j = rows // NA_QROWS
    tq = NA_QROWS * GRID_W
    cur = lambda i, j: (i, j, 0)
    first = lambda j: jnp.clip(j - 1, 0, n_j - 3)
    kspec = lambda d: pl.BlockSpec((1, tq, w), lambda i, j: (i, first(j) + d, 0))
    vspec = lambda d: pl.BlockSpec((1, w, tq), lambda i, j: (i, 0, first(j) + d))
    return pl.pallas_call(
        functools.partial(_na_kernel, n_j=n_j, rows=rows),
        grid=(b, n_j),
        in_specs=[
            pl.BlockSpec((1, tq, w), cur),
            kspec(0), kspec(1), kspec(2),
            vspec(0), vspec(1), vspec(2),
            pl.BlockSpec((1, nc, w), lambda i, j: (i, 0, 0)),
            pl.BlockSpec((1, w, nc), lambda i, j: (i, 0, 0)),
            pl.BlockSpec(tb.shape, lambda i, j: (0, 0, 0, 0)),
        ],
        out_specs=pl.BlockSpec((1, tq, w), cur),
        out_shape=jax.ShapeDtypeStruct((b, n, w), BF16),
        compiler_params=_cparams(2, VMEM_LIMIT),
        name="na",
    )(q, k, k, k, vt, vt, vt, kc, vtc, tb)


def _na_bias_table(rpb):
    col_q = np.arange(GRID_W)[None, :]
    col_k = np.arange(GRID_W)[:, None]
    c_start = np.clip(col_q - NA_COLS // 2, 0, GRID_W - NA_COLS)
    col_valid = (col_k >= c_start) & (col_k < c_start + NA_COLS)
    dc_idx = np.clip(col_k - col_q + NA_COLS - 1, 0, 2 * NA_COLS - 2)
    t = jnp.where(col_valid[None, None], (rpb.astype(F32) * LOG2E)[:, :, dc_idx], NEG)
    pad = jnp.full_like(t[:, :1], NEG)
    t = jnp.concatenate([pad, t, pad], axis=1)
    return jnp.concatenate([t[:, 1:], t[:, :-1]], axis=-1)


def _fourier_kernel(ab_ref, ca_ref, sa_ref, cb_ref, sb_ref, rev_ref, o_ref, fold_ref, rev_scr):
    n = ab_ref.shape[1]
    nh = n // 2
    tm, fw = o_ref.shape[1], o_ref.shape[2]

    @pl.when(pl.program_id(1) == 0)
    def _():
        for m in range(nh // LANES):
            blk = ab_ref[0, n - LANES * (m + 1):n - LANES * m, :]
            rev_scr[LANES * m:LANES * (m + 1), :] = _dot(rev_ref[...], blk)
        ck = min(256, nh)
        row = lax.broadcasted_iota(jnp.int32, (ck, 1), 0)
        for r0 in range(0, nh, ck):
            prev = (r0 - 1) % nh
            shifted = jnp.where(row == 0, rev_scr[prev:prev + 1, :], pltpu.roll(rev_scr[r0:r0 + ck, :], 1, 0))
            lo = ab_ref[0, r0:r0 + ck, :].astype(F32)
            a2 = lo[:, :fw] + shifted[:, :fw]
            b2 = lo[:, fw:] - shifted[:, fw:]
            if r0 == 0:
                a2 = jnp.where(row == 0, lo[:, :fw], a2)
                b2 = jnp.where(row == 0, shifted[:, :fw], b2)
            fold_ref[r0:r0 + ck, :fw] = a2.astype(BF16)
            fold_ref[r0:r0 + ck, fw:] = b2.astype(BF16)

    cb = cb_ref[...]
    sb = sb_ref[...]
    sub = min(ROW_SUB, tm)
    for r0 in range(0, tm, sub):
        cm, sm = [], []
        for a in range(r0 // DFT_SUB, (r0 + sub) // DFT_SUB):
            ca = ca_ref[a:a + 1, :]
            sa = sa_ref[a:a + 1, :]
            cm.append((ca * cb - sa * sb).astype(BF16))
            sm.append((sa * cb + ca * sb).astype(BF16))
        y = _dot(jnp.concatenate(cm, axis=0), fold_ref[:, :fw]) - _dot(jnp.concatenate(sm, axis=0), fold_ref[:, fw:])
        o_ref[0, r0:r0 + sub] = y.astype(BF16)


def _fourier_call(ab, ca, sa, cb, sb, *, tm):
    b, n, w2 = ab.shape
    nh = n // 2
    fw = w2 // 2
    na = tm // DFT_SUB
    rev = jnp.asarray(np.eye(LANES, dtype=np.float32)[::-1], BF16)
    return pl.pallas_call(
        _fourier_kernel,
        grid=(b, n // tm),
        in_specs=[
            pl.BlockSpec((1, n, w2), lambda i, j: (i, 0, 0)),
            pl.BlockSpec((na, nh), lambda i, j: (j, 0)),
            pl.BlockSpec((na, nh), lambda i, j: (j, 0)),
            pl.BlockSpec((DFT_SUB, nh), lambda i, j: (0, 0)),
            pl.BlockSpec((DFT_SUB, nh), lambda i, j: (0, 0)),
            pl.BlockSpec((LANES, LANES), lambda i, j: (0, 0)),
        ],
        out_specs=pl.BlockSpec((1, tm, fw), lambda i, j: (i, j, 0)),
        out_shape=jax.ShapeDtypeStruct((b, n, fw), BF16),
        scratch_shapes=[pltpu.VMEM((nh, w2), BF16), pltpu.VMEM((nh, w2), F32)],
        compiler_params=_cparams(2, VMEM_LIMIT),
        name="fourier",
    )(ab, ca, sa, cb, sb, rev)


def _dft_tables(n):
    nh = n // 2
    k = jnp.arange(nh, dtype=jnp.int32)
    a = jnp.arange(n // DFT_SUB, dtype=jnp.int32)
    bb = jnp.arange(DFT_SUB, dtype=jnp.int32)
    ang_a = ((DFT_SUB * a[:, None] * k[None, :]) % n).astype(F32) * (2.0 * math.pi / n)
    ang_b = ((bb[:, None] * k[None, :]) % n).astype(F32) * (2.0 * math.pi / n)
    s = float(n) ** -0.5
    nyq = -jnp.cos(math.pi * bb.astype(F32)) * s
    sin_b = jnp.concatenate([nyq[:, None], (jnp.sin(ang_b) * s)[:, 1:]], axis=1)
    return jnp.cos(ang_a), jnp.sin(ang_a), jnp.cos(ang_b) * s, sin_b


def _channel_dft():
    c = jnp.arange(LANES, dtype=jnp.int32)
    ang = ((c[:, None] * c[None, :]) % LANES).astype(F32) * (2.0 * math.pi / LANES)
    s = float(LANES) ** -0.5
    return jnp.concatenate([jnp.cos(ang) * s, jnp.sin(ang) * s], axis=1)


def _postmix_kernel(*refs, n_in):
    a_refs = refs[:n_in]
    w_refs = refs[n_in:2 * n_in]
    x_ref, mod_ref, g_ref, rw_ref, x1_ref, h3_ref, aff_ref = refs[2 * n_in:]
    tm, d = x_ref.shape[1], x_ref.shape[2]
    sub = min(ROW_SUB, tm)
    p = d // 2 // LANES
    m = mod_ref[0]
    for r0 in range(0, tm, sub):
        rs = slice(r0, r0 + sub)
        y = None
        for a_ref, w_ref in zip(a_refs, w_refs):
            t = _dot(a_ref[0, rs], w_ref[...])
            y = t if y is None else y + t
        x1 = x_ref[0, rs] + m[2:3] * y
        x1_ref[0, rs] = x1
        h2 = _rms_mod(x1, g_ref[...], m[3:4], m[4:5])
        hh, hl = _split(h2)
        parts = _dot_t(rw_ref[...], jnp.concatenate([hh, hl], axis=1))
        lt = parts[:N_EXPERTS] + parts[N_EXPERTS:2 * N_EXPERTS]
        ex = jnp.exp(lt - jnp.max(lt, axis=0, keepdims=True))
        aff_ref[0, :, rs] = ex / jnp.sum(ex, axis=0, keepdims=True)
        bits = pltpu.bitcast(hh.astype(F32), jnp.uint32)
        packed = (bits[:, d // 2:] & jnp.uint32(0xFFFF0000)) | (bits[:, :d // 2] >> 16)
        for jj in range(p):
            h3_ref[0, pl.ds(r0 * p + jj, sub, stride=p), :] = packed[:, LANES * jj:LANES * (jj + 1)]


def _postmix_call(a_list, w_list, x, mod, gain, rw, *, tm):
    b, n, d = x.shape
    mb = mod.shape[0]
    n_in = len(a_list)
    p = d // 2 // LANES
    mod_map = (lambda i, j: (i, 0, 0)) if mb > 1 else (lambda i, j: (0, 0, 0))
    tok = lambda i, j: (i, j, 0)
    const2 = lambda i, j: (0, 0)
    in_specs = [pl.BlockSpec((1, tm, a.shape[2]), tok) for a in a_list]
    in_specs += [pl.BlockSpec(w.shape, const2) for w in w_list]
    in_specs += [
        pl.BlockSpec((1, tm, d), tok),
        pl.BlockSpec((1, 6, d), mod_map),
        pl.BlockSpec((1, d), const2),
        pl.BlockSpec((LANES, 2 * d), const2),
    ]
    return pl.pallas_call(
        functools.partial(_postmix_kernel, n_in=n_in),
        grid=(b, n // tm),
        in_specs=in_specs,
        out_specs=[
            pl.BlockSpec((1, tm, d), tok),
            pl.BlockSpec((1, tm * p, LANES), tok),
            pl.BlockSpec((1, N_EXPERTS, tm), lambda i, j: (i, 0, j)),
        ],
        out_shape=[
            jax.ShapeDtypeStruct((b, n, d), F32),
            jax.ShapeDtypeStruct((b, n * p, LANES), jnp.uint32),
            jax.ShapeDtypeStruct((b, N_EXPERTS, n), F32),
        ],
        compiler_params=_cparams(2, VMEM_LIMIT),
        name="postmix",
    )(*a_list, *w_list, x, mod, gain, rw)


def _prefix_incl(ones, tri):
    carry = jnp.zeros((ones.shape[0], 1), F32)
    outs = []
    for c in range(ones.shape[1] // LANES):
        blk = ones[:, LANES * c:LANES * (c + 1)]
        outs.append(_dot(blk.astype(BF16), tri) + carry)
        carry = carry + jnp.sum(blk, axis=1, keepdims=True)
    return jnp.concatenate(outs, axis=1)


def _topk_kernel(aff_ref, tri_ref, idx_ref, gate_ref, *, cap):
    n = aff_ref.shape[2]
    aff = aff_ref[0]
    bits = pltpu.bitcast(aff, jnp.int32)
    capf = jnp.float32(cap)

    def body(it, lo):
        t = lo | jnp.left_shift(jnp.int32(1), 30 - it)
        cnt = jnp.sum(jnp.where(bits >= t, 1.0, 0.0), axis=1, keepdims=True)
        return jnp.where(cnt >= capf, t, lo)

    thr = lax.fori_loop(0, 31, body, jnp.zeros((N_EXPERTS, 1), jnp.int32))
    gt = jnp.where(bits > thr, 1.0, 0.0)
    eq = jnp.where(bits == thr, 1.0, 0.0)
    need = capf - jnp.sum(gt, axis=1, keepdims=True)
    tri = tri_ref[...]
    eq_before = _prefix_incl(eq, tri) - eq
    sel = gt + eq * jnp.where(eq_before < need, 1.0, 0.0)
    slot = _prefix_incl(sel, tri).astype(jnp.int32) - 1
    tok = lax.broadcasted_iota(jnp.int32, (N_EXPERTS, n), 1)
    disp = jnp.where(sel > 0.0, tok - slot, -1)
    gate = aff
    for bit in range((n - 1).bit_length()):
        step = 1 << bit
        disp_in = pltpu.roll(disp, n - step, 1)
        gate_in = pltpu.roll(gate, n - step, 1)
        move_in = jnp.logical_and(disp_in >= 0, ((disp_in >> bit) & 1) == 1)
        stay = jnp.logical_and(disp >= 0, ((disp >> bit) & 1) == 0)
        gate = jnp.where(move_in, gate_in, gate)
        disp = jnp.where(move_in, disp_in, jnp.where(stay, disp, -1))
    idx_ref[0] = (tok + disp)[:, :cap]
    gate_ref[0] = gate[:, :cap]


def _topk_call(aff_t, tri, *, cap):
    b, e, n = aff_t.shape
    return pl.pallas_call(
        functools.partial(_topk_kernel, cap=cap),
        grid=(b,),
        in_specs=[
            pl.BlockSpec((1, e, n), lambda i: (i, 0, 0)),
            pl.BlockSpec((LANES, LANES), lambda i: (0, 0)),
        ],
        out_specs=[pl.BlockSpec((1, e, cap), lambda i: (i, 0, 0))] * 2,
        out_shape=[jax.ShapeDtypeStruct((b, e, cap), jnp.int32), jax.ShapeDtypeStruct((b, e, cap), F32)],
        compiler_params=_cparams(1, VMEM_LIMIT),
        name="topk",
    )(aff_t, tri)


def _moe_kernel(idx_ref, h3_ref, gate_ref, wg_ref, wu_ref, wd_ref, o_ref, xs3, y3, *, cap, p, c8, sub):
    e = pl.program_id(1)

    @pl.when(e == 0)
    def _():
        o_ref[...] = jnp.zeros(o_ref.shape, o_ref.dtype)

    def gather(k):
        for r in range(sub):
            s = k * sub + r
            t = idx_ref[0, 0, s]
            xs3[pl.ds(s * p, p), :] = h3_ref[0, pl.ds(pl.multiple_of(t * p, p), p), :]

    def expert(k):
        base = k * sub
        lo, hi = [], []
        for j in range(p):
            w = xs3[pl.ds(base * p + j, sub, stride=p), :]
            lo.append(pltpu.bitcast(w << 16, F32).astype(BF16))
            hi.append(pltpu.bitcast(w & jnp.uint32(0xFFFF0000), F32).astype(BF16))
        xs = jnp.concatenate(lo + hi, axis=1)
        a = _dot(xs, wg)
        u_ = _dot(xs, wu)
        act = (_silu(a) * u_).astype(BF16)
        y = _dot(act, wd)
        for j in range(c8):
            y3[pl.ds(base * c8 + j, sub, stride=c8), :] = y[:, LANES * j:LANES * (j + 1)]

    group = 4

    def scatter(k):
        for g0 in range(0, sub, group):
            sums = []
            for u in range(group):
                s = k * sub + g0 + u
                t = idx_ref[0, 0, s]
                rows = pl.ds(pl.multiple_of(t * c8, c8), c8)
                sums.append((rows, o_ref[0, rows, :] + y3[pl.ds(s * c8, c8), :] * gate_ref[0, 0, s]))
            for rows, val in sums:
                o_ref[0, rows, :] = val

    wg = wg_ref[0, 0].astype(BF16)
    wu = wu_ref[0, 0].astype(BF16)
    wd = wd_ref[0, 0].astype(BF16)
    n_sub = cap // sub
    gather(0)
    for k in range(n_sub):
        if k + 1 < n_sub:
            gather(k + 1)
        expert(k)
        if k > 0:
            scatter(k - 1)
    scatter(n_sub - 1)


def _moe_call(idx, h3, gate, wg, wu, wd, layer):
    bm, ne, cap = idx.shape
    d = wg.shape[2]
    p = d // 2 // LANES
    c8 = d // LANES
    n = h3.shape[1] // p
    idx3 = idx.reshape(bm * ne, 1, cap)
    gate3 = gate.reshape(bm * ne, 1, cap)
    wmap = lambda i, j: (layer, j, 0, 0)
    return pl.pallas_call(
        functools.partial(_moe_kernel, cap=cap, p=p, c8=c8, sub=min(MOE_SUB, cap)),
        grid=(bm, ne),
        in_specs=[
            pl.BlockSpec((1, 1, cap), lambda i, j: (i * ne + j, 0, 0), memory_space=pltpu.SMEM),
            pl.BlockSpec((1, n * p, LANES), lambda i, j: (i, 0, 0), pipeline_mode=pl.Buffered(1)),
            pl.BlockSpec((1, 1, cap), lambda i, j: (i * ne + j, 0, 0), memory_space=pltpu.SMEM),
            pl.BlockSpec((1, 1, d, wg.shape[3]), wmap),
            pl.BlockSpec((1, 1, d, wu.shape[3]), wmap),
            pl.BlockSpec((1, 1, wd.shape[2], d), wmap),
        ],
        out_specs=pl.BlockSpec((1, n * c8, LANES), lambda i, j: (i, 0, 0), pipeline_mode=pl.Buffered(1)),
        out_shape=jax.ShapeDtypeStruct((bm, n * c8, LANES), F32),
        scratch_shapes=[pltpu.VMEM((cap * p, LANES), jnp.uint32), pltpu.VMEM((cap * c8, LANES), F32)],
        compiler_params=_cparams(2, VMEM_LIMIT),
        name="moe",
    )(idx3, h3, gate3, wg, wu, wd)


def _resid_kernel(x_ref, moe_ref, mod_ref, o_ref):
    o_ref[0] = _gated_moe_residual(x_ref, moe_ref, mod_ref[0][5:6], 0, x_ref.shape[1])


def _resid_call(x1, moe3, mod, *, tm):
    b, n, d = x1.shape
    mb = mod.shape[0]
    c8 = d // LANES
    mod_map = (lambda i, j: (i, 0, 0)) if mb > 1 else (lambda i, j: (0, 0, 0))
    tok = lambda i, j: (i, j, 0)
    return pl.pallas_call(
        _resid_kernel,
        grid=(b, n // tm),
        in_specs=[
            pl.BlockSpec((1, tm, d), tok),
            pl.BlockSpec((1, tm * c8, LANES), tok),
            pl.BlockSpec((1, 6, d), mod_map),
        ],
        out_specs=pl.BlockSpec((1, tm, d), tok),
        out_shape=jax.ShapeDtypeStruct((b, n, d), F32),
        compiler_params=_cparams(2, VMEM_LIMIT),
        name="resid",
    )(x1, moe3, mod)


def _rope_tables(n):
    t = jnp.arange(n)
    row = (t // GRID_W).astype(F32)
    col = (t % GRID_W).astype(F32)
    n_freq = HEAD_DIM // 4
    inv_freq = jnp.power(ROPE_BASE, -jnp.arange(n_freq, dtype=F32) / n_freq)
    ang = jnp.concatenate([row[:, None] * inv_freq, col[:, None] * inv_freq], axis=-1)
    cos, sin = jnp.cos(ang), jnp.sin(ang)
    cos_t = jnp.concatenate([cos, cos, cos, cos], axis=-1)
    sin_t = jnp.concatenate([-sin, sin, -sin, sin], axis=-1)
    return cos_t, sin_t


def _gain128(g):
    return jnp.concatenate([g, g]).reshape(1, LANES).astype(F32)


def _channel_mix(h_pack, aff_t, tri, weights, layer, *, flatten):
    wg, wu, wd = weights
    b, _, n = aff_t.shape
    cap = EC_CAPACITY_FACTOR * n // N_EXPERTS
    idx, gate = _topk_call(aff_t, tri, cap=cap)
    if flatten:
        idx = (idx + (jnp.arange(b, dtype=jnp.int32) * n)[:, None, None]).transpose(1, 0, 2).reshape(1, N_EXPERTS, b * cap)
        gate = gate.transpose(1, 0, 2).reshape(1, N_EXPERTS, b * cap)
        h_pack = h_pack.reshape(1, -1, LANES)
    return _moe_call(idx, h_pack, gate, wg, wu, wd, layer).reshape(b, -1, LANES)


def kernel(x, c, ctx, c_ctx, ada_w, ada_b, norm1_g, norm2_g, router_w, exp_w_gate, exp_w_up, exp_w_down, ev_w_in,
           ev_w_out, ev_q_gain, ev_k_gain, ev_sink, od_w_in, od_w_out, od_q_gain, od_k_gain, od_rpb):
    b, n, d = x.shape
    nc = ctx.shape[1]
    tm = min(512, n)
    tmc = min(512, nc)

    rows = -(-(b + 1) // 8) * 8
    cs = jnp.concatenate([c, c_ctx[None], jnp.zeros((rows - b - 1, d), F32)], axis=0)
    mods = _mod_call(cs, ada_w, ada_b).reshape(ada_w.shape[0], rows, 6, d)

    cos_t, sin_t = _rope_tables(n)
    cs_dft = _channel_dft()
    tri = jnp.asarray(np.triu(np.ones((LANES, LANES), np.float32)), BF16)
    fw = FOURIER_GROUPS * LANES

    def router_split(l):
        rw = router_w[l].T
        rh = rw.astype(BF16)
        rl = (rw - rh.astype(F32)).astype(BF16)
        left = jnp.pad(jnp.concatenate([rh, rl], axis=0), ((0, LANES - 2 * N_EXPERTS), (0, 0)))
        right = jnp.pad(rh, ((0, LANES - N_EXPERTS), (0, 0)))
        return jnp.concatenate([left, right], axis=1)

    wts = (exp_w_gate, exp_w_up, exp_w_down)

    mod_x, mod_c = mods[0, :b], mods[0, b:b + 1]
    g1 = norm1_g[0].reshape(1, d)
    g2 = norm2_g[0].reshape(1, d)
    w_in = ev_w_in[0].astype(BF16)
    w_out = ev_w_out[0].astype(BF16)
    qg, kg = _gain128(ev_q_gain[0]), _gain128(ev_k_gain[0])
    sink = ev_sink[0].astype(F32)
    rw = router_split(0)

    ab_x, q_x, kd_x, vd_x = _premix_even_call(x, mod_x, g1, w_in, cs_dft, cos_t, sin_t, qg, kg, rope=True, tm=tm)
    ab_c, q_c, kd_c, vd_c = _premix_even_call(ctx, mod_c, g1, w_in, cs_dft, cos_t[:nc], sin_t[:nc], qg, kg,
                                              rope=False, tm=tmc)
    a_x = _swa_call(sink, q_x, kd_x, vd_x, kd_c, vd_c)
    a_c = _ctx_attn_call(sink, q_c, kd_c, vd_c)
    four_x = _fourier_call(ab_x, *_dft_tables(n), tm=tm)
    four_c = _fourier_call(ab_c, *_dft_tables(nc), tm=tmc)
    w_out_parts = [w_out[:fw], w_out[fw:]]
    x1, h3_x, aff_x = _postmix_call([four_x, a_x], w_out_parts, x, mod_x, g2, rw, tm=tm)
    c1, h3_c, aff_c = _postmix_call([four_c, a_c], w_out_parts, ctx, mod_c, g2, rw, tm=tmc)
    moe_x = _channel_mix(h3_x, aff_x, tri, wts, 0, flatten=False)
    moe_c = _channel_mix(h3_c, aff_c, tri, wts, 0, flatten=True)

    mod_x0, mod_c0 = mod_x, mod_c
    mod_x, mod_c = mods[1, :b], mods[1, b:b + 1]
    g1 = norm1_g[1].reshape(1, d)
    g2 = norm2_g[1].reshape(1, d)
    w_in = od_w_in[0].astype(BF16)
    w_out = od_w_out[0].astype(BF16)
    qg, kg = _gain128(od_q_gain[0]), _gain128(od_k_gain[0])
    rw = router_split(1)

    q_x, k_x, v_x, x = _premix_odd_call(x1, moe_x, mod_x0, mod_x, g1, w_in, qg, kg, emit_x=True, tm=tm)
    _, k_c, v_c = _premix_odd_call(c1, moe_c, mod_c0, mod_c, g1, w_in, qg, kg, emit_x=False, tm=tmc)
    a_x = _na_call(q_x, k_x, v_x, k_c, v_c, _na_bias_table(od_rpb[0]))
    x1, h3_x, aff_x = _postmix_call([a_x], [w_out], x, mod_x, g2, rw, tm=tm)
    return _resid_call(x1, _channel_mix(h3_x, aff_x, tri, wts, 1, flatten=False), mod_x, tm=tm)
```

```python
import functools
import math

import numpy as np
import jax
import jax.numpy as jnp
from jax import lax
from jax.experimental import pallas as pl
from jax.experimental.pallas import tpu as pltpu

GRID_W = 64
HEAD_DIM = 64
FOURIER_GROUPS = 4
SWA_WINDOW = 128
SWA_BLOCK = 128
NA_ROWS_MAX = 8
NA_COLS = 16
N_EXPERTS = 16
EC_CAPACITY_FACTOR = 2
ROPE_BASE = 10000.0
EPS = 1e-6

LANES = 128
NEG = -1e30
NA_QROWS = 4
NA_KROWS = 12
DFT_SUB = 64
MOE_SUB = 128
ROW_SUB = 256
ONES_ROWS = 16
LOG2E = math.log2(math.e)
VMEM_LIMIT = 60 * 1024 * 1024

F32 = jnp.float32
BF16 = jnp.bfloat16


def _cparams(n_axes, vmem=None):
    return pltpu.CompilerParams(dimension_semantics=("arbitrary",) * n_axes, vmem_limit_bytes=vmem)


def _dot(a, b):
    return jnp.dot(a, b, preferred_element_type=F32)


def _dot_t(a, b):
    return lax.dot_general(a, b, (((1,), (1,)), ((), ())), preferred_element_type=F32)


def _split(a):
    hi = a.astype(BF16)
    lo = (a - hi.astype(F32)).astype(BF16)
    return hi, lo


def _dot3(a, b):
    ah, al = _split(a)
    bh, bl = _split(b)
    return _dot(ah, bh) + _dot(al, bh) + _dot(ah, bl)


def _silu(a):
    return a / (1.0 + jnp.exp(-a))


def _rms_mod(x, gain, shift, scale):
    ms = jnp.mean(x * x, axis=-1, keepdims=True)
    y = x * lax.rsqrt(ms + EPS) * gain
    return y * (1.0 + scale) + shift


def _mod_kernel(cs_ref, w_ref, b_ref, o_ref):
    o_ref[0] = _dot3(_silu(cs_ref[...]), w_ref[0]) + b_ref[0]


def _mod_call(cs, ada_w, ada_b):
    depth, d, d6 = ada_w.shape
    r = cs.shape[0]
    tn = 1024
    return pl.pallas_call(
        _mod_kernel,
        grid=(depth, d6 // tn),
        in_specs=[
            pl.BlockSpec((r, d), lambda l, j: (0, 0)),
            pl.BlockSpec((1, d, tn), lambda l, j: (l, 0, j)),
            pl.BlockSpec((1, 1, tn), lambda l, j: (l, 0, j)),
        ],
        out_specs=pl.BlockSpec((1, r, tn), lambda l, j: (l, 0, j)),
        out_shape=jax.ShapeDtypeStruct((depth, r, d6), F32),
        compiler_params=_cparams(2),
        name="mod",
    )(cs, ada_w, ada_b.reshape(depth, 1, d6))


def _head_norm(t, gain, lane):
    t2 = t * t
    s_lo = jnp.sum(jnp.where(lane < HEAD_DIM, t2, 0.0), axis=-1, keepdims=True)
    s_all = jnp.sum(t2, axis=-1, keepdims=True)
    ms = jnp.where(lane < HEAD_DIM, s_lo, s_all - s_lo) * (1.0 / HEAD_DIM)
    return t * lax.rsqrt(ms + EPS) * gain


def _rope(t, cos_t, sin_t, lane):
    partner = jnp.where((lane % HEAD_DIM) < HEAD_DIM // 2, pltpu.roll(t, LANES - HEAD_DIM // 2, 1),
                        pltpu.roll(t, HEAD_DIM // 2, 1))
    return t * cos_t + partner * sin_t


def _dup_halves(t, lane):
    sw = pltpu.roll(t, HEAD_DIM, 1)
    return jnp.where(lane < HEAD_DIM, t, sw), jnp.where(lane < HEAD_DIM, sw, t)


def _premix_even_kernel(x_ref, mod_ref, g_ref, w_ref, cs_ref, cos_ref, sin_ref, qg_ref, kg_ref,
                        ab_ref, q_ref, kd_ref, vt_ref, *, rope):
    tm = x_ref.shape[1]
    sub = min(ROW_SUB, tm)
    m = mod_ref[0]
    lane = lax.broadcasted_iota(jnp.int32, (sub, LANES), 1)
    fw = FOURIER_GROUPS * LANES
    qw = q_ref.shape[2]
    csb = cs_ref[...].astype(BF16)
    for r0 in range(0, tm, sub):
        rs = slice(r0, r0 + sub)
        hb = _rms_mod(x_ref[0, rs], g_ref[...], m[0:1], m[1:2]).astype(BF16)
        pf = _dot(hb, w_ref[:, 0:fw])
        for g in range(FOURIER_GROUPS):
            ab = _dot(pf[:, LANES * g:LANES * (g + 1)].astype(BF16), csb)
            ab_ref[0, rs, LANES * g:LANES * (g + 1)] = ab[:, :LANES].astype(BF16)
            ab_ref[0, rs, fw + LANES * g:fw + LANES * (g + 1)] = ab[:, LANES:].astype(BF16)
        pq = _dot(hb, w_ref[:, fw:fw + qw])
        if rope:
            cos_t = cos_ref[rs]
            sin_t = sin_ref[rs]
        for c in range(qw // LANES):
            t = _head_norm(pq[:, LANES * c:LANES * (c + 1)], qg_ref[...], lane)
            if rope:
                t = _rope(t, cos_t, sin_t, lane)
            q_ref[0, rs, LANES * c:LANES * (c + 1)] = t.astype(BF16)
        pk = _dot(hb, w_ref[:, fw + qw:fw + qw + 2 * LANES])
        k = _head_norm(pk[:, :LANES], kg_ref[...], lane)
        if rope:
            k = _rope(k, cos_t, sin_t, lane)
        k0, k1 = _dup_halves(k, lane)
        kd_ref[0, rs, :LANES] = k0.astype(BF16)
        kd_ref[0, rs, LANES:] = k1.astype(BF16)
        vt_ref[0, :, rs] = pk[:, LANES:].T.astype(BF16)


def _premix_even_call(x, mod, gain, w_bf, cs, cos_t, sin_t, qg, kg, *, rope, tm):
    b, n, d = x.shape
    mb = mod.shape[0]
    wtot = w_bf.shape[1]
    fw = FOURIER_GROUPS * LANES
    qw = wtot - fw - 2 * LANES
    mod_map = (lambda i, j: (i, 0, 0)) if mb > 1 else (lambda i, j: (0, 0, 0))
    tok = lambda i, j: (i, j, 0)
    const2 = lambda i, j: (0, 0)
    return pl.pallas_call(
        functools.partial(_premix_even_kernel, rope=rope),
        grid=(b, n // tm),
        in_specs=[
            pl.BlockSpec((1, tm, d), tok),
            pl.BlockSpec((1, 6, d), mod_map),
            pl.BlockSpec((1, d), const2),
            pl.BlockSpec((d, wtot), const2),
            pl.BlockSpec((LANES, 2 * LANES), const2),
            pl.BlockSpec((tm, LANES), lambda i, j: (j, 0)),
            pl.BlockSpec((tm, LANES), lambda i, j: (j, 0)),
            pl.BlockSpec((1, LANES), const2),
            pl.BlockSpec((1, LANES), const2),
        ],
        out_specs=[
            pl.BlockSpec((1, tm, 2 * fw), tok),
            pl.BlockSpec((1, tm, qw), tok),
            pl.BlockSpec((1, tm, 2 * LANES), tok),
            pl.BlockSpec((1, LANES, tm), lambda i, j: (i, 0, j)),
        ],
        out_shape=[
            jax.ShapeDtypeStruct((b, n, 2 * fw), BF16),
            jax.ShapeDtypeStruct((b, n, qw), BF16),
            jax.ShapeDtypeStruct((b, n, 2 * LANES), BF16),
            jax.ShapeDtypeStruct((b, LANES, n), BF16),
        ],
        compiler_params=_cparams(2, VMEM_LIMIT),
        name="premix_even",
    )(x, mod, gain, w_bf, cs, cos_t, sin_t, qg, kg)


def _gated_moe_residual(x_ref, moe_ref, gate_row, r0, sub):
    c8 = x_ref.shape[2] // LANES
    cols = []
    for j in range(c8):
        cs = slice(LANES * j, LANES * (j + 1))
        cols.append(x_ref[0, r0:r0 + sub, cs] + gate_row[:, cs] * moe_ref[0, pl.ds(r0 * c8 + j, sub, stride=c8), :])
    return jnp.concatenate(cols, axis=1)


def _premix_odd_kernel(x_ref, moe_ref, modp_ref, mod_ref, g_ref, w_ref, qg_ref, kg_ref, q_ref, k_ref, v_ref,
                       *xo_ref):
    tm = x_ref.shape[1]
    sub = min(ROW_SUB, tm)
    m = mod_ref[0]
    gate_prev = modp_ref[0][5:6]
    lane = lax.broadcasted_iota(jnp.int32, (sub, LANES), 1)
    wq = q_ref.shape[2]
    chunk = 4 * LANES
    for r0 in range(0, tm, sub):
        rs = slice(r0, r0 + sub)
        x = _gated_moe_residual(x_ref, moe_ref, gate_prev, r0, sub)
        if xo_ref:
            xo_ref[0][0, rs] = x
        hb = _rms_mod(x, g_ref[...], m[0:1], m[1:2]).astype(BF16)
        for c0 in range(0, 3 * wq, chunk):
            p = _dot(hb, w_ref[:, c0:c0 + chunk])
            for cc in range(chunk // LANES):
                col = c0 + cc * LANES
                t = p[:, cc * LANES:(cc + 1) * LANES]
                if col < wq:
                    q_ref[0, rs, col:col + LANES] = _head_norm(t, qg_ref[...], lane).astype(BF16)
                elif col < 2 * wq:
                    k_ref[0, rs, col - wq:col - wq + LANES] = _head_norm(t, kg_ref[...], lane).astype(BF16)
                else:
                    v_ref[0, col - 2 * wq:col - 2 * wq + LANES, rs] = t.T.astype(BF16)


def _premix_odd_call(x1, moe3, mod_prev, mod, gain, w_bf, qg, kg, *, emit_x, tm):
    b, n, d = x1.shape
    mb = mod.shape[0]
    wq = w_bf.shape[1] // 3
    c8 = d // LANES
    mod_map = (lambda i, j: (i, 0, 0)) if mb > 1 else (lambda i, j: (0, 0, 0))
    tok = lambda i, j: (i, j, 0)
    const2 = lambda i, j: (0, 0)
    out_specs = [pl.BlockSpec((1, tm, wq), tok)] * 2 + [pl.BlockSpec((1, wq, tm), lambda i, j: (i, 0, j))]
    out_shape = [jax.ShapeDtypeStruct((b, n, wq), BF16)] * 2 + [jax.ShapeDtypeStruct((b, wq, n), BF16)]
    if emit_x:
        out_specs.append(pl.BlockSpec((1, tm, d), tok))
        out_shape.append(jax.ShapeDtypeStruct((b, n, d), F32))
    return pl.pallas_call(
        _premix_odd_kernel,
        grid=(b, n // tm),
        in_specs=[
            pl.BlockSpec((1, tm, d), tok),
            pl.BlockSpec((1, tm * c8, LANES), tok),
            pl.BlockSpec((1, 6, d), mod_map),
            pl.BlockSpec((1, 6, d), mod_map),
            pl.BlockSpec((1, d), const2),
            pl.BlockSpec((d, 3 * wq), const2),
            pl.BlockSpec((1, LANES), const2),
            pl.BlockSpec((1, LANES), const2),
        ],
        out_specs=out_specs,
        out_shape=out_shape,
        compiler_params=_cparams(2, VMEM_LIMIT),
        name="premix_odd",
    )(x1, moe3, mod_prev, mod, gain, w_bf, qg, kg)


def _col_reduce(x, op, slab=64):
    r = x.shape[0]
    if r > slab and r % slab == 0:
        x = op(x.reshape(r // slab, slab, x.shape[1]), axis=0)
    return op(x, axis=0, keepdims=True)


def _softmax_pv_t(s_t, v_t, sink=None):
    mx = _col_reduce(s_t, jnp.max)
    if sink is not None:
        mx = jnp.maximum(mx, sink)
    p = jnp.exp2((s_t - mx).astype(BF16))
    nd = v_t.shape[0]
    o = _dot(jnp.concatenate([v_t, jnp.ones((ONES_ROWS, v_t.shape[1]), BF16)], axis=0), p)
    den = o[nd:nd + 1]
    if sink is not None:
        den = den + jnp.exp2(sink - mx)
    return o[:nd] / den


def _mask_half(qc, lane, half):
    q32 = qc.astype(F32) * (HEAD_DIM ** -0.5 * LOG2E)
    keep = (lane < HEAD_DIM) if half == 0 else (lane >= HEAD_DIM)
    return jnp.where(keep, q32, 0.0).astype(BF16)


def _gqa_scores(q_ref, h, kd, n_kv, masks, lane, r0, tq):
    rs = slice(r0, r0 + tq)
    cols_per_kv = q_ref.shape[2] // LANES // n_kv
    q_rows = []
    for cc in range(cols_per_kv):
        c = cols_per_kv * h + cc
        for half in range(2):
            q_rows.append(_mask_half(q_ref[0, rs, LANES * c:LANES * (c + 1)], lane, half))
    qm = jnp.concatenate(q_rows, axis=0)
    s = _dot_t(kd, qm)
    pieces, k0 = [], 0
    for rows, mask in masks:
        blk = s[k0:k0 + rows]
        pieces.append(blk if mask is None else jnp.where(mask, blk, NEG))
        k0 += rows
    return jnp.concatenate(pieces, axis=0) if len(pieces) > 1 else pieces[0]


def _gqa_finish(sink_ref, o_ref, h, s_t, v_t, r0, tq):
    rs = slice(r0, r0 + tq)
    n_kv = v_t.shape[0] // HEAD_DIM
    cols_per_kv = o_ref.shape[2] // LANES // n_kv
    heads = 2 * cols_per_kv
    head_of_lane = lax.broadcasted_iota(jnp.int32, (1, heads * tq), 1) // tq
    sink = jnp.zeros((1, heads * tq), F32)
    for a in range(heads):
        sink = jnp.where(head_of_lane == a, sink_ref[heads * h + a] * LOG2E, sink)
    o_t = _softmax_pv_t(s_t, v_t, sink)[HEAD_DIM * h:HEAD_DIM * (h + 1)]
    for cc in range(cols_per_kv):
        c = cols_per_kv * h + cc
        tile = jnp.concatenate([o_t[:, tq * (2 * cc):tq * (2 * cc + 1)], o_t[:, tq * (2 * cc + 1):tq * (2 * cc + 2)]],
                               axis=0)
        o_ref[0, rs, LANES * c:LANES * (c + 1)] = tile.T.astype(BF16)


def _swa_kernel(sink_ref, q_ref, km_ref, k0_ref, kp_ref, vm_ref, v0_ref, vp_ref, kc_ref, vc_ref, o_ref, *, n_pair):
    g = pl.program_id(1)
    tq = SWA_BLOCK
    nc = kc_ref.shape[1]
    n_kv = kc_ref.shape[2] // LANES
    heads = q_ref.shape[2] // HEAD_DIM // n_kv
    lane = lax.broadcasted_iota(jnp.int32, (tq, LANES), 1)
    jk = lax.broadcasted_iota(jnp.int32, (SWA_BLOCK, heads * tq), 0)
    iq = lax.broadcasted_iota(jnp.int32, (SWA_BLOCK, heads * tq), 1) % tq
    below, above = jk >= iq, jk <= iq
    prev_masks = [jnp.logical_and(below, g > 0), below]
    next_masks = [above, jnp.logical_and(above, g < n_pair - 1)]
    lo, hi = slice(0, SWA_BLOCK), slice(SWA_BLOCK, 2 * SWA_BLOCK)
    groups = []
    for sub in range(2):
        masks = [(nc, None), (SWA_BLOCK, prev_masks[sub]), (SWA_BLOCK, None), (SWA_BLOCK, next_masks[sub])]
        for h in range(n_kv):
            hs = slice(LANES * h, LANES * (h + 1))
            if sub == 0:
                k_loc = [km_ref[0, :, hs], k0_ref[0, lo, hs], k0_ref[0, hi, hs]]
            else:
                k_loc = [k0_ref[0, lo, hs], k0_ref[0, hi, hs], kp_ref[0, :, hs]]
            kd = jnp.concatenate([kc_ref[0, :, hs]] + k_loc, axis=0)
            groups.append((sub, h, _gqa_scores(q_ref, h, kd, n_kv, masks, lane, SWA_BLOCK * sub, tq)))
    for sub, h, s_t in groups:
        if sub == 0:
            v_loc = [vm_ref[0], v0_ref[0, :, lo], v0_ref[0, :, hi]]
        else:
            v_loc = [v0_ref[0, :, lo], v0_ref[0, :, hi], vp_ref[0]]
        v_t = jnp.concatenate([vc_ref[0]] + v_loc, axis=1)
        _gqa_finish(sink_ref, o_ref, h, s_t, v_t, SWA_BLOCK * sub, tq)


def _swa_call(sink, q, kd, vt, kdc, vtc):
    b, n, qw = q.shape
    nc = kdc.shape[1]
    kw = kd.shape[2]
    vw = vt.shape[1]
    n_blk = n // SWA_BLOCK
    n_pair = n_blk // 2
    prev = lambda j: jnp.maximum(2 * j - 1, 0)
    nxt = lambda j: jnp.minimum(2 * j + 2, n_blk - 1)
    kspec = lambda f: pl.BlockSpec((1, SWA_BLOCK, kw), lambda i, j: (i, f(j), 0))
    vspec = lambda f: pl.BlockSpec((1, vw, SWA_BLOCK), lambda i, j: (i, 0, f(j)))
    return pl.pallas_call(
        functools.partial(_swa_kernel, n_pair=n_pair),
        grid=(b, n_pair),
        in_specs=[
            pl.BlockSpec(memory_space=pltpu.SMEM),
            pl.BlockSpec((1, 2 * SWA_BLOCK, qw), lambda i, j: (i, j, 0)),
            kspec(prev), pl.BlockSpec((1, 2 * SWA_BLOCK, kw), lambda i, j: (i, j, 0)), kspec(nxt),
            vspec(prev), pl.BlockSpec((1, vw, 2 * SWA_BLOCK), lambda i, j: (i, 0, j)), vspec(nxt),
            pl.BlockSpec((1, nc, kw), lambda i, j: (i, 0, 0)),
            pl.BlockSpec((1, vw, nc), lambda i, j: (i, 0, 0)),
        ],
        out_specs=pl.BlockSpec((1, 2 * SWA_BLOCK, qw), lambda i, j: (i, j, 0)),
        out_shape=jax.ShapeDtypeStruct((b, n, qw), BF16),
        compiler_params=_cparams(2, VMEM_LIMIT),
        name="swa",
    )(sink, q, kd, kd, kd, vt, vt, vt, kdc, vtc)


def _ctx_attn_kernel(sink_ref, q_ref, kc_ref, vc_ref, o_ref):
    tq = q_ref.shape[1]
    lane = lax.broadcasted_iota(jnp.int32, (tq, LANES), 1)
    n_kv = kc_ref.shape[2] // LANES
    scores = [_gqa_scores(q_ref, h, kc_ref[0, :, LANES * h:LANES * (h + 1)], n_kv, [(kc_ref.shape[1], None)], lane,
                          0, tq) for h in range(n_kv)]
    for h, s_t in enumerate(scores):
        _gqa_finish(sink_ref, o_ref, h, s_t, vc_ref[0], 0, tq)


def _ctx_attn_call(sink, q, kd, vt):
    b, nc, qw = q.shape
    kw = kd.shape[2]
    vw = vt.shape[1]
    m = lambda i: (i, 0, 0)
    return pl.pallas_call(
        _ctx_attn_kernel,
        grid=(b,),
        in_specs=[
            pl.BlockSpec(memory_space=pltpu.SMEM),
            pl.BlockSpec((1, nc, qw), m),
            pl.BlockSpec((1, nc, kw), m),
            pl.BlockSpec((1, vw, nc), m),
        ],
        out_specs=pl.BlockSpec((1, nc, qw), m),
        out_shape=jax.ShapeDtypeStruct((b, nc, qw), BF16),
        compiler_params=_cparams(1, VMEM_LIMIT),
        name="ctx_attn",
    )(sink, q, kd, vt)


def _na_kernel(q_ref, k0_ref, k1_ref, k2_ref, v0_ref, v1_ref, v2_ref, kc_ref, vc_ref, tb_ref, o_ref, *, n_j, rows):
    j = pl.program_id(1)
    tq = q_ref.shape[1]
    tk = 3 * k0_ref.shape[1]
    nc = kc_ref.shape[1]
    w0 = NA_QROWS * jnp.clip(j - 1, 0, n_j - 3)
    lane = lax.broadcasted_iota(jnp.int32, (tq, LANES), 1)
    k_row = (lax.broadcasted_iota(jnp.int32, (nc + tk, LANES), 0) - nc) // GRID_W
    k_lane = lax.broadcasted_iota(jnp.int32, (nc + tk, LANES), 1)
    lo = jnp.zeros((nc + tk, LANES), jnp.int32)
    for rr in range(NA_QROWS):
        r = NA_QROWS * j + rr
        lo_rr = jnp.clip(r - NA_ROWS_MAX // 2, 0, rows - NA_ROWS_MAX) - w0
        lo = jnp.where(k_lane == rr, lo_rr, lo)
    outside = jnp.logical_or(k_row < lo, k_row >= lo + NA_ROWS_MAX)
    is_local = lax.broadcasted_iota(jnp.int32, (nc + tk, LANES), 0) >= nc
    k_extra = jnp.where(jnp.logical_and(jnp.logical_and(is_local, k_lane < NA_QROWS), outside), NEG, 0.0).astype(BF16)
    q_grid_row = (lax.broadcasted_iota(jnp.int32, (2 * tq, LANES), 0) % tq) // GRID_W
    q_extra = jnp.where(lax.broadcasted_iota(jnp.int32, (2 * tq, LANES), 1) == q_grid_row, 1.0, 0.0).astype(BF16)
    delta = w0 - NA_QROWS * j
    n_dr = 2 * NA_ROWS_MAX - 1
    def scores(c):
        cs = slice(LANES * c, LANES * (c + 1))
        qc = q_ref[0, :, cs]
        kl = jnp.concatenate([kc_ref[0, :, cs], k0_ref[0, :, cs], k1_ref[0, :, cs], k2_ref[0, :, cs]], axis=0)
        qm = jnp.concatenate([_mask_half(qc, lane, 0), _mask_half(qc, lane, 1)], axis=0)
        return _dot_t(jnp.concatenate([kl, k_extra], axis=1), jnp.concatenate([qm, q_extra], axis=1))

    n_cols = q_ref.shape[2] // LANES
    ahead = 1
    pending = [scores(c) for c in range(min(ahead, n_cols))]
    for c in range(n_cols):
        cs = slice(LANES * c, LANES * (c + 1))
        s = pending.pop(0)
        if c + ahead < n_cols:
            pending.append(scores(c + ahead))
        v_t = jnp.concatenate([vc_ref[0, cs, :], v0_ref[0, cs, :], v1_ref[0, cs, :], v2_ref[0, cs, :]], axis=1)
        bias_rows = []
        for kr in range(NA_KROWS):
            pieces = []
            for half in range(2):
                for t in range(NA_QROWS // 2):
                    dr = delta + kr - 2 * t + NA_ROWS_MAX - 1
                    pieces.append(tb_ref[2 * c + half, jnp.clip(dr, 0, n_dr)])
            bias_rows.append(jnp.concatenate(pieces, axis=1))
        bias = jnp.concatenate(bias_rows, axis=0)
        s_t = jnp.concatenate([s[:nc], s[nc:] + bias], axis=0)
        o_t = _softmax_pv_t(s_t, v_t)
        tile = jnp.concatenate([o_t[:HEAD_DIM, :tq], o_t[HEAD_DIM:, tq:]], axis=0)
        o_ref[0, :, cs] = tile.T.astype(BF16)


def _na_call(q, k, vt, kc, vtc, tb):
    b, n, w = q.shape
    nc = kc.shape[1]
    rows = n // GRID_W
    n_j = rows // NA_QROWS
    tq = NA_QROWS * GRID_W
    cur = lambda i, j: (i, j, 0)
    first = lambda j: jnp.clip(j - 1, 0, n_j - 3)
    kspec = lambda d: pl.BlockSpec((1, tq, w), lambda i, j: (i, first(j) + d, 0))
    vspec = lambda d: pl.BlockSpec((1, w, tq), lambda i, j: (i, 0, first(j) + d))
    return pl.pallas_call(
        functools.partial(_na_kernel, n_j=n_j, rows=rows),
        grid=(b, n_j),
        in_specs=[
            pl.BlockSpec((1, tq, w), cur),
            kspec(0), kspec(1), kspec(2),
            vspec(0), vspec(1), vspec(2),
            pl.BlockSpec((1, nc, w), lambda i, j: (i, 0, 0)),
            pl.BlockSpec((1, w, nc), lambda i, j: (i, 0, 0)),
            pl.BlockSpec(tb.shape, lambda i, j: (0, 0, 0, 0)),
        ],
        out_specs=pl.BlockSpec((1, tq, w), cur),
        out_shape=jax.ShapeDtypeStruct((b, n, w), BF16),
        compiler_params=_cparams(2, VMEM_LIMIT),
        name="na",
    )(q, k, k, k, vt, vt, vt, kc, vtc, tb)


def _na_bias_table(rpb):
    col_q = np.arange(GRID_W)[None, :]
    col_k = np.arange(GRID_W)[:, None]
    c_start = np.clip(col_q - NA_COLS // 2, 0, GRID_W - NA_COLS)
    col_valid = (col_k >= c_start) & (col_k < c_start + NA_COLS)
    dc_idx = np.clip(col_k - col_q + NA_COLS - 1, 0, 2 * NA_COLS - 2)
    t = jnp.where(col_valid[None, None], (rpb.astype(F32) * LOG2E)[:, :, dc_idx], NEG)
    pad = jnp.full_like(t[:, :1], NEG)
    t = jnp.concatenate([pad, t, pad], axis=1)
    return jnp.concatenate([t[:, 1:], t[:, :-1]], axis=-1)


def _fourier_kernel(ab_ref, ca_ref, sa_ref, cb_ref, sb_ref, rev_ref, o_ref, fold_ref, rev_scr):
    n = ab_ref.shape[1]
    nh = n // 2
    tm, fw = o_ref.shape[1], o_ref.shape[2]

    @pl.when(pl.program_id(1) == 0)
    def _():
        for m in range(nh // LANES):
            blk = ab_ref[0, n - LANES * (m + 1):n - LANES * m, :]
            rev_scr[LANES * m:LANES * (m + 1), :] = _dot(rev_ref[...], blk)
        ck = min(256, nh)
        row = lax.broadcasted_iota(jnp.int32, (ck, 1), 0)
        for r0 in range(0, nh, ck):
            prev = (r0 - 1) % nh
            shifted = jnp.where(row == 0, rev_scr[prev:prev + 1, :], pltpu.roll(rev_scr[r0:r0 + ck, :], 1, 0))
            lo = ab_ref[0, r0:r0 + ck, :].astype(F32)
            a2 = lo[:, :fw] + shifted[:, :fw]
            b2 = lo[:, fw:] - shifted[:, fw:]
            if r0 == 0:
                a2 = jnp.where(row == 0, lo[:, :fw], a2)
                b2 = jnp.where(row == 0, shifted[:, :fw], b2)
            fold_ref[r0:r0 + ck, :fw] = a2.astype(BF16)
            fold_ref[r0:r0 + ck, fw:] = b2.astype(BF16)

    cb = cb_ref[...]
    sb = sb_ref[...]
    sub = min(ROW_SUB, tm)
    for r0 in range(0, tm, sub):
        cm, sm = [], []
        for a in range(r0 // DFT_SUB, (r0 + sub) // DFT_SUB):
            ca = ca_ref[a:a + 1, :]
            sa = sa_ref[a:a + 1, :]
            cm.append((ca * cb - sa * sb).astype(BF16))
            sm.append((sa * cb + ca * sb).astype(BF16))
        y = _dot(jnp.concatenate(cm, axis=0), fold_ref[:, :fw]) - _dot(jnp.concatenate(sm, axis=0), fold_ref[:, fw:])
        o_ref[0, r0:r0 + sub] = y.astype(BF16)


def _fourier_call(ab, ca, sa, cb, sb, *, tm):
    b, n, w2 = ab.shape
    nh = n // 2
    fw = w2 // 2
    na = tm // DFT_SUB
    rev = jnp.asarray(np.eye(LANES, dtype=np.float32)[::-1], BF16)
    return pl.pallas_call(
        _fourier_kernel,
        grid=(b, n // tm),
        in_specs=[
            pl.BlockSpec((1, n, w2), lambda i, j: (i, 0, 0)),
            pl.BlockSpec((na, nh), lambda i, j: (j, 0)),
            pl.BlockSpec((na, nh), lambda i, j: (j, 0)),
            pl.BlockSpec((DFT_SUB, nh), lambda i, j: (0, 0)),
            pl.BlockSpec((DFT_SUB, nh), lambda i, j: (0, 0)),
            pl.BlockSpec((LANES, LANES), lambda i, j: (0, 0)),
        ],
        out_specs=pl.BlockSpec((1, tm, fw), lambda i, j: (i, j, 0)),
        out_shape=jax.ShapeDtypeStruct((b, n, fw), BF16),
        scratch_shapes=[pltpu.VMEM((nh, w2), BF16), pltpu.VMEM((nh, w2), F32)],
        compiler_params=_cparams(2, VMEM_LIMIT),
        name="fourier",
    )(ab, ca, sa, cb, sb, rev)


def _dft_tables(n):
    nh = n // 2
    k = jnp.arange(nh, dtype=jnp.int32)
    a = jnp.arange(n // DFT_SUB, dtype=jnp.int32)
    bb = jnp.arange(DFT_SUB, dtype=jnp.int32)
    ang_a = ((DFT_SUB * a[:, None] * k[None, :]) % n).astype(F32) * (2.0 * math.pi / n)
    ang_b = ((bb[:, None] * k[None, :]) % n).astype(F32) * (2.0 * math.pi / n)
    s = float(n) ** -0.5
    nyq = -jnp.cos(math.pi * bb.astype(F32)) * s
    sin_b = jnp.concatenate([nyq[:, None], (jnp.sin(ang_b) * s)[:, 1:]], axis=1)
    return jnp.cos(ang_a), jnp.sin(ang_a), jnp.cos(ang_b) * s, sin_b


def _channel_dft():
    c = jnp.arange(LANES, dtype=jnp.int32)
    ang = ((c[:, None] * c[None, :]) % LANES).astype(F32) * (2.0 * math.pi / LANES)
    s = float(LANES) ** -0.5
    return jnp.concatenate([jnp.cos(ang) * s, jnp.sin(ang) * s], axis=1)


def _postmix_kernel(*refs, n_in):
    a_refs = refs[:n_in]
    w_refs = refs[n_in:2 * n_in]
    x_ref, mod_ref, g_ref, rw_ref, x1_ref, h3_ref, aff_ref = refs[2 * n_in:]
    tm, d = x_ref.shape[1], x_ref.shape[2]
    sub = min(ROW_SUB, tm)
    p = d // 2 // LANES
    m = mod_ref[0]
    for r0 in range(0, tm, sub):
        rs = slice(r0, r0 + sub)
        y = None
        for a_ref, w_ref in zip(a_refs, w_refs):
            t = _dot(a_ref[0, rs], w_ref[...])
            y = t if y is None else y + t
        x1 = x_ref[0, rs] + m[2:3] * y
        x1_ref[0, rs] = x1
        h2 = _rms_mod(x1, g_ref[...], m[3:4], m[4:5])
        hh, hl = _split(h2)
        parts = _dot_t(rw_ref[...], jnp.concatenate([hh, hl], axis=1))
        lt = parts[:N_EXPERTS] + parts[N_EXPERTS:2 * N_EXPERTS]
        ex = jnp.exp(lt - jnp.max(lt, axis=0, keepdims=True))
        aff_ref[0, :, rs] = ex / jnp.sum(ex, axis=0, keepdims=True)
        bits = pltpu.bitcast(hh.astype(F32), jnp.uint32)
        packed = (bits[:, d // 2:] & jnp.uint32(0xFFFF0000)) | (bits[:, :d // 2] >> 16)
        for jj in range(p):
            h3_ref[0, pl.ds(r0 * p + jj, sub, stride=p), :] = packed[:, LANES * jj:LANES * (jj + 1)]


def _postmix_call(a_list, w_list, x, mod, gain, rw, *, tm):
    b, n, d = x.shape
    mb = mod.shape[0]
    n_in = len(a_list)
    p = d // 2 // LANES
    mod_map = (lambda i, j: (i, 0, 0)) if mb > 1 else (lambda i, j: (0, 0, 0))
    tok = lambda i, j: (i, j, 0)
    const2 = lambda i, j: (0, 0)
    in_specs = [pl.BlockSpec((1, tm, a.shape[2]), tok) for a in a_list]
    in_specs += [pl.BlockSpec(w.shape, const2) for w in w_list]
    in_specs += [
        pl.BlockSpec((1, tm, d), tok),
        pl.BlockSpec((1, 6, d), mod_map),
        pl.BlockSpec((1, d), const2),
        pl.BlockSpec((LANES, 2 * d), const2),
    ]
    return pl.pallas_call(
        functools.partial(_postmix_kernel, n_in=n_in),
        grid=(b, n // tm),
        in_specs=in_specs,
        out_specs=[
            pl.BlockSpec((1, tm, d), tok),
            pl.BlockSpec((1, tm * p, LANES), tok),
            pl.BlockSpec((1, N_EXPERTS, tm), lambda i, j: (i, 0, j)),
        ],
        out_shape=[
            jax.ShapeDtypeStruct((b, n, d), F32),
            jax.ShapeDtypeStruct((b, n * p, LANES), jnp.uint32),
            jax.ShapeDtypeStruct((b, N_EXPERTS, n), F32),
        ],
        compiler_params=_cparams(2, VMEM_LIMIT),
        name="postmix",
    )(*a_list, *w_list, x, mod, gain, rw)


def _prefix_incl(ones, tri):
    carry = jnp.zeros((ones.shape[0], 1), F32)
    outs = []
    for c in range(ones.shape[1] // LANES):
        blk = ones[:, LANES * c:LANES * (c + 1)]
        outs.append(_dot(blk.astype(BF16), tri) + carry)
        carry = carry + jnp.sum(blk, axis=1, keepdims=True)
    return jnp.concatenate(outs, axis=1)


def _topk_kernel(aff_ref, tri_ref, idx_ref, gate_ref, *, cap):
    n = aff_ref.shape[2]
    aff = aff_ref[0]
    bits = pltpu.bitcast(aff, jnp.int32)
    capf = jnp.float32(cap)

    def body(it, lo):
        t = lo | jnp.left_shift(jnp.int32(1), 30 - it)
        cnt = jnp.sum(jnp.where(bits >= t, 1.0, 0.0), axis=1, keepdims=True)
        return jnp.where(cnt >= capf, t, lo)

    thr = lax.fori_loop(0, 31, body, jnp.zeros((N_EXPERTS, 1), jnp.int32))
    gt = jnp.where(bits > thr, 1.0, 0.0)
    eq = jnp.where(bits == thr, 1.0, 0.0)
    need = capf - jnp.sum(gt, axis=1, keepdims=True)
    tri = tri_ref[...]
    eq_before = _prefix_incl(eq, tri) - eq
    sel = gt + eq * jnp.where(eq_before < need, 1.0, 0.0)
    slot = _prefix_incl(sel, tri).astype(jnp.int32) - 1
    tok = lax.broadcasted_iota(jnp.int32, (N_EXPERTS, n), 1)
    disp = jnp.where(sel > 0.0, tok - slot, -1)
    gate = aff
    for bit in range((n - 1).bit_length()):
        step = 1 << bit
        disp_in = pltpu.roll(disp, n - step, 1)
        gate_in = pltpu.roll(gate, n - step, 1)
        move_in = jnp.logical_and(disp_in >= 0, ((disp_in >> bit) & 1) == 1)
        stay = jnp.logical_and(disp >= 0, ((disp >> bit) & 1) == 0)
        gate = jnp.where(move_in, gate_in, gate)
        disp = jnp.where(move_in, disp_in, jnp.where(stay, disp, -1))
    idx_ref[0] = (tok + disp)[:, :cap]
    gate_ref[0] = gate[:, :cap]


def _topk_call(aff_t, tri, *, cap):
    b, e, n = aff_t.shape
    return pl.pallas_call(
        functools.partial(_topk_kernel, cap=cap),
        grid=(b,),
        in_specs=[
            pl.BlockSpec((1, e, n), lambda i: (i, 0, 0)),
            pl.BlockSpec((LANES, LANES), lambda i: (0, 0)),
        ],
        out_specs=[pl.BlockSpec((1, e, cap), lambda i: (i, 0, 0))] * 2,
        out_shape=[jax.ShapeDtypeStruct((b, e, cap), jnp.int32), jax.ShapeDtypeStruct((b, e, cap), F32)],
        compiler_params=_cparams(1, VMEM_LIMIT),
        name="topk",
    )(aff_t, tri)


def _moe_kernel(idx_a_ref, idx_c_ref, gate_c_ref, h3_ref, wg_ref, wu_ref, wd_ref, o_ref, xs_0, xs_1, y_0, y_1,
                *, cap, p, c8, ne, n_tiles):
    g = pl.program_id(0)
    valid_a = g < n_tiles
    valid_b = jnp.logical_and(g >= 1, g <= n_tiles)
    valid_c = g >= 2
    steady = jnp.logical_and(g >= 2, g < n_tiles)

    @pl.when(jnp.logical_and(valid_c, (g - 2) % ne == 0))
    def _():
        o_ref[...] = jnp.zeros(o_ref.shape, o_ref.dtype)

    unroll, group = 8, 4

    def gather_row(xs_a, s, dst):
        t = idx_a_ref[0, 0, s]
        xs_a[dst, :] = h3_ref[0, pl.ds(pl.multiple_of(t * p, p), p), :]

    def gather(xs_a, straight):
        if straight:
            for s in range(cap):
                gather_row(xs_a, s, pl.ds(s * p, p))
        else:
            def body(i, carry):
                for u in range(unroll):
                    s = i * unroll + u
                    gather_row(xs_a, s, pl.ds(pl.multiple_of(s * p, p), p))
                return carry
            lax.fori_loop(0, cap // unroll, body, 0)

    def scatter_rows(y_c, s_list, src_list):
        sums = []
        for s, src in zip(s_list, src_list):
            t = idx_c_ref[0, 0, s]
            rows = pl.ds(pl.multiple_of(t * c8, c8), c8)
            sums.append((rows, o_ref[0, rows, :] + y_c[src, :] * gate_c_ref[0, 0, s]))
        for rows, val in sums:
            o_ref[0, rows, :] = val

    def scatter(y_c, straight):
        if straight:
            for g0 in range(0, cap, group):
                ss = [g0 + u for u in range(group)]
                scatter_rows(y_c, ss, [pl.ds(s * c8, c8) for s in ss])
        else:
            def body(i, carry):
                ss = [i * group + u for u in range(group)]
                scatter_rows(y_c, ss, [pl.ds(pl.multiple_of(s * c8, c8), c8) for s in ss])
                return carry
            lax.fori_loop(0, cap // group, body, 0)

    def load_rows(xs_b):
        lo, hi = [], []
        for j in range(p):
            w = xs_b[pl.ds(j, cap, stride=p), :]
            lo.append(pltpu.bitcast(w << 16, F32).astype(BF16))
            hi.append(pltpu.bitcast(w & jnp.uint32(0xFFFF0000), F32).astype(BF16))
        return jnp.concatenate(lo + hi, axis=1)

    def expert(xs, y_b):
        a = _dot(xs, wg_ref[0, 0])
        u_ = _dot(xs, wu_ref[0, 0])
        act = (_silu(a) * u_).astype(BF16)
        y = _dot(act, wd_ref[0, 0])
        for j in range(c8):
            y_b[pl.ds(j, cap, stride=c8), :] = y[:, LANES * j:LANES * (j + 1)]

    def step(xs_a, xs_b, y_b, y_c):
        @pl.when(steady)
        def _():
            xs = load_rows(xs_b)
            gather(xs_a, True)
            scatter(y_c, True)
            expert(xs, y_b)

        @pl.when(jnp.logical_not(steady))
        def _():
            @pl.when(valid_b)
            def _():
                expert(load_rows(xs_b), y_b)

            @pl.when(valid_a)
            def _():
                gather(xs_a, False)

            @pl.when(valid_c)
            def _():
                scatter(y_c, False)

    @pl.when(g % 2 == 0)
    def _():
        step(xs_0, xs_1, y_1, y_0)

    @pl.when(g % 2 == 1)
    def _():
        step(xs_1, xs_0, y_0, y_1)


def _moe_call(idx, h3, gate, wg, wu, wd, layer):
    bm, ne, cap = idx.shape
    d = wg.shape[2]
    p = d // 2 // LANES
    c8 = d // LANES
    n = h3.shape[1] // p
    n_tiles = bm * ne
    idx3 = idx.reshape(n_tiles, 1, cap)
    gate3 = gate.reshape(n_tiles, 1, cap)
    tile_a = lambda i: jnp.minimum(i, n_tiles - 1)
    tile_b = lambda i: jnp.clip(i - 1, 0, n_tiles - 1)
    tile_c = lambda i: jnp.clip(i - 2, 0, n_tiles - 1)
    wmap = lambda i: (layer, tile_b(i) % ne, 0, 0)
    smem = lambda f: pl.BlockSpec((1, 1, cap), lambda i: (f(i), 0, 0), memory_space=pltpu.SMEM)
    return pl.pallas_call(
        functools.partial(_moe_kernel, cap=cap, p=p, c8=c8, ne=ne, n_tiles=n_tiles),
        grid=(n_tiles + 2,),
        in_specs=[
            smem(tile_a), smem(tile_c), smem(tile_c),
            pl.BlockSpec((1, n * p, LANES), lambda i: (tile_a(i) // ne, 0, 0), pipeline_mode=pl.Buffered(1)),
            pl.BlockSpec((1, 1, d, wg.shape[3]), wmap),
            pl.BlockSpec((1, 1, d, wu.shape[3]), wmap),
            pl.BlockSpec((1, 1, wd.shape[2], d), wmap),
        ],
        out_specs=pl.BlockSpec((1, n * c8, LANES), lambda i: (tile_c(i) // ne, 0, 0), pipeline_mode=pl.Buffered(1)),
        out_shape=jax.ShapeDtypeStruct((bm, n * c8, LANES), F32),
        scratch_shapes=[pltpu.VMEM((cap * p, LANES), jnp.uint32), pltpu.VMEM((cap * p, LANES), jnp.uint32),
                        pltpu.VMEM((cap * c8, LANES), F32), pltpu.VMEM((cap * c8, LANES), F32)],
        compiler_params=_cparams(1, VMEM_LIMIT),
        name="moe",
    )(idx3, idx3, gate3, h3, wg, wu, wd)


def _cast_kernel(*refs):
    half = len(refs) // 2
    for w_ref, o_ref in zip(refs[:half], refs[half:]):
        o_ref[...] = w_ref[...].astype(BF16)


def _cast_call(*ws):
    l, e = ws[0].shape[:2]
    spec = lambda w: pl.BlockSpec((1, 1) + w.shape[2:], lambda i, j: (i, j, 0, 0))
    return pl.pallas_call(
        _cast_kernel,
        grid=(l, e),
        in_specs=[spec(w) for w in ws],
        out_specs=[spec(w) for w in ws],
        out_shape=[jax.ShapeDtypeStruct(w.shape, BF16) for w in ws],
        compiler_params=_cparams(2, VMEM_LIMIT),
        name="cast",
    )(*ws)


def _resid_kernel(x_ref, moe_ref, mod_ref, o_ref):
    o_ref[0] = _gated_moe_residual(x_ref, moe_ref, mod_ref[0][5:6], 0, x_ref.shape[1])


def _resid_call(x1, moe3, mod, *, tm):
    b, n, d = x1.shape
    mb = mod.shape[0]
    c8 = d // LANES
    mod_map = (lambda i, j: (i, 0, 0)) if mb > 1 else (lambda i, j: (0, 0, 0))
    tok = lambda i, j: (i, j, 0)
    return pl.pallas_call(
        _resid_kernel,
        grid=(b, n // tm),
        in_specs=[
            pl.BlockSpec((1, tm, d), tok),
            pl.BlockSpec((1, tm * c8, LANES), tok),
            pl.BlockSpec((1, 6, d), mod_map),
        ],
        out_specs=pl.BlockSpec((1, tm, d), tok),
        out_shape=jax.ShapeDtypeStruct((b, n, d), F32),
        compiler_params=_cparams(2, VMEM_LIMIT),
        name="resid",
    )(x1, moe3, mod)


def _rope_tables(n):
    t = jnp.arange(n)
    row = (t // GRID_W).astype(F32)
    col = (t % GRID_W).astype(F32)
    n_freq = HEAD_DIM // 4
    inv_freq = jnp.power(ROPE_BASE, -jnp.arange(n_freq, dtype=F32) / n_freq)
    ang = jnp.concatenate([row[:, None] * inv_freq, col[:, None] * inv_freq], axis=-1)
    cos, sin = jnp.cos(ang), jnp.sin(ang)
    cos_t = jnp.concatenate([cos, cos, cos, cos], axis=-1)
    sin_t = jnp.concatenate([-sin, sin, -sin, sin], axis=-1)
    return cos_t, sin_t


def _gain128(g):
    return jnp.concatenate([g, g]).reshape(1, LANES).astype(F32)


def _channel_mix(h_pack, aff_t, tri, weights, layer, *, flatten):
    wg, wu, wd = weights
    b, _, n = aff_t.shape
    cap = EC_CAPACITY_FACTOR * n // N_EXPERTS
    idx, gate = _topk_call(aff_t, tri, cap=cap)
    if flatten:
        idx = (idx + (jnp.arange(b, dtype=jnp.int32) * n)[:, None, None]).transpose(1, 0, 2).reshape(1, N_EXPERTS, b * cap)
        gate = gate.transpose(1, 0, 2).reshape(1, N_EXPERTS, b * cap)
        h_pack = h_pack.reshape(1, -1, LANES)
    return _moe_call(idx, h_pack, gate, wg, wu, wd, layer).reshape(b, -1, LANES)


def kernel(x, c, ctx, c_ctx, ada_w, ada_b, norm1_g, norm2_g, router_w, exp_w_gate, exp_w_up, exp_w_down, ev_w_in,
           ev_w_out, ev_q_gain, ev_k_gain, ev_sink, od_w_in, od_w_out, od_q_gain, od_k_gain, od_rpb):
    b, n, d = x.shape
    nc = ctx.shape[1]
    tm = min(512, n)
    tmc = min(512, nc)

    rows = -(-(b + 1) // 8) * 8
    cs = jnp.concatenate([c, c_ctx[None], jnp.zeros((rows - b - 1, d), F32)], axis=0)
    mods = _mod_call(cs, ada_w, ada_b).reshape(ada_w.shape[0], rows, 6, d)

    cos_t, sin_t = _rope_tables(n)
    cs_dft = _channel_dft()
    tri = jnp.asarray(np.triu(np.ones((LANES, LANES), np.float32)), BF16)
    fw = FOURIER_GROUPS * LANES

    def router_split(l):
        rw = router_w[l].T
        rh = rw.astype(BF16)
        rl = (rw - rh.astype(F32)).astype(BF16)
        left = jnp.pad(jnp.concatenate([rh, rl], axis=0), ((0, LANES - 2 * N_EXPERTS), (0, 0)))
        right = jnp.pad(rh, ((0, LANES - N_EXPERTS), (0, 0)))
        return jnp.concatenate([left, right], axis=1)

    wts = _cast_call(exp_w_gate, exp_w_up, exp_w_down)

    mod_x, mod_c = mods[0, :b], mods[0, b:b + 1]
    g1 = norm1_g[0].reshape(1, d)
    g2 = norm2_g[0].reshape(1, d)
    w_in = ev_w_in[0].astype(BF16)
    w_out = ev_w_out[0].astype(BF16)
    qg, kg = _gain128(ev_q_gain[0]), _gain128(ev_k_gain[0])
    sink = ev_sink[0].astype(F32)
    rw = router_split(0)

    ab_x, q_x, kd_x, vd_x = _premix_even_call(x, mod_x, g1, w_in, cs_dft, cos_t, sin_t, qg, kg, rope=True, tm=tm)
    ab_c, q_c, kd_c, vd_c = _premix_even_call(ctx, mod_c, g1, w_in, cs_dft, cos_t[:nc], sin_t[:nc], qg, kg,
                                              rope=False, tm=tmc)
    a_x = _swa_call(sink, q_x, kd_x, vd_x, kd_c, vd_c)
    a_c = _ctx_attn_call(sink, q_c, kd_c, vd_c)
    four_x = _fourier_call(ab_x, *_dft_tables(n), tm=tm)
    four_c = _fourier_call(ab_c, *_dft_tables(nc), tm=tmc)
    w_out_parts = [w_out[:fw], w_out[fw:]]
    x1, h3_x, aff_x = _postmix_call([four_x, a_x], w_out_parts, x, mod_x, g2, rw, tm=tm)
    c1, h3_c, aff_c = _postmix_call([four_c, a_c], w_out_parts, ctx, mod_c, g2, rw, tm=tmc)
    moe_x = _channel_mix(h3_x, aff_x, tri, wts, 0, flatten=False)
    moe_c = _channel_mix(h3_c, aff_c, tri, wts, 0, flatten=True)

    mod_x0, mod_c0 = mod_x, mod_c
    mod_x, mod_c = mods[1, :b], mods[1, b:b + 1]
    g1 = norm1_g[1].reshape(1, d)
    g2 = norm2_g[1].reshape(1, d)
    w_in = od_w_in[0].astype(BF16)
    w_out = od_w_out[0].astype(BF16)
    qg, kg = _gain128(od_q_gain[0]), _gain128(od_k_gain[0])
    rw = router_split(1)

    q_x, k_x, v_x, x = _premix_odd_call(x1, moe_x, mod_x0, mod_x, g1, w_in, qg, kg, emit_x=True, tm=tm)
    _, k_c, v_c = _premix_odd_call(c1, moe_c, mod_c0, mod_c, g1, w_in, qg, kg, emit_x=False, tm=tmc)
    a_x = _na_call(q_x, k_x, v_x, k_c, v_c, _na_bias_table(od_rpb[0]))
    x1, h3_x, aff_x = _postmix_call([a_x], [w_out], x, mod_x, g2, rw, tm=tm)
    return _resid_call(x1, _channel_mix(h3_x, aff_x, tri, wts, 1, flatten=False), mod_x, tm=tm)
```

```python
import functools
import math

import numpy as np
import jax
import jax.numpy as jnp
from jax import lax
from jax.experimental import pallas as pl
from jax.experimental.pallas import tpu as pltpu

GRID_W = 64
HEAD_DIM = 64
FOURIER_GROUPS = 4
SWA_WINDOW = 128
SWA_BLOCK = 128
NA_ROWS_MAX = 8
NA_COLS = 16
N_EXPERTS = 16
EC_CAPACITY_FACTOR = 2
ROPE_BASE = 10000.0
EPS = 1e-6

LANES = 128
NEG = -1e30
NA_QROWS = 4
NA_KROWS = 12
DFT_SUB = 64
SWA_QBLK = 4
ROW_SUB = 256
ONES_ROWS = 16
LOG2E = math.log2(math.e)
VMEM_LIMIT = 60 * 1024 * 1024

F32 = jnp.float32
BF16 = jnp.bfloat16


def _cparams(n_axes, vmem=None):
    return pltpu.CompilerParams(dimension_semantics=("arbitrary",) * n_axes, vmem_limit_bytes=vmem)


def _dot(a, b):
    return jnp.dot(a, b, preferred_element_type=F32)


def _dot_t(a, b):
    return lax.dot_general(a, b, (((1,), (1,)), ((), ())), preferred_element_type=F32)


def _split(a):
    hi = a.astype(BF16)
    lo = (a - hi.astype(F32)).astype(BF16)
    return hi, lo


def _dot3(a, b):
    ah, al = _split(a)
    bh, bl = _split(b)
    return _dot(ah, bh) + _dot(al, bh) + _dot(ah, bl)


def _silu(a):
    return a / (1.0 + jnp.exp(-a))


def _rms_mod(x, gain, shift, scale):
    ms = jnp.mean(x * x, axis=-1, keepdims=True)
    y = x * lax.rsqrt(ms + EPS) * gain
    return y * (1.0 + scale) + shift


def _mod_kernel(cs_ref, w_ref, b_ref, o_ref):
    o_ref[0] = _dot3(_silu(cs_ref[...]), w_ref[0]) + b_ref[0]


def _mod_call(cs, ada_w, ada_b):
    depth, d, d6 = ada_w.shape
    r = cs.shape[0]
    tn = 1024
    return pl.pallas_call(
        _mod_kernel,
        grid=(depth, d6 // tn),
        in_specs=[
            pl.BlockSpec((r, d), lambda l, j: (0, 0)),
            pl.BlockSpec((1, d, tn), lambda l, j: (l, 0, j)),
            pl.BlockSpec((1, 1, tn), lambda l, j: (l, 0, j)),
        ],
        out_specs=pl.BlockSpec((1, r, tn), lambda l, j: (l, 0, j)),
        out_shape=jax.ShapeDtypeStruct((depth, r, d6), F32),
        compiler_params=_cparams(2),
        name="mod",
    )(cs, ada_w, ada_b.reshape(depth, 1, d6))


def _head_norm(t, gain, lane):
    t2 = t * t
    s_lo = jnp.sum(jnp.where(lane < HEAD_DIM, t2, 0.0), axis=-1, keepdims=True)
    s_all = jnp.sum(t2, axis=-1, keepdims=True)
    ms = jnp.where(lane < HEAD_DIM, s_lo, s_all - s_lo) * (1.0 / HEAD_DIM)
    return t * lax.rsqrt(ms + EPS) * gain


def _rope(t, cos_t, sin_t, lane):
    partner = jnp.where((lane % HEAD_DIM) < HEAD_DIM // 2, pltpu.roll(t, LANES - HEAD_DIM // 2, 1),
                        pltpu.roll(t, HEAD_DIM // 2, 1))
    return t * cos_t + partner * sin_t


def _dup_halves(t, lane):
    sw = pltpu.roll(t, HEAD_DIM, 1)
    return jnp.where(lane < HEAD_DIM, t, sw), jnp.where(lane < HEAD_DIM, sw, t)


def _premix_even_kernel(x_ref, mod_ref, g_ref, w_ref, cs_ref, cos_ref, sin_ref, qg_ref, kg_ref,
                        ab_ref, q_ref, kd_ref, vt_ref, *, rope):
    tm = x_ref.shape[1]
    sub = min(ROW_SUB, tm)
    m = mod_ref[0]
    lane = lax.broadcasted_iota(jnp.int32, (sub, LANES), 1)
    fw = FOURIER_GROUPS * LANES
    qw = q_ref.shape[2]
    csb = cs_ref[...].astype(BF16)
    for r0 in range(0, tm, sub):
        rs = slice(r0, r0 + sub)
        hb = _rms_mod(x_ref[0, rs], g_ref[...], m[0:1], m[1:2]).astype(BF16)
        pf = _dot(hb, w_ref[:, 0:fw])
        for g in range(FOURIER_GROUPS):
            ab = _dot(pf[:, LANES * g:LANES * (g + 1)].astype(BF16), csb)
            ab_ref[0, rs, LANES * g:LANES * (g + 1)] = ab[:, :LANES].astype(BF16)
            ab_ref[0, rs, fw + LANES * g:fw + LANES * (g + 1)] = ab[:, LANES:].astype(BF16)
        pq = _dot(hb, w_ref[:, fw:fw + qw])
        if rope:
            cos_t = cos_ref[rs]
            sin_t = sin_ref[rs]
        for c in range(qw // LANES):
            t = _head_norm(pq[:, LANES * c:LANES * (c + 1)], qg_ref[...], lane)
            if rope:
                t = _rope(t, cos_t, sin_t, lane)
            q_ref[0, rs, LANES * c:LANES * (c + 1)] = t.astype(BF16)
        pk = _dot(hb, w_ref[:, fw + qw:fw + qw + 2 * LANES])
        k = _head_norm(pk[:, :LANES], kg_ref[...], lane)
        if rope:
            k = _rope(k, cos_t, sin_t, lane)
        k0, k1 = _dup_halves(k, lane)
        kd_ref[0, rs, :LANES] = k0.astype(BF16)
        kd_ref[0, rs, LANES:] = k1.astype(BF16)
        vt_ref[0, :, rs] = pk[:, LANES:].T.astype(BF16)


def _premix_even_call(x, mod, gain, w_bf, cs, cos_t, sin_t, qg, kg, *, rope, tm):
    b, n, d = x.shape
    mb = mod.shape[0]
    wtot = w_bf.shape[1]
    fw = FOURIER_GROUPS * LANES
    qw = wtot - fw - 2 * LANES
    mod_map = (lambda i, j: (i, 0, 0)) if mb > 1 else (lambda i, j: (0, 0, 0))
    tok = lambda i, j: (i, j, 0)
    const2 = lambda i, j: (0, 0)
    return pl.pallas_call(
        functools.partial(_premix_even_kernel, rope=rope),
        grid=(b, n // tm),
        in_specs=[
            pl.BlockSpec((1, tm, d), tok),
            pl.BlockSpec((1, 6, d), mod_map),
            pl.BlockSpec((1, d), const2),
            pl.BlockSpec((d, wtot), const2),
            pl.BlockSpec((LANES, 2 * LANES), const2),
            pl.BlockSpec((tm, LANES), lambda i, j: (j, 0)),
            pl.BlockSpec((tm, LANES), lambda i, j: (j, 0)),
            pl.BlockSpec((1, LANES), const2),
            pl.BlockSpec((1, LANES), const2),
        ],
        out_specs=[
            pl.BlockSpec((1, tm, 2 * fw), tok),
            pl.BlockSpec((1, tm, qw), tok),
            pl.BlockSpec((1, tm, 2 * LANES), tok),
            pl.BlockSpec((1, LANES, tm), lambda i, j: (i, 0, j)),
        ],
        out_shape=[
            jax.ShapeDtypeStruct((b, n, 2 * fw), BF16),
            jax.ShapeDtypeStruct((b, n, qw), BF16),
            jax.ShapeDtypeStruct((b, n, 2 * LANES), BF16),
            jax.ShapeDtypeStruct((b, LANES, n), BF16),
        ],
        compiler_params=_cparams(2, VMEM_LIMIT),
        name="premix_even",
    )(x, mod, gain, w_bf, cs, cos_t, sin_t, qg, kg)


def _gated_moe_residual(x_ref, moe_ref, gate_row, r0, sub):
    c8 = x_ref.shape[2] // LANES
    cols = []
    for j in range(c8):
        cs = slice(LANES * j, LANES * (j + 1))
        cols.append(x_ref[0, r0:r0 + sub, cs] + gate_row[:, cs] * moe_ref[0, pl.ds(r0 * c8 + j, sub, stride=c8), :])
    return jnp.concatenate(cols, axis=1)


def _premix_odd_kernel(x_ref, moe_ref, modp_ref, mod_ref, g_ref, w_ref, qg_ref, kg_ref, q_ref, k_ref, v_ref,
                       *xo_ref):
    tm = x_ref.shape[1]
    sub = min(ROW_SUB, tm)
    m = mod_ref[0]
    gate_prev = modp_ref[0][5:6]
    lane = lax.broadcasted_iota(jnp.int32, (sub, LANES), 1)
    wq = q_ref.shape[2]
    chunk = 4 * LANES
    for r0 in range(0, tm, sub):
        rs = slice(r0, r0 + sub)
        x = _gated_moe_residual(x_ref, moe_ref, gate_prev, r0, sub)
        if xo_ref:
            xo_ref[0][0, rs] = x
        hb = _rms_mod(x, g_ref[...], m[0:1], m[1:2]).astype(BF16)
        for c0 in range(0, 3 * wq, chunk):
            p = _dot(hb, w_ref[:, c0:c0 + chunk])
            for cc in range(chunk // LANES):
                col = c0 + cc * LANES
                t = p[:, cc * LANES:(cc + 1) * LANES]
                if col < wq:
                    q_ref[0, rs, col:col + LANES] = _head_norm(t, qg_ref[...], lane).astype(BF16)
                elif col < 2 * wq:
                    k_ref[0, rs, col - wq:col - wq + LANES] = _head_norm(t, kg_ref[...], lane).astype(BF16)
                else:
                    v_ref[0, col - 2 * wq:col - 2 * wq + LANES, rs] = t.T.astype(BF16)


def _premix_odd_call(x1, moe3, mod_prev, mod, gain, w_bf, qg, kg, *, emit_x, tm):
    b, n, d = x1.shape
    mb = mod.shape[0]
    wq = w_bf.shape[1] // 3
    c8 = d // LANES
    mod_map = (lambda i, j: (i, 0, 0)) if mb > 1 else (lambda i, j: (0, 0, 0))
    tok = lambda i, j: (i, j, 0)
    const2 = lambda i, j: (0, 0)
    out_specs = [pl.BlockSpec((1, tm, wq), tok)] * 2 + [pl.BlockSpec((1, wq, tm), lambda i, j: (i, 0, j))]
    out_shape = [jax.ShapeDtypeStruct((b, n, wq), BF16)] * 2 + [jax.ShapeDtypeStruct((b, wq, n), BF16)]
    if emit_x:
        out_specs.append(pl.BlockSpec((1, tm, d), tok))
        out_shape.append(jax.ShapeDtypeStruct((b, n, d), F32))
    return pl.pallas_call(
        _premix_odd_kernel,
        grid=(b, n // tm),
        in_specs=[
            pl.BlockSpec((1, tm, d), tok),
            pl.BlockSpec((1, tm * c8, LANES), tok),
            pl.BlockSpec((1, 6, d), mod_map),
            pl.BlockSpec((1, 6, d), mod_map),
            pl.BlockSpec((1, d), const2),
            pl.BlockSpec((d, 3 * wq), const2),
            pl.BlockSpec((1, LANES), const2),
            pl.BlockSpec((1, LANES), const2),
        ],
        out_specs=out_specs,
        out_shape=out_shape,
        compiler_params=_cparams(2, VMEM_LIMIT),
        name="premix_odd",
    )(x1, moe3, mod_prev, mod, gain, w_bf, qg, kg)


def _col_reduce(x, op, slab=64):
    r = x.shape[0]
    if r > slab and r % slab == 0:
        x = op(x.reshape(r // slab, slab, x.shape[1]), axis=0)
    return op(x, axis=0, keepdims=True)


def _softmax_pv_t(s_t, v_t, sink=None):
    mx = _col_reduce(s_t, jnp.max)
    if sink is not None:
        mx = jnp.maximum(mx, sink)
    p = jnp.exp2((s_t - mx).astype(BF16))
    nd = v_t.shape[0]
    o = _dot(jnp.concatenate([v_t, jnp.ones((ONES_ROWS, v_t.shape[1]), BF16)], axis=0), p)
    den = o[nd:nd + 1]
    if sink is not None:
        den = den + jnp.exp2(sink - mx)
    return o[:nd] / den


def _mask_half(qc, lane, half):
    q32 = qc.astype(F32) * (HEAD_DIM ** -0.5 * LOG2E)
    keep = (lane < HEAD_DIM) if half == 0 else (lane >= HEAD_DIM)
    return jnp.where(keep, q32, 0.0).astype(BF16)


def _gqa_scores(q_ref, h, kd, n_kv, masks, lane, r0, tq):
    rs = slice(r0, r0 + tq)
    cols_per_kv = q_ref.shape[2] // LANES // n_kv
    q_rows = []
    for cc in range(cols_per_kv):
        c = cols_per_kv * h + cc
        for half in range(2):
            q_rows.append(_mask_half(q_ref[0, rs, LANES * c:LANES * (c + 1)], lane, half))
    qm = jnp.concatenate(q_rows, axis=0)
    s = _dot_t(kd, qm)
    pieces, k0 = [], 0
    for rows, mask in masks:
        blk = s[k0:k0 + rows]
        pieces.append(blk if mask is None else jnp.where(mask, blk, NEG))
        k0 += rows
    return jnp.concatenate(pieces, axis=0) if len(pieces) > 1 else pieces[0]


def _gqa_finish(sink_ref, o_ref, h, s_t, v_t, r0, tq):
    rs = slice(r0, r0 + tq)
    n_kv = v_t.shape[0] // HEAD_DIM
    cols_per_kv = o_ref.shape[2] // LANES // n_kv
    heads = 2 * cols_per_kv
    head_of_lane = lax.broadcasted_iota(jnp.int32, (1, heads * tq), 1) // tq
    sink = jnp.zeros((1, heads * tq), F32)
    for a in range(heads):
        sink = jnp.where(head_of_lane == a, sink_ref[heads * h + a] * LOG2E, sink)
    o_t = _softmax_pv_t(s_t, v_t, sink)[HEAD_DIM * h:HEAD_DIM * (h + 1)]
    for cc in range(cols_per_kv):
        c = cols_per_kv * h + cc
        tile = jnp.concatenate([o_t[:, tq * (2 * cc):tq * (2 * cc + 1)], o_t[:, tq * (2 * cc + 1):tq * (2 * cc + 2)]],
                               axis=0)
        o_ref[0, rs, LANES * c:LANES * (c + 1)] = tile.T.astype(BF16)


def _swa_kernel(sink_ref, q_ref, km_ref, k0_ref, kp_ref, vm_ref, v0_ref, vp_ref, kc_ref, vc_ref, o_ref, *, n_step):
    g = pl.program_id(1)
    tq = SWA_BLOCK
    nc = kc_ref.shape[1]
    n_kv = kc_ref.shape[2] // LANES
    heads = q_ref.shape[2] // HEAD_DIM // n_kv
    lane = lax.broadcasted_iota(jnp.int32, (tq, LANES), 1)
    jk = lax.broadcasted_iota(jnp.int32, (SWA_BLOCK, heads * tq), 0)
    iq = lax.broadcasted_iota(jnp.int32, (SWA_BLOCK, heads * tq), 1) % tq
    below, above = jk >= iq, jk <= iq
    nq = q_ref.shape[1] // SWA_BLOCK
    blk = lambda s: slice(SWA_BLOCK * s, SWA_BLOCK * (s + 1))

    def key_block(s, hs):
        return km_ref[0, :, hs] if s < 0 else kp_ref[0, :, hs] if s == nq else k0_ref[0, blk(s), hs]

    def value_block(s):
        return vm_ref[0] if s < 0 else vp_ref[0] if s == nq else v0_ref[0, :, blk(s)]

    def scores(sub, h):
        hs = slice(LANES * h, LANES * (h + 1))
        first = jnp.logical_and(below, g > 0) if sub == 0 else below
        last = jnp.logical_and(above, g < n_step - 1) if sub == nq - 1 else above
        masks = [(nc, None), (SWA_BLOCK, first), (SWA_BLOCK, None), (SWA_BLOCK, last)]
        kd = jnp.concatenate([kc_ref[0, :, hs]] + [key_block(s, hs) for s in (sub - 1, sub, sub + 1)], axis=0)
        return _gqa_scores(q_ref, h, kd, n_kv, masks, lane, SWA_BLOCK * sub, tq)

    groups = [(sub, h) for sub in range(nq) for h in range(n_kv)]
    ahead = 3
    pending = [scores(*grp) for grp in groups[:ahead]]
    for i, (sub, h) in enumerate(groups):
        s_t = pending.pop(0)
        if i + ahead < len(groups):
            pending.append(scores(*groups[i + ahead]))
        v_t = jnp.concatenate([vc_ref[0]] + [value_block(s) for s in (sub - 1, sub, sub + 1)], axis=1)
        _gqa_finish(sink_ref, o_ref, h, s_t, v_t, SWA_BLOCK * sub, tq)


def _swa_call(sink, q, kd, vt, kdc, vtc):
    b, n, qw = q.shape
    nc = kdc.shape[1]
    kw = kd.shape[2]
    vw = vt.shape[1]
    n_blk = n // SWA_BLOCK
    nq = min(SWA_QBLK, n_blk)
    n_step = n_blk // nq
    prev = lambda j: jnp.maximum(nq * j - 1, 0)
    nxt = lambda j: jnp.minimum(nq * j + nq, n_blk - 1)
    kspec = lambda f: pl.BlockSpec((1, SWA_BLOCK, kw), lambda i, j: (i, f(j), 0))
    vspec = lambda f: pl.BlockSpec((1, vw, SWA_BLOCK), lambda i, j: (i, 0, f(j)))
    return pl.pallas_call(
        functools.partial(_swa_kernel, n_step=n_step),
        grid=(b, n_step),
        in_specs=[
            pl.BlockSpec(memory_space=pltpu.SMEM),
            pl.BlockSpec((1, nq * SWA_BLOCK, qw), lambda i, j: (i, j, 0)),
            kspec(prev), pl.BlockSpec((1, nq * SWA_BLOCK, kw), lambda i, j: (i, j, 0)), kspec(nxt),
            vspec(prev), pl.BlockSpec((1, vw, nq * SWA_BLOCK), lambda i, j: (i, 0, j)), vspec(nxt),
            pl.BlockSpec((1, nc, kw), lambda i, j: (i, 0, 0)),
            pl.BlockSpec((1, vw, nc), lambda i, j: (i, 0, 0)),
        ],
        out_specs=pl.BlockSpec((1, nq * SWA_BLOCK, qw), lambda i, j: (i, j, 0)),
        out_shape=jax.ShapeDtypeStruct((b, n, qw), BF16),
        compiler_params=_cparams(2, VMEM_LIMIT),
        name="swa",
    )(sink, q, kd, kd, kd, vt, vt, vt, kdc, vtc)


def _ctx_attn_kernel(sink_ref, q_ref, kc_ref, vc_ref, o_ref):
    tq = q_ref.shape[1]
    lane = lax.broadcasted_iota(jnp.int32, (tq, LANES), 1)
    n_kv = kc_ref.shape[2] // LANES
    scores = [_gqa_scores(q_ref, h, kc_ref[0, :, LANES * h:LANES * (h + 1)], n_kv, [(kc_ref.shape[1], None)], lane,
                          0, tq) for h in range(n_kv)]
    for h, s_t in enumerate(scores):
        _gqa_finish(sink_ref, o_ref, h, s_t, vc_ref[0], 0, tq)


def _ctx_attn_call(sink, q, kd, vt):
    b, nc, qw = q.shape
    kw = kd.shape[2]
    vw = vt.shape[1]
    m = lambda i: (i, 0, 0)
    return pl.pallas_call(
        _ctx_attn_kernel,
        grid=(b,),
        in_specs=[
            pl.BlockSpec(memory_space=pltpu.SMEM),
            pl.BlockSpec((1, nc, qw), m),
            pl.BlockSpec((1, nc, kw), m),
            pl.BlockSpec((1, vw, nc), m),
        ],
        out_specs=pl.BlockSpec((1, nc, qw), m),
        out_shape=jax.ShapeDtypeStruct((b, nc, qw), BF16),
        compiler_params=_cparams(1, VMEM_LIMIT),
        name="ctx_attn",
    )(sink, q, kd, vt)


def _na_kernel(q_ref, k0_ref, k1_ref, k2_ref, v0_ref, v1_ref, v2_ref, kc_ref, vc_ref, tb_ref, o_ref, *, n_j, rows):
    j = pl.program_id(1)
    tq = q_ref.shape[1]
    tk = 3 * k0_ref.shape[1]
    nc = kc_ref.shape[1]
    w0 = NA_QROWS * jnp.clip(j - 1, 0, n_j - 3)
    lane = lax.broadcasted_iota(jnp.int32, (tq, LANES), 1)
    k_row = (lax.broadcasted_iota(jnp.int32, (nc + tk, LANES), 0) - nc) // GRID_W
    k_lane = lax.broadcasted_iota(jnp.int32, (nc + tk, LANES), 1)
    lo = jnp.zeros((nc + tk, LANES), jnp.int32)
    for rr in range(NA_QROWS):
        r = NA_QROWS * j + rr
        lo_rr = jnp.clip(r - NA_ROWS_MAX // 2, 0, rows - NA_ROWS_MAX) - w0
        lo = jnp.where(k_lane == rr, lo_rr, lo)
    outside = jnp.logical_or(k_row < lo, k_row >= lo + NA_ROWS_MAX)
    is_local = lax.broadcasted_iota(jnp.int32, (nc + tk, LANES), 0) >= nc
    k_extra = jnp.where(jnp.logical_and(jnp.logical_and(is_local, k_lane < NA_QROWS), outside), NEG, 0.0).astype(BF16)
    q_grid_row = (lax.broadcasted_iota(jnp.int32, (2 * tq, LANES), 0) % tq) // GRID_W
    q_extra = jnp.where(lax.broadcasted_iota(jnp.int32, (2 * tq, LANES), 1) == q_grid_row, 1.0, 0.0).astype(BF16)
    delta = w0 - NA_QROWS * j
    n_dr = 2 * NA_ROWS_MAX - 1
    def scores(c):
        cs = slice(LANES * c, LANES * (c + 1))
        qc = q_ref[0, :, cs]
        kl = jnp.concatenate([kc_ref[0, :, cs], k0_ref[0, :, cs], k1_ref[0, :, cs], k2_ref[0, :, cs]], axis=0)
        qm = jnp.concatenate([_mask_half(qc, lane, 0), _mask_half(qc, lane, 1)], axis=0)
        return _dot_t(jnp.concatenate([kl, k_extra], axis=1), jnp.concatenate([qm, q_extra], axis=1))

    n_cols = q_ref.shape[2] // LANES
    ahead = 1
    pending = [scores(c) for c in range(min(ahead, n_cols))]
    for c in range(n_cols):
        cs = slice(LANES * c, LANES * (c + 1))
        s = pending.pop(0)
        if c + ahead < n_cols:
            pending.append(scores(c + ahead))
        v_t = jnp.concatenate([vc_ref[0, cs, :], v0_ref[0, cs, :], v1_ref[0, cs, :], v2_ref[0, cs, :]], axis=1)
        bias_rows = []
        for kr in range(NA_KROWS):
            pieces = []
            for half in range(2):
                for t in range(NA_QROWS // 2):
                    dr = delta + kr - 2 * t + NA_ROWS_MAX - 1
                    pieces.append(tb_ref[2 * c + half, jnp.clip(dr, 0, n_dr)])
            bias_rows.append(jnp.concatenate(pieces, axis=1))
        bias = jnp.concatenate(bias_rows, axis=0)
        s_t = jnp.concatenate([s[:nc], s[nc:] + bias], axis=0)
        o_t = _softmax_pv_t(s_t, v_t)
        tile = jnp.concatenate([o_t[:HEAD_DIM, :tq], o_t[HEAD_DIM:, tq:]], axis=0)
        o_ref[0, :, cs] = tile.T.astype(BF16)


def _na_call(q, k, vt, kc, vtc, tb):
    b, n, w = q.shape
    nc = kc.shape[1]
    rows = n // GRID_W
    n_j = rows // NA_QROWS
    tq = NA_QROWS * GRID_W
    cur = lambda i, j: (i, j, 0)
    first = lambda j: jnp.clip(j - 1, 0, n_j - 3)
    kspec = lambda d: pl.BlockSpec((1, tq, w), lambda i, j: (i, first(j) + d, 0))
    vspec = lambda d: pl.BlockSpec((1, w, tq), lambda i, j: (i, 0, first(j) + d))
    return pl.pallas_call(
        functools.partial(_na_kernel, n_j=n_j, rows=rows),
        grid=(b, n_j),
        in_specs=[
            pl.BlockSpec((1, tq, w), cur),
            kspec(0), kspec(1), kspec(2),
            vspec(0), vspec(1), vspec(2),
            pl.BlockSpec((1, nc, w), lambda i, j: (i, 0, 0)),
            pl.BlockSpec((1, w, nc), lambda i, j: (i, 0, 0)),
            pl.BlockSpec(tb.shape, lambda i, j: (0, 0, 0, 0)),
        ],
        out_specs=pl.BlockSpec((1, tq, w), cur),
        out_shape=jax.ShapeDtypeStruct((b, n, w), BF16),
        compiler_params=_cparams(2, VMEM_LIMIT),
        name="na",
    )(q, k, k, k, vt, vt, vt, kc, vtc, tb)


def _na_bias_table(rpb):
    col_q = np.arange(GRID_W)[None, :]
    col_k = np.arange(GRID_W)[:, None]
    c_start = np.clip(col_q - NA_COLS // 2, 0, GRID_W - NA_COLS)
    col_valid = (col_k >= c_start) & (col_k < c_start + NA_COLS)
    dc_idx = np.clip(col_k - col_q + NA_COLS - 1, 0, 2 * NA_COLS - 2)
    t = jnp.where(col_valid[None, None], (rpb.astype(F32) * LOG2E)[:, :, dc_idx], NEG)
    pad = jnp.full_like(t[:, :1], NEG)
    t = jnp.concatenate([pad, t, pad], axis=1)
    return jnp.concatenate([t[:, 1:], t[:, :-1]], axis=-1)


def _fourier_kernel(ab_ref, ca_ref, sa_ref, cb_ref, sb_ref, rev_ref, o_ref, fold_ref, rev_scr):
    n = ab_ref.shape[1]
    nh = n // 2
    tm, fw = o_ref.shape[1], o_ref.shape[2]

    @pl.when(pl.program_id(1) == 0)
    def _():
        for m in range(nh // LANES):
            blk = ab_ref[0, n - LANES * (m + 1):n - LANES * m, :]
            rev_scr[LANES * m:LANES * (m + 1), :] = _dot(rev_ref[...], blk)
        ck = min(256, nh)
        row = lax.broadcasted_iota(jnp.int32, (ck, 1), 0)
        for r0 in range(0, nh, ck):
            prev = (r0 - 1) % nh
            shifted = jnp.where(row == 0, rev_scr[prev:prev + 1, :], pltpu.roll(rev_scr[r0:r0 + ck, :], 1, 0))
            lo = ab_ref[0, r0:r0 + ck, :].astype(F32)
            a2 = lo[:, :fw] + shifted[:, :fw]
            b2 = lo[:, fw:] - shifted[:, fw:]
            if r0 == 0:
                a2 = jnp.where(row == 0, lo[:, :fw], a2)
                b2 = jnp.where(row == 0, shifted[:, :fw], b2)
            fold_ref[r0:r0 + ck, :fw] = a2.astype(BF16)
            fold_ref[r0:r0 + ck, fw:] = b2.astype(BF16)

    cb = cb_ref[...]
    sb = sb_ref[...]
    sub = min(ROW_SUB, tm)
    for r0 in range(0, tm, sub):
        cm, sm = [], []
        for a in range(r0 // DFT_SUB, (r0 + sub) // DFT_SUB):
            ca = ca_ref[a:a + 1, :]
            sa = sa_ref[a:a + 1, :]
            cm.append((ca * cb - sa * sb).astype(BF16))
            sm.append((sa * cb + ca * sb).astype(BF16))
        y = _dot(jnp.concatenate(cm, axis=0), fold_ref[:, :fw]) - _dot(jnp.concatenate(sm, axis=0), fold_ref[:, fw:])
        o_ref[0, r0:r0 + sub] = y.astype(BF16)


def _fourier_call(ab, ca, sa, cb, sb, *, tm):
    b, n, w2 = ab.shape
    nh = n // 2
    fw = w2 // 2
    na = tm // DFT_SUB
    rev = jnp.asarray(np.eye(LANES, dtype=np.float32)[::-1], BF16)
    return pl.pallas_call(
        _fourier_kernel,
        grid=(b, n // tm),
        in_specs=[
            pl.BlockSpec((1, n, w2), lambda i, j: (i, 0, 0)),
            pl.BlockSpec((na, nh), lambda i, j: (j, 0)),
            pl.BlockSpec((na, nh), lambda i, j: (j, 0)),
            pl.BlockSpec((DFT_SUB, nh), lambda i, j: (0, 0)),
            pl.BlockSpec((DFT_SUB, nh), lambda i, j: (0, 0)),
            pl.BlockSpec((LANES, LANES), lambda i, j: (0, 0)),
        ],
        out_specs=pl.BlockSpec((1, tm, fw), lambda i, j: (i, j, 0)),
        out_shape=jax.ShapeDtypeStruct((b, n, fw), BF16),
        scratch_shapes=[pltpu.VMEM((nh, w2), BF16), pltpu.VMEM((nh, w2), F32)],
        compiler_params=_cparams(2, VMEM_LIMIT),
        name="fourier",
    )(ab, ca, sa, cb, sb, rev)


def _dft_tables(n):
    nh = n // 2
    k = jnp.arange(nh, dtype=jnp.int32)
    a = jnp.arange(n // DFT_SUB, dtype=jnp.int32)
    bb = jnp.arange(DFT_SUB, dtype=jnp.int32)
    ang_a = ((DFT_SUB * a[:, None] * k[None, :]) % n).astype(F32) * (2.0 * math.pi / n)
    ang_b = ((bb[:, None] * k[None, :]) % n).astype(F32) * (2.0 * math.pi / n)
    s = float(n) ** -0.5
    nyq = -jnp.cos(math.pi * bb.astype(F32)) * s
    sin_b = jnp.concatenate([nyq[:, None], (jnp.sin(ang_b) * s)[:, 1:]], axis=1)
    return jnp.cos(ang_a), jnp.sin(ang_a), jnp.cos(ang_b) * s, sin_b


def _channel_dft():
    c = jnp.arange(LANES, dtype=jnp.int32)
    ang = ((c[:, None] * c[None, :]) % LANES).astype(F32) * (2.0 * math.pi / LANES)
    s = float(LANES) ** -0.5
    return jnp.concatenate([jnp.cos(ang) * s, jnp.sin(ang) * s], axis=1)


def _postmix_kernel(*refs, n_in):
    a_refs = refs[:n_in]
    w_refs = refs[n_in:2 * n_in]
    x_ref, mod_ref, g_ref, rw_ref, x1_ref, h3_ref, aff_ref = refs[2 * n_in:]
    tm, d = x_ref.shape[1], x_ref.shape[2]
    sub = min(ROW_SUB, tm)
    p = d // 2 // LANES
    m = mod_ref[0]
    for r0 in range(0, tm, sub):
        rs = slice(r0, r0 + sub)
        y = None
        for a_ref, w_ref in zip(a_refs, w_refs):
            t = _dot(a_ref[0, rs], w_ref[...])
            y = t if y is None else y + t
        x1 = x_ref[0, rs] + m[2:3] * y
        x1_ref[0, rs] = x1
        h2 = _rms_mod(x1, g_ref[...], m[3:4], m[4:5])
        hh, hl = _split(h2)
        parts = _dot_t(rw_ref[...], jnp.concatenate([hh, hl], axis=1))
        lt = parts[:N_EXPERTS] + parts[N_EXPERTS:2 * N_EXPERTS]
        ex = jnp.exp(lt - jnp.max(lt, axis=0, keepdims=True))
        aff_ref[0, :, rs] = ex / jnp.sum(ex, axis=0, keepdims=True)
        bits = pltpu.bitcast(hh.astype(F32), jnp.uint32)
        packed = (bits[:, d // 2:] & jnp.uint32(0xFFFF0000)) | (bits[:, :d // 2] >> 16)
        for jj in range(p):
            h3_ref[0, pl.ds(r0 * p + jj, sub, stride=p), :] = packed[:, LANES * jj:LANES * (jj + 1)]


def _postmix_call(a_list, w_list, x, mod, gain, rw, *, tm):
    b, n, d = x.shape
    mb = mod.shape[0]
    n_in = len(a_list)
    p = d // 2 // LANES
    mod_map = (lambda i, j: (i, 0, 0)) if mb > 1 else (lambda i, j: (0, 0, 0))
    tok = lambda i, j: (i, j, 0)
    const2 = lambda i, j: (0, 0)
    in_specs = [pl.BlockSpec((1, tm, a.shape[2]), tok) for a in a_list]
    in_specs += [pl.BlockSpec(w.shape, const2) for w in w_list]
    in_specs += [
        pl.BlockSpec((1, tm, d), tok),
        pl.BlockSpec((1, 6, d), mod_map),
        pl.BlockSpec((1, d), const2),
        pl.BlockSpec((LANES, 2 * d), const2),
    ]
    return pl.pallas_call(
        functools.partial(_postmix_kernel, n_in=n_in),
        grid=(b, n // tm),
        in_specs=in_specs,
        out_specs=[
            pl.BlockSpec((1, tm, d), tok),
            pl.BlockSpec((1, tm * p, LANES), tok),
            pl.BlockSpec((1, N_EXPERTS, tm), lambda i, j: (i, 0, j)),
        ],
        out_shape=[
            jax.ShapeDtypeStruct((b, n, d), F32),
            jax.ShapeDtypeStruct((b, n * p, LANES), jnp.uint32),
            jax.ShapeDtypeStruct((b, N_EXPERTS, n), F32),
        ],
        compiler_params=_cparams(2, VMEM_LIMIT),
        name="postmix",
    )(*a_list, *w_list, x, mod, gain, rw)


def _prefix_incl(ones, tri):
    carry = jnp.zeros((ones.shape[0], 1), F32)
    outs = []
    for c in range(ones.shape[1] // LANES):
        blk = ones[:, LANES * c:LANES * (c + 1)]
        outs.append(_dot(blk.astype(BF16), tri) + carry)
        carry = carry + jnp.sum(blk, axis=1, keepdims=True)
    return jnp.concatenate(outs, axis=1)


def _topk_kernel(aff_ref, tri_ref, idx_ref, gate_ref, *, cap):
    n = aff_ref.shape[2]
    aff = aff_ref[0]
    bits = pltpu.bitcast(aff, jnp.int32)
    capf = jnp.float32(cap)

    def body(it, lo):
        t = lo | jnp.left_shift(jnp.int32(1), 30 - it)
        cnt = jnp.sum(jnp.where(bits >= t, 1.0, 0.0), axis=1, keepdims=True)
        return jnp.where(cnt >= capf, t, lo)

    thr = lax.fori_loop(0, 31, body, jnp.zeros((N_EXPERTS, 1), jnp.int32))
    gt = jnp.where(bits > thr, 1.0, 0.0)
    eq = jnp.where(bits == thr, 1.0, 0.0)
    need = capf - jnp.sum(gt, axis=1, keepdims=True)
    tri = tri_ref[...]
    eq_before = _prefix_incl(eq, tri) - eq
    sel = gt + eq * jnp.where(eq_before < need, 1.0, 0.0)
    slot = _prefix_incl(sel, tri).astype(jnp.int32) - 1
    tok = lax.broadcasted_iota(jnp.int32, (N_EXPERTS, n), 1)
    disp = jnp.where(sel > 0.0, tok - slot, -1)
    gate = aff
    for bit in range((n - 1).bit_length()):
        step = 1 << bit
        disp_in = pltpu.roll(disp, n - step, 1)
        gate_in = pltpu.roll(gate, n - step, 1)
        move_in = jnp.logical_and(disp_in >= 0, ((disp_in >> bit) & 1) == 1)
        stay = jnp.logical_and(disp >= 0, ((disp >> bit) & 1) == 0)
        gate = jnp.where(move_in, gate_in, gate)
        disp = jnp.where(move_in, disp_in, jnp.where(stay, disp, -1))
    idx_ref[0] = (tok + disp)[:, :cap]
    gate_ref[0] = gate[:, :cap]


def _topk_call(aff_t, tri, *, cap):
    b, e, n = aff_t.shape
    return pl.pallas_call(
        functools.partial(_topk_kernel, cap=cap),
        grid=(b,),
        in_specs=[
            pl.BlockSpec((1, e, n), lambda i: (i, 0, 0)),
            pl.BlockSpec((LANES, LANES), lambda i: (0, 0)),
        ],
        out_specs=[pl.BlockSpec((1, e, cap), lambda i: (i, 0, 0))] * 2,
        out_shape=[jax.ShapeDtypeStruct((b, e, cap), jnp.int32), jax.ShapeDtypeStruct((b, e, cap), F32)],
        compiler_params=_cparams(1, VMEM_LIMIT),
        name="topk",
    )(aff_t, tri)


def _moe_kernel(idx_a_ref, idx_c_ref, gate_c_ref, h3_ref, wg_ref, wu_ref, wd_ref, o_ref, xs_0, xs_1, y_0, y_1,
                *, cap, p, c8, ne, n_tiles):
    g = pl.program_id(0)
    valid_a = g < n_tiles
    valid_b = jnp.logical_and(g >= 1, g <= n_tiles)
    valid_c = g >= 2
    steady = jnp.logical_and(g >= 2, g < n_tiles)

    @pl.when(jnp.logical_and(valid_c, (g - 2) % ne == 0))
    def _():
        o_ref[...] = jnp.zeros(o_ref.shape, o_ref.dtype)

    unroll, group = 8, 4

    def gather_row(xs_a, s, dst):
        t = idx_a_ref[0, 0, s]
        xs_a[dst, :] = h3_ref[0, pl.ds(pl.multiple_of(t * p, p), p), :]

    def gather(xs_a, straight):
        if straight:
            for s in range(cap):
                gather_row(xs_a, s, pl.ds(s * p, p))
        else:
            def body(i, carry):
                for u in range(unroll):
                    s = i * unroll + u
                    gather_row(xs_a, s, pl.ds(pl.multiple_of(s * p, p), p))
                return carry
            lax.fori_loop(0, cap // unroll, body, 0)

    def scatter_rows(y_c, s_list, src_list):
        sums = []
        for s, src in zip(s_list, src_list):
            t = idx_c_ref[0, 0, s]
            rows = pl.ds(pl.multiple_of(t * c8, c8), c8)
            sums.append((rows, o_ref[0, rows, :] + y_c[src, :] * gate_c_ref[0, 0, s]))
        for rows, val in sums:
            o_ref[0, rows, :] = val

    def scatter(y_c, straight):
        if straight:
            for g0 in range(0, cap, group):
                ss = [g0 + u for u in range(group)]
                scatter_rows(y_c, ss, [pl.ds(s * c8, c8) for s in ss])
        else:
            def body(i, carry):
                ss = [i * group + u for u in range(group)]
                scatter_rows(y_c, ss, [pl.ds(pl.multiple_of(s * c8, c8), c8) for s in ss])
                return carry
            lax.fori_loop(0, cap // group, body, 0)

    def load_rows(xs_b):
        lo, hi = [], []
        for j in range(p):
            w = xs_b[pl.ds(j, cap, stride=p), :]
            lo.append(pltpu.bitcast(w << 16, F32).astype(BF16))
            hi.append(pltpu.bitcast(w & jnp.uint32(0xFFFF0000), F32).astype(BF16))
        return jnp.concatenate(lo + hi, axis=1)

    def expert(xs, y_b):
        a = _dot(xs, wg_ref[0, 0])
        u_ = _dot(xs, wu_ref[0, 0])
        act = (_silu(a) * u_).astype(BF16)
        y = _dot(act, wd_ref[0, 0])
        for j in range(c8):
            y_b[pl.ds(j, cap, stride=c8), :] = y[:, LANES * j:LANES * (j + 1)]

    def step(xs_a, xs_b, y_b, y_c):
        @pl.when(steady)
        def _():
            xs = load_rows(xs_b)
            gather(xs_a, True)
            scatter(y_c, True)
            expert(xs, y_b)

        @pl.when(jnp.logical_not(steady))
        def _():
            @pl.when(valid_b)
            def _():
                expert(load_rows(xs_b), y_b)

            @pl.when(valid_a)
            def _():
                gather(xs_a, False)

            @pl.when(valid_c)
            def _():
                scatter(y_c, False)

    @pl.when(g % 2 == 0)
    def _():
        step(xs_0, xs_1, y_1, y_0)

    @pl.when(g % 2 == 1)
    def _():
        step(xs_1, xs_0, y_0, y_1)


def _moe_call(idx, h3, gate, wg, wu, wd, layer):
    bm, ne, cap = idx.shape
    d = wg.shape[2]
    p = d // 2 // LANES
    c8 = d // LANES
    n = h3.shape[1] // p
    n_tiles = bm * ne
    idx3 = idx.reshape(n_tiles, 1, cap)
    gate3 = gate.reshape(n_tiles, 1, cap)
    tile_a = lambda i: jnp.minimum(i, n_tiles - 1)
    tile_b = lambda i: jnp.clip(i - 1, 0, n_tiles - 1)
    tile_c = lambda i: jnp.clip(i - 2, 0, n_tiles - 1)
    wmap = lambda i: (layer, tile_b(i) % ne, 0, 0)
    smem = lambda f: pl.BlockSpec((1, 1, cap), lambda i: (f(i), 0, 0), memory_space=pltpu.SMEM)
    return pl.pallas_call(
        functools.partial(_moe_kernel, cap=cap, p=p, c8=c8, ne=ne, n_tiles=n_tiles),
        grid=(n_tiles + 2,),
        in_specs=[
            smem(tile_a), smem(tile_c), smem(tile_c),
            pl.BlockSpec((1, n * p, LANES), lambda i: (tile_a(i) // ne, 0, 0)),
            pl.BlockSpec((1, 1, d, wg.shape[3]), wmap),
            pl.BlockSpec((1, 1, d, wu.shape[3]), wmap),
            pl.BlockSpec((1, 1, wd.shape[2], d), wmap),
        ],
        out_specs=pl.BlockSpec((1, n * c8, LANES), lambda i: (tile_c(i) // ne, 0, 0), pipeline_mode=pl.Buffered(1)),
        out_shape=jax.ShapeDtypeStruct((bm, n * c8, LANES), F32),
        scratch_shapes=[pltpu.VMEM((cap * p, LANES), jnp.uint32), pltpu.VMEM((cap * p, LANES), jnp.uint32),
                        pltpu.VMEM((cap * c8, LANES), F32), pltpu.VMEM((cap * c8, LANES), F32)],
        compiler_params=_cparams(1, VMEM_LIMIT),
        name="moe",
    )(idx3, idx3, gate3, h3, wg, wu, wd)


def _cast_kernel(*refs):
    half = len(refs) // 2
    for w_ref, o_ref in zip(refs[:half], refs[half:]):
        o_ref[...] = w_ref[...].astype(BF16)


def _cast_call(*ws):
    l, e = ws[0].shape[:2]
    spec = lambda w: pl.BlockSpec((1, 1) + w.shape[2:], lambda i, j: (i, j, 0, 0))
    return pl.pallas_call(
        _cast_kernel,
        grid=(l, e),
        in_specs=[spec(w) for w in ws],
        out_specs=[spec(w) for w in ws],
        out_shape=[jax.ShapeDtypeStruct(w.shape, BF16) for w in ws],
        compiler_params=_cparams(2, VMEM_LIMIT),
        name="cast",
    )(*ws)


def _resid_kernel(x_ref, moe_ref, mod_ref, o_ref):
    tm = x_ref.shape[1]
    sub = min(ROW_SUB, tm)
    for r0 in range(0, tm, sub):
        o_ref[0, r0:r0 + sub] = _gated_moe_residual(x_ref, moe_ref, mod_ref[0][5:6], r0, sub)


def _resid_call(x1, moe3, mod, *, tm):
    b, n, d = x1.shape
    mb = mod.shape[0]
    c8 = d // LANES
    mod_map = (lambda i, j: (i, 0, 0)) if mb > 1 else (lambda i, j: (0, 0, 0))
    tok = lambda i, j: (i, j, 0)
    return pl.pallas_call(
        _resid_kernel,
        grid=(b, n // tm),
        in_specs=[
            pl.BlockSpec((1, tm, d), tok),
            pl.BlockSpec((1, tm * c8, LANES), tok),
            pl.BlockSpec((1, 6, d), mod_map),
        ],
        out_specs=pl.BlockSpec((1, tm, d), tok),
        out_shape=jax.ShapeDtypeStruct((b, n, d), F32),
        compiler_params=_cparams(2, VMEM_LIMIT),
        name="resid",
    )(x1, moe3, mod)


def _rope_tables(n):
    t = jnp.arange(n)
    row = (t // GRID_W).astype(F32)
    col = (t % GRID_W).astype(F32)
    n_freq = HEAD_DIM // 4
    inv_freq = jnp.power(ROPE_BASE, -jnp.arange(n_freq, dtype=F32) / n_freq)
    ang = jnp.concatenate([row[:, None] * inv_freq, col[:, None] * inv_freq], axis=-1)
    cos, sin = jnp.cos(ang), jnp.sin(ang)
    cos_t = jnp.concatenate([cos, cos, cos, cos], axis=-1)
    sin_t = jnp.concatenate([-sin, sin, -sin, sin], axis=-1)
    return cos_t, sin_t


def _gain128(g):
    return jnp.concatenate([g, g]).reshape(1, LANES).astype(F32)


def _channel_mix(h_pack, aff_t, tri, weights, layer, *, flatten):
    wg, wu, wd = weights
    b, _, n = aff_t.shape
    cap = EC_CAPACITY_FACTOR * n // N_EXPERTS
    idx, gate = _topk_call(aff_t, tri, cap=cap)
    if flatten:
        idx = (idx + (jnp.arange(b, dtype=jnp.int32) * n)[:, None, None]).transpose(1, 0, 2).reshape(1, N_EXPERTS, b * cap)
        gate = gate.transpose(1, 0, 2).reshape(1, N_EXPERTS, b * cap)
        h_pack = h_pack.reshape(1, -1, LANES)
    return _moe_call(idx, h_pack, gate, wg, wu, wd, layer).reshape(b, -1, LANES)


def kernel(x, c, ctx, c_ctx, ada_w, ada_b, norm1_g, norm2_g, router_w, exp_w_gate, exp_w_up, exp_w_down, ev_w_in,
           ev_w_out, ev_q_gain, ev_k_gain, ev_sink, od_w_in, od_w_out, od_q_gain, od_k_gain, od_rpb):
    b, n, d = x.shape
    nc = ctx.shape[1]
    tm = min(512, n)
    tmb = min(1024, n)
    tmc = min(512, nc)

    rows = -(-(b + 1) // 8) * 8
    cs = jnp.concatenate([c, c_ctx[None], jnp.zeros((rows - b - 1, d), F32)], axis=0)
    mods = _mod_call(cs, ada_w, ada_b).reshape(ada_w.shape[0], rows, 6, d)

    cos_t, sin_t = _rope_tables(n)
    cs_dft = _channel_dft()
    tri = jnp.asarray(np.triu(np.ones((LANES, LANES), np.float32)), BF16)
    fw = FOURIER_GROUPS * LANES

    def router_split(l):
        rw = router_w[l].T
        rh = rw.astype(BF16)
        rl = (rw - rh.astype(F32)).astype(BF16)
        left = jnp.pad(jnp.concatenate([rh, rl], axis=0), ((0, LANES - 2 * N_EXPERTS), (0, 0)))
        right = jnp.pad(rh, ((0, LANES - N_EXPERTS), (0, 0)))
        return jnp.concatenate([left, right], axis=1)

    wts = _cast_call(exp_w_gate, exp_w_up, exp_w_down)

    mod_x, mod_c = mods[0, :b], mods[0, b:b + 1]
    g1 = norm1_g[0].reshape(1, d)
    g2 = norm2_g[0].reshape(1, d)
    w_in = ev_w_in[0].astype(BF16)
    w_out = ev_w_out[0].astype(BF16)
    qg, kg = _gain128(ev_q_gain[0]), _gain128(ev_k_gain[0])
    sink = ev_sink[0].astype(F32)
    rw = router_split(0)

    ab_x, q_x, kd_x, vd_x = _premix_even_call(x, mod_x, g1, w_in, cs_dft, cos_t, sin_t, qg, kg, rope=True, tm=tmb)
    ab_c, q_c, kd_c, vd_c = _premix_even_call(ctx, mod_c, g1, w_in, cs_dft, cos_t[:nc], sin_t[:nc], qg, kg,
                                              rope=False, tm=tmc)
    a_x = _swa_call(sink, q_x, kd_x, vd_x, kd_c, vd_c)
    a_c = _ctx_attn_call(sink, q_c, kd_c, vd_c)
    four_x = _fourier_call(ab_x, *_dft_tables(n), tm=tmb)
    four_c = _fourier_call(ab_c, *_dft_tables(nc), tm=tmc)
    w_out_parts = [w_out[:fw], w_out[fw:]]
    x1, h3_x, aff_x = _postmix_call([four_x, a_x], w_out_parts, x, mod_x, g2, rw, tm=tmb)
    c1, h3_c, aff_c = _postmix_call([four_c, a_c], w_out_parts, ctx, mod_c, g2, rw, tm=tmc)
    moe_x = _channel_mix(h3_x, aff_x, tri, wts, 0, flatten=False)
    moe_c = _channel_mix(h3_c, aff_c, tri, wts, 0, flatten=True)

    mod_x0, mod_c0 = mod_x, mod_c
    mod_x, mod_c = mods[1, :b], mods[1, b:b + 1]
    g1 = norm1_g[1].reshape(1, d)
    g2 = norm2_g[1].reshape(1, d)
    w_in = od_w_in[0].astype(BF16)
    w_out = od_w_out[0].astype(BF16)
    qg, kg = _gain128(od_q_gain[0]), _gain128(od_k_gain[0])
    rw = router_split(1)

    q_x, k_x, v_x, x = _premix_odd_call(x1, moe_x, mod_x0, mod_x, g1, w_in, qg, kg, emit_x=True, tm=tm)
    _, k_c, v_c = _premix_odd_call(c1, moe_c, mod_c0, mod_c, g1, w_in, qg, kg, emit_x=False, tm=tmc)
    a_x = _na_call(q_x, k_x, v_x, k_c, v_c, _na_bias_table(od_rpb[0]))
    x1, h3_x, aff_x = _postmix_call([a_x], [w_out], x, mod_x, g2, rw, tm=tmb)
    return _resid_call(x1, _channel_mix(h3_x, aff_x, tri, wts, 1, flatten=False), mod_x, tm=tmb)
```

```python
import functools
import math

import numpy as np
import jax
import jax.numpy as jnp
from jax import lax
from jax.experimental import pallas as pl
from jax.experimental.pallas import tpu as pltpu

GRID_W = 64
HEAD_DIM = 64
FOURIER_GROUPS = 4
SWA_WINDOW = 128
SWA_BLOCK = 128
NA_ROWS_MAX = 8
NA_COLS = 16
N_EXPERTS = 16
EC_CAPACITY_FACTOR = 2
ROPE_BASE = 10000.0
EPS = 1e-6

LANES = 128
NEG = -1e30
NA_QROWS = 4
NA_KROWS = 12
DFT_SUB = 64
SWA_QBLK = 4
ROW_SUB = 256
ONES_ROWS = 16
LOG2E = math.log2(math.e)
VMEM_LIMIT = 60 * 1024 * 1024

F32 = jnp.float32
BF16 = jnp.bfloat16


def _cparams(n_axes, vmem=None):
    return pltpu.CompilerParams(dimension_semantics=("arbitrary",) * n_axes, vmem_limit_bytes=vmem)


def _dot(a, b):
    return jnp.dot(a, b, preferred_element_type=F32)


def _dot_t(a, b):
    return lax.dot_general(a, b, (((1,), (1,)), ((), ())), preferred_element_type=F32)


def _split(a):
    hi = a.astype(BF16)
    lo = (a - hi.astype(F32)).astype(BF16)
    return hi, lo


def _dot3(a, b):
    ah, al = _split(a)
    bh, bl = _split(b)
    return _dot(ah, bh) + _dot(al, bh) + _dot(ah, bl)


def _silu(a):
    return a / (1.0 + jnp.exp(-a))


def _rms_mod(x, gain, shift, scale):
    ms = jnp.mean(x * x, axis=-1, keepdims=True)
    y = x * lax.rsqrt(ms + EPS) * gain
    return y * (1.0 + scale) + shift


def _mod_kernel(cs_ref, w_ref, b_ref, o_ref):
    o_ref[0] = _dot3(_silu(cs_ref[...]), w_ref[0]) + b_ref[0]


def _mod_call(cs, ada_w, ada_b):
    depth, d, d6 = ada_w.shape
    r = cs.shape[0]
    tn = 1024
    return pl.pallas_call(
        _mod_kernel,
        grid=(depth, d6 // tn),
        in_specs=[
            pl.BlockSpec((r, d), lambda l, j: (0, 0)),
            pl.BlockSpec((1, d, tn), lambda l, j: (l, 0, j)),
            pl.BlockSpec((1, 1, tn), lambda l, j: (l, 0, j)),
        ],
        out_specs=pl.BlockSpec((1, r, tn), lambda l, j: (l, 0, j)),
        out_shape=jax.ShapeDtypeStruct((depth, r, d6), F32),
        compiler_params=_cparams(2),
        name="mod",
    )(cs, ada_w, ada_b.reshape(depth, 1, d6))


def _head_norm(t, gain, lane):
    t2 = t * t
    s_lo = jnp.sum(jnp.where(lane < HEAD_DIM, t2, 0.0), axis=-1, keepdims=True)
    s_all = jnp.sum(t2, axis=-1, keepdims=True)
    ms = jnp.where(lane < HEAD_DIM, s_lo, s_all - s_lo) * (1.0 / HEAD_DIM)
    return t * lax.rsqrt(ms + EPS) * gain


def _rope(t, cos_t, sin_t, lane):
    partner = jnp.where((lane % HEAD_DIM) < HEAD_DIM // 2, pltpu.roll(t, LANES - HEAD_DIM // 2, 1),
                        pltpu.roll(t, HEAD_DIM // 2, 1))
    return t * cos_t + partner * sin_t


def _dup_halves(t, lane):
    sw = pltpu.roll(t, HEAD_DIM, 1)
    return jnp.where(lane < HEAD_DIM, t, sw), jnp.where(lane < HEAD_DIM, sw, t)


def _premix_even_kernel(x_ref, mod_ref, g_ref, w_ref, cs_ref, cos_ref, sin_ref, qg_ref, kg_ref,
                        ab_ref, q_ref, kd_ref, vt_ref, *, rope):
    tm = x_ref.shape[1]
    sub = min(ROW_SUB, tm)
    m = mod_ref[0]
    lane = lax.broadcasted_iota(jnp.int32, (sub, LANES), 1)
    fw = FOURIER_GROUPS * LANES
    qw = q_ref.shape[2]
    csb = cs_ref[...].astype(BF16)
    for r0 in range(0, tm, sub):
        rs = slice(r0, r0 + sub)
        hb = _rms_mod(x_ref[0, rs], g_ref[...], m[0:1], m[1:2]).astype(BF16)
        pf = _dot(hb, w_ref[:, 0:fw])
        for g in range(FOURIER_GROUPS):
            ab = _dot(pf[:, LANES * g:LANES * (g + 1)].astype(BF16), csb)
            ab_ref[0, rs, LANES * g:LANES * (g + 1)] = ab[:, :LANES].astype(BF16)
            ab_ref[0, rs, fw + LANES * g:fw + LANES * (g + 1)] = ab[:, LANES:].astype(BF16)
        pq = _dot(hb, w_ref[:, fw:fw + qw])
        if rope:
            cos_t = cos_ref[rs]
            sin_t = sin_ref[rs]
        for c in range(qw // LANES):
            t = _head_norm(pq[:, LANES * c:LANES * (c + 1)], qg_ref[...], lane)
            if rope:
                t = _rope(t, cos_t, sin_t, lane)
            q_ref[0, rs, LANES * c:LANES * (c + 1)] = t.astype(BF16)
        pk = _dot(hb, w_ref[:, fw + qw:fw + qw + 2 * LANES])
        k = _head_norm(pk[:, :LANES], kg_ref[...], lane)
        if rope:
            k = _rope(k, cos_t, sin_t, lane)
        k0, k1 = _dup_halves(k, lane)
        kd_ref[0, rs, :LANES] = k0.astype(BF16)
        kd_ref[0, rs, LANES:] = k1.astype(BF16)
        vt_ref[0, :, rs] = pk[:, LANES:].T.astype(BF16)


def _premix_even_call(x, mod, gain, w_bf, cs, cos_t, sin_t, qg, kg, *, rope, tm):
    b, n, d = x.shape
    mb = mod.shape[0]
    wtot = w_bf.shape[1]
    fw = FOURIER_GROUPS * LANES
    qw = wtot - fw - 2 * LANES
    mod_map = (lambda i, j: (i, 0, 0)) if mb > 1 else (lambda i, j: (0, 0, 0))
    tok = lambda i, j: (i, j, 0)
    const2 = lambda i, j: (0, 0)
    return pl.pallas_call(
        functools.partial(_premix_even_kernel, rope=rope),
        grid=(b, n // tm),
        in_specs=[
            pl.BlockSpec((1, tm, d), tok),
            pl.BlockSpec((1, 6, d), mod_map),
            pl.BlockSpec((1, d), const2),
            pl.BlockSpec((d, wtot), const2),
            pl.BlockSpec((LANES, 2 * LANES), const2),
            pl.BlockSpec((tm, LANES), lambda i, j: (j, 0)),
            pl.BlockSpec((tm, LANES), lambda i, j: (j, 0)),
            pl.BlockSpec((1, LANES), const2),
            pl.BlockSpec((1, LANES), const2),
        ],
        out_specs=[
            pl.BlockSpec((1, tm, 2 * fw), tok),
            pl.BlockSpec((1, tm, qw), tok),
            pl.BlockSpec((1, tm, 2 * LANES), tok),
            pl.BlockSpec((1, LANES, tm), lambda i, j: (i, 0, j)),
        ],
        out_shape=[
            jax.ShapeDtypeStruct((b, n, 2 * fw), BF16),
            jax.ShapeDtypeStruct((b, n, qw), BF16),
            jax.ShapeDtypeStruct((b, n, 2 * LANES), BF16),
            jax.ShapeDtypeStruct((b, LANES, n), BF16),
        ],
        compiler_params=_cparams(2, VMEM_LIMIT),
        name="premix_even",
    )(x, mod, gain, w_bf, cs, cos_t, sin_t, qg, kg)


def _gated_moe_residual(x_ref, moe_ref, gate_row, r0, sub):
    c8 = x_ref.shape[2] // LANES
    cols = []
    for j in range(c8):
        cs = slice(LANES * j, LANES * (j + 1))
        cols.append(x_ref[0, r0:r0 + sub, cs] + gate_row[:, cs] * moe_ref[0, pl.ds(r0 * c8 + j, sub, stride=c8), :])
    return jnp.concatenate(cols, axis=1)


def _premix_odd_kernel(x_ref, moe_ref, modp_ref, mod_ref, g_ref, w_ref, qg_ref, kg_ref, q_ref, k_ref, v_ref,
                       *xo_ref):
    tm = x_ref.shape[1]
    sub = min(ROW_SUB, tm)
    m = mod_ref[0]
    gate_prev = modp_ref[0][5:6]
    lane = lax.broadcasted_iota(jnp.int32, (sub, LANES), 1)
    wq = q_ref.shape[2]
    chunk = 4 * LANES
    for r0 in range(0, tm, sub):
        rs = slice(r0, r0 + sub)
        x = _gated_moe_residual(x_ref, moe_ref, gate_prev, r0, sub)
        if xo_ref:
            xo_ref[0][0, rs] = x
        hb = _rms_mod(x, g_ref[...], m[0:1], m[1:2]).astype(BF16)
        for c0 in range(0, 3 * wq, chunk):
            p = _dot(hb, w_ref[:, c0:c0 + chunk])
            for cc in range(chunk // LANES):
                col = c0 + cc * LANES
                t = p[:, cc * LANES:(cc + 1) * LANES]
                if col < wq:
                    q_ref[0, rs, col:col + LANES] = _head_norm(t, qg_ref[...], lane).astype(BF16)
                elif col < 2 * wq:
                    k_ref[0, rs, col - wq:col - wq + LANES] = _head_norm(t, kg_ref[...], lane).astype(BF16)
                else:
                    v_ref[0, col - 2 * wq:col - 2 * wq + LANES, rs] = t.T.astype(BF16)


def _premix_odd_call(x1, moe3, mod_prev, mod, gain, w_bf, qg, kg, *, emit_x, tm):
    b, n, d = x1.shape
    mb = mod.shape[0]
    wq = w_bf.shape[1] // 3
    c8 = d // LANES
    mod_map = (lambda i, j: (i, 0, 0)) if mb > 1 else (lambda i, j: (0, 0, 0))
    tok = lambda i, j: (i, j, 0)
    const2 = lambda i, j: (0, 0)
    out_specs = [pl.BlockSpec((1, tm, wq), tok)] * 2 + [pl.BlockSpec((1, wq, tm), lambda i, j: (i, 0, j))]
    out_shape = [jax.ShapeDtypeStruct((b, n, wq), BF16)] * 2 + [jax.ShapeDtypeStruct((b, wq, n), BF16)]
    if emit_x:
        out_specs.append(pl.BlockSpec((1, tm, d), tok))
        out_shape.append(jax.ShapeDtypeStruct((b, n, d), F32))
    return pl.pallas_call(
        _premix_odd_kernel,
        grid=(b, n // tm),
        in_specs=[
            pl.BlockSpec((1, tm, d), tok),
            pl.BlockSpec((1, tm * c8, LANES), tok),
            pl.BlockSpec((1, 6, d), mod_map),
            pl.BlockSpec((1, 6, d), mod_map),
            pl.BlockSpec((1, d), const2),
            pl.BlockSpec((d, 3 * wq), const2),
            pl.BlockSpec((1, LANES), const2),
            pl.BlockSpec((1, LANES), const2),
        ],
        out_specs=out_specs,
        out_shape=out_shape,
        compiler_params=_cparams(2, VMEM_LIMIT),
        name="premix_odd",
    )(x1, moe3, mod_prev, mod, gain, w_bf, qg, kg)


def _col_reduce(x, op, slab=64):
    r = x.shape[0]
    if r > slab and r % slab == 0:
        x = op(x.reshape(r // slab, slab, x.shape[1]), axis=0)
    return op(x, axis=0, keepdims=True)


def _softmax_pv_t(s_t, v_t, sink=None):
    mx = _col_reduce(s_t, jnp.max)
    if sink is not None:
        mx = jnp.maximum(mx, sink)
    p = jnp.exp2((s_t - mx).astype(BF16))
    nd = v_t.shape[0]
    o = _dot(jnp.concatenate([v_t, jnp.ones((ONES_ROWS, v_t.shape[1]), BF16)], axis=0), p)
    den = o[nd:nd + 1]
    if sink is not None:
        den = den + jnp.exp2(sink - mx)
    return o[:nd] / den


def _mask_half(qc, lane, half):
    q32 = qc.astype(F32) * (HEAD_DIM ** -0.5 * LOG2E)
    keep = (lane < HEAD_DIM) if half == 0 else (lane >= HEAD_DIM)
    return jnp.where(keep, q32, 0.0).astype(BF16)


def _gqa_scores(q_ref, h, kd, n_kv, masks, lane, r0, tq):
    rs = slice(r0, r0 + tq)
    cols_per_kv = q_ref.shape[2] // LANES // n_kv
    q_rows = []
    for cc in range(cols_per_kv):
        c = cols_per_kv * h + cc
        for half in range(2):
            q_rows.append(_mask_half(q_ref[0, rs, LANES * c:LANES * (c + 1)], lane, half))
    qm = jnp.concatenate(q_rows, axis=0)
    s = _dot_t(kd, qm)
    pieces, k0 = [], 0
    for rows, mask in masks:
        blk = s[k0:k0 + rows]
        pieces.append(blk if mask is None else jnp.where(mask, blk, NEG))
        k0 += rows
    return jnp.concatenate(pieces, axis=0) if len(pieces) > 1 else pieces[0]


def _gqa_finish(sink_ref, o_ref, h, s_t, v_t, r0, tq):
    rs = slice(r0, r0 + tq)
    n_kv = v_t.shape[0] // HEAD_DIM
    cols_per_kv = o_ref.shape[2] // LANES // n_kv
    heads = 2 * cols_per_kv
    head_of_lane = lax.broadcasted_iota(jnp.int32, (1, heads * tq), 1) // tq
    sink = jnp.zeros((1, heads * tq), F32)
    for a in range(heads):
        sink = jnp.where(head_of_lane == a, sink_ref[heads * h + a] * LOG2E, sink)
    o_t = _softmax_pv_t(s_t, v_t, sink)[HEAD_DIM * h:HEAD_DIM * (h + 1)]
    for cc in range(cols_per_kv):
        c = cols_per_kv * h + cc
        tile = jnp.concatenate([o_t[:, tq * (2 * cc):tq * (2 * cc + 1)], o_t[:, tq * (2 * cc + 1):tq * (2 * cc + 2)]],
                               axis=0)
        o_ref[0, rs, LANES * c:LANES * (c + 1)] = tile.T.astype(BF16)


def _swa_kernel(sink_ref, q_ref, km_ref, k0_ref, kp_ref, vm_ref, v0_ref, vp_ref, kc_ref, vc_ref, *rest, n_step):
    o_ref = _ride_cast(rest)
    g = pl.program_id(1)
    tq = SWA_BLOCK
    nc = kc_ref.shape[1]
    n_kv = kc_ref.shape[2] // LANES
    heads = q_ref.shape[2] // HEAD_DIM // n_kv
    lane = lax.broadcasted_iota(jnp.int32, (tq, LANES), 1)
    jk = lax.broadcasted_iota(jnp.int32, (SWA_BLOCK, heads * tq), 0)
    iq = lax.broadcasted_iota(jnp.int32, (SWA_BLOCK, heads * tq), 1) % tq
    below, above = jk >= iq, jk <= iq
    nq = q_ref.shape[1] // SWA_BLOCK
    blk = lambda s: slice(SWA_BLOCK * s, SWA_BLOCK * (s + 1))

    def key_block(s, hs):
        return km_ref[0, :, hs] if s < 0 else kp_ref[0, :, hs] if s == nq else k0_ref[0, blk(s), hs]

    def value_block(s):
        return vm_ref[0] if s < 0 else vp_ref[0] if s == nq else v0_ref[0, :, blk(s)]

    def scores(sub, h):
        hs = slice(LANES * h, LANES * (h + 1))
        first = jnp.logical_and(below, g > 0) if sub == 0 else below
        last = jnp.logical_and(above, g < n_step - 1) if sub == nq - 1 else above
        masks = [(nc, None), (SWA_BLOCK, first), (SWA_BLOCK, None), (SWA_BLOCK, last)]
        kd = jnp.concatenate([kc_ref[0, :, hs]] + [key_block(s, hs) for s in (sub - 1, sub, sub + 1)], axis=0)
        return _gqa_scores(q_ref, h, kd, n_kv, masks, lane, SWA_BLOCK * sub, tq)

    groups = [(sub, h) for sub in range(nq) for h in range(n_kv)]
    ahead = 3
    pending = [scores(*grp) for grp in groups[:ahead]]
    for i, (sub, h) in enumerate(groups):
        s_t = pending.pop(0)
        if i + ahead < len(groups):
            pending.append(scores(*groups[i + ahead]))
        v_t = jnp.concatenate([vc_ref[0]] + [value_block(s) for s in (sub - 1, sub, sub + 1)], axis=1)
        _gqa_finish(sink_ref, o_ref, h, s_t, v_t, SWA_BLOCK * sub, tq)


def _cast_riders(ws, layer, total, flat):
    ins, outs, shapes, args = [], [], [], []
    for w in ws:
        l, e, r, c = w.shape
        rows = e * r // total
        ins.append(pl.BlockSpec((rows, c), lambda i, j: (layer * total + flat(i, j), 0)))
        outs.append(pl.BlockSpec((rows, c), lambda i, j: (flat(i, j), 0)))
        shapes.append(jax.ShapeDtypeStruct((e * r, c), BF16))
        args.append(w.reshape(l * e * r, c))
    return ins, outs, shapes, args


def _ride_cast(rest):
    n_cast = (len(rest) - 1) // 2
    for w_ref, c_ref in zip(rest[:n_cast], rest[n_cast + 1:]):
        c_ref[...] = w_ref[...].astype(BF16)
    return rest[n_cast]


def _swa_call(sink, q, kd, vt, kdc, vtc, ws, layer):
    b, n, qw = q.shape
    nc = kdc.shape[1]
    kw = kd.shape[2]
    vw = vt.shape[1]
    n_blk = n // SWA_BLOCK
    nq = min(SWA_QBLK, n_blk)
    n_step = n_blk // nq
    prev = lambda j: jnp.maximum(nq * j - 1, 0)
    nxt = lambda j: jnp.minimum(nq * j + nq, n_blk - 1)
    kspec = lambda f: pl.BlockSpec((1, SWA_BLOCK, kw), lambda i, j: (i, f(j), 0))
    vspec = lambda f: pl.BlockSpec((1, vw, SWA_BLOCK), lambda i, j: (i, 0, f(j)))
    c_in, c_out, c_shape, c_args = _cast_riders(ws, layer, b * n_step, lambda i, j: i * n_step + j)
    return pl.pallas_call(
        functools.partial(_swa_kernel, n_step=n_step),
        grid=(b, n_step),
        in_specs=[
            pl.BlockSpec(memory_space=pltpu.SMEM),
            pl.BlockSpec((1, nq * SWA_BLOCK, qw), lambda i, j: (i, j, 0)),
            kspec(prev), pl.BlockSpec((1, nq * SWA_BLOCK, kw), lambda i, j: (i, j, 0)), kspec(nxt),
            vspec(prev), pl.BlockSpec((1, vw, nq * SWA_BLOCK), lambda i, j: (i, 0, j)), vspec(nxt),
            pl.BlockSpec((1, nc, kw), lambda i, j: (i, 0, 0)),
            pl.BlockSpec((1, vw, nc), lambda i, j: (i, 0, 0)),
        ] + c_in,
        out_specs=[pl.BlockSpec((1, nq * SWA_BLOCK, qw), lambda i, j: (i, j, 0))] + c_out,
        out_shape=[jax.ShapeDtypeStruct((b, n, qw), BF16)] + c_shape,
        compiler_params=_cparams(2, VMEM_LIMIT),
        name="swa",
    )(sink, q, kd, kd, kd, vt, vt, vt, kdc, vtc, *c_args)


def _ctx_attn_kernel(sink_ref, q_ref, kc_ref, vc_ref, o_ref):
    tq = q_ref.shape[1]
    lane = lax.broadcasted_iota(jnp.int32, (tq, LANES), 1)
    n_kv = kc_ref.shape[2] // LANES
    scores = [_gqa_scores(q_ref, h, kc_ref[0, :, LANES * h:LANES * (h + 1)], n_kv, [(kc_ref.shape[1], None)], lane,
                          0, tq) for h in range(n_kv)]
    for h, s_t in enumerate(scores):
        _gqa_finish(sink_ref, o_ref, h, s_t, vc_ref[0], 0, tq)


def _ctx_attn_call(sink, q, kd, vt):
    b, nc, qw = q.shape
    kw = kd.shape[2]
    vw = vt.shape[1]
    m = lambda i: (i, 0, 0)
    return pl.pallas_call(
        _ctx_attn_kernel,
        grid=(b,),
        in_specs=[
            pl.BlockSpec(memory_space=pltpu.SMEM),
            pl.BlockSpec((1, nc, qw), m),
            pl.BlockSpec((1, nc, kw), m),
            pl.BlockSpec((1, vw, nc), m),
        ],
        out_specs=pl.BlockSpec((1, nc, qw), m),
        out_shape=jax.ShapeDtypeStruct((b, nc, qw), BF16),
        compiler_params=_cparams(1, VMEM_LIMIT),
        name="ctx_attn",
    )(sink, q, kd, vt)


def _na_kernel(q_ref, k0_ref, k1_ref, k2_ref, v0_ref, v1_ref, v2_ref, kc_ref, vc_ref, tb_ref, *rest, n_j, rows):
    o_ref = _ride_cast(rest)
    j = pl.program_id(1)
    tq = q_ref.shape[1]
    tk = 3 * k0_ref.shape[1]
    nc = kc_ref.shape[1]
    w0 = NA_QROWS * jnp.clip(j - 1, 0, n_j - 3)
    lane = lax.broadcasted_iota(jnp.int32, (tq, LANES), 1)
    k_row = (lax.broadcasted_iota(jnp.int32, (nc + tk, LANES), 0) - nc) // GRID_W
    k_lane = lax.broadcasted_iota(jnp.int32, (nc + tk, LANES), 1)
    lo = jnp.zeros((nc + tk, LANES), jnp.int32)
    for rr in range(NA_QROWS):
        r = NA_QROWS * j + rr
        lo_rr = jnp.clip(r - NA_ROWS_MAX // 2, 0, rows - NA_ROWS_MAX) - w0
        lo = jnp.where(k_lane == rr, lo_rr, lo)
    outside = jnp.logical_or(k_row < lo, k_row >= lo + NA_ROWS_MAX)
    is_local = lax.broadcasted_iota(jnp.int32, (nc + tk, LANES), 0) >= nc
    k_extra = jnp.where(jnp.logical_and(jnp.logical_and(is_local, k_lane < NA_QROWS), outside), NEG, 0.0).astype(BF16)
    q_grid_row = (lax.broadcasted_iota(jnp.int32, (2 * tq, LANES), 0) % tq) // GRID_W
    q_extra = jnp.where(lax.broadcasted_iota(jnp.int32, (2 * tq, LANES), 1) == q_grid_row, 1.0, 0.0).astype(BF16)
    delta = w0 - NA_QROWS * j
    n_dr = 2 * NA_ROWS_MAX - 1
    def scores(c):
        cs = slice(LANES * c, LANES * (c + 1))
        qc = q_ref[0, :, cs]
        kl = jnp.concatenate([kc_ref[0, :, cs], k0_ref[0, :, cs], k1_ref[0, :, cs], k2_ref[0, :, cs]], axis=0)
        qm = jnp.concatenate([_mask_half(qc, lane, 0), _mask_half(qc, lane, 1)], axis=0)
        return _dot_t(jnp.concatenate([kl, k_extra], axis=1), jnp.concatenate([qm, q_extra], axis=1))

    n_cols = q_ref.shape[2] // LANES
    ahead = 1
    pending = [scores(c) for c in range(min(ahead, n_cols))]
    for c in range(n_cols):
        cs = slice(LANES * c, LANES * (c + 1))
        s = pending.pop(0)
        if c + ahead < n_cols:
            pending.append(scores(c + ahead))
        v_t = jnp.concatenate([vc_ref[0, cs, :], v0_ref[0, cs, :], v1_ref[0, cs, :], v2_ref[0, cs, :]], axis=1)
        bias_rows = []
        for kr in range(NA_KROWS):
            pieces = []
            for half in range(2):
                for t in range(NA_QROWS // 2):
                    dr = delta + kr - 2 * t + NA_ROWS_MAX - 1
                    pieces.append(tb_ref[2 * c + half, jnp.clip(dr, 0, n_dr)])
            bias_rows.append(jnp.concatenate(pieces, axis=1))
        bias = jnp.concatenate(bias_rows, axis=0)
        s_t = jnp.concatenate([s[:nc], s[nc:] + bias], axis=0)
        o_t = _softmax_pv_t(s_t, v_t)
        tile = jnp.concatenate([o_t[:HEAD_DIM, :tq], o_t[HEAD_DIM:, tq:]], axis=0)
        o_ref[0, :, cs] = tile.T.astype(BF16)


def _na_call(q, k, vt, kc, vtc, tb, ws, layer):
    b, n, w = q.shape
    nc = kc.shape[1]
    rows = n // GRID_W
    n_j = rows // NA_QROWS
    tq = NA_QROWS * GRID_W
    cur = lambda i, j: (i, j, 0)
    first = lambda j: jnp.clip(j - 1, 0, n_j - 3)
    kspec = lambda d: pl.BlockSpec((1, tq, w), lambda i, j: (i, first(j) + d, 0))
    vspec = lambda d: pl.BlockSpec((1, w, tq), lambda i, j: (i, 0, first(j) + d))
    c_in, c_out, c_shape, c_args = _cast_riders(ws, layer, b * n_j, lambda i, j: i * n_j + j)
    return pl.pallas_call(
        functools.partial(_na_kernel, n_j=n_j, rows=rows),
        grid=(b, n_j),
        in_specs=[
            pl.BlockSpec((1, tq, w), cur),
            kspec(0), kspec(1), kspec(2),
            vspec(0), vspec(1), vspec(2),
            pl.BlockSpec((1, nc, w), lambda i, j: (i, 0, 0)),
            pl.BlockSpec((1, w, nc), lambda i, j: (i, 0, 0)),
            pl.BlockSpec(tb.shape, lambda i, j: (0, 0, 0, 0)),
        ] + c_in,
        out_specs=[pl.BlockSpec((1, tq, w), cur)] + c_out,
        out_shape=[jax.ShapeDtypeStruct((b, n, w), BF16)] + c_shape,
        compiler_params=_cparams(2, VMEM_LIMIT),
        name="na",
    )(q, k, k, k, vt, vt, vt, kc, vtc, tb, *c_args)


def _na_bias_table(rpb):
    col_q = np.arange(GRID_W)[None, :]
    col_k = np.arange(GRID_W)[:, None]
    c_start = np.clip(col_q - NA_COLS // 2, 0, GRID_W - NA_COLS)
    col_valid = (col_k >= c_start) & (col_k < c_start + NA_COLS)
    dc_idx = np.clip(col_k - col_q + NA_COLS - 1, 0, 2 * NA_COLS - 2)
    t = jnp.where(col_valid[None, None], (rpb.astype(F32) * LOG2E)[:, :, dc_idx], NEG)
    pad = jnp.full_like(t[:, :1], NEG)
    t = jnp.concatenate([pad, t, pad], axis=1)
    return jnp.concatenate([t[:, 1:], t[:, :-1]], axis=-1)


def _fourier_kernel(ab_ref, ca_ref, sa_ref, cb_ref, sb_ref, rev_ref, o_ref, fold_ref, rev_scr):
    n = ab_ref.shape[1]
    nh = n // 2
    tm, fw = o_ref.shape[1], o_ref.shape[2]

    @pl.when(pl.program_id(1) == 0)
    def _():
        for m in range(nh // LANES):
            blk = ab_ref[0, n - LANES * (m + 1):n - LANES * m, :]
            rev_scr[LANES * m:LANES * (m + 1), :] = _dot(rev_ref[...], blk)
        ck = min(256, nh)
        row = lax.broadcasted_iota(jnp.int32, (ck, 1), 0)
        for r0 in range(0, nh, ck):
            prev = (r0 - 1) % nh
            shifted = jnp.where(row == 0, rev_scr[prev:prev + 1, :], pltpu.roll(rev_scr[r0:r0 + ck, :], 1, 0))
            lo = ab_ref[0, r0:r0 + ck, :].astype(F32)
            a2 = lo[:, :fw] + shifted[:, :fw]
            b2 = lo[:, fw:] - shifted[:, fw:]
            if r0 == 0:
                a2 = jnp.where(row == 0, lo[:, :fw], a2)
                b2 = jnp.where(row == 0, shifted[:, :fw], b2)
            fold_ref[r0:r0 + ck, :fw] = a2.astype(BF16)
            fold_ref[r0:r0 + ck, fw:] = b2.astype(BF16)

    cb = cb_ref[...]
    sb = sb_ref[...]
    sub = min(ROW_SUB, tm)
    for r0 in range(0, tm, sub):
        cm, sm = [], []
        for a in range(r0 // DFT_SUB, (r0 + sub) // DFT_SUB):
            ca = ca_ref[a:a + 1, :]
            sa = sa_ref[a:a + 1, :]
            cm.append((ca * cb - sa * sb).astype(BF16))
            sm.append((sa * cb + ca * sb).astype(BF16))
        y = _dot(jnp.concatenate(cm, axis=0), fold_ref[:, :fw]) - _dot(jnp.concatenate(sm, axis=0), fold_ref[:, fw:])
        o_ref[0, r0:r0 + sub] = y.astype(BF16)


def _fourier_call(ab, ca, sa, cb, sb, *, tm):
    b, n, w2 = ab.shape
    nh = n // 2
    fw = w2 // 2
    na = tm // DFT_SUB
    rev = jnp.asarray(np.eye(LANES, dtype=np.float32)[::-1], BF16)
    return pl.pallas_call(
        _fourier_kernel,
        grid=(b, n // tm),
        in_specs=[
            pl.BlockSpec((1, n, w2), lambda i, j: (i, 0, 0)),
            pl.BlockSpec((na, nh), lambda i, j: (j, 0)),
            pl.BlockSpec((na, nh), lambda i, j: (j, 0)),
            pl.BlockSpec((DFT_SUB, nh), lambda i, j: (0, 0)),
            pl.BlockSpec((DFT_SUB, nh), lambda i, j: (0, 0)),
            pl.BlockSpec((LANES, LANES), lambda i, j: (0, 0)),
        ],
        out_specs=pl.BlockSpec((1, tm, fw), lambda i, j: (i, j, 0)),
        out_shape=jax.ShapeDtypeStruct((b, n, fw), BF16),
        scratch_shapes=[pltpu.VMEM((nh, w2), BF16), pltpu.VMEM((nh, w2), F32)],
        compiler_params=_cparams(2, VMEM_LIMIT),
        name="fourier",
    )(ab, ca, sa, cb, sb, rev)


def _dft_tables(n):
    nh = n // 2
    k = jnp.arange(nh, dtype=jnp.int32)
    a = jnp.arange(n // DFT_SUB, dtype=jnp.int32)
    bb = jnp.arange(DFT_SUB, dtype=jnp.int32)
    ang_a = ((DFT_SUB * a[:, None] * k[None, :]) % n).astype(F32) * (2.0 * math.pi / n)
    ang_b = ((bb[:, None] * k[None, :]) % n).astype(F32) * (2.0 * math.pi / n)
    s = float(n) ** -0.5
    nyq = -jnp.cos(math.pi * bb.astype(F32)) * s
    sin_b = jnp.concatenate([nyq[:, None], (jnp.sin(ang_b) * s)[:, 1:]], axis=1)
    return jnp.cos(ang_a), jnp.sin(ang_a), jnp.cos(ang_b) * s, sin_b


def _channel_dft():
    c = jnp.arange(LANES, dtype=jnp.int32)
    ang = ((c[:, None] * c[None, :]) % LANES).astype(F32) * (2.0 * math.pi / LANES)
    s = float(LANES) ** -0.5
    return jnp.concatenate([jnp.cos(ang) * s, jnp.sin(ang) * s], axis=1)


def _postmix_kernel(*refs, n_in):
    a_refs = refs[:n_in]
    w_refs = refs[n_in:2 * n_in]
    x_ref, mod_ref, g_ref, rw_ref, x1_ref, h3_ref, aff_ref = refs[2 * n_in:]
    tm, d = x_ref.shape[1], x_ref.shape[2]
    sub = min(ROW_SUB, tm)
    p = d // 2 // LANES
    m = mod_ref[0]
    for r0 in range(0, tm, sub):
        rs = slice(r0, r0 + sub)
        y = None
        for a_ref, w_ref in zip(a_refs, w_refs):
            t = _dot(a_ref[0, rs], w_ref[...])
            y = t if y is None else y + t
        x1 = x_ref[0, rs] + m[2:3] * y
        x1_ref[0, rs] = x1
        h2 = _rms_mod(x1, g_ref[...], m[3:4], m[4:5])
        hh, hl = _split(h2)
        parts = _dot_t(rw_ref[...], jnp.concatenate([hh, hl], axis=1))
        lt = parts[:N_EXPERTS] + parts[N_EXPERTS:2 * N_EXPERTS]
        ex = jnp.exp(lt - jnp.max(lt, axis=0, keepdims=True))
        aff_ref[0, :, rs] = ex / jnp.sum(ex, axis=0, keepdims=True)
        bits = pltpu.bitcast(hh.astype(F32), jnp.uint32)
        packed = (bits[:, d // 2:] & jnp.uint32(0xFFFF0000)) | (bits[:, :d // 2] >> 16)
        for jj in range(p):
            h3_ref[0, pl.ds(r0 * p + jj, sub, stride=p), :] = packed[:, LANES * jj:LANES * (jj + 1)]


def _postmix_call(a_list, w_list, x, mod, gain, rw, *, tm):
    b, n, d = x.shape
    mb = mod.shape[0]
    n_in = len(a_list)
    p = d // 2 // LANES
    mod_map = (lambda i, j: (i, 0, 0)) if mb > 1 else (lambda i, j: (0, 0, 0))
    tok = lambda i, j: (i, j, 0)
    const2 = lambda i, j: (0, 0)
    in_specs = [pl.BlockSpec((1, tm, a.shape[2]), tok) for a in a_list]
    in_specs += [pl.BlockSpec(w.shape, const2) for w in w_list]
    in_specs += [
        pl.BlockSpec((1, tm, d), tok),
        pl.BlockSpec((1, 6, d), mod_map),
        pl.BlockSpec((1, d), const2),
        pl.BlockSpec((LANES, 2 * d), const2),
    ]
    return pl.pallas_call(
        functools.partial(_postmix_kernel, n_in=n_in),
        grid=(b, n // tm),
        in_specs=in_specs,
        out_specs=[
            pl.BlockSpec((1, tm, d), tok),
            pl.BlockSpec((1, tm * p, LANES), tok),
            pl.BlockSpec((1, N_EXPERTS, tm), lambda i, j: (i, 0, j)),
        ],
        out_shape=[
            jax.ShapeDtypeStruct((b, n, d), F32),
            jax.ShapeDtypeStruct((b, n * p, LANES), jnp.uint32),
            jax.ShapeDtypeStruct((b, N_EXPERTS, n), F32),
        ],
        compiler_params=_cparams(2, VMEM_LIMIT),
        name="postmix",
    )(*a_list, *w_list, x, mod, gain, rw)


def _prefix_incl(ones, tri):
    carry = jnp.zeros((ones.shape[0], 1), F32)
    outs = []
    for c in range(ones.shape[1] // LANES):
        blk = ones[:, LANES * c:LANES * (c + 1)]
        outs.append(_dot(blk.astype(BF16), tri) + carry)
        carry = carry + jnp.sum(blk, axis=1, keepdims=True)
    return jnp.concatenate(outs, axis=1)


def _topk_kernel(aff_ref, tri_ref, idx_ref, gate_ref, *, cap):
    g, ne, n = aff_ref.shape
    rows = g * ne
    aff = aff_ref[...].reshape(rows, n)
    bits = pltpu.bitcast(aff, jnp.int32)
    capf = jnp.float32(cap)

    def body(it, lo):
        sh = 28 - 2 * it
        new = lo
        for k in (1, 2, 3):
            cand = lo | jnp.left_shift(jnp.int32(k), sh)
            cnt = jnp.sum(jnp.where(bits >= cand, 1.0, 0.0), axis=1, keepdims=True)
            new = jnp.where(cnt >= capf, cand, new)
        return new

    thr = lax.fori_loop(0, 15, body, jnp.zeros((rows, 1), jnp.int32))
    gt = jnp.where(bits > thr, 1.0, 0.0)
    eq = jnp.where(bits == thr, 1.0, 0.0)
    need = capf - jnp.sum(gt, axis=1, keepdims=True)
    tri = tri_ref[...]
    eq_before = _prefix_incl(eq, tri) - eq
    sel = gt + eq * jnp.where(eq_before < need, 1.0, 0.0)
    slot = _prefix_incl(sel, tri).astype(jnp.int32) - 1
    tok = lax.broadcasted_iota(jnp.int32, (rows, n), 1)
    disp = jnp.where(sel > 0.0, tok - slot, -1)
    gate = aff
    for bit in range((n - 1).bit_length()):
        step = 1 << bit
        disp_in = pltpu.roll(disp, n - step, 1)
        gate_in = pltpu.roll(gate, n - step, 1)
        move_in = jnp.logical_and(disp_in >= 0, ((disp_in >> bit) & 1) == 1)
        stay = jnp.logical_and(disp >= 0, ((disp >> bit) & 1) == 0)
        gate = jnp.where(move_in, gate_in, gate)
        disp = jnp.where(move_in, disp_in, jnp.where(stay, disp, -1))
    idx_ref[...] = (tok + disp)[:, :cap].reshape(g, ne, cap)
    gate_ref[...] = gate[:, :cap].reshape(g, ne, cap)


def _topk_call(aff_t, tri, *, cap):
    b, e, n = aff_t.shape
    g = math.gcd(b, max(1, 4096 // n))
    return pl.pallas_call(
        functools.partial(_topk_kernel, cap=cap),
        grid=(b // g,),
        in_specs=[
            pl.BlockSpec((g, e, n), lambda i: (i, 0, 0)),
            pl.BlockSpec((LANES, LANES), lambda i: (0, 0)),
        ],
        out_specs=[pl.BlockSpec((g, e, cap), lambda i: (i, 0, 0))] * 2,
        out_shape=[jax.ShapeDtypeStruct((b, e, cap), jnp.int32), jax.ShapeDtypeStruct((b, e, cap), F32)],
        compiler_params=_cparams(1, VMEM_LIMIT),
        name="topk",
    )(aff_t, tri)


def _moe_kernel(idx_a_ref, idx_c_ref, gate_c_ref, h3_ref, wg_ref, wu_ref, wd_ref, o_ref, xs_0, xs_1, y_0, y_1,
                *, cap, p, c8, ne, n_tiles):
    g = pl.program_id(0)
    valid_a = g < n_tiles
    valid_b = jnp.logical_and(g >= 1, g <= n_tiles)
    valid_c = g >= 2
    steady = jnp.logical_and(g >= 2, g < n_tiles)

    @pl.when(jnp.logical_and(valid_c, (g - 2) % ne == 0))
    def _():
        o_ref[...] = jnp.zeros(o_ref.shape, o_ref.dtype)

    unroll, group = 8, 4

    def gather_row(xs_a, s, dst):
        t = idx_a_ref[0, 0, s]
        xs_a[dst, :] = h3_ref[0, pl.ds(pl.multiple_of(t * p, p), p), :]

    def gather(xs_a, straight):
        if straight:
            for s in range(cap):
                gather_row(xs_a, s, pl.ds(s * p, p))
        else:
            def body(i, carry):
                for u in range(unroll):
                    s = i * unroll + u
                    gather_row(xs_a, s, pl.ds(pl.multiple_of(s * p, p), p))
                return carry
            lax.fori_loop(0, cap // unroll, body, 0)

    def scatter_rows(y_c, s_list, src_list):
        sums = []
        for s, src in zip(s_list, src_list):
            t = idx_c_ref[0, 0, s]
            rows = pl.ds(pl.multiple_of(t * c8, c8), c8)
            sums.append((rows, o_ref[0, rows, :] + y_c[src, :] * gate_c_ref[0, 0, s]))
        for rows, val in sums:
            o_ref[0, rows, :] = val

    def scatter(y_c, straight):
        if straight:
            for g0 in range(0, cap, group):
                ss = [g0 + u for u in range(group)]
                scatter_rows(y_c, ss, [pl.ds(s * c8, c8) for s in ss])
        else:
            def body(i, carry):
                ss = [i * group + u for u in range(group)]
                scatter_rows(y_c, ss, [pl.ds(pl.multiple_of(s * c8, c8), c8) for s in ss])
                return carry
            lax.fori_loop(0, cap // group, body, 0)

    def load_rows(xs_b):
        lo, hi = [], []
        for j in range(p):
            w = xs_b[pl.ds(j, cap, stride=p), :]
            lo.append(pltpu.bitcast(w << 16, F32).astype(BF16))
            hi.append(pltpu.bitcast(w & jnp.uint32(0xFFFF0000), F32).astype(BF16))
        return jnp.concatenate(lo + hi, axis=1)

    def expert(xs, y_b):
        a = _dot(xs, wg_ref[0, 0])
        u_ = _dot(xs, wu_ref[0, 0])
        act = (_silu(a) * u_).astype(BF16)
        y = _dot(act, wd_ref[0, 0])
        for j in range(c8):
            y_b[pl.ds(j, cap, stride=c8), :] = y[:, LANES * j:LANES * (j + 1)]

    def step(xs_a, xs_b, y_b, y_c):
        @pl.when(steady)
        def _():
            xs = load_rows(xs_b)
            gather(xs_a, True)
            scatter(y_c, True)
            expert(xs, y_b)

        @pl.when(jnp.logical_not(steady))
        def _():
            @pl.when(valid_b)
            def _():
                expert(load_rows(xs_b), y_b)

            @pl.when(valid_a)
            def _():
                gather(xs_a, False)

            @pl.when(valid_c)
            def _():
                scatter(y_c, False)

    @pl.when(g % 2 == 0)
    def _():
        step(xs_0, xs_1, y_1, y_0)

    @pl.when(g % 2 == 1)
    def _():
        step(xs_1, xs_0, y_0, y_1)


def _moe_call(idx, h3, gate, wg, wu, wd, layer):
    bm, ne, cap = idx.shape
    d = wg.shape[2]
    p = d // 2 // LANES
    c8 = d // LANES
    n = h3.shape[1] // p
    n_tiles = bm * ne
    idx3 = idx.reshape(n_tiles, 1, cap)
    gate3 = gate.reshape(n_tiles, 1, cap)
    tile_a = lambda i: jnp.minimum(i, n_tiles - 1)
    tile_b = lambda i: jnp.clip(i - 1, 0, n_tiles - 1)
    tile_c = lambda i: jnp.clip(i - 2, 0, n_tiles - 1)
    wmap = lambda i: (layer, tile_b(i) % ne, 0, 0)
    smem = lambda f: pl.BlockSpec((1, 1, cap), lambda i: (f(i), 0, 0), memory_space=pltpu.SMEM)
    return pl.pallas_call(
        functools.partial(_moe_kernel, cap=cap, p=p, c8=c8, ne=ne, n_tiles=n_tiles),
        grid=(n_tiles + 2,),
        in_specs=[
            smem(tile_a), smem(tile_c), smem(tile_c),
            pl.BlockSpec((1, n * p, LANES), lambda i: (tile_a(i) // ne, 0, 0)),
            pl.BlockSpec((1, 1, d, wg.shape[3]), wmap),
            pl.BlockSpec((1, 1, d, wu.shape[3]), wmap),
            pl.BlockSpec((1, 1, wd.shape[2], d), wmap),
        ],
        out_specs=pl.BlockSpec((1, n * c8, LANES), lambda i: (tile_c(i) // ne, 0, 0), pipeline_mode=pl.Buffered(1)),
        out_shape=jax.ShapeDtypeStruct((bm, n * c8, LANES), F32),
        scratch_shapes=[pltpu.VMEM((cap * p, LANES), jnp.uint32), pltpu.VMEM((cap * p, LANES), jnp.uint32),
                        pltpu.VMEM((cap * c8, LANES), F32), pltpu.VMEM((cap * c8, LANES), F32)],
        compiler_params=_cparams(1, VMEM_LIMIT),
        name="moe",
    )(idx3, idx3, gate3, h3, wg, wu, wd)


def _resid_kernel(x_ref, moe_ref, mod_ref, o_ref):
    tm = x_ref.shape[1]
    sub = min(ROW_SUB, tm)
    for r0 in range(0, tm, sub):
        o_ref[0, r0:r0 + sub] = _gated_moe_residual(x_ref, moe_ref, mod_ref[0][5:6], r0, sub)


def _resid_call(x1, moe3, mod, *, tm):
    b, n, d = x1.shape
    mb = mod.shape[0]
    c8 = d // LANES
    mod_map = (lambda i, j: (i, 0, 0)) if mb > 1 else (lambda i, j: (0, 0, 0))
    tok = lambda i, j: (i, j, 0)
    return pl.pallas_call(
        _resid_kernel,
        grid=(b, n // tm),
        in_specs=[
            pl.BlockSpec((1, tm, d), tok),
            pl.BlockSpec((1, tm * c8, LANES), tok),
            pl.BlockSpec((1, 6, d), mod_map),
        ],
        out_specs=pl.BlockSpec((1, tm, d), tok),
        out_shape=jax.ShapeDtypeStruct((b, n, d), F32),
        compiler_params=_cparams(2, VMEM_LIMIT),
        name="resid",
    )(x1, moe3, mod)


def _rope_tables(n):
    t = jnp.arange(n)
    row = (t // GRID_W).astype(F32)
    col = (t % GRID_W).astype(F32)
    n_freq = HEAD_DIM // 4
    inv_freq = jnp.power(ROPE_BASE, -jnp.arange(n_freq, dtype=F32) / n_freq)
    ang = jnp.concatenate([row[:, None] * inv_freq, col[:, None] * inv_freq], axis=-1)
    cos, sin = jnp.cos(ang), jnp.sin(ang)
    cos_t = jnp.concatenate([cos, cos, cos, cos], axis=-1)
    sin_t = jnp.concatenate([-sin, sin, -sin, sin], axis=-1)
    return cos_t, sin_t


def _gain128(g):
    return jnp.concatenate([g, g]).reshape(1, LANES).astype(F32)


def _channel_mix(h_pack, aff_t, tri, weights, layer, *, flatten):
    wg, wu, wd = weights
    b, _, n = aff_t.shape
    cap = EC_CAPACITY_FACTOR * n // N_EXPERTS
    idx, gate = _topk_call(aff_t, tri, cap=cap)
    if flatten:
        idx = (idx + (jnp.arange(b, dtype=jnp.int32) * n)[:, None, None]).transpose(1, 0, 2).reshape(1, N_EXPERTS, b * cap)
        gate = gate.transpose(1, 0, 2).reshape(1, N_EXPERTS, b * cap)
        h_pack = h_pack.reshape(1, -1, LANES)
    return _moe_call(idx, h_pack, gate, wg, wu, wd, layer).reshape(b, -1, LANES)


def kernel(x, c, ctx, c_ctx, ada_w, ada_b, norm1_g, norm2_g, router_w, exp_w_gate, exp_w_up, exp_w_down, ev_w_in,
           ev_w_out, ev_q_gain, ev_k_gain, ev_sink, od_w_in, od_w_out, od_q_gain, od_k_gain, od_rpb):
    b, n, d = x.shape
    nc = ctx.shape[1]
    tm = min(512, n)
    tmb = min(1024, n)
    tmc = min(512, nc)

    rows = -(-(b + 1) // 8) * 8
    cs = jnp.concatenate([c, c_ctx[None], jnp.zeros((rows - b - 1, d), F32)], axis=0)
    mods = _mod_call(cs, ada_w, ada_b).reshape(ada_w.shape[0], rows, 6, d)

    cos_t, sin_t = _rope_tables(n)
    cs_dft = _channel_dft()
    tri = jnp.asarray(np.triu(np.ones((LANES, LANES), np.float32)), BF16)
    fw = FOURIER_GROUPS * LANES

    def router_split(l):
        rw = router_w[l].T
        rh = rw.astype(BF16)
        rl = (rw - rh.astype(F32)).astype(BF16)
        left = jnp.pad(jnp.concatenate([rh, rl], axis=0), ((0, LANES - 2 * N_EXPERTS), (0, 0)))
        right = jnp.pad(rh, ((0, LANES - N_EXPERTS), (0, 0)))
        return jnp.concatenate([left, right], axis=1)

    w_stacks = (exp_w_gate, exp_w_up, exp_w_down)
    as_layer = lambda ws: [w.reshape((1,) + s.shape[1:]) for w, s in zip(ws, w_stacks)]

    mod_x, mod_c = mods[0, :b], mods[0, b:b + 1]
    g1 = norm1_g[0].reshape(1, d)
    g2 = norm2_g[0].reshape(1, d)
    w_in = ev_w_in[0].astype(BF16)
    w_out = ev_w_out[0].astype(BF16)
    qg, kg = _gain128(ev_q_gain[0]), _gain128(ev_k_gain[0])
    sink = ev_sink[0].astype(F32)
    rw = router_split(0)

    ab_x, q_x, kd_x, vd_x = _premix_even_call(x, mod_x, g1, w_in, cs_dft, cos_t, sin_t, qg, kg, rope=True, tm=tmb)
    ab_c, q_c, kd_c, vd_c = _premix_even_call(ctx, mod_c, g1, w_in, cs_dft, cos_t[:nc], sin_t[:nc], qg, kg,
                                              rope=False, tm=tmc)
    a_x, *wts = _swa_call(sink, q_x, kd_x, vd_x, kd_c, vd_c, w_stacks, 0)
    wts = as_layer(wts)
    a_c = _ctx_attn_call(sink, q_c, kd_c, vd_c)
    four_x = _fourier_call(ab_x, *_dft_tables(n), tm=tmb)
    four_c = _fourier_call(ab_c, *_dft_tables(nc), tm=tmc)
    w_out_parts = [w_out[:fw], w_out[fw:]]
    x1, h3_x, aff_x = _postmix_call([four_x, a_x], w_out_parts, x, mod_x, g2, rw, tm=tmb)
    c1, h3_c, aff_c = _postmix_call([four_c, a_c], w_out_parts, ctx, mod_c, g2, rw, tm=tmc)
    moe_x = _channel_mix(h3_x, aff_x, tri, wts, 0, flatten=False)
    moe_c = _channel_mix(h3_c, aff_c, tri, wts, 0, flatten=True)

    mod_x0, mod_c0 = mod_x, mod_c
    mod_x, mod_c = mods[1, :b], mods[1, b:b + 1]
    g1 = norm1_g[1].reshape(1, d)
    g2 = norm2_g[1].reshape(1, d)
    w_in = od_w_in[0].astype(BF16)
    w_out = od_w_out[0].astype(BF16)
    qg, kg = _gain128(od_q_gain[0]), _gain128(od_k_gain[0])
    rw = router_split(1)

    q_x, k_x, v_x, x = _premix_odd_call(x1, moe_x, mod_x0, mod_x, g1, w_in, qg, kg, emit_x=True, tm=tmb)
    _, k_c, v_c = _premix_odd_call(c1, moe_c, mod_c0, mod_c, g1, w_in, qg, kg, emit_x=False, tm=tmc)
    a_x, *wts = _na_call(q_x, k_x, v_x, k_c, v_c, _na_bias_table(od_rpb[0]), w_stacks, 1)
    wts = as_layer(wts)
    x1, h3_x, aff_x = _postmix_call([a_x], [w_out], x, mod_x, g2, rw, tm=tmb)
    return _resid_call(x1, _channel_mix(h3_x, aff_x, tri, wts, 0, flatten=False), mod_x, tm=tmb)
```

```python
import functools
import math

import numpy as np
import jax
import jax.numpy as jnp
from jax import lax
from jax.experimental import pallas as pl
from jax.experimental.pallas import tpu as pltpu

GRID_W = 64
HEAD_DIM = 64
FOURIER_GROUPS = 4
SWA_WINDOW = 128
SWA_BLOCK = 128
NA_ROWS_MAX = 8
NA_COLS = 16
N_EXPERTS = 16
EC_CAPACITY_FACTOR = 2
ROPE_BASE = 10000.0
EPS = 1e-6

LANES = 128
NEG = -1e30
NA_QROWS = 4
NA_JBLK = 2
NA_KROWS = 12
DFT_SUB = 64
MOE_ROW_CHUNK = 128
SWA_QBLK = 4
ROW_SUB = 256
ONES_ROWS = 16
LOG2E = math.log2(math.e)
VMEM_LIMIT = 60 * 1024 * 1024

F32 = jnp.float32
BF16 = jnp.bfloat16


def _cparams(n_axes, vmem=None):
    return pltpu.CompilerParams(dimension_semantics=("arbitrary",) * n_axes, vmem_limit_bytes=vmem)


def _dot(a, b):
    return jnp.dot(a, b, preferred_element_type=F32)


def _dot_t(a, b):
    return lax.dot_general(a, b, (((1,), (1,)), ((), ())), preferred_element_type=F32)


def _split(a):
    hi = a.astype(BF16)
    lo = (a - hi.astype(F32)).astype(BF16)
    return hi, lo


def _dot3(a, b):
    ah, al = _split(a)
    bh, bl = _split(b)
    return _dot(ah, bh) + _dot(al, bh) + _dot(ah, bl)


def _silu(a):
    return a / (1.0 + jnp.exp(-a))


def _rms_mod(x, gain, shift, scale):
    ms = jnp.mean(x * x, axis=-1, keepdims=True)
    y = x * lax.rsqrt(ms + EPS) * gain
    return y * (1.0 + scale) + shift


def _mod_kernel(cs_ref, w_ref, b_ref, o_ref):
    o_ref[0] = _dot3(_silu(cs_ref[...]), w_ref[0]) + b_ref[0]


def _mod_call(cs, ada_w, ada_b):
    depth, d, d6 = ada_w.shape
    r = cs.shape[0]
    tn = 1024
    return pl.pallas_call(
        _mod_kernel,
        grid=(depth, d6 // tn),
        in_specs=[
            pl.BlockSpec((r, d), lambda l, j: (0, 0)),
            pl.BlockSpec((1, d, tn), lambda l, j: (l, 0, j)),
            pl.BlockSpec((1, 1, tn), lambda l, j: (l, 0, j)),
        ],
        out_specs=pl.BlockSpec((1, r, tn), lambda l, j: (l, 0, j)),
        out_shape=jax.ShapeDtypeStruct((depth, r, d6), F32),
        compiler_params=_cparams(2),
        name="mod",
    )(cs, ada_w, ada_b.reshape(depth, 1, d6))


def _head_norm(t, gain, lane):
    t2 = t * t
    s_lo = jnp.sum(jnp.where(lane < HEAD_DIM, t2, 0.0), axis=-1, keepdims=True)
    s_all = jnp.sum(t2, axis=-1, keepdims=True)
    ms = jnp.where(lane < HEAD_DIM, s_lo, s_all - s_lo) * (1.0 / HEAD_DIM)
    return t * lax.rsqrt(ms + EPS) * gain


def _rope(t, cos_t, sin_t, lane):
    partner = jnp.where((lane % HEAD_DIM) < HEAD_DIM // 2, pltpu.roll(t, LANES - HEAD_DIM // 2, 1),
                        pltpu.roll(t, HEAD_DIM // 2, 1))
    return t * cos_t + partner * sin_t


def _dup_halves(t, lane):
    sw = pltpu.roll(t, HEAD_DIM, 1)
    return jnp.where(lane < HEAD_DIM, t, sw), jnp.where(lane < HEAD_DIM, sw, t)


def _premix_even_kernel(x_ref, mod_ref, g_ref, w_ref, cs_ref, cos_ref, sin_ref, qg_ref, kg_ref,
                        ab_ref, q_ref, kd_ref, vt_ref, *, rope):
    tm = x_ref.shape[1]
    sub = min(ROW_SUB, tm)
    m = mod_ref[0]
    lane = lax.broadcasted_iota(jnp.int32, (sub, LANES), 1)
    fw = FOURIER_GROUPS * LANES
    qw = q_ref.shape[2]
    csb = cs_ref[...].astype(BF16)
    for r0 in range(0, tm, sub):
        rs = slice(r0, r0 + sub)
        hb = _rms_mod(x_ref[0, rs], g_ref[...], m[0:1], m[1:2]).astype(BF16)
        pf = _dot(hb, w_ref[:, 0:fw])
        for g in range(FOURIER_GROUPS):
            ab = _dot(pf[:, LANES * g:LANES * (g + 1)].astype(BF16), csb)
            ab_ref[0, rs, LANES * g:LANES * (g + 1)] = ab[:, :LANES].astype(BF16)
            ab_ref[0, rs, fw + LANES * g:fw + LANES * (g + 1)] = ab[:, LANES:].astype(BF16)
        pq = _dot(hb, w_ref[:, fw:fw + qw])
        if rope:
            cos_t = cos_ref[rs]
            sin_t = sin_ref[rs]
        for c in range(qw // LANES):
            t = _head_norm(pq[:, LANES * c:LANES * (c + 1)], qg_ref[...], lane)
            if rope:
                t = _rope(t, cos_t, sin_t, lane)
            q_ref[0, rs, LANES * c:LANES * (c + 1)] = t.astype(BF16)
        pk = _dot(hb, w_ref[:, fw + qw:fw + qw + 2 * LANES])
        k = _head_norm(pk[:, :LANES], kg_ref[...], lane)
        if rope:
            k = _rope(k, cos_t, sin_t, lane)
        k0, k1 = _dup_halves(k, lane)
        kd_ref[0, rs, :LANES] = k0.astype(BF16)
        kd_ref[0, rs, LANES:] = k1.astype(BF16)
        vt_ref[0, :, rs] = pk[:, LANES:].T.astype(BF16)


def _premix_even_call(x, mod, gain, w_bf, cs, cos_t, sin_t, qg, kg, *, rope, tm):
    b, n, d = x.shape
    mb = mod.shape[0]
    wtot = w_bf.shape[1]
    fw = FOURIER_GROUPS * LANES
    qw = wtot - fw - 2 * LANES
    mod_map = (lambda i, j: (i, 0, 0)) if mb > 1 else (lambda i, j: (0, 0, 0))
    tok = lambda i, j: (i, j, 0)
    const2 = lambda i, j: (0, 0)
    return pl.pallas_call(
        functools.partial(_premix_even_kernel, rope=rope),
        grid=(b, n // tm),
        in_specs=[
            pl.BlockSpec((1, tm, d), tok),
            pl.BlockSpec((1, 6, d), mod_map),
            pl.BlockSpec((1, d), const2),
            pl.BlockSpec((d, wtot), const2),
            pl.BlockSpec((LANES, 2 * LANES), const2),
            pl.BlockSpec((tm, LANES), lambda i, j: (j, 0)),
            pl.BlockSpec((tm, LANES), lambda i, j: (j, 0)),
            pl.BlockSpec((1, LANES), const2),
            pl.BlockSpec((1, LANES), const2),
        ],
        out_specs=[
            pl.BlockSpec((1, tm, 2 * fw), tok),
            pl.BlockSpec((1, tm, qw), tok),
            pl.BlockSpec((1, tm, 2 * LANES), tok),
            pl.BlockSpec((1, LANES, tm), lambda i, j: (i, 0, j)),
        ],
        out_shape=[
            jax.ShapeDtypeStruct((b, n, 2 * fw), BF16),
            jax.ShapeDtypeStruct((b, n, qw), BF16),
            jax.ShapeDtypeStruct((b, n, 2 * LANES), BF16),
            jax.ShapeDtypeStruct((b, LANES, n), BF16),
        ],
        compiler_params=_cparams(2, VMEM_LIMIT),
        name="premix_even",
    )(x, mod, gain, w_bf, cs, cos_t, sin_t, qg, kg)


def _gated_moe_residual(x_ref, moe_ref, gate_row, r0, sub):
    c8 = x_ref.shape[2] // LANES
    cols = []
    for j in range(c8):
        cs = slice(LANES * j, LANES * (j + 1))
        cols.append(x_ref[0, r0:r0 + sub, cs] + gate_row[:, cs] * moe_ref[0, pl.ds(r0 * c8 + j, sub, stride=c8), :])
    return jnp.concatenate(cols, axis=1)


def _premix_odd_kernel(x_ref, moe_ref, modp_ref, mod_ref, g_ref, w_ref, qg_ref, kg_ref, q_ref, k_ref, v_ref,
                       *xo_ref):
    tm = x_ref.shape[1]
    sub = min(ROW_SUB, tm)
    m = mod_ref[0]
    gate_prev = modp_ref[0][5:6]
    lane = lax.broadcasted_iota(jnp.int32, (sub, LANES), 1)
    wq = q_ref.shape[2]
    chunk = 4 * LANES
    for r0 in range(0, tm, sub):
        rs = slice(r0, r0 + sub)
        x = _gated_moe_residual(x_ref, moe_ref, gate_prev, r0, sub)
        if xo_ref:
            xo_ref[0][0, rs] = x
        hb = _rms_mod(x, g_ref[...], m[0:1], m[1:2]).astype(BF16)
        for c0 in range(0, 3 * wq, chunk):
            p = _dot(hb, w_ref[:, c0:c0 + chunk])
            for cc in range(chunk // LANES):
                col = c0 + cc * LANES
                t = p[:, cc * LANES:(cc + 1) * LANES]
                if col < wq:
                    q_ref[0, rs, col:col + LANES] = _head_norm(t, qg_ref[...], lane).astype(BF16)
                elif col < 2 * wq:
                    k_ref[0, rs, col - wq:col - wq + LANES] = _head_norm(t, kg_ref[...], lane).astype(BF16)
                else:
                    v_ref[0, col - 2 * wq:col - 2 * wq + LANES, rs] = t.T.astype(BF16)


def _premix_odd_call(x1, moe3, mod_prev, mod, gain, w_bf, qg, kg, *, emit_x, tm):
    b, n, d = x1.shape
    mb = mod.shape[0]
    wq = w_bf.shape[1] // 3
    c8 = d // LANES
    mod_map = (lambda i, j: (i, 0, 0)) if mb > 1 else (lambda i, j: (0, 0, 0))
    tok = lambda i, j: (i, j, 0)
    const2 = lambda i, j: (0, 0)
    out_specs = [pl.BlockSpec((1, tm, wq), tok)] * 2 + [pl.BlockSpec((1, wq, tm), lambda i, j: (i, 0, j))]
    out_shape = [jax.ShapeDtypeStruct((b, n, wq), BF16)] * 2 + [jax.ShapeDtypeStruct((b, wq, n), BF16)]
    if emit_x:
        out_specs.append(pl.BlockSpec((1, tm, d), tok))
        out_shape.append(jax.ShapeDtypeStruct((b, n, d), F32))
    return pl.pallas_call(
        _premix_odd_kernel,
        grid=(b, n // tm),
        in_specs=[
            pl.BlockSpec((1, tm, d), tok),
            pl.BlockSpec((1, tm * c8, LANES), tok),
            pl.BlockSpec((1, 6, d), mod_map),
            pl.BlockSpec((1, 6, d), mod_map),
            pl.BlockSpec((1, d), const2),
            pl.BlockSpec((d, 3 * wq), const2),
            pl.BlockSpec((1, LANES), const2),
            pl.BlockSpec((1, LANES), const2),
        ],
        out_specs=out_specs,
        out_shape=out_shape,
        compiler_params=_cparams(2, VMEM_LIMIT),
        name="premix_odd",
    )(x1, moe3, mod_prev, mod, gain, w_bf, qg, kg)


def _col_reduce(x, op, slab=64):
    r = x.shape[0]
    if r > slab and r % slab == 0:
        x = op(x.reshape(r // slab, slab, x.shape[1]), axis=0)
    return op(x, axis=0, keepdims=True)


def _softmax_pv_t(s_t, v_t, sink=None):
    mx = _col_reduce(s_t, jnp.max)
    if sink is not None:
        mx = jnp.maximum(mx, sink)
    p = jnp.exp2((s_t - mx).astype(BF16))
    nd = v_t.shape[0]
    o = _dot(jnp.concatenate([v_t, jnp.ones((ONES_ROWS, v_t.shape[1]), BF16)], axis=0), p)
    den = o[nd:nd + 1]
    if sink is not None:
        den = den + jnp.exp2(sink - mx)
    return o[:nd] / den


def _mask_half(qc, lane, half):
    q32 = qc.astype(F32) * (HEAD_DIM ** -0.5 * LOG2E)
    keep = (lane < HEAD_DIM) if half == 0 else (lane >= HEAD_DIM)
    return jnp.where(keep, q32, 0.0).astype(BF16)


def _gqa_scores(q_ref, h, kd, n_kv, masks, lane, r0, tq):
    rs = slice(r0, r0 + tq)
    cols_per_kv = q_ref.shape[2] // LANES // n_kv
    q_rows = []
    for cc in range(cols_per_kv):
        c = cols_per_kv * h + cc
        for half in range(2):
            q_rows.append(_mask_half(q_ref[0, rs, LANES * c:LANES * (c + 1)], lane, half))
    qm = jnp.concatenate(q_rows, axis=0)
    s = _dot_t(kd, qm)
    pieces, k0 = [], 0
    for rows, mask in masks:
        blk = s[k0:k0 + rows]
        pieces.append(blk if mask is None else jnp.where(mask, blk, NEG))
        k0 += rows
    return jnp.concatenate(pieces, axis=0) if len(pieces) > 1 else pieces[0]


def _gqa_finish(sink_ref, o_ref, h, s_t, v_t, r0, tq):
    rs = slice(r0, r0 + tq)
    n_kv = v_t.shape[0] // HEAD_DIM
    cols_per_kv = o_ref.shape[2] // LANES // n_kv
    heads = 2 * cols_per_kv
    head_of_lane = lax.broadcasted_iota(jnp.int32, (1, heads * tq), 1) // tq
    sink = jnp.zeros((1, heads * tq), F32)
    for a in range(heads):
        sink = jnp.where(head_of_lane == a, sink_ref[heads * h + a] * LOG2E, sink)
    o_t = _softmax_pv_t(s_t, v_t, sink)[HEAD_DIM * h:HEAD_DIM * (h + 1)]
    for cc in range(cols_per_kv):
        c = cols_per_kv * h + cc
        tile = jnp.concatenate([o_t[:, tq * (2 * cc):tq * (2 * cc + 1)], o_t[:, tq * (2 * cc + 1):tq * (2 * cc + 2)]],
                               axis=0)
        o_ref[0, rs, LANES * c:LANES * (c + 1)] = tile.T.astype(BF16)


def _swa_kernel(sink_ref, q_ref, km_ref, k0_ref, kp_ref, vm_ref, v0_ref, vp_ref, kc_ref, vc_ref, *rest, n_step):
    o_ref = _ride_cast(rest)
    g = pl.program_id(1)
    tq = SWA_BLOCK
    nc = kc_ref.shape[1]
    n_kv = kc_ref.shape[2] // LANES
    heads = q_ref.shape[2] // HEAD_DIM // n_kv
    lane = lax.broadcasted_iota(jnp.int32, (tq, LANES), 1)
    jk = lax.broadcasted_iota(jnp.int32, (SWA_BLOCK, heads * tq), 0)
    iq = lax.broadcasted_iota(jnp.int32, (SWA_BLOCK, heads * tq), 1) % tq
    below, above = jk >= iq, jk <= iq
    nq = q_ref.shape[1] // SWA_BLOCK
    blk = lambda s: slice(SWA_BLOCK * s, SWA_BLOCK * (s + 1))

    def key_block(s, hs):
        return km_ref[0, :, hs] if s < 0 else kp_ref[0, :, hs] if s == nq else k0_ref[0, blk(s), hs]

    def value_block(s):
        return vm_ref[0] if s < 0 else vp_ref[0] if s == nq else v0_ref[0, :, blk(s)]

    def scores(sub, h):
        hs = slice(LANES * h, LANES * (h + 1))
        first = jnp.logical_and(below, g > 0) if sub == 0 else below
        last = jnp.logical_and(above, g < n_step - 1) if sub == nq - 1 else above
        masks = [(nc, None), (SWA_BLOCK, first), (SWA_BLOCK, None), (SWA_BLOCK, last)]
        kd = jnp.concatenate([kc_ref[0, :, hs]] + [key_block(s, hs) for s in (sub - 1, sub, sub + 1)], axis=0)
        return _gqa_scores(q_ref, h, kd, n_kv, masks, lane, SWA_BLOCK * sub, tq)

    groups = [(sub, h) for sub in range(nq) for h in range(n_kv)]
    ahead = 3
    pending = [scores(*grp) for grp in groups[:ahead]]
    for i, (sub, h) in enumerate(groups):
        s_t = pending.pop(0)
        if i + ahead < len(groups):
            pending.append(scores(*groups[i + ahead]))
        v_t = jnp.concatenate([vc_ref[0]] + [value_block(s) for s in (sub - 1, sub, sub + 1)], axis=1)
        _gqa_finish(sink_ref, o_ref, h, s_t, v_t, SWA_BLOCK * sub, tq)


def _cast_riders(ws, layer, total, flat):
    ins, outs, shapes, args = [], [], [], []
    for w in ws:
        l, e, r, c = w.shape
        rows = e * r // total
        ins.append(pl.BlockSpec((rows, c), lambda i, j: (layer * total + flat(i, j), 0)))
        outs.append(pl.BlockSpec((rows, c), lambda i, j: (flat(i, j), 0)))
        shapes.append(jax.ShapeDtypeStruct((e * r, c), BF16))
        args.append(w.reshape(l * e * r, c))
    return ins, outs, shapes, args


def _ride_cast(rest):
    n_cast = (len(rest) - 1) // 2
    for w_ref, c_ref in zip(rest[:n_cast], rest[n_cast + 1:]):
        c_ref[...] = w_ref[...].astype(BF16)
    return rest[n_cast]


def _swa_call(sink, q, kd, vt, kdc, vtc, ws, layer):
    b, n, qw = q.shape
    nc = kdc.shape[1]
    kw = kd.shape[2]
    vw = vt.shape[1]
    n_blk = n // SWA_BLOCK
    nq = min(SWA_QBLK, n_blk)
    n_step = n_blk // nq
    prev = lambda j: jnp.maximum(nq * j - 1, 0)
    nxt = lambda j: jnp.minimum(nq * j + nq, n_blk - 1)
    kspec = lambda f: pl.BlockSpec((1, SWA_BLOCK, kw), lambda i, j: (i, f(j), 0))
    vspec = lambda f: pl.BlockSpec((1, vw, SWA_BLOCK), lambda i, j: (i, 0, f(j)))
    c_in, c_out, c_shape, c_args = _cast_riders(ws, layer, b * n_step, lambda i, j: i * n_step + j)
    return pl.pallas_call(
        functools.partial(_swa_kernel, n_step=n_step),
        grid=(b, n_step),
        in_specs=[
            pl.BlockSpec(memory_space=pltpu.SMEM),
            pl.BlockSpec((1, nq * SWA_BLOCK, qw), lambda i, j: (i, j, 0)),
            kspec(prev), pl.BlockSpec((1, nq * SWA_BLOCK, kw), lambda i, j: (i, j, 0)), kspec(nxt),
            vspec(prev), pl.BlockSpec((1, vw, nq * SWA_BLOCK), lambda i, j: (i, 0, j)), vspec(nxt),
            pl.BlockSpec((1, nc, kw), lambda i, j: (i, 0, 0)),
            pl.BlockSpec((1, vw, nc), lambda i, j: (i, 0, 0)),
        ] + c_in,
        out_specs=[pl.BlockSpec((1, nq * SWA_BLOCK, qw), lambda i, j: (i, j, 0))] + c_out,
        out_shape=[jax.ShapeDtypeStruct((b, n, qw), BF16)] + c_shape,
        compiler_params=_cparams(2, VMEM_LIMIT),
        name="swa",
    )(sink, q, kd, kd, kd, vt, vt, vt, kdc, vtc, *c_args)


def _ctx_attn_kernel(sink_ref, q_ref, kc_ref, vc_ref, o_ref):
    tq = q_ref.shape[1]
    lane = lax.broadcasted_iota(jnp.int32, (tq, LANES), 1)
    n_kv = kc_ref.shape[2] // LANES
    scores = [_gqa_scores(q_ref, h, kc_ref[0, :, LANES * h:LANES * (h + 1)], n_kv, [(kc_ref.shape[1], None)], lane,
                          0, tq) for h in range(n_kv)]
    for h, s_t in enumerate(scores):
        _gqa_finish(sink_ref, o_ref, h, s_t, vc_ref[0], 0, tq)


def _ctx_attn_call(sink, q, kd, vt):
    b, nc, qw = q.shape
    kw = kd.shape[2]
    vw = vt.shape[1]
    m = lambda i: (i, 0, 0)
    return pl.pallas_call(
        _ctx_attn_kernel,
        grid=(b,),
        in_specs=[
            pl.BlockSpec(memory_space=pltpu.SMEM),
            pl.BlockSpec((1, nc, qw), m),
            pl.BlockSpec((1, nc, kw), m),
            pl.BlockSpec((1, vw, nc), m),
        ],
        out_specs=pl.BlockSpec((1, nc, qw), m),
        out_shape=jax.ShapeDtypeStruct((b, nc, qw), BF16),
        compiler_params=_cparams(1, VMEM_LIMIT),
        name="ctx_attn",
    )(sink, q, kd, vt)


def _na_kernel(q_ref, k0_ref, k1_ref, k2_ref, k3_ref, v0_ref, v1_ref, v2_ref, v3_ref, kc_ref, vc_ref, tb_ref, *rest,
               n_j, rows):
    o_ref = _ride_cast(rest)
    g = pl.program_id(1)
    tq = NA_QROWS * GRID_W
    tk = NA_KROWS * GRID_W
    nc = kc_ref.shape[1]
    k_blocks = [k0_ref, k1_ref, k2_ref, k3_ref]
    v_blocks = [v0_ref, v1_ref, v2_ref, v3_ref]
    base = jnp.clip(NA_JBLK * g - 1, 0, n_j - (NA_JBLK + 2))
    lane = lax.broadcasted_iota(jnp.int32, (tq, LANES), 1)
    k_row = (lax.broadcasted_iota(jnp.int32, (nc + tk, LANES), 0) - nc) // GRID_W
    k_lane = lax.broadcasted_iota(jnp.int32, (nc + tk, LANES), 1)
    is_local = lax.broadcasted_iota(jnp.int32, (nc + tk, LANES), 0) >= nc
    q_grid_row = (lax.broadcasted_iota(jnp.int32, (2 * tq, LANES), 0) % tq) // GRID_W
    q_extra = jnp.where(lax.broadcasted_iota(jnp.int32, (2 * tq, LANES), 1) == q_grid_row, 1.0, 0.0).astype(BF16)
    n_dr = 2 * NA_ROWS_MAX - 1
    shifted, k_extras, deltas = [], [], []
    for u in range(NA_JBLK):
        j = NA_JBLK * g + u
        w0_blk = jnp.clip(j - 1, 0, n_j - 3)
        w0 = NA_QROWS * w0_blk
        lo = jnp.zeros((nc + tk, LANES), jnp.int32)
        for rr in range(NA_QROWS):
            r = NA_QROWS * j + rr
            lo_rr = jnp.clip(r - NA_ROWS_MAX // 2, 0, rows - NA_ROWS_MAX) - w0
            lo = jnp.where(k_lane == rr, lo_rr, lo)
        outside = jnp.logical_or(k_row < lo, k_row >= lo + NA_ROWS_MAX)
        k_extras.append(jnp.where(jnp.logical_and(jnp.logical_and(is_local, k_lane < NA_QROWS), outside), NEG, 0.0)
                        .astype(BF16))
        shifted.append(w0_blk - base == 1)
        deltas.append(w0 - NA_QROWS * j)

    def scores(u, c):
        cs = slice(LANES * c, LANES * (c + 1))
        qc = q_ref[0, tq * u:tq * (u + 1), cs]
        k_loc = [jnp.where(shifted[u], k_blocks[d + 1][0, :, cs], k_blocks[d][0, :, cs]) for d in range(3)]
        kl = jnp.concatenate([kc_ref[0, :, cs]] + k_loc, axis=0)
        qm = jnp.concatenate([_mask_half(qc, lane, 0), _mask_half(qc, lane, 1)], axis=0)
        return _dot_t(jnp.concatenate([kl, k_extras[u]], axis=1), jnp.concatenate([qm, q_extra], axis=1))

    groups = [(u, c) for c in range(q_ref.shape[2] // LANES) for u in range(NA_JBLK)]
    ahead = 2
    pending = [scores(*grp) for grp in groups[:ahead]]
    for i, (u, c) in enumerate(groups):
        cs = slice(LANES * c, LANES * (c + 1))
        s = pending.pop(0)
        if i + ahead < len(groups):
            pending.append(scores(*groups[i + ahead]))
        v_loc = [jnp.where(shifted[u], v_blocks[d + 1][0, cs, :], v_blocks[d][0, cs, :]) for d in range(3)]
        v_t = jnp.concatenate([vc_ref[0, cs, :]] + v_loc, axis=1)
        bias_rows = []
        for kr in range(NA_KROWS):
            pieces = []
            for half in range(2):
                for t in range(NA_QROWS // 2):
                    dr = deltas[u] + kr - 2 * t + NA_ROWS_MAX - 1
                    pieces.append(tb_ref[2 * c + half, jnp.clip(dr, 0, n_dr)])
            bias_rows.append(jnp.concatenate(pieces, axis=1))
        bias = jnp.concatenate(bias_rows, axis=0)
        s_t = jnp.concatenate([s[:nc], s[nc:] + bias], axis=0)
        o_t = _softmax_pv_t(s_t, v_t)
        tile = jnp.concatenate([o_t[:HEAD_DIM, :tq], o_t[HEAD_DIM:, tq:]], axis=0)
        o_ref[0, tq * u:tq * (u + 1), cs] = tile.T.astype(BF16)


def _na_call(q, k, vt, kc, vtc, tb, ws, layer):
    b, n, w = q.shape
    nc = kc.shape[1]
    rows = n // GRID_W
    n_j = rows // NA_QROWS
    n_g = n_j // NA_JBLK
    tq = NA_QROWS * GRID_W
    cur = lambda i, j: (i, j, 0)
    first = lambda j: jnp.clip(NA_JBLK * j - 1, 0, n_j - (NA_JBLK + 2))
    kspec = lambda d: pl.BlockSpec((1, tq, w), lambda i, j: (i, first(j) + d, 0))
    vspec = lambda d: pl.BlockSpec((1, w, tq), lambda i, j: (i, 0, first(j) + d))
    c_in, c_out, c_shape, c_args = _cast_riders(ws, layer, b * n_g, lambda i, j: i * n_g + j)
    return pl.pallas_call(
        functools.partial(_na_kernel, n_j=n_j, rows=rows),
        grid=(b, n_g),
        in_specs=[
            pl.BlockSpec((1, NA_JBLK * tq, w), cur),
            kspec(0), kspec(1), kspec(2), kspec(3),
            vspec(0), vspec(1), vspec(2), vspec(3),
            pl.BlockSpec((1, nc, w), lambda i, j: (i, 0, 0)),
            pl.BlockSpec((1, w, nc), lambda i, j: (i, 0, 0)),
            pl.BlockSpec(tb.shape, lambda i, j: (0, 0, 0, 0)),
        ] + c_in,
        out_specs=[pl.BlockSpec((1, NA_JBLK * tq, w), cur)] + c_out,
        out_shape=[jax.ShapeDtypeStruct((b, n, w), BF16)] + c_shape,
        compiler_params=_cparams(2, VMEM_LIMIT),
        name="na",
    )(q, k, k, k, k, vt, vt, vt, vt, kc, vtc, tb, *c_args)


def _na_bias_table(rpb):
    col_q = np.arange(GRID_W)[None, :]
    col_k = np.arange(GRID_W)[:, None]
    c_start = np.clip(col_q - NA_COLS // 2, 0, GRID_W - NA_COLS)
    col_valid = (col_k >= c_start) & (col_k < c_start + NA_COLS)
    dc_idx = np.clip(col_k - col_q + NA_COLS - 1, 0, 2 * NA_COLS - 2)
    t = jnp.where(col_valid[None, None], (rpb.astype(F32) * LOG2E)[:, :, dc_idx], NEG)
    pad = jnp.full_like(t[:, :1], NEG)
    t = jnp.concatenate([pad, t, pad], axis=1)
    return jnp.concatenate([t[:, 1:], t[:, :-1]], axis=-1)


def _fourier_kernel(ab_ref, ca_ref, sa_ref, cb_ref, sb_ref, rev_ref, o_ref, fold_ref, rev_scr):
    n = ab_ref.shape[1]
    nh = n // 2
    tm, fw = o_ref.shape[1], o_ref.shape[2]

    @pl.when(pl.program_id(1) == 0)
    def _():
        for m in range(nh // LANES):
            blk = ab_ref[0, n - LANES * (m + 1):n - LANES * m, :]
            rev_scr[LANES * m:LANES * (m + 1), :] = _dot(rev_ref[...], blk)
        ck = min(256, nh)
        row = lax.broadcasted_iota(jnp.int32, (ck, 1), 0)
        for r0 in range(0, nh, ck):
            prev = (r0 - 1) % nh
            shifted = jnp.where(row == 0, rev_scr[prev:prev + 1, :], pltpu.roll(rev_scr[r0:r0 + ck, :], 1, 0))
            lo = ab_ref[0, r0:r0 + ck, :].astype(F32)
            a2 = lo[:, :fw] + shifted[:, :fw]
            b2 = lo[:, fw:] - shifted[:, fw:]
            if r0 == 0:
                a2 = jnp.where(row == 0, lo[:, :fw], a2)
                b2 = jnp.where(row == 0, shifted[:, :fw], b2)
            fold_ref[r0:r0 + ck, :fw] = a2.astype(BF16)
            fold_ref[r0:r0 + ck, fw:] = b2.astype(BF16)

    cb = cb_ref[...]
    sb = sb_ref[...]
    sub = min(ROW_SUB, tm)
    for r0 in range(0, tm, sub):
        cm, sm = [], []
        for a in range(r0 // DFT_SUB, (r0 + sub) // DFT_SUB):
            ca = ca_ref[a:a + 1, :]
            sa = sa_ref[a:a + 1, :]
            cm.append((ca * cb - sa * sb).astype(BF16))
            sm.append((sa * cb + ca * sb).astype(BF16))
        y = _dot(jnp.concatenate(cm, axis=0), fold_ref[:, :fw]) - _dot(jnp.concatenate(sm, axis=0), fold_ref[:, fw:])
        o_ref[0, r0:r0 + sub] = y.astype(BF16)


def _fourier_call(ab, ca, sa, cb, sb, *, tm):
    b, n, w2 = ab.shape
    nh = n // 2
    fw = w2 // 2
    na = tm // DFT_SUB
    rev = jnp.asarray(np.eye(LANES, dtype=np.float32)[::-1], BF16)
    return pl.pallas_call(
        _fourier_kernel,
        grid=(b, n // tm),
        in_specs=[
            pl.BlockSpec((1, n, w2), lambda i, j: (i, 0, 0)),
            pl.BlockSpec((na, nh), lambda i, j: (j, 0)),
            pl.BlockSpec((na, nh), lambda i, j: (j, 0)),
            pl.BlockSpec((DFT_SUB, nh), lambda i, j: (0, 0)),
            pl.BlockSpec((DFT_SUB, nh), lambda i, j: (0, 0)),
            pl.BlockSpec((LANES, LANES), lambda i, j: (0, 0)),
        ],
        out_specs=pl.BlockSpec((1, tm, fw), lambda i, j: (i, j, 0)),
        out_shape=jax.ShapeDtypeStruct((b, n, fw), BF16),
        scratch_shapes=[pltpu.VMEM((nh, w2), BF16), pltpu.VMEM((nh, w2), F32)],
        compiler_params=_cparams(2, VMEM_LIMIT),
        name="fourier",
    )(ab, ca, sa, cb, sb, rev)


def _dft_tables(n):
    nh = n // 2
    k = jnp.arange(nh, dtype=jnp.int32)
    a = jnp.arange(n // DFT_SUB, dtype=jnp.int32)
    bb = jnp.arange(DFT_SUB, dtype=jnp.int32)
    ang_a = ((DFT_SUB * a[:, None] * k[None, :]) % n).astype(F32) * (2.0 * math.pi / n)
    ang_b = ((bb[:, None] * k[None, :]) % n).astype(F32) * (2.0 * math.pi / n)
    s = float(n) ** -0.5
    nyq = -jnp.cos(math.pi * bb.astype(F32)) * s
    sin_b = jnp.concatenate([nyq[:, None], (jnp.sin(ang_b) * s)[:, 1:]], axis=1)
    return jnp.cos(ang_a), jnp.sin(ang_a), jnp.cos(ang_b) * s, sin_b


def _channel_dft():
    c = jnp.arange(LANES, dtype=jnp.int32)
    ang = ((c[:, None] * c[None, :]) % LANES).astype(F32) * (2.0 * math.pi / LANES)
    s = float(LANES) ** -0.5
    return jnp.concatenate([jnp.cos(ang) * s, jnp.sin(ang) * s], axis=1)


def _postmix_kernel(*refs, n_in):
    a_refs = refs[:n_in]
    w_refs = refs[n_in:2 * n_in]
    x_ref, mod_ref, g_ref, rw_ref, x1_ref, h3_ref, aff_ref = refs[2 * n_in:]
    tm, d = x_ref.shape[1], x_ref.shape[2]
    sub = min(ROW_SUB, tm)
    p = d // 2 // LANES
    m = mod_ref[0]
    for r0 in range(0, tm, sub):
        rs = slice(r0, r0 + sub)
        y = None
        for a_ref, w_ref in zip(a_refs, w_refs):
            t = _dot(a_ref[0, rs], w_ref[...])
            y = t if y is None else y + t
        x1 = x_ref[0, rs] + m[2:3] * y
        x1_ref[0, rs] = x1
        h2 = _rms_mod(x1, g_ref[...], m[3:4], m[4:5])
        hh, hl = _split(h2)
        parts = _dot_t(rw_ref[...], jnp.concatenate([hh, hl], axis=1))
        lt = parts[:N_EXPERTS] + parts[N_EXPERTS:2 * N_EXPERTS]
        ex = jnp.exp(lt - jnp.max(lt, axis=0, keepdims=True))
        aff_ref[0, :, rs] = ex / jnp.sum(ex, axis=0, keepdims=True)
        bits = pltpu.bitcast(hh.astype(F32), jnp.uint32)
        packed = (bits[:, d // 2:] & jnp.uint32(0xFFFF0000)) | (bits[:, :d // 2] >> 16)
        for jj in range(p):
            h3_ref[0, pl.ds(r0 * p + jj, sub, stride=p), :] = packed[:, LANES * jj:LANES * (jj + 1)]


def _postmix_call(a_list, w_list, x, mod, gain, rw, *, tm):
    b, n, d = x.shape
    mb = mod.shape[0]
    n_in = len(a_list)
    p = d // 2 // LANES
    mod_map = (lambda i, j: (i, 0, 0)) if mb > 1 else (lambda i, j: (0, 0, 0))
    tok = lambda i, j: (i, j, 0)
    const2 = lambda i, j: (0, 0)
    in_specs = [pl.BlockSpec((1, tm, a.shape[2]), tok) for a in a_list]
    in_specs += [pl.BlockSpec(w.shape, const2) for w in w_list]
    in_specs += [
        pl.BlockSpec((1, tm, d), tok),
        pl.BlockSpec((1, 6, d), mod_map),
        pl.BlockSpec((1, d), const2),
        pl.BlockSpec((LANES, 2 * d), const2),
    ]
    return pl.pallas_call(
        functools.partial(_postmix_kernel, n_in=n_in),
        grid=(b, n // tm),
        in_specs=in_specs,
        out_specs=[
            pl.BlockSpec((1, tm, d), tok),
            pl.BlockSpec((1, tm * p, LANES), tok),
            pl.BlockSpec((1, N_EXPERTS, tm), lambda i, j: (i, 0, j)),
        ],
        out_shape=[
            jax.ShapeDtypeStruct((b, n, d), F32),
            jax.ShapeDtypeStruct((b, n * p, LANES), jnp.uint32),
            jax.ShapeDtypeStruct((b, N_EXPERTS, n), F32),
        ],
        compiler_params=_cparams(2, VMEM_LIMIT),
        name="postmix",
    )(*a_list, *w_list, x, mod, gain, rw)


def _prefix_incl(ones, tri):
    carry = jnp.zeros((ones.shape[0], 1), F32)
    outs = []
    for c in range(ones.shape[1] // LANES):
        blk = ones[:, LANES * c:LANES * (c + 1)]
        outs.append(_dot(blk.astype(BF16), tri) + carry)
        carry = carry + jnp.sum(blk, axis=1, keepdims=True)
    return jnp.concatenate(outs, axis=1)


def _topk_kernel(aff_ref, tri_ref, idx_ref, gate_ref, *, cap):
    g, ne, n = aff_ref.shape
    rows = g * ne
    aff = aff_ref[...].reshape(rows, n)
    bits = pltpu.bitcast(aff, jnp.int32)
    capf = jnp.float32(cap)

    def body(it, lo):
        sh = 28 - 2 * it
        new = lo
        for k in (1, 2, 3):
            cand = lo | jnp.left_shift(jnp.int32(k), sh)
            cnt = jnp.sum(jnp.where(bits >= cand, 1.0, 0.0), axis=1, keepdims=True)
            new = jnp.where(cnt >= capf, cand, new)
        return new

    thr = lax.fori_loop(0, 15, body, jnp.zeros((rows, 1), jnp.int32))
    gt = jnp.where(bits > thr, 1.0, 0.0)
    eq = jnp.where(bits == thr, 1.0, 0.0)
    need = capf - jnp.sum(gt, axis=1, keepdims=True)
    tri = tri_ref[...]
    eq_before = _prefix_incl(eq, tri) - eq
    sel = gt + eq * jnp.where(eq_before < need, 1.0, 0.0)
    slot = _prefix_incl(sel, tri).astype(jnp.int32) - 1
    tok = lax.broadcasted_iota(jnp.int32, (rows, n), 1)
    disp = jnp.where(sel > 0.0, tok - slot, -1)
    gate = aff
    for bit in range((n - 1).bit_length()):
        step = 1 << bit
        disp_in = pltpu.roll(disp, n - step, 1)
        gate_in = pltpu.roll(gate, n - step, 1)
        move_in = jnp.logical_and(disp_in >= 0, ((disp_in >> bit) & 1) == 1)
        stay = jnp.logical_and(disp >= 0, ((disp >> bit) & 1) == 0)
        gate = jnp.where(move_in, gate_in, gate)
        disp = jnp.where(move_in, disp_in, jnp.where(stay, disp, -1))
    idx_ref[...] = (tok + disp)[:, :cap].reshape(g, ne, cap)
    gate_ref[...] = gate[:, :cap].reshape(g, ne, cap)


def _topk_call(aff_t, tri, *, cap):
    b, e, n = aff_t.shape
    g = math.gcd(b, max(1, 4096 // n))
    return pl.pallas_call(
        functools.partial(_topk_kernel, cap=cap),
        grid=(b // g,),
        in_specs=[
            pl.BlockSpec((g, e, n), lambda i: (i, 0, 0)),
            pl.BlockSpec((LANES, LANES), lambda i: (0, 0)),
        ],
        out_specs=[pl.BlockSpec((g, e, cap), lambda i: (i, 0, 0))] * 2,
        out_shape=[jax.ShapeDtypeStruct((b, e, cap), jnp.int32), jax.ShapeDtypeStruct((b, e, cap), F32)],
        compiler_params=_cparams(1, VMEM_LIMIT),
        name="topk",
    )(aff_t, tri)


def _moe_kernel(idx_a_ref, idx_c_ref, gate_c_ref, h3_ref, wg_ref, wu_ref, wd_ref, o_ref, xs_0, xs_1, y_0, y_1,
                *, cap, p, c8, ne, n_tiles):
    g = pl.program_id(0)
    valid_a = g < n_tiles
    valid_b = jnp.logical_and(g >= 1, g <= n_tiles)
    valid_c = g >= 2
    steady = jnp.logical_and(g >= 2, g < n_tiles)

    @pl.when(jnp.logical_and(valid_c, (g - 2) % ne == 0))
    def _():
        o_ref[...] = jnp.zeros(o_ref.shape, o_ref.dtype)

    unroll, group = 8, 4

    def gather_row(xs_a, s, dst):
        t = idx_a_ref[0, 0, s]
        xs_a[dst, :] = h3_ref[0, pl.ds(pl.multiple_of(t * p, p), p), :]

    def gather(xs_a, straight):
        if straight:
            for s in range(cap):
                gather_row(xs_a, s, pl.ds(s * p, p))
        else:
            def body(i, carry):
                for u in range(unroll):
                    s = i * unroll + u
                    gather_row(xs_a, s, pl.ds(pl.multiple_of(s * p, p), p))
                return carry
            lax.fori_loop(0, cap // unroll, body, 0)

    def scatter_rows(y_c, s_list, src_list):
        sums = []
        for s, src in zip(s_list, src_list):
            t = idx_c_ref[0, 0, s]
            rows = pl.ds(pl.multiple_of(t * c8, c8), c8)
            sums.append((rows, o_ref[0, rows, :] + y_c[src, :] * gate_c_ref[0, 0, s]))
        for rows, val in sums:
            o_ref[0, rows, :] = val

    def scatter(y_c, straight):
        if straight:
            for g0 in range(0, cap, group):
                ss = [g0 + u for u in range(group)]
                scatter_rows(y_c, ss, [pl.ds(s * c8, c8) for s in ss])
        else:
            def body(i, carry):
                ss = [i * group + u for u in range(group)]
                scatter_rows(y_c, ss, [pl.ds(pl.multiple_of(s * c8, c8), c8) for s in ss])
                return carry
            lax.fori_loop(0, cap // group, body, 0)

    def load_rows(xs_b):
        lo, hi = [], []
        for j in range(p):
            w = xs_b[pl.ds(j, cap, stride=p), :]
            lo.append(pltpu.bitcast(w << 16, F32).astype(BF16))
            hi.append(pltpu.bitcast(w & jnp.uint32(0xFFFF0000), F32).astype(BF16))
        return jnp.concatenate(lo + hi, axis=1)

    def expert(xs, y_b):
        rc = min(MOE_ROW_CHUNK, cap)
        chunks = range(0, cap, rc)
        au = [(_dot(xs[r0:r0 + rc], wg_ref[0, 0]), _dot(xs[r0:r0 + rc], wu_ref[0, 0])) for r0 in chunks]
        for r0, (a, u_) in zip(chunks, au):
            y = _dot((_silu(a) * u_).astype(BF16), wd_ref[0, 0])
            for j in range(c8):
                y_b[pl.ds(r0 * c8 + j, rc, stride=c8), :] = y[:, LANES * j:LANES * (j + 1)]

    def step(xs_a, xs_b, y_b, y_c):
        @pl.when(steady)
        def _():
            xs = load_rows(xs_b)
            gather(xs_a, True)
            scatter(y_c, True)
            expert(xs, y_b)

        @pl.when(jnp.logical_not(steady))
        def _():
            @pl.when(valid_b)
            def _():
                expert(load_rows(xs_b), y_b)

            @pl.when(valid_a)
            def _():
                gather(xs_a, False)

            @pl.when(valid_c)
            def _():
                scatter(y_c, False)

    @pl.when(g % 2 == 0)
    def _():
        step(xs_0, xs_1, y_1, y_0)

    @pl.when(g % 2 == 1)
    def _():
        step(xs_1, xs_0, y_0, y_1)


def _moe_call(idx, h3, gate, wg, wu, wd, layer):
    bm, ne, cap = idx.shape
    d = wg.shape[2]
    p = d // 2 // LANES
    c8 = d // LANES
    n = h3.shape[1] // p
    n_tiles = bm * ne
    idx3 = idx.reshape(n_tiles, 1, cap)
    gate3 = gate.reshape(n_tiles, 1, cap)
    tile_a = lambda i: jnp.minimum(i, n_tiles - 1)
    tile_b = lambda i: jnp.clip(i - 1, 0, n_tiles - 1)
    tile_c = lambda i: jnp.clip(i - 2, 0, n_tiles - 1)
    wmap = lambda i: (layer, tile_b(i) % ne, 0, 0)
    smem = lambda f: pl.BlockSpec((1, 1, cap), lambda i: (f(i), 0, 0), memory_space=pltpu.SMEM)
    return pl.pallas_call(
        functools.partial(_moe_kernel, cap=cap, p=p, c8=c8, ne=ne, n_tiles=n_tiles),
        grid=(n_tiles + 2,),
        in_specs=[
            smem(tile_a), smem(tile_c), smem(tile_c),
            pl.BlockSpec((1, n * p, LANES), lambda i: (tile_a(i) // ne, 0, 0)),
            pl.BlockSpec((1, 1, d, wg.shape[3]), wmap),
            pl.BlockSpec((1, 1, d, wu.shape[3]), wmap),
            pl.BlockSpec((1, 1, wd.shape[2], d), wmap),
        ],
        out_specs=pl.BlockSpec((1, n * c8, LANES), lambda i: (tile_c(i) // ne, 0, 0), pipeline_mode=pl.Buffered(1)),
        out_shape=jax.ShapeDtypeStruct((bm, n * c8, LANES), F32),
        scratch_shapes=[pltpu.VMEM((cap * p, LANES), jnp.uint32), pltpu.VMEM((cap * p, LANES), jnp.uint32),
                        pltpu.VMEM((cap * c8, LANES), F32), pltpu.VMEM((cap * c8, LANES), F32)],
        compiler_params=_cparams(1, VMEM_LIMIT),
        name="moe",
    )(idx3, idx3, gate3, h3, wg, wu, wd)


def _resid_kernel(x_ref, moe_ref, mod_ref, o_ref):
    tm = x_ref.shape[1]
    sub = min(ROW_SUB, tm)
    for r0 in range(0, tm, sub):
        o_ref[0, r0:r0 + sub] = _gated_moe_residual(x_ref, moe_ref, mod_ref[0][5:6], r0, sub)


def _resid_call(x1, moe3, mod, *, tm):
    b, n, d = x1.shape
    mb = mod.shape[0]
    c8 = d // LANES
    mod_map = (lambda i, j: (i, 0, 0)) if mb > 1 else (lambda i, j: (0, 0, 0))
    tok = lambda i, j: (i, j, 0)
    return pl.pallas_call(
        _resid_kernel,
        grid=(b, n // tm),
        in_specs=[
            pl.BlockSpec((1, tm, d), tok),
            pl.BlockSpec((1, tm * c8, LANES), tok),
            pl.BlockSpec((1, 6, d), mod_map),
        ],
        out_specs=pl.BlockSpec((1, tm, d), tok),
        out_shape=jax.ShapeDtypeStruct((b, n, d), F32),
        compiler_params=_cparams(2, VMEM_LIMIT),
        name="resid",
    )(x1, moe3, mod)


def _rope_tables(n):
    t = jnp.arange(n)
    row = (t // GRID_W).astype(F32)
    col = (t % GRID_W).astype(F32)
    n_freq = HEAD_DIM // 4
    inv_freq = jnp.power(ROPE_BASE, -jnp.arange(n_freq, dtype=F32) / n_freq)
    ang = jnp.concatenate([row[:, None] * inv_freq, col[:, None] * inv_freq], axis=-1)
    cos, sin = jnp.cos(ang), jnp.sin(ang)
    cos_t = jnp.concatenate([cos, cos, cos, cos], axis=-1)
    sin_t = jnp.concatenate([-sin, sin, -sin, sin], axis=-1)
    return cos_t, sin_t


def _gain128(g):
    return jnp.concatenate([g, g]).reshape(1, LANES).astype(F32)


def _channel_mix(h_pack, aff_t, tri, weights, layer, *, flatten):
    wg, wu, wd = weights
    b, _, n = aff_t.shape
    cap = EC_CAPACITY_FACTOR * n // N_EXPERTS
    idx, gate = _topk_call(aff_t, tri, cap=cap)
    if flatten:
        idx = (idx + (jnp.arange(b, dtype=jnp.int32) * n)[:, None, None]).transpose(1, 0, 2).reshape(1, N_EXPERTS, b * cap)
        gate = gate.transpose(1, 0, 2).reshape(1, N_EXPERTS, b * cap)
        h_pack = h_pack.reshape(1, -1, LANES)
    return _moe_call(idx, h_pack, gate, wg, wu, wd, layer).reshape(b, -1, LANES)


def kernel(x, c, ctx, c_ctx, ada_w, ada_b, norm1_g, norm2_g, router_w, exp_w_gate, exp_w_up, exp_w_down, ev_w_in,
           ev_w_out, ev_q_gain, ev_k_gain, ev_sink, od_w_in, od_w_out, od_q_gain, od_k_gain, od_rpb):
    b, n, d = x.shape
    nc = ctx.shape[1]
    tm = min(512, n)
    tmb = min(1024, n)
    tmc = min(512, nc)

    rows = -(-(b + 1) // 8) * 8
    cs = jnp.concatenate([c, c_ctx[None], jnp.zeros((rows - b - 1, d), F32)], axis=0)
    mods = _mod_call(cs, ada_w, ada_b).reshape(ada_w.shape[0], rows, 6, d)

    cos_t, sin_t = _rope_tables(n)
    cs_dft = _channel_dft()
    tri = jnp.asarray(np.triu(np.ones((LANES, LANES), np.float32)), BF16)
    fw = FOURIER_GROUPS * LANES

    def router_split(l):
        rw = router_w[l].T
        rh = rw.astype(BF16)
        rl = (rw - rh.astype(F32)).astype(BF16)
        left = jnp.pad(jnp.concatenate([rh, rl], axis=0), ((0, LANES - 2 * N_EXPERTS), (0, 0)))
        right = jnp.pad(rh, ((0, LANES - N_EXPERTS), (0, 0)))
        return jnp.concatenate([left, right], axis=1)

    w_stacks = (exp_w_gate, exp_w_up, exp_w_down)
    as_layer = lambda ws: [w.reshape((1,) + s.shape[1:]) for w, s in zip(ws, w_stacks)]

    mod_x, mod_c = mods[0, :b], mods[0, b:b + 1]
    g1 = norm1_g[0].reshape(1, d)
    g2 = norm2_g[0].reshape(1, d)
    w_in = ev_w_in[0].astype(BF16)
    w_out = ev_w_out[0].astype(BF16)
    qg, kg = _gain128(ev_q_gain[0]), _gain128(ev_k_gain[0])
    sink = ev_sink[0].astype(F32)
    rw = router_split(0)

    ab_x, q_x, kd_x, vd_x = _premix_even_call(x, mod_x, g1, w_in, cs_dft, cos_t, sin_t, qg, kg, rope=True, tm=tmb)
    ab_c, q_c, kd_c, vd_c = _premix_even_call(ctx, mod_c, g1, w_in, cs_dft, cos_t[:nc], sin_t[:nc], qg, kg,
                                              rope=False, tm=tmc)
    a_x, *wts = _swa_call(sink, q_x, kd_x, vd_x, kd_c, vd_c, w_stacks, 0)
    wts = as_layer(wts)
    a_c = _ctx_attn_call(sink, q_c, kd_c, vd_c)
    four_x = _fourier_call(ab_x, *_dft_tables(n), tm=tmb)
    four_c = _fourier_call(ab_c, *_dft_tables(nc), tm=tmc)
    w_out_parts = [w_out[:fw], w_out[fw:]]
    x1, h3_x, aff_x = _postmix_call([four_x, a_x], w_out_parts, x, mod_x, g2, rw, tm=tmb)
    c1, h3_c, aff_c = _postmix_call([four_c, a_c], w_out_parts, ctx, mod_c, g2, rw, tm=tmc)
    moe_x = _channel_mix(h3_x, aff_x, tri, wts, 0, flatten=False)
    moe_c = _channel_mix(h3_c, aff_c, tri, wts, 0, flatten=True)

    mod_x0, mod_c0 = mod_x, mod_c
    mod_x, mod_c = mods[1, :b], mods[1, b:b + 1]
    g1 = norm1_g[1].reshape(1, d)
    g2 = norm2_g[1].reshape(1, d)
    w_in = od_w_in[0].astype(BF16)
    w_out = od_w_out[0].astype(BF16)
    qg, kg = _gain128(od_q_gain[0]), _gain128(od_k_gain[0])
    rw = router_split(1)

    q_x, k_x, v_x, x = _premix_odd_call(x1, moe_x, mod_x0, mod_x, g1, w_in, qg, kg, emit_x=True, tm=tmb)
    _, k_c, v_c = _premix_odd_call(c1, moe_c, mod_c0, mod_c, g1, w_in, qg, kg, emit_x=False, tm=tmc)
    a_x, *wts = _na_call(q_x, k_x, v_x, k_c, v_c, _na_bias_table(od_rpb[0]), w_stacks, 1)
    wts = as_layer(wts)
    x1, h3_x, aff_x = _postmix_call([a_x], [w_out], x, mod_x, g2, rw, tm=tmb)
    return _resid_call(x1, _channel_mix(h3_x, aff_x, tri, wts, 0, flatten=False), mod_x, tm=tmb)
```

```python
import functools
import math

import numpy as np
import jax
import jax.numpy as jnp
from jax import lax
from jax.experimental import pallas as pl
from jax.experimental.pallas import tpu as pltpu

GRID_W = 64
HEAD_DIM = 64
FOURIER_GROUPS = 4
SWA_WINDOW = 128
SWA_BLOCK = 128
NA_ROWS_MAX = 8
NA_COLS = 16
N_EXPERTS = 16
EC_CAPACITY_FACTOR = 2
ROPE_BASE = 10000.0
EPS = 1e-6

LANES = 128
NEG = -1e30
NA_QROWS = 4
NA_JBLK = 2
NA_KROWS = 12
DFT_SUB = 64
MOE_ROW_CHUNK = 128
SWA_QBLK = 8
ROW_SUB = 256
ONES_ROWS = 16
LOG2E = math.log2(math.e)
VMEM_LIMIT = 60 * 1024 * 1024

F32 = jnp.float32
BF16 = jnp.bfloat16


def _cparams(n_axes, vmem=None):
    return pltpu.CompilerParams(dimension_semantics=("arbitrary",) * n_axes, vmem_limit_bytes=vmem)


def _dot(a, b):
    return jnp.dot(a, b, preferred_element_type=F32)


def _dot_t(a, b):
    return lax.dot_general(a, b, (((1,), (1,)), ((), ())), preferred_element_type=F32)


def _split(a):
    hi = a.astype(BF16)
    lo = (a - hi.astype(F32)).astype(BF16)
    return hi, lo


def _dot3(a, b):
    ah, al = _split(a)
    bh, bl = _split(b)
    return _dot(ah, bh) + _dot(al, bh) + _dot(ah, bl)


def _silu(a):
    return a / (1.0 + jnp.exp(-a))


def _rms_mod(x, gain, shift, scale):
    ms = jnp.mean(x * x, axis=-1, keepdims=True)
    y = x * lax.rsqrt(ms + EPS) * gain
    return y * (1.0 + scale) + shift


def _mod_kernel(cs_ref, w_ref, b_ref, o_ref):
    o_ref[0] = _dot3(_silu(cs_ref[...]), w_ref[0]) + b_ref[0]


def _mod_call(cs, ada_w, ada_b):
    depth, d, d6 = ada_w.shape
    r = cs.shape[0]
    tn = 1024
    return pl.pallas_call(
        _mod_kernel,
        grid=(depth, d6 // tn),
        in_specs=[
            pl.BlockSpec((r, d), lambda l, j: (0, 0)),
            pl.BlockSpec((1, d, tn), lambda l, j: (l, 0, j)),
            pl.BlockSpec((1, 1, tn), lambda l, j: (l, 0, j)),
        ],
        out_specs=pl.BlockSpec((1, r, tn), lambda l, j: (l, 0, j)),
        out_shape=jax.ShapeDtypeStruct((depth, r, d6), F32),
        compiler_params=_cparams(2),
        name="mod",
    )(cs, ada_w, ada_b.reshape(depth, 1, d6))


def _head_norm(t, gain, lane):
    t2 = t * t
    s_lo = jnp.sum(jnp.where(lane < HEAD_DIM, t2, 0.0), axis=-1, keepdims=True)
    s_all = jnp.sum(t2, axis=-1, keepdims=True)
    ms = jnp.where(lane < HEAD_DIM, s_lo, s_all - s_lo) * (1.0 / HEAD_DIM)
    return t * lax.rsqrt(ms + EPS) * gain


def _rope(t, cos_t, sin_t, lane):
    partner = jnp.where((lane % HEAD_DIM) < HEAD_DIM // 2, pltpu.roll(t, LANES - HEAD_DIM // 2, 1),
                        pltpu.roll(t, HEAD_DIM // 2, 1))
    return t * cos_t + partner * sin_t


def _dup_halves(t, lane):
    sw = pltpu.roll(t, HEAD_DIM, 1)
    return jnp.where(lane < HEAD_DIM, t, sw), jnp.where(lane < HEAD_DIM, sw, t)


def _premix_even_kernel(x_ref, mod_ref, g_ref, w_ref, cs_ref, cos_ref, sin_ref, qg_ref, kg_ref,
                        ab_ref, q_ref, kd_ref, vt_ref, *, rope):
    tm = x_ref.shape[1]
    sub = min(ROW_SUB, tm)
    m = mod_ref[0]
    lane = lax.broadcasted_iota(jnp.int32, (sub, LANES), 1)
    fw = FOURIER_GROUPS * LANES
    qw = q_ref.shape[2]
    csb = cs_ref[...].astype(BF16)
    for r0 in range(0, tm, sub):
        rs = slice(r0, r0 + sub)
        hb = _rms_mod(x_ref[0, rs], g_ref[...], m[0:1], m[1:2]).astype(BF16)
        pf = _dot(hb, w_ref[:, 0:fw])
        for g in range(FOURIER_GROUPS):
            ab = _dot(pf[:, LANES * g:LANES * (g + 1)].astype(BF16), csb)
            ab_ref[0, rs, LANES * g:LANES * (g + 1)] = ab[:, :LANES].astype(BF16)
            ab_ref[0, rs, fw + LANES * g:fw + LANES * (g + 1)] = ab[:, LANES:].astype(BF16)
        pq = _dot(hb, w_ref[:, fw:fw + qw])
        if rope:
            cos_t = cos_ref[rs]
            sin_t = sin_ref[rs]
        for c in range(qw // LANES):
            t = _head_norm(pq[:, LANES * c:LANES * (c + 1)], qg_ref[...], lane)
            if rope:
                t = _rope(t, cos_t, sin_t, lane)
            q_ref[0, rs, LANES * c:LANES * (c + 1)] = t.astype(BF16)
        pk = _dot(hb, w_ref[:, fw + qw:fw + qw + 2 * LANES])
        k = _head_norm(pk[:, :LANES], kg_ref[...], lane)
        if rope:
            k = _rope(k, cos_t, sin_t, lane)
        k0, k1 = _dup_halves(k, lane)
        kd_ref[0, rs, :LANES] = k0.astype(BF16)
        kd_ref[0, rs, LANES:] = k1.astype(BF16)
        vt_ref[0, :, rs] = pk[:, LANES:].T.astype(BF16)


def _premix_even_call(x, mod, gain, w_bf, cs, cos_t, sin_t, qg, kg, *, rope, tm):
    b, n, d = x.shape
    mb = mod.shape[0]
    wtot = w_bf.shape[1]
    fw = FOURIER_GROUPS * LANES
    qw = wtot - fw - 2 * LANES
    mod_map = (lambda i, j: (i, 0, 0)) if mb > 1 else (lambda i, j: (0, 0, 0))
    tok = lambda i, j: (i, j, 0)
    const2 = lambda i, j: (0, 0)
    return pl.pallas_call(
        functools.partial(_premix_even_kernel, rope=rope),
        grid=(b, n // tm),
        in_specs=[
            pl.BlockSpec((1, tm, d), tok),
            pl.BlockSpec((1, 6, d), mod_map),
            pl.BlockSpec((1, d), const2),
            pl.BlockSpec((d, wtot), const2),
            pl.BlockSpec((LANES, 2 * LANES), const2),
            pl.BlockSpec((tm, LANES), lambda i, j: (j, 0)),
            pl.BlockSpec((tm, LANES), lambda i, j: (j, 0)),
            pl.BlockSpec((1, LANES), const2),
            pl.BlockSpec((1, LANES), const2),
        ],
        out_specs=[
            pl.BlockSpec((1, tm, 2 * fw), tok),
            pl.BlockSpec((1, tm, qw), tok),
            pl.BlockSpec((1, tm, 2 * LANES), tok),
            pl.BlockSpec((1, LANES, tm), lambda i, j: (i, 0, j)),
        ],
        out_shape=[
            jax.ShapeDtypeStruct((b, n, 2 * fw), BF16),
            jax.ShapeDtypeStruct((b, n, qw), BF16),
            jax.ShapeDtypeStruct((b, n, 2 * LANES), BF16),
            jax.ShapeDtypeStruct((b, LANES, n), BF16),
        ],
        compiler_params=_cparams(2, VMEM_LIMIT),
        name="premix_even",
    )(x, mod, gain, w_bf, cs, cos_t, sin_t, qg, kg)


def _gated_moe_residual(x_ref, moe_ref, gate_row, r0, sub):
    c8 = x_ref.shape[2] // LANES
    cols = []
    for j in range(c8):
        cs = slice(LANES * j, LANES * (j + 1))
        cols.append(x_ref[0, r0:r0 + sub, cs] + gate_row[:, cs] * moe_ref[0, pl.ds(r0 * c8 + j, sub, stride=c8), :])
    return jnp.concatenate(cols, axis=1)


def _premix_odd_kernel(x_ref, moe_ref, modp_ref, mod_ref, g_ref, w_ref, qg_ref, kg_ref, q_ref, k_ref, v_ref,
                       *xo_ref):
    tm = x_ref.shape[1]
    sub = min(ROW_SUB, tm)
    m = mod_ref[0]
    gate_prev = modp_ref[0][5:6]
    lane = lax.broadcasted_iota(jnp.int32, (sub, LANES), 1)
    wq = q_ref.shape[2]
    chunk = 4 * LANES
    for r0 in range(0, tm, sub):
        rs = slice(r0, r0 + sub)
        x = _gated_moe_residual(x_ref, moe_ref, gate_prev, r0, sub)
        if xo_ref:
            xo_ref[0][0, rs] = x
        hb = _rms_mod(x, g_ref[...], m[0:1], m[1:2]).astype(BF16)
        for c0 in range(0, 3 * wq, chunk):
            p = _dot(hb, w_ref[:, c0:c0 + chunk])
            for cc in range(chunk // LANES):
                col = c0 + cc * LANES
                t = p[:, cc * LANES:(cc + 1) * LANES]
                if col < wq:
                    q_ref[0, rs, col:col + LANES] = _head_norm(t, qg_ref[...], lane).astype(BF16)
                elif col < 2 * wq:
                    k_ref[0, rs, col - wq:col - wq + LANES] = _head_norm(t, kg_ref[...], lane).astype(BF16)
                else:
                    v_ref[0, col - 2 * wq:col - 2 * wq + LANES, rs] = t.T.astype(BF16)


def _premix_odd_call(x1, moe3, mod_prev, mod, gain, w_bf, qg, kg, *, emit_x, tm):
    b, n, d = x1.shape
    mb = mod.shape[0]
    wq = w_bf.shape[1] // 3
    c8 = d // LANES
    mod_map = (lambda i, j: (i, 0, 0)) if mb > 1 else (lambda i, j: (0, 0, 0))
    tok = lambda i, j: (i, j, 0)
    const2 = lambda i, j: (0, 0)
    out_specs = [pl.BlockSpec((1, tm, wq), tok)] * 2 + [pl.BlockSpec((1, wq, tm), lambda i, j: (i, 0, j))]
    out_shape = [jax.ShapeDtypeStruct((b, n, wq), BF16)] * 2 + [jax.ShapeDtypeStruct((b, wq, n), BF16)]
    if emit_x:
        out_specs.append(pl.BlockSpec((1, tm, d), tok))
        out_shape.append(jax.ShapeDtypeStruct((b, n, d), F32))
    return pl.pallas_call(
        _premix_odd_kernel,
        grid=(b, n // tm),
        in_specs=[
            pl.BlockSpec((1, tm, d), tok),
            pl.BlockSpec((1, tm * c8, LANES), tok),
            pl.BlockSpec((1, 6, d), mod_map),
            pl.BlockSpec((1, 6, d), mod_map),
            pl.BlockSpec((1, d), const2),
            pl.BlockSpec((d, 3 * wq), const2),
            pl.BlockSpec((1, LANES), const2),
            pl.BlockSpec((1, LANES), const2),
        ],
        out_specs=out_specs,
        out_shape=out_shape,
        compiler_params=_cparams(2, VMEM_LIMIT),
        name="premix_odd",
    )(x1, moe3, mod_prev, mod, gain, w_bf, qg, kg)


def _col_reduce(x, op, slab=64):
    r = x.shape[0]
    if r > slab and r % slab == 0:
        x = op(x.reshape(r // slab, slab, x.shape[1]), axis=0)
    return op(x, axis=0, keepdims=True)


def _softmax_pv_t(s_t, v_t, sink=None):
    mx = _col_reduce(s_t, jnp.max)
    if sink is not None:
        mx = jnp.maximum(mx, sink)
    p = jnp.exp2((s_t - mx).astype(BF16))
    nd = v_t.shape[0]
    o = _dot(jnp.concatenate([v_t, jnp.ones((ONES_ROWS, v_t.shape[1]), BF16)], axis=0), p)
    den = o[nd:nd + 1]
    if sink is not None:
        den = den + jnp.exp2(sink - mx)
    return o[:nd] / den


def _mask_half(qc, lane, half):
    q32 = qc.astype(F32) * (HEAD_DIM ** -0.5 * LOG2E)
    keep = (lane < HEAD_DIM) if half == 0 else (lane >= HEAD_DIM)
    return jnp.where(keep, q32, 0.0).astype(BF16)


def _gqa_scores(q_ref, h, kd, n_kv, masks, lane, r0, tq):
    rs = slice(r0, r0 + tq)
    cols_per_kv = q_ref.shape[2] // LANES // n_kv
    q_rows = []
    for cc in range(cols_per_kv):
        c = cols_per_kv * h + cc
        for half in range(2):
            q_rows.append(_mask_half(q_ref[0, rs, LANES * c:LANES * (c + 1)], lane, half))
    qm = jnp.concatenate(q_rows, axis=0)
    s = _dot_t(kd, qm)
    pieces, k0 = [], 0
    for rows, mask in masks:
        blk = s[k0:k0 + rows]
        pieces.append(blk if mask is None else jnp.where(mask, blk, NEG))
        k0 += rows
    return jnp.concatenate(pieces, axis=0) if len(pieces) > 1 else pieces[0]


def _gqa_finish(sink_ref, o_ref, h, s_t, v_t, r0, tq):
    rs = slice(r0, r0 + tq)
    n_kv = v_t.shape[0] // HEAD_DIM
    cols_per_kv = o_ref.shape[2] // LANES // n_kv
    heads = 2 * cols_per_kv
    head_of_lane = lax.broadcasted_iota(jnp.int32, (1, heads * tq), 1) // tq
    sink = jnp.zeros((1, heads * tq), F32)
    for a in range(heads):
        sink = jnp.where(head_of_lane == a, sink_ref[heads * h + a] * LOG2E, sink)
    o_t = _softmax_pv_t(s_t, v_t, sink)[HEAD_DIM * h:HEAD_DIM * (h + 1)]
    for cc in range(cols_per_kv):
        c = cols_per_kv * h + cc
        tile = jnp.concatenate([o_t[:, tq * (2 * cc):tq * (2 * cc + 1)], o_t[:, tq * (2 * cc + 1):tq * (2 * cc + 2)]],
                               axis=0)
        o_ref[0, rs, LANES * c:LANES * (c + 1)] = tile.T.astype(BF16)


def _swa_kernel(sink_ref, q_ref, km_ref, k0_ref, kp_ref, vm_ref, v0_ref, vp_ref, kc_ref, vc_ref, *rest, n_step):
    o_ref = _ride_cast(rest)
    g = pl.program_id(1)
    tq = SWA_BLOCK
    nc = kc_ref.shape[1]
    n_kv = kc_ref.shape[2] // LANES
    heads = q_ref.shape[2] // HEAD_DIM // n_kv
    lane = lax.broadcasted_iota(jnp.int32, (tq, LANES), 1)
    jk = lax.broadcasted_iota(jnp.int32, (SWA_BLOCK, heads * tq), 0)
    iq = lax.broadcasted_iota(jnp.int32, (SWA_BLOCK, heads * tq), 1) % tq
    below, above = jk >= iq, jk <= iq
    nq = q_ref.shape[1] // SWA_BLOCK
    blk = lambda s: slice(SWA_BLOCK * s, SWA_BLOCK * (s + 1))

    def key_block(s, hs):
        return km_ref[0, :, hs] if s < 0 else kp_ref[0, :, hs] if s == nq else k0_ref[0, blk(s), hs]

    def value_block(s):
        return vm_ref[0] if s < 0 else vp_ref[0] if s == nq else v0_ref[0, :, blk(s)]

    def scores(sub, h):
        hs = slice(LANES * h, LANES * (h + 1))
        first = jnp.logical_and(below, g > 0) if sub == 0 else below
        last = jnp.logical_and(above, g < n_step - 1) if sub == nq - 1 else above
        masks = [(nc, None), (SWA_BLOCK, first), (SWA_BLOCK, None), (SWA_BLOCK, last)]
        kd = jnp.concatenate([kc_ref[0, :, hs]] + [key_block(s, hs) for s in (sub - 1, sub, sub + 1)], axis=0)
        return _gqa_scores(q_ref, h, kd, n_kv, masks, lane, SWA_BLOCK * sub, tq)

    groups = [(sub, h) for sub in range(nq) for h in range(n_kv)]
    ahead = 3
    pending = [scores(*grp) for grp in groups[:ahead]]
    for i, (sub, h) in enumerate(groups):
        s_t = pending.pop(0)
        if i + ahead < len(groups):
            pending.append(scores(*groups[i + ahead]))
        v_t = jnp.concatenate([vc_ref[0]] + [value_block(s) for s in (sub - 1, sub, sub + 1)], axis=1)
        _gqa_finish(sink_ref, o_ref, h, s_t, v_t, SWA_BLOCK * sub, tq)


def _cast_riders(ws, layer, total, flat):
    ins, outs, shapes, args = [], [], [], []
    for w in ws:
        l, e, r, c = w.shape
        rows = e * r // total
        ins.append(pl.BlockSpec((rows, c), lambda i, j: (layer * total + flat(i, j), 0)))
        outs.append(pl.BlockSpec((rows, c), lambda i, j: (flat(i, j), 0)))
        shapes.append(jax.ShapeDtypeStruct((e * r, c), BF16))
        args.append(w.reshape(l * e * r, c))
    return ins, outs, shapes, args


def _ride_cast(rest):
    n_cast = (len(rest) - 1) // 2
    for w_ref, c_ref in zip(rest[:n_cast], rest[n_cast + 1:]):
        c_ref[...] = w_ref[...].astype(BF16)
    return rest[n_cast]


def _swa_call(sink, q, kd, vt, kdc, vtc, ws, layer):
    b, n, qw = q.shape
    nc = kdc.shape[1]
    kw = kd.shape[2]
    vw = vt.shape[1]
    n_blk = n // SWA_BLOCK
    nq = min(SWA_QBLK, n_blk)
    n_step = n_blk // nq
    prev = lambda j: jnp.maximum(nq * j - 1, 0)
    nxt = lambda j: jnp.minimum(nq * j + nq, n_blk - 1)
    kspec = lambda f: pl.BlockSpec((1, SWA_BLOCK, kw), lambda i, j: (i, f(j), 0))
    vspec = lambda f: pl.BlockSpec((1, vw, SWA_BLOCK), lambda i, j: (i, 0, f(j)))
    c_in, c_out, c_shape, c_args = _cast_riders(ws, layer, b * n_step, lambda i, j: i * n_step + j)
    return pl.pallas_call(
        functools.partial(_swa_kernel, n_step=n_step),
        grid=(b, n_step),
        in_specs=[
            pl.BlockSpec(memory_space=pltpu.SMEM),
            pl.BlockSpec((1, nq * SWA_BLOCK, qw), lambda i, j: (i, j, 0)),
            kspec(prev), pl.BlockSpec((1, nq * SWA_BLOCK, kw), lambda i, j: (i, j, 0)), kspec(nxt),
            vspec(prev), pl.BlockSpec((1, vw, nq * SWA_BLOCK), lambda i, j: (i, 0, j)), vspec(nxt),
            pl.BlockSpec((1, nc, kw), lambda i, j: (i, 0, 0)),
            pl.BlockSpec((1, vw, nc), lambda i, j: (i, 0, 0)),
        ] + c_in,
        out_specs=[pl.BlockSpec((1, nq * SWA_BLOCK, qw), lambda i, j: (i, j, 0))] + c_out,
        out_shape=[jax.ShapeDtypeStruct((b, n, qw), BF16)] + c_shape,
        compiler_params=_cparams(2, VMEM_LIMIT),
        name="swa",
    )(sink, q, kd, kd, kd, vt, vt, vt, kdc, vtc, *c_args)


def _ctx_attn_kernel(sink_ref, q_ref, kc_ref, vc_ref, o_ref):
    tq = q_ref.shape[1]
    lane = lax.broadcasted_iota(jnp.int32, (tq, LANES), 1)
    n_kv = kc_ref.shape[2] // LANES
    scores = [_gqa_scores(q_ref, h, kc_ref[0, :, LANES * h:LANES * (h + 1)], n_kv, [(kc_ref.shape[1], None)], lane,
                          0, tq) for h in range(n_kv)]
    for h, s_t in enumerate(scores):
        _gqa_finish(sink_ref, o_ref, h, s_t, vc_ref[0], 0, tq)


def _ctx_attn_call(sink, q, kd, vt):
    b, nc, qw = q.shape
    kw = kd.shape[2]
    vw = vt.shape[1]
    m = lambda i: (i, 0, 0)
    return pl.pallas_call(
        _ctx_attn_kernel,
        grid=(b,),
        in_specs=[
            pl.BlockSpec(memory_space=pltpu.SMEM),
            pl.BlockSpec((1, nc, qw), m),
            pl.BlockSpec((1, nc, kw), m),
            pl.BlockSpec((1, vw, nc), m),
        ],
        out_specs=pl.BlockSpec((1, nc, qw), m),
        out_shape=jax.ShapeDtypeStruct((b, nc, qw), BF16),
        compiler_params=_cparams(1, VMEM_LIMIT),
        name="ctx_attn",
    )(sink, q, kd, vt)


def _na_kernel(q_ref, k0_ref, k1_ref, k2_ref, k3_ref, v0_ref, v1_ref, v2_ref, v3_ref, kc_ref, vc_ref, tb_ref, *rest,
               n_j, rows):
    o_ref = _ride_cast(rest)
    g = pl.program_id(1)
    tq = NA_QROWS * GRID_W
    tk = NA_KROWS * GRID_W
    nc = kc_ref.shape[1]
    k_blocks = [k0_ref, k1_ref, k2_ref, k3_ref]
    v_blocks = [v0_ref, v1_ref, v2_ref, v3_ref]
    base = jnp.clip(NA_JBLK * g - 1, 0, n_j - (NA_JBLK + 2))
    lane = lax.broadcasted_iota(jnp.int32, (tq, LANES), 1)
    k_row = (lax.broadcasted_iota(jnp.int32, (nc + tk, LANES), 0) - nc) // GRID_W
    k_lane = lax.broadcasted_iota(jnp.int32, (nc + tk, LANES), 1)
    is_local = lax.broadcasted_iota(jnp.int32, (nc + tk, LANES), 0) >= nc
    q_grid_row = (lax.broadcasted_iota(jnp.int32, (2 * tq, LANES), 0) % tq) // GRID_W
    q_extra = jnp.where(lax.broadcasted_iota(jnp.int32, (2 * tq, LANES), 1) == q_grid_row, 1.0, 0.0).astype(BF16)
    n_dr = 2 * NA_ROWS_MAX - 1
    shifted, k_extras, deltas = [], [], []
    for u in range(NA_JBLK):
        j = NA_JBLK * g + u
        w0_blk = jnp.clip(j - 1, 0, n_j - 3)
        w0 = NA_QROWS * w0_blk
        lo = jnp.zeros((nc + tk, LANES), jnp.int32)
        for rr in range(NA_QROWS):
            r = NA_QROWS * j + rr
            lo_rr = jnp.clip(r - NA_ROWS_MAX // 2, 0, rows - NA_ROWS_MAX) - w0
            lo = jnp.where(k_lane == rr, lo_rr, lo)
        outside = jnp.logical_or(k_row < lo, k_row >= lo + NA_ROWS_MAX)
        k_extras.append(jnp.where(jnp.logical_and(jnp.logical_and(is_local, k_lane < NA_QROWS), outside), NEG, 0.0)
                        .astype(BF16))
        shifted.append(w0_blk - base == 1)
        deltas.append(w0 - NA_QROWS * j)

    def scores(u, c):
        cs = slice(LANES * c, LANES * (c + 1))
        qc = q_ref[0, tq * u:tq * (u + 1), cs]
        k_loc = [jnp.where(shifted[u], k_blocks[d + 1][0, :, cs], k_blocks[d][0, :, cs]) for d in range(3)]
        kl = jnp.concatenate([kc_ref[0, :, cs]] + k_loc, axis=0)
        qm = jnp.concatenate([_mask_half(qc, lane, 0), _mask_half(qc, lane, 1)], axis=0)
        return _dot_t(jnp.concatenate([kl, k_extras[u]], axis=1), jnp.concatenate([qm, q_extra], axis=1))

    groups = [(u, c) for c in range(q_ref.shape[2] // LANES) for u in range(NA_JBLK)]
    ahead = 2
    pending = [scores(*grp) for grp in groups[:ahead]]
    for i, (u, c) in enumerate(groups):
        cs = slice(LANES * c, LANES * (c + 1))
        s = pending.pop(0)
        if i + ahead < len(groups):
            pending.append(scores(*groups[i + ahead]))
        v_loc = [jnp.where(shifted[u], v_blocks[d + 1][0, cs, :], v_blocks[d][0, cs, :]) for d in range(3)]
        v_t = jnp.concatenate([vc_ref[0, cs, :]] + v_loc, axis=1)
        bias_rows = []
        for kr in range(NA_KROWS):
            pieces = []
            for half in range(2):
                for t in range(NA_QROWS // 2):
                    dr = deltas[u] + kr - 2 * t + NA_ROWS_MAX - 1
                    pieces.append(tb_ref[2 * c + half, jnp.clip(dr, 0, n_dr)])
            bias_rows.append(jnp.concatenate(pieces, axis=1))
        bias = jnp.concatenate(bias_rows, axis=0)
        s_t = jnp.concatenate([s[:nc], s[nc:] + bias], axis=0)
        o_t = _softmax_pv_t(s_t, v_t)
        tile = jnp.concatenate([o_t[:HEAD_DIM, :tq], o_t[HEAD_DIM:, tq:]], axis=0)
        o_ref[0, tq * u:tq * (u + 1), cs] = tile.T.astype(BF16)


def _na_call(q, k, vt, kc, vtc, tb, ws, layer):
    b, n, w = q.shape
    nc = kc.shape[1]
    rows = n // GRID_W
    n_j = rows // NA_QROWS
    n_g = n_j // NA_JBLK
    tq = NA_QROWS * GRID_W
    cur = lambda i, j: (i, j, 0)
    first = lambda j: jnp.clip(NA_JBLK * j - 1, 0, n_j - (NA_JBLK + 2))
    kspec = lambda d: pl.BlockSpec((1, tq, w), lambda i, j: (i, first(j) + d, 0))
    vspec = lambda d: pl.BlockSpec((1, w, tq), lambda i, j: (i, 0, first(j) + d))
    c_in, c_out, c_shape, c_args = _cast_riders(ws, layer, b * n_g, lambda i, j: i * n_g + j)
    return pl.pallas_call(
        functools.partial(_na_kernel, n_j=n_j, rows=rows),
        grid=(b, n_g),
        in_specs=[
            pl.BlockSpec((1, NA_JBLK * tq, w), cur),
            kspec(0), kspec(1), kspec(2), kspec(3),
            vspec(0), vspec(1), vspec(2), vspec(3),
            pl.BlockSpec((1, nc, w), lambda i, j: (i, 0, 0)),
            pl.BlockSpec((1, w, nc), lambda i, j: (i, 0, 0)),
            pl.BlockSpec(tb.shape, lambda i, j: (0, 0, 0, 0)),
        ] + c_in,
        out_specs=[pl.BlockSpec((1, NA_JBLK * tq, w), cur)] + c_out,
        out_shape=[jax.ShapeDtypeStruct((b, n, w), BF16)] + c_shape,
        compiler_params=_cparams(2, VMEM_LIMIT),
        name="na",
    )(q, k, k, k, k, vt, vt, vt, vt, kc, vtc, tb, *c_args)


def _na_bias_table(rpb):
    n_dr, n_dc = rpb.shape[1], rpb.shape[2]
    lane = np.arange(LANES)[None, :]
    col_q = lane % GRID_W
    col_k = np.arange(GRID_W)[:, None]
    c_start = np.clip(col_q - NA_COLS // 2, 0, GRID_W - NA_COLS)
    col_valid = (col_k >= c_start) & (col_k < c_start + NA_COLS)
    dc_idx = np.where(col_valid, np.clip(col_k - col_q + NA_COLS - 1, 0, n_dc - 1), n_dc)
    dr_idx = np.arange(n_dr + 1)[:, None] - lane // GRID_W + 1
    ext = jnp.pad(rpb.astype(F32) * LOG2E, ((0, 0), (1, 1), (0, 1)), constant_values=NEG)
    return ext[:, dr_idx[:, None, :], dc_idx[None, :, :]]


def _fourier_kernel(ab_ref, ca_ref, sa_ref, cb_ref, sb_ref, rev_ref, o_ref, fold_ref, rev_scr):
    n = ab_ref.shape[1]
    nh = n // 2
    tm, fw = o_ref.shape[1], o_ref.shape[2]

    @pl.when(pl.program_id(1) == 0)
    def _():
        for m in range(nh // LANES):
            blk = ab_ref[0, n - LANES * (m + 1):n - LANES * m, :]
            rev_scr[LANES * m:LANES * (m + 1), :] = _dot(rev_ref[...], blk)
        ck = min(256, nh)
        row = lax.broadcasted_iota(jnp.int32, (ck, 1), 0)
        for r0 in range(0, nh, ck):
            prev = (r0 - 1) % nh
            shifted = jnp.where(row == 0, rev_scr[prev:prev + 1, :], pltpu.roll(rev_scr[r0:r0 + ck, :], 1, 0))
            lo = ab_ref[0, r0:r0 + ck, :].astype(F32)
            a2 = lo[:, :fw] + shifted[:, :fw]
            b2 = lo[:, fw:] - shifted[:, fw:]
            if r0 == 0:
                a2 = jnp.where(row == 0, lo[:, :fw], a2)
                b2 = jnp.where(row == 0, shifted[:, :fw], b2)
            fold_ref[r0:r0 + ck, :fw] = a2.astype(BF16)
            fold_ref[r0:r0 + ck, fw:] = b2.astype(BF16)

    cb = cb_ref[...]
    sb = sb_ref[...]
    sub = min(ROW_SUB, tm)
    for r0 in range(0, tm, sub):
        cm, sm = [], []
        for a in range(r0 // DFT_SUB, (r0 + sub) // DFT_SUB):
            ca = ca_ref[a:a + 1, :]
            sa = sa_ref[a:a + 1, :]
            cm.append((ca * cb - sa * sb).astype(BF16))
            sm.append((sa * cb + ca * sb).astype(BF16))
        y = _dot(jnp.concatenate(cm, axis=0), fold_ref[:, :fw]) - _dot(jnp.concatenate(sm, axis=0), fold_ref[:, fw:])
        o_ref[0, r0:r0 + sub] = y.astype(BF16)


def _fourier_call(ab, ca, sa, cb, sb, *, tm):
    b, n, w2 = ab.shape
    nh = n // 2
    fw = w2 // 2
    na = tm // DFT_SUB
    rev = jnp.asarray(np.eye(LANES, dtype=np.float32)[::-1], BF16)
    return pl.pallas_call(
        _fourier_kernel,
        grid=(b, n // tm),
        in_specs=[
            pl.BlockSpec((1, n, w2), lambda i, j: (i, 0, 0)),
            pl.BlockSpec((na, nh), lambda i, j: (j, 0)),
            pl.BlockSpec((na, nh), lambda i, j: (j, 0)),
            pl.BlockSpec((DFT_SUB, nh), lambda i, j: (0, 0)),
            pl.BlockSpec((DFT_SUB, nh), lambda i, j: (0, 0)),
            pl.BlockSpec((LANES, LANES), lambda i, j: (0, 0)),
        ],
        out_specs=pl.BlockSpec((1, tm, fw), lambda i, j: (i, j, 0)),
        out_shape=jax.ShapeDtypeStruct((b, n, fw), BF16),
        scratch_shapes=[pltpu.VMEM((nh, w2), BF16), pltpu.VMEM((nh, w2), F32)],
        compiler_params=_cparams(2, VMEM_LIMIT),
        name="fourier",
    )(ab, ca, sa, cb, sb, rev)


def _dft_tables(n):
    nh = n // 2
    k = jnp.arange(nh, dtype=jnp.int32)
    a = jnp.arange(n // DFT_SUB, dtype=jnp.int32)
    bb = jnp.arange(DFT_SUB, dtype=jnp.int32)
    ang_a = ((DFT_SUB * a[:, None] * k[None, :]) % n).astype(F32) * (2.0 * math.pi / n)
    ang_b = ((bb[:, None] * k[None, :]) % n).astype(F32) * (2.0 * math.pi / n)
    s = float(n) ** -0.5
    nyq = -jnp.cos(math.pi * bb.astype(F32)) * s
    sin_b = jnp.concatenate([nyq[:, None], (jnp.sin(ang_b) * s)[:, 1:]], axis=1)
    return jnp.cos(ang_a), jnp.sin(ang_a), jnp.cos(ang_b) * s, sin_b


def _channel_dft():
    c = jnp.arange(LANES, dtype=jnp.int32)
    ang = ((c[:, None] * c[None, :]) % LANES).astype(F32) * (2.0 * math.pi / LANES)
    s = float(LANES) ** -0.5
    return jnp.concatenate([jnp.cos(ang) * s, jnp.sin(ang) * s], axis=1)


def _postmix_kernel(*refs, n_in):
    a_refs = refs[:n_in]
    w_refs = refs[n_in:2 * n_in]
    x_ref, mod_ref, g_ref, rw_ref, x1_ref, h3_ref, aff_ref = refs[2 * n_in:]
    tm, d = x_ref.shape[1], x_ref.shape[2]
    sub = min(ROW_SUB, tm)
    p = d // 2 // LANES
    m = mod_ref[0]
    for r0 in range(0, tm, sub):
        rs = slice(r0, r0 + sub)
        y = None
        for a_ref, w_ref in zip(a_refs, w_refs):
            t = _dot(a_ref[0, rs], w_ref[...])
            y = t if y is None else y + t
        x1 = x_ref[0, rs] + m[2:3] * y
        x1_ref[0, rs] = x1
        h2 = _rms_mod(x1, g_ref[...], m[3:4], m[4:5])
        hh, hl = _split(h2)
        parts = _dot_t(rw_ref[...], jnp.concatenate([hh, hl], axis=1))
        lt = parts[:N_EXPERTS] + parts[N_EXPERTS:2 * N_EXPERTS]
        ex = jnp.exp(lt - jnp.max(lt, axis=0, keepdims=True))
        aff_ref[0, :, rs] = ex / jnp.sum(ex, axis=0, keepdims=True)
        bits = pltpu.bitcast(hh.astype(F32), jnp.uint32)
        packed = (bits[:, d // 2:] & jnp.uint32(0xFFFF0000)) | (bits[:, :d // 2] >> 16)
        for jj in range(p):
            h3_ref[0, pl.ds(r0 * p + jj, sub, stride=p), :] = packed[:, LANES * jj:LANES * (jj + 1)]


def _postmix_call(a_list, w_list, x, mod, gain, rw, *, tm):
    b, n, d = x.shape
    mb = mod.shape[0]
    n_in = len(a_list)
    p = d // 2 // LANES
    mod_map = (lambda i, j: (i, 0, 0)) if mb > 1 else (lambda i, j: (0, 0, 0))
    tok = lambda i, j: (i, j, 0)
    const2 = lambda i, j: (0, 0)
    in_specs = [pl.BlockSpec((1, tm, a.shape[2]), tok) for a in a_list]
    in_specs += [pl.BlockSpec(w.shape, const2) for w in w_list]
    in_specs += [
        pl.BlockSpec((1, tm, d), tok),
        pl.BlockSpec((1, 6, d), mod_map),
        pl.BlockSpec((1, d), const2),
        pl.BlockSpec((LANES, 2 * d), const2),
    ]
    return pl.pallas_call(
        functools.partial(_postmix_kernel, n_in=n_in),
        grid=(b, n // tm),
        in_specs=in_specs,
        out_specs=[
            pl.BlockSpec((1, tm, d), tok),
            pl.BlockSpec((1, tm * p, LANES), tok),
            pl.BlockSpec((1, N_EXPERTS, tm), lambda i, j: (i, 0, j)),
        ],
        out_shape=[
            jax.ShapeDtypeStruct((b, n, d), F32),
            jax.ShapeDtypeStruct((b, n * p, LANES), jnp.uint32),
            jax.ShapeDtypeStruct((b, N_EXPERTS, n), F32),
        ],
        compiler_params=_cparams(2, VMEM_LIMIT),
        name="postmix",
    )(*a_list, *w_list, x, mod, gain, rw)


def _prefix_incl(ones, tri):
    carry = jnp.zeros((ones.shape[0], 1), F32)
    outs = []
    for c in range(ones.shape[1] // LANES):
        blk = ones[:, LANES * c:LANES * (c + 1)]
        outs.append(_dot(blk.astype(BF16), tri) + carry)
        carry = carry + jnp.sum(blk, axis=1, keepdims=True)
    return jnp.concatenate(outs, axis=1)


def _topk_kernel(aff_ref, tri_ref, idx_ref, gate_ref, *, cap):
    g, ne, n = aff_ref.shape
    rows = g * ne
    aff = aff_ref[...].reshape(rows, n)
    bits = pltpu.bitcast(aff, jnp.int32)
    capf = jnp.float32(cap)

    def body(it, lo):
        sh = 28 - 2 * it
        new = lo
        for k in (1, 2, 3):
            cand = lo | jnp.left_shift(jnp.int32(k), sh)
            cnt = jnp.sum(jnp.where(bits >= cand, 1.0, 0.0), axis=1, keepdims=True)
            new = jnp.where(cnt >= capf, cand, new)
        return new

    thr = lax.fori_loop(0, 15, body, jnp.zeros((rows, 1), jnp.int32))
    gt = jnp.where(bits > thr, 1.0, 0.0)
    eq = jnp.where(bits == thr, 1.0, 0.0)
    need = capf - jnp.sum(gt, axis=1, keepdims=True)
    tri = tri_ref[...]
    eq_before = _prefix_incl(eq, tri) - eq
    sel = gt + eq * jnp.where(eq_before < need, 1.0, 0.0)
    slot = _prefix_incl(sel, tri).astype(jnp.int32) - 1
    tok = lax.broadcasted_iota(jnp.int32, (rows, n), 1)
    disp = jnp.where(sel > 0.0, tok - slot, -1)
    gate = aff
    for bit in range((n - 1).bit_length()):
        step = 1 << bit
        disp_in = pltpu.roll(disp, n - step, 1)
        gate_in = pltpu.roll(gate, n - step, 1)
        move_in = jnp.logical_and(disp_in >= 0, ((disp_in >> bit) & 1) == 1)
        stay = jnp.logical_and(disp >= 0, ((disp >> bit) & 1) == 0)
        gate = jnp.where(move_in, gate_in, gate)
        disp = jnp.where(move_in, disp_in, jnp.where(stay, disp, -1))
    idx_ref[...] = (tok + disp)[:, :cap].reshape(g, ne, cap)
    gate_ref[...] = gate[:, :cap].reshape(g, ne, cap)


def _topk_call(aff_t, tri, *, cap):
    b, e, n = aff_t.shape
    g = math.gcd(b, max(1, 4096 // n))
    return pl.pallas_call(
        functools.partial(_topk_kernel, cap=cap),
        grid=(b // g,),
        in_specs=[
            pl.BlockSpec((g, e, n), lambda i: (i, 0, 0)),
            pl.BlockSpec((LANES, LANES), lambda i: (0, 0)),
        ],
        out_specs=[pl.BlockSpec((g, e, cap), lambda i: (i, 0, 0))] * 2,
        out_shape=[jax.ShapeDtypeStruct((b, e, cap), jnp.int32), jax.ShapeDtypeStruct((b, e, cap), F32)],
        compiler_params=_cparams(1, VMEM_LIMIT),
        name="topk",
    )(aff_t, tri)


def _moe_kernel(idx_a_ref, idx_c_ref, gate_c_ref, h3_ref, wg_ref, wu_ref, wd_ref, o_ref, xs_0, xs_1, y_0, y_1,
                *, cap, p, c8, ne, n_tiles):
    g = pl.program_id(0)
    valid_a = g < n_tiles
    valid_b = jnp.logical_and(g >= 1, g <= n_tiles)
    valid_c = g >= 2
    steady = jnp.logical_and(g >= 2, g < n_tiles)

    @pl.when(jnp.logical_and(valid_c, (g - 2) % ne == 0))
    def _():
        o_ref[...] = jnp.zeros(o_ref.shape, o_ref.dtype)

    unroll, group = 8, 4

    def gather_row(xs_a, s, dst):
        t = idx_a_ref[0, 0, s]
        xs_a[dst, :] = h3_ref[0, pl.ds(pl.multiple_of(t * p, p), p), :]

    def gather(xs_a, straight):
        if straight:
            for s in range(cap):
                gather_row(xs_a, s, pl.ds(s * p, p))
        else:
            def body(i, carry):
                for u in range(unroll):
                    s = i * unroll + u
                    gather_row(xs_a, s, pl.ds(pl.multiple_of(s * p, p), p))
                return carry
            lax.fori_loop(0, cap // unroll, body, 0)

    def scatter_rows(y_c, s_list, src_list):
        sums = []
        for s, src in zip(s_list, src_list):
            t = idx_c_ref[0, 0, s]
            rows = pl.ds(pl.multiple_of(t * c8, c8), c8)
            sums.append((rows, o_ref[0, rows, :] + y_c[src, :] * gate_c_ref[0, 0, s]))
        for rows, val in sums:
            o_ref[0, rows, :] = val

    def scatter(y_c, straight):
        if straight:
            for g0 in range(0, cap, group):
                ss = [g0 + u for u in range(group)]
                scatter_rows(y_c, ss, [pl.ds(s * c8, c8) for s in ss])
        else:
            def body(i, carry):
                ss = [i * group + u for u in range(group)]
                scatter_rows(y_c, ss, [pl.ds(pl.multiple_of(s * c8, c8), c8) for s in ss])
                return carry
            lax.fori_loop(0, cap // group, body, 0)

    def load_rows(xs_b):
        lo, hi = [], []
        for j in range(p):
            w = xs_b[pl.ds(j, cap, stride=p), :]
            lo.append(pltpu.bitcast(w << 16, F32).astype(BF16))
            hi.append(pltpu.bitcast(w & jnp.uint32(0xFFFF0000), F32).astype(BF16))
        return jnp.concatenate(lo + hi, axis=1)

    def expert(xs, y_b):
        rc = min(MOE_ROW_CHUNK, cap)
        chunks = range(0, cap, rc)
        au = [(_dot(xs[r0:r0 + rc], wg_ref[0, 0]), _dot(xs[r0:r0 + rc], wu_ref[0, 0])) for r0 in chunks]
        for r0, (a, u_) in zip(chunks, au):
            y = _dot((_silu(a) * u_).astype(BF16), wd_ref[0, 0])
            for j in range(c8):
                y_b[pl.ds(r0 * c8 + j, rc, stride=c8), :] = y[:, LANES * j:LANES * (j + 1)]

    def step(xs_a, xs_b, y_b, y_c):
        @pl.when(steady)
        def _():
            xs = load_rows(xs_b)
            gather(xs_a, True)
            scatter(y_c, True)
            expert(xs, y_b)

        @pl.when(jnp.logical_not(steady))
        def _():
            @pl.when(valid_b)
            def _():
                expert(load_rows(xs_b), y_b)

            @pl.when(valid_a)
            def _():
                gather(xs_a, False)

            @pl.when(valid_c)
            def _():
                scatter(y_c, False)

    @pl.when(g % 2 == 0)
    def _():
        step(xs_0, xs_1, y_1, y_0)

    @pl.when(g % 2 == 1)
    def _():
        step(xs_1, xs_0, y_0, y_1)


def _moe_call(idx, h3, gate, wg, wu, wd, layer):
    bm, ne, cap = idx.shape
    d = wg.shape[2]
    p = d // 2 // LANES
    c8 = d // LANES
    n = h3.shape[1] // p
    n_tiles = bm * ne
    idx3 = idx.reshape(n_tiles, 1, cap)
    gate3 = gate.reshape(n_tiles, 1, cap)
    tile_a = lambda i: jnp.minimum(i, n_tiles - 1)
    tile_b = lambda i: jnp.clip(i - 1, 0, n_tiles - 1)
    tile_c = lambda i: jnp.clip(i - 2, 0, n_tiles - 1)
    wmap = lambda i: (layer, tile_b(i) % ne, 0, 0)
    smem = lambda f: pl.BlockSpec((1, 1, cap), lambda i: (f(i), 0, 0), memory_space=pltpu.SMEM)
    return pl.pallas_call(
        functools.partial(_moe_kernel, cap=cap, p=p, c8=c8, ne=ne, n_tiles=n_tiles),
        grid=(n_tiles + 2,),
        in_specs=[
            smem(tile_a), smem(tile_c), smem(tile_c),
            pl.BlockSpec((1, n * p, LANES), lambda i: (tile_a(i) // ne, 0, 0)),
            pl.BlockSpec((1, 1, d, wg.shape[3]), wmap),
            pl.BlockSpec((1, 1, d, wu.shape[3]), wmap),
            pl.BlockSpec((1, 1, wd.shape[2], d), wmap),
        ],
        out_specs=pl.BlockSpec((1, n * c8, LANES), lambda i: (tile_c(i) // ne, 0, 0), pipeline_mode=pl.Buffered(1)),
        out_shape=jax.ShapeDtypeStruct((bm, n * c8, LANES), F32),
        scratch_shapes=[pltpu.VMEM((cap * p, LANES), jnp.uint32), pltpu.VMEM((cap * p, LANES), jnp.uint32),
                        pltpu.VMEM((cap * c8, LANES), F32), pltpu.VMEM((cap * c8, LANES), F32)],
        compiler_params=_cparams(1, VMEM_LIMIT),
        name="moe",
    )(idx3, idx3, gate3, h3, wg, wu, wd)


def _resid_kernel(x_ref, moe_ref, mod_ref, o_ref):
    tm = x_ref.shape[1]
    sub = min(ROW_SUB, tm)
    for r0 in range(0, tm, sub):
        o_ref[0, r0:r0 + sub] = _gated_moe_residual(x_ref, moe_ref, mod_ref[0][5:6], r0, sub)


def _resid_call(x1, moe3, mod, *, tm):
    b, n, d = x1.shape
    mb = mod.shape[0]
    c8 = d // LANES
    mod_map = (lambda i, j: (i, 0, 0)) if mb > 1 else (lambda i, j: (0, 0, 0))
    tok = lambda i, j: (i, j, 0)
    return pl.pallas_call(
        _resid_kernel,
        grid=(b, n // tm),
        in_specs=[
            pl.BlockSpec((1, tm, d), tok),
            pl.BlockSpec((1, tm * c8, LANES), tok),
            pl.BlockSpec((1, 6, d), mod_map),
        ],
        out_specs=pl.BlockSpec((1, tm, d), tok),
        out_shape=jax.ShapeDtypeStruct((b, n, d), F32),
        compiler_params=_cparams(2, VMEM_LIMIT),
        name="resid",
    )(x1, moe3, mod)


def _rope_tables(n):
    t = jnp.arange(n)
    row = (t // GRID_W).astype(F32)
    col = (t % GRID_W).astype(F32)
    n_freq = HEAD_DIM // 4
    inv_freq = jnp.power(ROPE_BASE, -jnp.arange(n_freq, dtype=F32) / n_freq)
    ang = jnp.concatenate([row[:, None] * inv_freq, col[:, None] * inv_freq], axis=-1)
    cos, sin = jnp.cos(ang), jnp.sin(ang)
    cos_t = jnp.concatenate([cos, cos, cos, cos], axis=-1)
    sin_t = jnp.concatenate([-sin, sin, -sin, sin], axis=-1)
    return cos_t, sin_t


def _gain128(g):
    return jnp.concatenate([g, g]).reshape(1, LANES).astype(F32)


def _channel_mix(h_pack, aff_t, tri, weights, layer, *, flatten):
    wg, wu, wd = weights
    b, _, n = aff_t.shape
    cap = EC_CAPACITY_FACTOR * n // N_EXPERTS
    idx, gate = _topk_call(aff_t, tri, cap=cap)
    if flatten:
        idx = (idx + (jnp.arange(b, dtype=jnp.int32) * n)[:, None, None]).transpose(1, 0, 2).reshape(1, N_EXPERTS, b * cap)
        gate = gate.transpose(1, 0, 2).reshape(1, N_EXPERTS, b * cap)
        h_pack = h_pack.reshape(1, -1, LANES)
    return _moe_call(idx, h_pack, gate, wg, wu, wd, layer).reshape(b, -1, LANES)


def kernel(x, c, ctx, c_ctx, ada_w, ada_b, norm1_g, norm2_g, router_w, exp_w_gate, exp_w_up, exp_w_down, ev_w_in,
           ev_w_out, ev_q_gain, ev_k_gain, ev_sink, od_w_in, od_w_out, od_q_gain, od_k_gain, od_rpb):
    b, n, d = x.shape
    nc = ctx.shape[1]
    tm = min(512, n)
    tmb = min(1024, n)
    tmc = min(512, nc)

    rows = -(-(b + 1) // 8) * 8
    cs = jnp.concatenate([c, c_ctx[None], jnp.zeros((rows - b - 1, d), F32)], axis=0)
    mods = _mod_call(cs, ada_w, ada_b).reshape(ada_w.shape[0], rows, 6, d)

    cos_t, sin_t = _rope_tables(n)
    cs_dft = _channel_dft()
    tri = jnp.asarray(np.triu(np.ones((LANES, LANES), np.float32)), BF16)
    fw = FOURIER_GROUPS * LANES

    def router_split(l):
        rw = router_w[l].T
        rh = rw.astype(BF16)
        rl = (rw - rh.astype(F32)).astype(BF16)
        left = jnp.pad(jnp.concatenate([rh, rl], axis=0), ((0, LANES - 2 * N_EXPERTS), (0, 0)))
        right = jnp.pad(rh, ((0, LANES - N_EXPERTS), (0, 0)))
        return jnp.concatenate([left, right], axis=1)

    w_stacks = (exp_w_gate, exp_w_up, exp_w_down)
    as_layer = lambda ws: [w.reshape((1,) + s.shape[1:]) for w, s in zip(ws, w_stacks)]

    mod_x, mod_c = mods[0, :b], mods[0, b:b + 1]
    g1 = norm1_g[0].reshape(1, d)
    g2 = norm2_g[0].reshape(1, d)
    w_in = ev_w_in[0].astype(BF16)
    w_out = ev_w_out[0].astype(BF16)
    qg, kg = _gain128(ev_q_gain[0]), _gain128(ev_k_gain[0])
    sink = ev_sink[0].astype(F32)
    rw = router_split(0)

    ab_x, q_x, kd_x, vd_x = _premix_even_call(x, mod_x, g1, w_in, cs_dft, cos_t, sin_t, qg, kg, rope=True, tm=tmb)
    ab_c, q_c, kd_c, vd_c = _premix_even_call(ctx, mod_c, g1, w_in, cs_dft, cos_t[:nc], sin_t[:nc], qg, kg,
                                              rope=False, tm=tmc)
    a_x, *wts = _swa_call(sink, q_x, kd_x, vd_x, kd_c, vd_c, w_stacks, 0)
    wts = as_layer(wts)
    a_c = _ctx_attn_call(sink, q_c, kd_c, vd_c)
    four_x = _fourier_call(ab_x, *_dft_tables(n), tm=tmb)
    four_c = _fourier_call(ab_c, *_dft_tables(nc), tm=tmc)
    w_out_parts = [w_out[:fw], w_out[fw:]]
    x1, h3_x, aff_x = _postmix_call([four_x, a_x], w_out_parts, x, mod_x, g2, rw, tm=tmb)
    c1, h3_c, aff_c = _postmix_call([four_c, a_c], w_out_parts, ctx, mod_c, g2, rw, tm=tmc)
    moe_x = _channel_mix(h3_x, aff_x, tri, wts, 0, flatten=False)
    moe_c = _channel_mix(h3_c, aff_c, tri, wts, 0, flatten=True)

    mod_x0, mod_c0 = mod_x, mod_c
    mod_x, mod_c = mods[1, :b], mods[1, b:b + 1]
    g1 = norm1_g[1].reshape(1, d)
    g2 = norm2_g[1].reshape(1, d)
    w_in = od_w_in[0].astype(BF16)
    w_out = od_w_out[0].astype(BF16)
    qg, kg = _gain128(od_q_gain[0]), _gain128(od_k_gain[0])
    rw = router_split(1)

    q_x, k_x, v_x, x = _premix_odd_call(x1, moe_x, mod_x0, mod_x, g1, w_in, qg, kg, emit_x=True, tm=tmb)
    _, k_c, v_c = _premix_odd_call(c1, moe_c, mod_c0, mod_c, g1, w_in, qg, kg, emit_x=False, tm=tmc)
    a_x, *wts = _na_call(q_x, k_x, v_x, k_c, v_c, _na_bias_table(od_rpb[0]), w_stacks, 1)
    wts = as_layer(wts)
    x1, h3_x, aff_x = _postmix_call([a_x], [w_out], x, mod_x, g2, rw, tm=tmb)
    return _resid_call(x1, _channel_mix(h3_x, aff_x, tri, wts, 0, flatten=False), mod_x, tm=tmb)
```

```python
import functools
import math

import numpy as np
import jax
import jax.numpy as jnp
from jax import lax
from jax.experimental import pallas as pl
from jax.experimental.pallas import tpu as pltpu

GRID_W = 64
HEAD_DIM = 64
FOURIER_GROUPS = 4
SWA_WINDOW = 128
SWA_BLOCK = 128
NA_ROWS_MAX = 8
NA_COLS = 16
N_EXPERTS = 16
EC_CAPACITY_FACTOR = 2
ROPE_BASE = 10000.0
EPS = 1e-6

LANES = 128
NEG = -1e30
NA_QROWS = 4
NA_JBLK = 2
NA_KROWS = 12
DFT_SUB = 64
MOE_ROW_CHUNK = 128
SWA_QBLK = 8
ROW_SUB = 256
ONES_ROWS = 16
LOG2E = math.log2(math.e)
VMEM_LIMIT = 60 * 1024 * 1024

F32 = jnp.float32
BF16 = jnp.bfloat16


def _cparams(n_axes, vmem=None):
    return pltpu.CompilerParams(dimension_semantics=("arbitrary",) * n_axes, vmem_limit_bytes=vmem)


def _dot(a, b):
    return jnp.dot(a, b, preferred_element_type=F32)


def _dot_t(a, b):
    return lax.dot_general(a, b, (((1,), (1,)), ((), ())), preferred_element_type=F32)


def _split(a):
    hi = a.astype(BF16)
    lo = (a - hi.astype(F32)).astype(BF16)
    return hi, lo


def _dot3(a, b):
    ah, al = _split(a)
    bh, bl = _split(b)
    return _dot(ah, bh) + _dot(al, bh) + _dot(ah, bl)


def _silu(a):
    return a / (1.0 + jnp.exp(-a))


def _rms_mod(x, gain, shift, scale):
    ms = jnp.mean(x * x, axis=-1, keepdims=True)
    y = x * lax.rsqrt(ms + EPS) * gain
    return y * (1.0 + scale) + shift


def _mod_kernel(cs_ref, w_ref, b_ref, o_ref):
    o_ref[0] = _dot3(_silu(cs_ref[...]), w_ref[0]) + b_ref[0]


def _mod_call(cs, ada_w, ada_b):
    depth, d, d6 = ada_w.shape
    r = cs.shape[0]
    tn = 1024
    return pl.pallas_call(
        _mod_kernel,
        grid=(depth, d6 // tn),
        in_specs=[
            pl.BlockSpec((r, d), lambda l, j: (0, 0)),
            pl.BlockSpec((1, d, tn), lambda l, j: (l, 0, j)),
            pl.BlockSpec((1, 1, tn), lambda l, j: (l, 0, j)),
        ],
        out_specs=pl.BlockSpec((1, r, tn), lambda l, j: (l, 0, j)),
        out_shape=jax.ShapeDtypeStruct((depth, r, d6), F32),
        compiler_params=_cparams(2),
        name="mod",
    )(cs, ada_w, ada_b.reshape(depth, 1, d6))


def _head_norm(t, gain, lane):
    t2 = t * t
    s_lo = jnp.sum(jnp.where(lane < HEAD_DIM, t2, 0.0), axis=-1, keepdims=True)
    s_all = jnp.sum(t2, axis=-1, keepdims=True)
    ms = jnp.where(lane < HEAD_DIM, s_lo, s_all - s_lo) * (1.0 / HEAD_DIM)
    return t * lax.rsqrt(ms + EPS) * gain


def _rope(t, cos_t, sin_t, lane):
    partner = jnp.where((lane % HEAD_DIM) < HEAD_DIM // 2, pltpu.roll(t, LANES - HEAD_DIM // 2, 1),
                        pltpu.roll(t, HEAD_DIM // 2, 1))
    return t * cos_t + partner * sin_t


def _dup_halves(t, lane):
    sw = pltpu.roll(t, HEAD_DIM, 1)
    return jnp.where(lane < HEAD_DIM, t, sw), jnp.where(lane < HEAD_DIM, sw, t)


def _premix_even_kernel(x_ref, mod_ref, g_ref, w_ref, cs_ref, cos_ref, sin_ref, qg_ref, kg_ref,
                        ab_ref, q_ref, kd_ref, vt_ref, *, rope):
    tm = x_ref.shape[1]
    sub = min(ROW_SUB, tm)
    m = mod_ref[0]
    lane = lax.broadcasted_iota(jnp.int32, (sub, LANES), 1)
    fw = FOURIER_GROUPS * LANES
    qw = q_ref.shape[2]
    csb = cs_ref[...].astype(BF16)
    for r0 in range(0, tm, sub):
        rs = slice(r0, r0 + sub)
        hb = _rms_mod(x_ref[0, rs], g_ref[...], m[0:1], m[1:2]).astype(BF16)
        pf = _dot(hb, w_ref[:, 0:fw])
        for g in range(FOURIER_GROUPS):
            ab = _dot(pf[:, LANES * g:LANES * (g + 1)].astype(BF16), csb)
            ab_ref[0, rs, LANES * g:LANES * (g + 1)] = ab[:, :LANES].astype(BF16)
            ab_ref[0, rs, fw + LANES * g:fw + LANES * (g + 1)] = ab[:, LANES:].astype(BF16)
        pq = _dot(hb, w_ref[:, fw:fw + qw])
        if rope:
            cos_t = cos_ref[rs]
            sin_t = sin_ref[rs]
        for c in range(qw // LANES):
            t = _head_norm(pq[:, LANES * c:LANES * (c + 1)], qg_ref[...], lane)
            if rope:
                t = _rope(t, cos_t, sin_t, lane)
            q_ref[0, rs, LANES * c:LANES * (c + 1)] = t.astype(BF16)
        pk = _dot(hb, w_ref[:, fw + qw:fw + qw + 2 * LANES])
        k = _head_norm(pk[:, :LANES], kg_ref[...], lane)
        if rope:
            k = _rope(k, cos_t, sin_t, lane)
        k0, k1 = _dup_halves(k, lane)
        kd_ref[0, rs, :LANES] = k0.astype(BF16)
        kd_ref[0, rs, LANES:] = k1.astype(BF16)
        vt_ref[0, :, rs] = pk[:, LANES:].T.astype(BF16)


def _premix_even_call(x, mod, gain, w_bf, cs, cos_t, sin_t, qg, kg, *, rope, tm):
    b, n, d = x.shape
    mb = mod.shape[0]
    wtot = w_bf.shape[1]
    fw = FOURIER_GROUPS * LANES
    qw = wtot - fw - 2 * LANES
    mod_map = (lambda i, j: (i, 0, 0)) if mb > 1 else (lambda i, j: (0, 0, 0))
    tok = lambda i, j: (i, j, 0)
    const2 = lambda i, j: (0, 0)
    return pl.pallas_call(
        functools.partial(_premix_even_kernel, rope=rope),
        grid=(b, n // tm),
        in_specs=[
            pl.BlockSpec((1, tm, d), tok),
            pl.BlockSpec((1, 6, d), mod_map),
            pl.BlockSpec((1, d), const2),
            pl.BlockSpec((d, wtot), const2),
            pl.BlockSpec((LANES, 2 * LANES), const2),
            pl.BlockSpec((tm, LANES), lambda i, j: (j, 0)),
            pl.BlockSpec((tm, LANES), lambda i, j: (j, 0)),
            pl.BlockSpec((1, LANES), const2),
            pl.BlockSpec((1, LANES), const2),
        ],
        out_specs=[
            pl.BlockSpec((1, tm, 2 * fw), tok),
            pl.BlockSpec((1, tm, qw), tok),
            pl.BlockSpec((1, tm, 2 * LANES), tok),
            pl.BlockSpec((1, LANES, tm), lambda i, j: (i, 0, j)),
        ],
        out_shape=[
            jax.ShapeDtypeStruct((b, n, 2 * fw), BF16),
            jax.ShapeDtypeStruct((b, n, qw), BF16),
            jax.ShapeDtypeStruct((b, n, 2 * LANES), BF16),
            jax.ShapeDtypeStruct((b, LANES, n), BF16),
        ],
        compiler_params=_cparams(2, VMEM_LIMIT),
        name="premix_even",
    )(x, mod, gain, w_bf, cs, cos_t, sin_t, qg, kg)


def _gated_moe_residual(x_ref, moe_ref, gate_row, r0, sub):
    c8 = x_ref.shape[2] // LANES
    cols = []
    for j in range(c8):
        cs = slice(LANES * j, LANES * (j + 1))
        cols.append(x_ref[0, r0:r0 + sub, cs] + gate_row[:, cs] * moe_ref[0, pl.ds(r0 * c8 + j, sub, stride=c8), :])
    return jnp.concatenate(cols, axis=1)


def _premix_odd_kernel(x_ref, moe_ref, modp_ref, mod_ref, g_ref, w_ref, qg_ref, kg_ref, q_ref, k_ref, v_ref,
                       *xo_ref):
    tm = x_ref.shape[1]
    sub = min(ROW_SUB, tm)
    m = mod_ref[0]
    gate_prev = modp_ref[0][5:6]
    lane = lax.broadcasted_iota(jnp.int32, (sub, LANES), 1)
    wq = q_ref.shape[2]
    chunk = 4 * LANES
    for r0 in range(0, tm, sub):
        rs = slice(r0, r0 + sub)
        x = _gated_moe_residual(x_ref, moe_ref, gate_prev, r0, sub)
        if xo_ref:
            xo_ref[0][0, rs] = x
        hb = _rms_mod(x, g_ref[...], m[0:1], m[1:2]).astype(BF16)
        for c0 in range(0, 3 * wq, chunk):
            p = _dot(hb, w_ref[:, c0:c0 + chunk])
            for cc in range(chunk // LANES):
                col = c0 + cc * LANES
                t = p[:, cc * LANES:(cc + 1) * LANES]
                if col < wq:
                    q_ref[0, rs, col:col + LANES] = _head_norm(t, qg_ref[...], lane).astype(BF16)
                elif col < 2 * wq:
                    k_ref[0, rs, col - wq:col - wq + LANES] = _head_norm(t, kg_ref[...], lane).astype(BF16)
                else:
                    v_ref[0, col - 2 * wq:col - 2 * wq + LANES, rs] = t.T.astype(BF16)


def _premix_odd_call(x1, moe3, mod_prev, mod, gain, w_bf, qg, kg, *, emit_x, tm):
    b, n, d = x1.shape
    mb = mod.shape[0]
    wq = w_bf.shape[1] // 3
    c8 = d // LANES
    mod_map = (lambda i, j: (i, 0, 0)) if mb > 1 else (lambda i, j: (0, 0, 0))
    tok = lambda i, j: (i, j, 0)
    const2 = lambda i, j: (0, 0)
    out_specs = [pl.BlockSpec((1, tm, wq), tok)] * 2 + [pl.BlockSpec((1, wq, tm), lambda i, j: (i, 0, j))]
    out_shape = [jax.ShapeDtypeStruct((b, n, wq), BF16)] * 2 + [jax.ShapeDtypeStruct((b, wq, n), BF16)]
    if emit_x:
        out_specs.append(pl.BlockSpec((1, tm, d), tok))
        out_shape.append(jax.ShapeDtypeStruct((b, n, d), F32))
    return pl.pallas_call(
        _premix_odd_kernel,
        grid=(b, n // tm),
        in_specs=[
            pl.BlockSpec((1, tm, d), tok),
            pl.BlockSpec((1, tm * c8, LANES), tok),
            pl.BlockSpec((1, 6, d), mod_map),
            pl.BlockSpec((1, 6, d), mod_map),
            pl.BlockSpec((1, d), const2),
            pl.BlockSpec((d, 3 * wq), const2),
            pl.BlockSpec((1, LANES), const2),
            pl.BlockSpec((1, LANES), const2),
        ],
        out_specs=out_specs,
        out_shape=out_shape,
        compiler_params=_cparams(2, VMEM_LIMIT),
        name="premix_odd",
    )(x1, moe3, mod_prev, mod, gain, w_bf, qg, kg)


def _col_reduce(x, op, slab=64):
    r = x.shape[0]
    if r > slab and r % slab == 0:
        x = op(x.reshape(r // slab, slab, x.shape[1]), axis=0)
    return op(x, axis=0, keepdims=True)


def _softmax_pv_t(s_t, v_t, sink=None):
    mx = _col_reduce(s_t, jnp.max)
    if sink is not None:
        mx = jnp.maximum(mx, sink)
    p = jnp.exp2((s_t - mx).astype(BF16))
    nd = v_t.shape[0]
    o = _dot(jnp.concatenate([v_t, jnp.ones((ONES_ROWS, v_t.shape[1]), BF16)], axis=0), p)
    den = o[nd:nd + 1]
    if sink is not None:
        den = den + jnp.exp2(sink - mx)
    return o[:nd] / den


def _mask_half(qc, lane, half):
    q32 = qc.astype(F32) * (HEAD_DIM ** -0.5 * LOG2E)
    keep = (lane < HEAD_DIM) if half == 0 else (lane >= HEAD_DIM)
    return jnp.where(keep, q32, 0.0).astype(BF16)


def _gqa_scores(q_ref, h, kd, n_kv, masks, lane, r0, tq):
    rs = slice(r0, r0 + tq)
    cols_per_kv = q_ref.shape[2] // LANES // n_kv
    q_rows = []
    for cc in range(cols_per_kv):
        c = cols_per_kv * h + cc
        for half in range(2):
            q_rows.append(_mask_half(q_ref[0, rs, LANES * c:LANES * (c + 1)], lane, half))
    qm = jnp.concatenate(q_rows, axis=0)
    s = _dot_t(kd, qm)
    pieces, k0 = [], 0
    for rows, mask in masks:
        blk = s[k0:k0 + rows]
        pieces.append(blk if mask is None else jnp.where(mask, blk, NEG))
        k0 += rows
    return jnp.concatenate(pieces, axis=0) if len(pieces) > 1 else pieces[0]


def _gqa_finish(sink_ref, o_ref, h, s_t, v_t, r0, tq):
    rs = slice(r0, r0 + tq)
    n_kv = v_t.shape[0] // HEAD_DIM
    cols_per_kv = o_ref.shape[2] // LANES // n_kv
    heads = 2 * cols_per_kv
    head_of_lane = lax.broadcasted_iota(jnp.int32, (1, heads * tq), 1) // tq
    sink = jnp.zeros((1, heads * tq), F32)
    for a in range(heads):
        sink = jnp.where(head_of_lane == a, sink_ref[heads * h + a] * LOG2E, sink)
    o_t = _softmax_pv_t(s_t, v_t, sink)[HEAD_DIM * h:HEAD_DIM * (h + 1)]
    for cc in range(cols_per_kv):
        c = cols_per_kv * h + cc
        tile = jnp.concatenate([o_t[:, tq * (2 * cc):tq * (2 * cc + 1)], o_t[:, tq * (2 * cc + 1):tq * (2 * cc + 2)]],
                               axis=0)
        o_ref[0, rs, LANES * c:LANES * (c + 1)] = tile.T.astype(BF16)


def _swa_kernel(sink_ref, q_ref, km_ref, k0_ref, kp_ref, vm_ref, v0_ref, vp_ref, kc_ref, vc_ref, *rest, n_step):
    o_ref = _ride_cast(rest)
    g = pl.program_id(1)
    tq = SWA_BLOCK
    nc = kc_ref.shape[1]
    n_kv = kc_ref.shape[2] // LANES
    heads = q_ref.shape[2] // HEAD_DIM // n_kv
    lane = lax.broadcasted_iota(jnp.int32, (tq, LANES), 1)
    jk = lax.broadcasted_iota(jnp.int32, (SWA_BLOCK, heads * tq), 0)
    iq = lax.broadcasted_iota(jnp.int32, (SWA_BLOCK, heads * tq), 1) % tq
    below, above = jk >= iq, jk <= iq
    nq = q_ref.shape[1] // SWA_BLOCK
    blk = lambda s: slice(SWA_BLOCK * s, SWA_BLOCK * (s + 1))

    def key_block(s, hs):
        return km_ref[0, :, hs] if s < 0 else kp_ref[0, :, hs] if s == nq else k0_ref[0, blk(s), hs]

    def value_block(s):
        return vm_ref[0] if s < 0 else vp_ref[0] if s == nq else v0_ref[0, :, blk(s)]

    def scores(sub, h):
        hs = slice(LANES * h, LANES * (h + 1))
        first = jnp.logical_and(below, g > 0) if sub == 0 else below
        last = jnp.logical_and(above, g < n_step - 1) if sub == nq - 1 else above
        masks = [(nc, None), (SWA_BLOCK, first), (SWA_BLOCK, None), (SWA_BLOCK, last)]
        kd = jnp.concatenate([kc_ref[0, :, hs]] + [key_block(s, hs) for s in (sub - 1, sub, sub + 1)], axis=0)
        return _gqa_scores(q_ref, h, kd, n_kv, masks, lane, SWA_BLOCK * sub, tq)

    groups = [(sub, h) for sub in range(nq) for h in range(n_kv)]
    ahead = 3
    pending = [scores(*grp) for grp in groups[:ahead]]
    for i, (sub, h) in enumerate(groups):
        s_t = pending.pop(0)
        if i + ahead < len(groups):
            pending.append(scores(*groups[i + ahead]))
        v_t = jnp.concatenate([vc_ref[0]] + [value_block(s) for s in (sub - 1, sub, sub + 1)], axis=1)
        _gqa_finish(sink_ref, o_ref, h, s_t, v_t, SWA_BLOCK * sub, tq)


def _cast_riders(ws, layer, total, flat):
    ins, outs, shapes, args = [], [], [], []
    for w in ws:
        l, e, r, c = w.shape
        rows = e * r // total
        ins.append(pl.BlockSpec((rows, c), lambda i, j: (layer * total + flat(i, j), 0)))
        outs.append(pl.BlockSpec((rows, c), lambda i, j: (flat(i, j), 0)))
        shapes.append(jax.ShapeDtypeStruct((e * r, c), BF16))
        args.append(w.reshape(l * e * r, c))
    return ins, outs, shapes, args


def _ride_cast(rest):
    n_cast = (len(rest) - 1) // 2
    for w_ref, c_ref in zip(rest[:n_cast], rest[n_cast + 1:]):
        c_ref[...] = w_ref[...].astype(BF16)
    return rest[n_cast]


def _swa_call(sink, q, kd, vt, kdc, vtc, ws, layer):
    b, n, qw = q.shape
    nc = kdc.shape[1]
    kw = kd.shape[2]
    vw = vt.shape[1]
    n_blk = n // SWA_BLOCK
    nq = min(SWA_QBLK, n_blk)
    n_step = n_blk // nq
    prev = lambda j: jnp.maximum(nq * j - 1, 0)
    nxt = lambda j: jnp.minimum(nq * j + nq, n_blk - 1)
    kspec = lambda f: pl.BlockSpec((1, SWA_BLOCK, kw), lambda i, j: (i, f(j), 0))
    vspec = lambda f: pl.BlockSpec((1, vw, SWA_BLOCK), lambda i, j: (i, 0, f(j)))
    c_in, c_out, c_shape, c_args = _cast_riders(ws, layer, b * n_step, lambda i, j: i * n_step + j)
    return pl.pallas_call(
        functools.partial(_swa_kernel, n_step=n_step),
        grid=(b, n_step),
        in_specs=[
            pl.BlockSpec(memory_space=pltpu.SMEM),
            pl.BlockSpec((1, nq * SWA_BLOCK, qw), lambda i, j: (i, j, 0)),
            kspec(prev), pl.BlockSpec((1, nq * SWA_BLOCK, kw), lambda i, j: (i, j, 0)), kspec(nxt),
            vspec(prev), pl.BlockSpec((1, vw, nq * SWA_BLOCK), lambda i, j: (i, 0, j)), vspec(nxt),
            pl.BlockSpec((1, nc, kw), lambda i, j: (i, 0, 0)),
            pl.BlockSpec((1, vw, nc), lambda i, j: (i, 0, 0)),
        ] + c_in,
        out_specs=[pl.BlockSpec((1, nq * SWA_BLOCK, qw), lambda i, j: (i, j, 0))] + c_out,
        out_shape=[jax.ShapeDtypeStruct((b, n, qw), BF16)] + c_shape,
        compiler_params=_cparams(2, VMEM_LIMIT),
        name="swa",
    )(sink, q, kd, kd, kd, vt, vt, vt, kdc, vtc, *c_args)


def _ctx_attn_kernel(sink_ref, q_ref, kc_ref, vc_ref, o_ref):
    tq = q_ref.shape[1]
    lane = lax.broadcasted_iota(jnp.int32, (tq, LANES), 1)
    n_kv = kc_ref.shape[2] // LANES
    scores = [_gqa_scores(q_ref, h, kc_ref[0, :, LANES * h:LANES * (h + 1)], n_kv, [(kc_ref.shape[1], None)], lane,
                          0, tq) for h in range(n_kv)]
    for h, s_t in enumerate(scores):
        _gqa_finish(sink_ref, o_ref, h, s_t, vc_ref[0], 0, tq)


def _ctx_attn_call(sink, q, kd, vt):
    b, nc, qw = q.shape
    kw = kd.shape[2]
    vw = vt.shape[1]
    m = lambda i: (i, 0, 0)
    return pl.pallas_call(
        _ctx_attn_kernel,
        grid=(b,),
        in_specs=[
            pl.BlockSpec(memory_space=pltpu.SMEM),
            pl.BlockSpec((1, nc, qw), m),
            pl.BlockSpec((1, nc, kw), m),
            pl.BlockSpec((1, vw, nc), m),
        ],
        out_specs=pl.BlockSpec((1, nc, qw), m),
        out_shape=jax.ShapeDtypeStruct((b, nc, qw), BF16),
        compiler_params=_cparams(1, VMEM_LIMIT),
        name="ctx_attn",
    )(sink, q, kd, vt)


def _na_kernel(q_ref, k0_ref, k1_ref, k2_ref, k3_ref, v0_ref, v1_ref, v2_ref, v3_ref, kc_ref, vc_ref, tb_ref, *rest,
               n_j, rows):
    o_ref = _ride_cast(rest)
    g = pl.program_id(1)
    tq = NA_QROWS * GRID_W
    tk = NA_KROWS * GRID_W
    nc = kc_ref.shape[1]
    k_blocks = [k0_ref, k1_ref, k2_ref, k3_ref]
    v_blocks = [v0_ref, v1_ref, v2_ref, v3_ref]
    base = jnp.clip(NA_JBLK * g - 1, 0, n_j - (NA_JBLK + 2))
    lane = lax.broadcasted_iota(jnp.int32, (tq, LANES), 1)
    k_row = (lax.broadcasted_iota(jnp.int32, (nc + tk, LANES), 0) - nc) // GRID_W
    k_lane = lax.broadcasted_iota(jnp.int32, (nc + tk, LANES), 1)
    is_local = lax.broadcasted_iota(jnp.int32, (nc + tk, LANES), 0) >= nc
    q_grid_row = (lax.broadcasted_iota(jnp.int32, (2 * tq, LANES), 0) % tq) // GRID_W
    q_extra = jnp.where(lax.broadcasted_iota(jnp.int32, (2 * tq, LANES), 1) == q_grid_row, 1.0, 0.0).astype(BF16)
    n_dr = 2 * NA_ROWS_MAX - 1
    shifted, k_extras, deltas = [], [], []
    for u in range(NA_JBLK):
        j = NA_JBLK * g + u
        w0_blk = jnp.clip(j - 1, 0, n_j - 3)
        w0 = NA_QROWS * w0_blk
        lo = jnp.zeros((nc + tk, LANES), jnp.int32)
        for rr in range(NA_QROWS):
            r = NA_QROWS * j + rr
            lo_rr = jnp.clip(r - NA_ROWS_MAX // 2, 0, rows - NA_ROWS_MAX) - w0
            lo = jnp.where(k_lane == rr, lo_rr, lo)
        outside = jnp.logical_or(k_row < lo, k_row >= lo + NA_ROWS_MAX)
        k_extras.append(jnp.where(jnp.logical_and(jnp.logical_and(is_local, k_lane < NA_QROWS), outside), NEG, 0.0)
                        .astype(BF16))
        shifted.append(w0_blk - base == 1)
        deltas.append(w0 - NA_QROWS * j)

    def scores(u, c):
        cs = slice(LANES * c, LANES * (c + 1))
        qc = q_ref[0, tq * u:tq * (u + 1), cs]
        k_loc = [jnp.where(shifted[u], k_blocks[d + 1][0, :, cs], k_blocks[d][0, :, cs]) for d in range(3)]
        kl = jnp.concatenate([kc_ref[0, :, cs]] + k_loc, axis=0)
        qm = jnp.concatenate([_mask_half(qc, lane, 0), _mask_half(qc, lane, 1)], axis=0)
        return _dot_t(jnp.concatenate([kl, k_extras[u]], axis=1), jnp.concatenate([qm, q_extra], axis=1))

    groups = [(u, c) for c in range(q_ref.shape[2] // LANES) for u in range(NA_JBLK)]
    ahead = 2
    pending = [scores(*grp) for grp in groups[:ahead]]
    for i, (u, c) in enumerate(groups):
        cs = slice(LANES * c, LANES * (c + 1))
        s = pending.pop(0)
        if i + ahead < len(groups):
            pending.append(scores(*groups[i + ahead]))
        v_loc = [jnp.where(shifted[u], v_blocks[d + 1][0, cs, :], v_blocks[d][0, cs, :]) for d in range(3)]
        v_t = jnp.concatenate([vc_ref[0, cs, :]] + v_loc, axis=1)
        bias_rows = []
        for kr in range(NA_KROWS):
            pieces = []
            for half in range(2):
                for t in range(NA_QROWS // 2):
                    dr = deltas[u] + kr - 2 * t + NA_ROWS_MAX - 1
                    pieces.append(tb_ref[2 * c + half, jnp.clip(dr, 0, n_dr)])
            bias_rows.append(jnp.concatenate(pieces, axis=1))
        bias = jnp.concatenate(bias_rows, axis=0)
        s_t = jnp.concatenate([s[:nc], s[nc:] + bias], axis=0)
        o_t = _softmax_pv_t(s_t, v_t)
        tile = jnp.concatenate([o_t[:HEAD_DIM, :tq], o_t[HEAD_DIM:, tq:]], axis=0)
        o_ref[0, tq * u:tq * (u + 1), cs] = tile.T.astype(BF16)


def _na_call(q, k, vt, kc, vtc, tb, ws, layer):
    b, n, w = q.shape
    nc = kc.shape[1]
    rows = n // GRID_W
    n_j = rows // NA_QROWS
    n_g = n_j // NA_JBLK
    tq = NA_QROWS * GRID_W
    cur = lambda i, j: (i, j, 0)
    first = lambda j: jnp.clip(NA_JBLK * j - 1, 0, n_j - (NA_JBLK + 2))
    kspec = lambda d: pl.BlockSpec((1, tq, w), lambda i, j: (i, first(j) + d, 0))
    vspec = lambda d: pl.BlockSpec((1, w, tq), lambda i, j: (i, 0, first(j) + d))
    c_in, c_out, c_shape, c_args = _cast_riders(ws, layer, b * n_g, lambda i, j: i * n_g + j)
    return pl.pallas_call(
        functools.partial(_na_kernel, n_j=n_j, rows=rows),
        grid=(b, n_g),
        in_specs=[
            pl.BlockSpec((1, NA_JBLK * tq, w), cur),
            kspec(0), kspec(1), kspec(2), kspec(3),
            vspec(0), vspec(1), vspec(2), vspec(3),
            pl.BlockSpec((1, nc, w), lambda i, j: (i, 0, 0)),
            pl.BlockSpec((1, w, nc), lambda i, j: (i, 0, 0)),
            pl.BlockSpec(tb.shape, lambda i, j: (0, 0, 0, 0)),
        ] + c_in,
        out_specs=[pl.BlockSpec((1, NA_JBLK * tq, w), cur)] + c_out,
        out_shape=[jax.ShapeDtypeStruct((b, n, w), BF16)] + c_shape,
        compiler_params=_cparams(2, VMEM_LIMIT),
        name="na",
    )(q, k, k, k, k, vt, vt, vt, vt, kc, vtc, tb, *c_args)


def _na_bias_table(rpb):
    col_q = np.arange(GRID_W)[None, :]
    col_k = np.arange(GRID_W)[:, None]
    c_start = np.clip(col_q - NA_COLS // 2, 0, GRID_W - NA_COLS)
    col_valid = (col_k >= c_start) & (col_k < c_start + NA_COLS)
    dc_idx = np.clip(col_k - col_q + NA_COLS - 1, 0, 2 * NA_COLS - 2)
    t = jnp.where(col_valid[None, None], (rpb.astype(F32) * LOG2E)[:, :, dc_idx], NEG)
    pad = jnp.full_like(t[:, :1], NEG)
    t = jnp.concatenate([pad, t, pad], axis=1)
    return jnp.concatenate([t[:, 1:], t[:, :-1]], axis=-1)


def _fourier_kernel(ab_ref, ca_ref, sa_ref, cb_ref, sb_ref, rev_ref, o_ref, fold_ref, rev_scr):
    n = ab_ref.shape[1]
    nh = n // 2
    tm, fw = o_ref.shape[1], o_ref.shape[2]

    @pl.when(pl.program_id(1) == 0)
    def _():
        for m in range(nh // LANES):
            blk = ab_ref[0, n - LANES * (m + 1):n - LANES * m, :]
            rev_scr[LANES * m:LANES * (m + 1), :] = _dot(rev_ref[...], blk)
        ck = min(256, nh)
        row = lax.broadcasted_iota(jnp.int32, (ck, 1), 0)
        for r0 in range(0, nh, ck):
            prev = (r0 - 1) % nh
            shifted = jnp.where(row == 0, rev_scr[prev:prev + 1, :], pltpu.roll(rev_scr[r0:r0 + ck, :], 1, 0))
            lo = ab_ref[0, r0:r0 + ck, :].astype(F32)
            a2 = lo[:, :fw] + shifted[:, :fw]
            b2 = lo[:, fw:] - shifted[:, fw:]
            if r0 == 0:
                a2 = jnp.where(row == 0, lo[:, :fw], a2)
                b2 = jnp.where(row == 0, shifted[:, :fw], b2)
            fold_ref[r0:r0 + ck, :fw] = a2.astype(BF16)
            fold_ref[r0:r0 + ck, fw:] = b2.astype(BF16)

    cb = cb_ref[...]
    sb = sb_ref[...]
    sub = min(ROW_SUB, tm)
    for r0 in range(0, tm, sub):
        cm, sm = [], []
        for a in range(r0 // DFT_SUB, (r0 + sub) // DFT_SUB):
            ca = ca_ref[a:a + 1, :]
            sa = sa_ref[a:a + 1, :]
            cm.append((ca * cb - sa * sb).astype(BF16))
            sm.append((sa * cb + ca * sb).astype(BF16))
        y = _dot(jnp.concatenate(cm, axis=0), fold_ref[:, :fw]) - _dot(jnp.concatenate(sm, axis=0), fold_ref[:, fw:])
        o_ref[0, r0:r0 + sub] = y.astype(BF16)


def _fourier_call(ab, ca, sa, cb, sb, *, tm):
    b, n, w2 = ab.shape
    nh = n // 2
    fw = w2 // 2
    na = tm // DFT_SUB
    rev = jnp.asarray(np.eye(LANES, dtype=np.float32)[::-1], BF16)
    return pl.pallas_call(
        _fourier_kernel,
        grid=(b, n // tm),
        in_specs=[
            pl.BlockSpec((1, n, w2), lambda i, j: (i, 0, 0)),
            pl.BlockSpec((na, nh), lambda i, j: (j, 0)),
            pl.BlockSpec((na, nh), lambda i, j: (j, 0)),
            pl.BlockSpec((DFT_SUB, nh), lambda i, j: (0, 0)),
            pl.BlockSpec((DFT_SUB, nh), lambda i, j: (0, 0)),
            pl.BlockSpec((LANES, LANES), lambda i, j: (0, 0)),
        ],
        out_specs=pl.BlockSpec((1, tm, fw), lambda i, j: (i, j, 0)),
        out_shape=jax.ShapeDtypeStruct((b, n, fw), BF16),
        scratch_shapes=[pltpu.VMEM((nh, w2), BF16), pltpu.VMEM((nh, w2), F32)],
        compiler_params=_cparams(2, VMEM_LIMIT),
        name="fourier",
    )(ab, ca, sa, cb, sb, rev)


def _dft_tables(n):
    nh = n // 2
    k = jnp.arange(nh, dtype=jnp.int32)
    a = jnp.arange(n // DFT_SUB, dtype=jnp.int32)
    bb = jnp.arange(DFT_SUB, dtype=jnp.int32)
    ang_a = ((DFT_SUB * a[:, None] * k[None, :]) % n).astype(F32) * (2.0 * math.pi / n)
    ang_b = ((bb[:, None] * k[None, :]) % n).astype(F32) * (2.0 * math.pi / n)
    s = float(n) ** -0.5
    nyq = -jnp.cos(math.pi * bb.astype(F32)) * s
    sin_b = jnp.concatenate([nyq[:, None], (jnp.sin(ang_b) * s)[:, 1:]], axis=1)
    return jnp.cos(ang_a), jnp.sin(ang_a), jnp.cos(ang_b) * s, sin_b


def _channel_dft():
    c = jnp.arange(LANES, dtype=jnp.int32)
    ang = ((c[:, None] * c[None, :]) % LANES).astype(F32) * (2.0 * math.pi / LANES)
    s = float(LANES) ** -0.5
    return jnp.concatenate([jnp.cos(ang) * s, jnp.sin(ang) * s], axis=1)


def _postmix_kernel(*refs, n_in):
    a_refs = refs[:n_in]
    w_refs = refs[n_in:2 * n_in]
    x_ref, mod_ref, g_ref, rw_ref, x1_ref, h3_ref, aff_ref = refs[2 * n_in:]
    tm, d = x_ref.shape[1], x_ref.shape[2]
    sub = min(ROW_SUB, tm)
    p = d // 2 // LANES
    m = mod_ref[0]
    for r0 in range(0, tm, sub):
        rs = slice(r0, r0 + sub)
        y = None
        for a_ref, w_ref in zip(a_refs, w_refs):
            t = _dot(a_ref[0, rs], w_ref[...])
            y = t if y is None else y + t
        x1 = x_ref[0, rs] + m[2:3] * y
        x1_ref[0, rs] = x1
        h2 = _rms_mod(x1, g_ref[...], m[3:4], m[4:5])
        hh, hl = _split(h2)
        parts = _dot_t(rw_ref[...], jnp.concatenate([hh, hl], axis=1))
        lt = parts[:N_EXPERTS] + parts[N_EXPERTS:2 * N_EXPERTS]
        ex = jnp.exp(lt - jnp.max(lt, axis=0, keepdims=True))
        aff_ref[0, :, rs] = ex / jnp.sum(ex, axis=0, keepdims=True)
        bits = pltpu.bitcast(hh.astype(F32), jnp.uint32)
        packed = (bits[:, d // 2:] & jnp.uint32(0xFFFF0000)) | (bits[:, :d // 2] >> 16)
        for jj in range(p):
            h3_ref[0, pl.ds(r0 * p + jj, sub, stride=p), :] = packed[:, LANES * jj:LANES * (jj + 1)]


def _postmix_call(a_list, w_list, x, mod, gain, rw, *, tm):
    b, n, d = x.shape
    mb = mod.shape[0]
    n_in = len(a_list)
    p = d // 2 // LANES
    mod_map = (lambda i, j: (i, 0, 0)) if mb > 1 else (lambda i, j: (0, 0, 0))
    tok = lambda i, j: (i, j, 0)
    const2 = lambda i, j: (0, 0)
    in_specs = [pl.BlockSpec((1, tm, a.shape[2]), tok) for a in a_list]
    in_specs += [pl.BlockSpec(w.shape, const2) for w in w_list]
    in_specs += [
        pl.BlockSpec((1, tm, d), tok),
        pl.BlockSpec((1, 6, d), mod_map),
        pl.BlockSpec((1, d), const2),
        pl.BlockSpec((LANES, 2 * d), const2),
    ]
    return pl.pallas_call(
        functools.partial(_postmix_kernel, n_in=n_in),
        grid=(b, n // tm),
        in_specs=in_specs,
        out_specs=[
            pl.BlockSpec((1, tm, d), tok),
            pl.BlockSpec((1, tm * p, LANES), tok),
            pl.BlockSpec((1, N_EXPERTS, tm), lambda i, j: (i, 0, j)),
        ],
        out_shape=[
            jax.ShapeDtypeStruct((b, n, d), F32),
            jax.ShapeDtypeStruct((b, n * p, LANES), jnp.uint32),
            jax.ShapeDtypeStruct((b, N_EXPERTS, n), F32),
        ],
        compiler_params=_cparams(2, VMEM_LIMIT),
        name="postmix",
    )(*a_list, *w_list, x, mod, gain, rw)


def _prefix_incl(ones, tri):
    carry = jnp.zeros((ones.shape[0], 1), F32)
    outs = []
    for c in range(ones.shape[1] // LANES):
        blk = ones[:, LANES * c:LANES * (c + 1)]
        outs.append(_dot(blk.astype(BF16), tri) + carry)
        carry = carry + jnp.sum(blk, axis=1, keepdims=True)
    return jnp.concatenate(outs, axis=1)


def _topk_kernel(aff_ref, tri_ref, idx_ref, gate_ref, *, cap):
    g, ne, n = aff_ref.shape
    rows = g * ne
    aff = aff_ref[...].reshape(rows, n)
    bits = pltpu.bitcast(aff, jnp.int32)
    capf = jnp.float32(cap)

    def body(it, lo):
        sh = 28 - 2 * it
        new = lo
        for k in (1, 2, 3):
            cand = lo | jnp.left_shift(jnp.int32(k), sh)
            cnt = jnp.sum(jnp.where(bits >= cand, 1.0, 0.0), axis=1, keepdims=True)
            new = jnp.where(cnt >= capf, cand, new)
        return new

    thr = lax.fori_loop(0, 15, body, jnp.zeros((rows, 1), jnp.int32))
    gt = jnp.where(bits > thr, 1.0, 0.0)
    eq = jnp.where(bits == thr, 1.0, 0.0)
    need = capf - jnp.sum(gt, axis=1, keepdims=True)
    tri = tri_ref[...]
    eq_before = _prefix_incl(eq, tri) - eq
    sel = gt + eq * jnp.where(eq_before < need, 1.0, 0.0)
    slot = _prefix_incl(sel, tri).astype(jnp.int32) - 1
    tok = lax.broadcasted_iota(jnp.int32, (rows, n), 1)
    disp = jnp.where(sel > 0.0, tok - slot, -1)
    gate = aff
    for bit in range((n - 1).bit_length()):
        step = 1 << bit
        disp_in = pltpu.roll(disp, n - step, 1)
        gate_in = pltpu.roll(gate, n - step, 1)
        move_in = jnp.logical_and(disp_in >= 0, ((disp_in >> bit) & 1) == 1)
        stay = jnp.logical_and(disp >= 0, ((disp >> bit) & 1) == 0)
        gate = jnp.where(move_in, gate_in, gate)
        disp = jnp.where(move_in, disp_in, jnp.where(stay, disp, -1))
    idx_ref[...] = (tok + disp)[:, :cap].reshape(g, ne, cap)
    gate_ref[...] = gate[:, :cap].reshape(g, ne, cap)


def _topk_call(aff_t, tri, *, cap):
    b, e, n = aff_t.shape
    g = math.gcd(b, max(1, 4096 // n))
    return pl.pallas_call(
        functools.partial(_topk_kernel, cap=cap),
        grid=(b // g,),
        in_specs=[
            pl.BlockSpec((g, e, n), lambda i: (i, 0, 0)),
            pl.BlockSpec((LANES, LANES), lambda i: (0, 0)),
        ],
        out_specs=[pl.BlockSpec((g, e, cap), lambda i: (i, 0, 0))] * 2,
        out_shape=[jax.ShapeDtypeStruct((b, e, cap), jnp.int32), jax.ShapeDtypeStruct((b, e, cap), F32)],
        compiler_params=_cparams(1, VMEM_LIMIT),
        name="topk",
    )(aff_t, tri)


def _moe_kernel(idx_a_ref, idx_c_ref, gate_c_ref, h3_ref, wg_ref, wu_ref, wd_ref, o_ref, xs_0, xs_1, y_0, y_1,
                *, cap, p, c8, ne, n_tiles):
    g = pl.program_id(0)
    valid_a = g < n_tiles
    valid_b = jnp.logical_and(g >= 1, g <= n_tiles)
    valid_c = g >= 2
    steady = jnp.logical_and(g >= 2, g < n_tiles)

    @pl.when(jnp.logical_and(valid_c, (g - 2) % ne == 0))
    def _():
        o_ref[...] = jnp.zeros(o_ref.shape, o_ref.dtype)

    unroll, group = 8, 4

    def gather_row(xs_a, s, dst):
        t = idx_a_ref[0, 0, s]
        xs_a[dst, :] = h3_ref[0, pl.ds(pl.multiple_of(t * p, p), p), :]

    def gather(xs_a, straight):
        if straight:
            for s in range(cap):
                gather_row(xs_a, s, pl.ds(s * p, p))
        else:
            def body(i, carry):
                for u in range(unroll):
                    s = i * unroll + u
                    gather_row(xs_a, s, pl.ds(pl.multiple_of(s * p, p), p))
                return carry
            lax.fori_loop(0, cap // unroll, body, 0)

    def scatter_rows(y_c, s_list, src_list):
        sums = []
        for s, src in zip(s_list, src_list):
            t = idx_c_ref[0, 0, s]
            rows = pl.ds(pl.multiple_of(t * c8, c8), c8)
            sums.append((rows, o_ref[0, rows, :] + y_c[src, :] * gate_c_ref[0, 0, s]))
        for rows, val in sums:
            o_ref[0, rows, :] = val

    def scatter(y_c, straight):
        if straight:
            for g0 in range(0, cap, group):
                ss = [g0 + u for u in range(group)]
                scatter_rows(y_c, ss, [pl.ds(s * c8, c8) for s in ss])
        else:
            def body(i, carry):
                ss = [i * group + u for u in range(group)]
                scatter_rows(y_c, ss, [pl.ds(pl.multiple_of(s * c8, c8), c8) for s in ss])
                return carry
            lax.fori_loop(0, cap // group, body, 0)

    def load_rows(xs_b):
        lo, hi = [], []
        for j in range(p):
            w = xs_b[pl.ds(j, cap, stride=p), :]
            lo.append(pltpu.bitcast(w << 16, F32).astype(BF16))
            hi.append(pltpu.bitcast(w & jnp.uint32(0xFFFF0000), F32).astype(BF16))
        return jnp.concatenate(lo + hi, axis=1)

    def expert(xs, y_b):
        rc = min(MOE_ROW_CHUNK, cap)
        chunks = range(0, cap, rc)
        au = [(_dot(xs[r0:r0 + rc], wg_ref[0, 0]), _dot(xs[r0:r0 + rc], wu_ref[0, 0])) for r0 in chunks]
        for r0, (a, u_) in zip(chunks, au):
            y = _dot((_silu(a) * u_).astype(BF16), wd_ref[0, 0])
            for j in range(c8):
                y_b[pl.ds(r0 * c8 + j, rc, stride=c8), :] = y[:, LANES * j:LANES * (j + 1)]

    def step(xs_a, xs_b, y_b, y_c):
        @pl.when(steady)
        def _():
            xs = load_rows(xs_b)
            gather(xs_a, True)
            scatter(y_c, True)
            expert(xs, y_b)

        @pl.when(jnp.logical_not(steady))
        def _():
            @pl.when(valid_b)
            def _():
                expert(load_rows(xs_b), y_b)

            @pl.when(valid_a)
            def _():
                gather(xs_a, False)

            @pl.when(valid_c)
            def _():
                scatter(y_c, False)

    @pl.when(g % 2 == 0)
    def _():
        step(xs_0, xs_1, y_1, y_0)

    @pl.when(g % 2 == 1)
    def _():
        step(xs_1, xs_0, y_0, y_1)


def _moe_call(idx, h3, gate, wg, wu, wd, layer):
    bm, ne, cap = idx.shape
    d = wg.shape[2]
    p = d // 2 // LANES
    c8 = d // LANES
    n = h3.shape[1] // p
    n_tiles = bm * ne
    idx3 = idx.reshape(n_tiles, 1, cap)
    gate3 = gate.reshape(n_tiles, 1, cap)
    tile_a = lambda i: jnp.minimum(i, n_tiles - 1)
    tile_b = lambda i: jnp.clip(i - 1, 0, n_tiles - 1)
    tile_c = lambda i: jnp.clip(i - 2, 0, n_tiles - 1)
    wmap = lambda i: (layer, tile_b(i) % ne, 0, 0)
    smem = lambda f: pl.BlockSpec((1, 1, cap), lambda i: (f(i), 0, 0), memory_space=pltpu.SMEM)
    return pl.pallas_call(
        functools.partial(_moe_kernel, cap=cap, p=p, c8=c8, ne=ne, n_tiles=n_tiles),
        grid=(n_tiles + 2,),
        in_specs=[
            smem(tile_a), smem(tile_c), smem(tile_c),
            pl.BlockSpec((1, n * p, LANES), lambda i: (tile_a(i) // ne, 0, 0)),
            pl.BlockSpec((1, 1, d, wg.shape[3]), wmap),
            pl.BlockSpec((1, 1, d, wu.shape[3]), wmap),
            pl.BlockSpec((1, 1, wd.shape[2], d), wmap),
        ],
        out_specs=pl.BlockSpec((1, n * c8, LANES), lambda i: (tile_c(i) // ne, 0, 0), pipeline_mode=pl.Buffered(1)),
        out_shape=jax.ShapeDtypeStruct((bm, n * c8, LANES), F32),
        scratch_shapes=[pltpu.VMEM((cap * p, LANES), jnp.uint32), pltpu.VMEM((cap * p, LANES), jnp.uint32),
                        pltpu.VMEM((cap * c8, LANES), F32), pltpu.VMEM((cap * c8, LANES), F32)],
        compiler_params=_cparams(1, VMEM_LIMIT),
        name="moe",
    )(idx3, idx3, gate3, h3, wg, wu, wd)


def _resid_kernel(x_ref, moe_ref, mod_ref, o_ref):
    tm = x_ref.shape[1]
    sub = min(ROW_SUB, tm)
    for r0 in range(0, tm, sub):
        o_ref[0, r0:r0 + sub] = _gated_moe_residual(x_ref, moe_ref, mod_ref[0][5:6], r0, sub)


def _resid_call(x1, moe3, mod, *, tm):
    b, n, d = x1.shape
    mb = mod.shape[0]
    c8 = d // LANES
    mod_map = (lambda i, j: (i, 0, 0)) if mb > 1 else (lambda i, j: (0, 0, 0))
    tok = lambda i, j: (i, j, 0)
    return pl.pallas_call(
        _resid_kernel,
        grid=(b, n // tm),
        in_specs=[
            pl.BlockSpec((1, tm, d), tok),
            pl.BlockSpec((1, tm * c8, LANES), tok),
            pl.BlockSpec((1, 6, d), mod_map),
        ],
        out_specs=pl.BlockSpec((1, tm, d), tok),
        out_shape=jax.ShapeDtypeStruct((b, n, d), F32),
        compiler_params=_cparams(2, VMEM_LIMIT),
        name="resid",
    )(x1, moe3, mod)


def _rope_tables(n):
    t = jnp.arange(n)
    row = (t // GRID_W).astype(F32)
    col = (t % GRID_W).astype(F32)
    n_freq = HEAD_DIM // 4
    inv_freq = jnp.power(ROPE_BASE, -jnp.arange(n_freq, dtype=F32) / n_freq)
    ang = jnp.concatenate([row[:, None] * inv_freq, col[:, None] * inv_freq], axis=-1)
    cos, sin = jnp.cos(ang), jnp.sin(ang)
    cos_t = jnp.concatenate([cos, cos, cos, cos], axis=-1)
    sin_t = jnp.concatenate([-sin, sin, -sin, sin], axis=-1)
    return cos_t, sin_t


def _gain128(g):
    return jnp.concatenate([g, g]).reshape(1, LANES).astype(F32)


def _channel_mix(h_pack, aff_t, tri, weights, layer, *, flatten):
    wg, wu, wd = weights
    b, _, n = aff_t.shape
    cap = EC_CAPACITY_FACTOR * n // N_EXPERTS
    idx, gate = _topk_call(aff_t, tri, cap=cap)
    if flatten:
        idx = (idx + (jnp.arange(b, dtype=jnp.int32) * n)[:, None, None]).transpose(1, 0, 2).reshape(1, N_EXPERTS, b * cap)
        gate = gate.transpose(1, 0, 2).reshape(1, N_EXPERTS, b * cap)
        h_pack = h_pack.reshape(1, -1, LANES)
    return _moe_call(idx, h_pack, gate, wg, wu, wd, layer).reshape(b, -1, LANES)


def kernel(x, c, ctx, c_ctx, ada_w, ada_b, norm1_g, norm2_g, router_w, exp_w_gate, exp_w_up, exp_w_down, ev_w_in,
           ev_w_out, ev_q_gain, ev_k_gain, ev_sink, od_w_in, od_w_out, od_q_gain, od_k_gain, od_rpb):
    b, n, d = x.shape
    nc = ctx.shape[1]
    tm = min(512, n)
    tmb = min(1024, n)
    tmc = min(512, nc)

    rows = -(-(b + 1) // 8) * 8
    cs = jnp.concatenate([c, c_ctx[None], jnp.zeros((rows - b - 1, d), F32)], axis=0)
    mods = _mod_call(cs, ada_w, ada_b).reshape(ada_w.shape[0], rows, 6, d)

    cos_t, sin_t = _rope_tables(n)
    cs_dft = _channel_dft()
    tri = jnp.asarray(np.triu(np.ones((LANES, LANES), np.float32)), BF16)
    fw = FOURIER_GROUPS * LANES

    def router_split(l):
        rw = router_w[l].T
        rh = rw.astype(BF16)
        rl = (rw - rh.astype(F32)).astype(BF16)
        left = jnp.pad(jnp.concatenate([rh, rl], axis=0), ((0, LANES - 2 * N_EXPERTS), (0, 0)))
        right = jnp.pad(rh, ((0, LANES - N_EXPERTS), (0, 0)))
        return jnp.concatenate([left, right], axis=1)

    w_stacks = (exp_w_gate, exp_w_up, exp_w_down)
    as_layer = lambda ws: [w.reshape((1,) + s.shape[1:]) for w, s in zip(ws, w_stacks)]

    mod_x, mod_c = mods[0, :b], mods[0, b:b + 1]
    g1 = norm1_g[0].reshape(1, d)
    g2 = norm2_g[0].reshape(1, d)
    w_in = ev_w_in[0].astype(BF16)
    w_out = ev_w_out[0].astype(BF16)
    qg, kg = _gain128(ev_q_gain[0]), _gain128(ev_k_gain[0])
    sink = ev_sink[0].astype(F32)
    rw = router_split(0)

    ab_x, q_x, kd_x, vd_x = _premix_even_call(x, mod_x, g1, w_in, cs_dft, cos_t, sin_t, qg, kg, rope=True, tm=tmb)
    ab_c, q_c, kd_c, vd_c = _premix_even_call(ctx, mod_c, g1, w_in, cs_dft, cos_t[:nc], sin_t[:nc], qg, kg,
                                              rope=False, tm=tmc)
    a_x, *wts = _swa_call(sink, q_x, kd_x, vd_x, kd_c, vd_c, w_stacks, 0)
    wts = as_layer(wts)
    a_c = _ctx_attn_call(sink, q_c, kd_c, vd_c)
    four_x = _fourier_call(ab_x, *_dft_tables(n), tm=tmb)
    four_c = _fourier_call(ab_c, *_dft_tables(nc), tm=tmc)
    w_out_parts = [w_out[:fw], w_out[fw:]]
    x1, h3_x, aff_x = _postmix_call([four_x, a_x], w_out_parts, x, mod_x, g2, rw, tm=tmb)
    c1, h3_c, aff_c = _postmix_call([four_c, a_c], w_out_parts, ctx, mod_c, g2, rw, tm=tmc)
    moe_x = _channel_mix(h3_x, aff_x, tri, wts, 0, flatten=False)
    moe_c = _channel_mix(h3_c, aff_c, tri, wts, 0, flatten=True)

    mod_x0, mod_c0 = mod_x, mod_c
    mod_x, mod_c = mods[1, :b], mods[1, b:b + 1]
    g1 = norm1_g[1].reshape(1, d)
    g2 = norm2_g[1].reshape(1, d)
    w_in = od_w_in[0].astype(BF16)
    w_out = od_w_out[0].astype(BF16)
    qg, kg = _gain128(od_q_gain[0]), _gain128(od_k_gain[0])
    rw = router_split(1)

    q_x, k_x, v_x, x = _premix_odd_call(x1, moe_x, mod_x0, mod_x, g1, w_in, qg, kg, emit_x=True, tm=tmb)
    _, k_c, v_c = _premix_odd_call(c1, moe_c, mod_c0, mod_c, g1, w_in, qg, kg, emit_x=False, tm=tmc)
    a_x, *wts = _na_call(q_x, k_x, v_x, k_c, v_c, _na_bias_table(od_rpb[0]), w_stacks, 1)
    wts = as_layer(wts)
    x1, h3_x, aff_x = _postmix_call([a_x], [w_out], x, mod_x, g2, rw, tm=tmb)
    return _resid_call(x1, _channel_mix(h3_x, aff_x, tri, wts, 0, flatten=False), mod_x, tm=tmb)
```

```python
import functools
import math

import numpy as np
import jax
import jax.numpy as jnp
from jax import lax
from jax.experimental import pallas as pl
from jax.experimental.pallas import tpu as pltpu

GRID_W = 64
HEAD_DIM = 64
FOURIER_GROUPS = 4
SWA_WINDOW = 128
SWA_BLOCK = 128
NA_ROWS_MAX = 8
NA_COLS = 16
N_EXPERTS = 16
EC_CAPACITY_FACTOR = 2
ROPE_BASE = 10000.0
EPS = 1e-6

LANES = 128
NEG = -1e30
NA_QROWS = 4
NA_JBLK = 2
NA_KROWS = 12
DFT_SUB = 64
MOE_ROW_CHUNK = 128
SWA_QBLK = 8
ROW_SUB = 256
ONES_ROWS = 16
LOG2E = math.log2(math.e)
VMEM_LIMIT = 60 * 1024 * 1024

F32 = jnp.float32
BF16 = jnp.bfloat16


def _cparams(n_axes, vmem=None):
    return pltpu.CompilerParams(dimension_semantics=("arbitrary",) * n_axes, vmem_limit_bytes=vmem)


def _dot(a, b):
    return jnp.dot(a, b, preferred_element_type=F32)


def _dot_t(a, b):
    return lax.dot_general(a, b, (((1,), (1,)), ((), ())), preferred_element_type=F32)


def _split(a):
    hi = a.astype(BF16)
    lo = (a - hi.astype(F32)).astype(BF16)
    return hi, lo


def _dot3(a, b):
    ah, al = _split(a)
    bh, bl = _split(b)
    return _dot(ah, bh) + _dot(al, bh) + _dot(ah, bl)


def _silu(a):
    return a / (1.0 + jnp.exp(-a))


def _rms_mod(x, gain, shift, scale):
    ms = jnp.mean(x * x, axis=-1, keepdims=True)
    y = x * lax.rsqrt(ms + EPS) * gain
    return y * (1.0 + scale) + shift


def _mod_kernel(cs_ref, w_ref, b_ref, o_ref):
    o_ref[0] = _dot3(_silu(cs_ref[...]), w_ref[0]) + b_ref[0]


def _mod_call(cs, ada_w, ada_b):
    depth, d, d6 = ada_w.shape
    r = cs.shape[0]
    tn = 1024
    return pl.pallas_call(
        _mod_kernel,
        grid=(depth, d6 // tn),
        in_specs=[
            pl.BlockSpec((r, d), lambda l, j: (0, 0)),
            pl.BlockSpec((1, d, tn), lambda l, j: (l, 0, j)),
            pl.BlockSpec((1, 1, tn), lambda l, j: (l, 0, j)),
        ],
        out_specs=pl.BlockSpec((1, r, tn), lambda l, j: (l, 0, j)),
        out_shape=jax.ShapeDtypeStruct((depth, r, d6), F32),
        compiler_params=_cparams(2),
        name="mod",
    )(cs, ada_w, ada_b.reshape(depth, 1, d6))


def _head_norm(t, gain, lane):
    t2 = t * t
    s_lo = jnp.sum(jnp.where(lane < HEAD_DIM, t2, 0.0), axis=-1, keepdims=True)
    s_all = jnp.sum(t2, axis=-1, keepdims=True)
    ms = jnp.where(lane < HEAD_DIM, s_lo, s_all - s_lo) * (1.0 / HEAD_DIM)
    return t * lax.rsqrt(ms + EPS) * gain


def _rope(t, cos_t, sin_t, lane):
    partner = jnp.where((lane % HEAD_DIM) < HEAD_DIM // 2, pltpu.roll(t, LANES - HEAD_DIM // 2, 1),
                        pltpu.roll(t, HEAD_DIM // 2, 1))
    return t * cos_t + partner * sin_t


def _dup_halves(t, lane):
    sw = pltpu.roll(t, HEAD_DIM, 1)
    return jnp.where(lane < HEAD_DIM, t, sw), jnp.where(lane < HEAD_DIM, sw, t)


def _premix_even_kernel(x_ref, mod_ref, g_ref, w_ref, cs_ref, cos_ref, sin_ref, qg_ref, kg_ref,
                        ab_ref, q_ref, kd_ref, vt_ref, *, rope):
    tm = x_ref.shape[1]
    sub = min(ROW_SUB, tm)
    m = mod_ref[0]
    lane = lax.broadcasted_iota(jnp.int32, (sub, LANES), 1)
    fw = FOURIER_GROUPS * LANES
    qw = q_ref.shape[2]
    csb = cs_ref[...].astype(BF16)
    for r0 in range(0, tm, sub):
        rs = slice(r0, r0 + sub)
        hb = _rms_mod(x_ref[0, rs], g_ref[...], m[0:1], m[1:2]).astype(BF16)
        pf = _dot(hb, w_ref[:, 0:fw])
        for g in range(FOURIER_GROUPS):
            ab = _dot(pf[:, LANES * g:LANES * (g + 1)].astype(BF16), csb)
            ab_ref[0, rs, LANES * g:LANES * (g + 1)] = ab[:, :LANES].astype(BF16)
            ab_ref[0, rs, fw + LANES * g:fw + LANES * (g + 1)] = ab[:, LANES:].astype(BF16)
        pq = _dot(hb, w_ref[:, fw:fw + qw])
        if rope:
            cos_t = cos_ref[rs]
            sin_t = sin_ref[rs]
        for c in range(qw // LANES):
            t = _head_norm(pq[:, LANES * c:LANES * (c + 1)], qg_ref[...], lane)
            if rope:
                t = _rope(t, cos_t, sin_t, lane)
            q_ref[0, rs, LANES * c:LANES * (c + 1)] = t.astype(BF16)
        pk = _dot(hb, w_ref[:, fw + qw:fw + qw + 2 * LANES])
        k = _head_norm(pk[:, :LANES], kg_ref[...], lane)
        if rope:
            k = _rope(k, cos_t, sin_t, lane)
        k0, k1 = _dup_halves(k, lane)
        kd_ref[0, rs, :LANES] = k0.astype(BF16)
        kd_ref[0, rs, LANES:] = k1.astype(BF16)
        vt_ref[0, :, rs] = pk[:, LANES:].T.astype(BF16)


def _premix_even_call(x, mod, gain, w_bf, cs, cos_t, sin_t, qg, kg, *, rope, tm):
    b, n, d = x.shape
    mb = mod.shape[0]
    wtot = w_bf.shape[1]
    fw = FOURIER_GROUPS * LANES
    qw = wtot - fw - 2 * LANES
    mod_map = (lambda i, j: (i, 0, 0)) if mb > 1 else (lambda i, j: (0, 0, 0))
    tok = lambda i, j: (i, j, 0)
    const2 = lambda i, j: (0, 0)
    return pl.pallas_call(
        functools.partial(_premix_even_kernel, rope=rope),
        grid=(b, n // tm),
        in_specs=[
            pl.BlockSpec((1, tm, d), tok),
            pl.BlockSpec((1, 6, d), mod_map),
            pl.BlockSpec((1, d), const2),
            pl.BlockSpec((d, wtot), const2),
            pl.BlockSpec((LANES, 2 * LANES), const2),
            pl.BlockSpec((tm, LANES), lambda i, j: (j, 0)),
            pl.BlockSpec((tm, LANES), lambda i, j: (j, 0)),
            pl.BlockSpec((1, LANES), const2),
            pl.BlockSpec((1, LANES), const2),
        ],
        out_specs=[
            pl.BlockSpec((1, tm, 2 * fw), tok),
            pl.BlockSpec((1, tm, qw), tok),
            pl.BlockSpec((1, tm, 2 * LANES), tok),
            pl.BlockSpec((1, LANES, tm), lambda i, j: (i, 0, j)),
        ],
        out_shape=[
            jax.ShapeDtypeStruct((b, n, 2 * fw), BF16),
            jax.ShapeDtypeStruct((b, n, qw), BF16),
            jax.ShapeDtypeStruct((b, n, 2 * LANES), BF16),
            jax.ShapeDtypeStruct((b, LANES, n), BF16),
        ],
        compiler_params=_cparams(2, VMEM_LIMIT),
        name="premix_even",
    )(x, mod, gain, w_bf, cs, cos_t, sin_t, qg, kg)


def _gated_moe_residual(x_ref, moe_ref, gate_row, r0, sub):
    c8 = x_ref.shape[2] // LANES
    cols = []
    for j in range(c8):
        cs = slice(LANES * j, LANES * (j + 1))
        cols.append(x_ref[0, r0:r0 + sub, cs] + gate_row[:, cs] * moe_ref[0, pl.ds(r0 * c8 + j, sub, stride=c8), :])
    return jnp.concatenate(cols, axis=1)


def _premix_odd_kernel(x_ref, moe_ref, modp_ref, mod_ref, g_ref, w_ref, qg_ref, kg_ref, q_ref, k_ref, v_ref,
                       *xo_ref):
    tm = x_ref.shape[1]
    sub = min(ROW_SUB, tm)
    m = mod_ref[0]
    gate_prev = modp_ref[0][5:6]
    lane = lax.broadcasted_iota(jnp.int32, (sub, LANES), 1)
    wq = q_ref.shape[2]
    chunk = 4 * LANES
    for r0 in range(0, tm, sub):
        rs = slice(r0, r0 + sub)
        x = _gated_moe_residual(x_ref, moe_ref, gate_prev, r0, sub)
        if xo_ref:
            xo_ref[0][0, rs] = x
        hb = _rms_mod(x, g_ref[...], m[0:1], m[1:2]).astype(BF16)
        for c0 in range(0, 3 * wq, chunk):
            p = _dot(hb, w_ref[:, c0:c0 + chunk])
            for cc in range(chunk // LANES):
                col = c0 + cc * LANES
                t = p[:, cc * LANES:(cc + 1) * LANES]
                if col < wq:
                    q_ref[0, rs, col:col + LANES] = _head_norm(t, qg_ref[...], lane).astype(BF16)
                elif col < 2 * wq:
                    k_ref[0, rs, col - wq:col - wq + LANES] = _head_norm(t, kg_ref[...], lane).astype(BF16)
                else:
                    v_ref[0, col - 2 * wq:col - 2 * wq + LANES, rs] = t.T.astype(BF16)


def _premix_odd_call(x1, moe3, mod_prev, mod, gain, w_bf, qg, kg, *, emit_x, tm):
    b, n, d = x1.shape
    mb = mod.shape[0]
    wq = w_bf.shape[1] // 3
    c8 = d // LANES
    mod_map = (lambda i, j: (i, 0, 0)) if mb > 1 else (lambda i, j: (0, 0, 0))
    tok = lambda i, j: (i, j, 0)
    const2 = lambda i, j: (0, 0)
    out_specs = [pl.BlockSpec((1, tm, wq), tok)] * 2 + [pl.BlockSpec((1, wq, tm), lambda i, j: (i, 0, j))]
    out_shape = [jax.ShapeDtypeStruct((b, n, wq), BF16)] * 2 + [jax.ShapeDtypeStruct((b, wq, n), BF16)]
    if emit_x:
        out_specs.append(pl.BlockSpec((1, tm, d), tok))
        out_shape.append(jax.ShapeDtypeStruct((b, n, d), F32))
    return pl.pallas_call(
        _premix_odd_kernel,
        grid=(b, n // tm),
        in_specs=[
            pl.BlockSpec((1, tm, d), tok),
            pl.BlockSpec((1, tm * c8, LANES), tok),
            pl.BlockSpec((1, 6, d), mod_map),
            pl.BlockSpec((1, 6, d), mod_map),
            pl.BlockSpec((1, d), const2),
            pl.BlockSpec((d, 3 * wq), const2),
            pl.BlockSpec((1, LANES), const2),
            pl.BlockSpec((1, LANES), const2),
        ],
        out_specs=out_specs,
        out_shape=out_shape,
        compiler_params=_cparams(2, VMEM_LIMIT),
        name="premix_odd",
    )(x1, moe3, mod_prev, mod, gain, w_bf, qg, kg)


def _col_reduce(x, op, slab=64):
    r = x.shape[0]
    if r > slab and r % slab == 0:
        x = op(x.reshape(r // slab, slab, x.shape[1]), axis=0)
    return op(x, axis=0, keepdims=True)


def _softmax_pv_t(s_t, v_t, sink=None):
    mx = _col_reduce(s_t, jnp.max)
    if sink is not None:
        mx = jnp.maximum(mx, sink)
    p = jnp.exp2((s_t - mx).astype(BF16))
    nd = v_t.shape[0]
    o = _dot(jnp.concatenate([v_t, jnp.ones((ONES_ROWS, v_t.shape[1]), BF16)], axis=0), p)
    den = o[nd:nd + 1]
    if sink is not None:
        den = den + jnp.exp2(sink - mx)
    return o[:nd] / den


def _mask_half(qc, lane, half):
    q32 = qc.astype(F32) * (HEAD_DIM ** -0.5 * LOG2E)
    keep = (lane < HEAD_DIM) if half == 0 else (lane >= HEAD_DIM)
    return jnp.where(keep, q32, 0.0).astype(BF16)


def _gqa_scores(q_ref, h, kd, n_kv, masks, lane, r0, tq):
    rs = slice(r0, r0 + tq)
    cols_per_kv = q_ref.shape[2] // LANES // n_kv
    q_rows = []
    for cc in range(cols_per_kv):
        c = cols_per_kv * h + cc
        for half in range(2):
            q_rows.append(_mask_half(q_ref[0, rs, LANES * c:LANES * (c + 1)], lane, half))
    qm = jnp.concatenate(q_rows, axis=0)
    s = _dot_t(kd, qm)
    pieces, k0 = [], 0
    for rows, mask in masks:
        blk = s[k0:k0 + rows]
        pieces.append(blk if mask is None else jnp.where(mask, blk, NEG))
        k0 += rows
    return jnp.concatenate(pieces, axis=0) if len(pieces) > 1 else pieces[0]


def _gqa_finish(sink_ref, o_ref, h, s_t, v_t, r0, tq):
    rs = slice(r0, r0 + tq)
    n_kv = v_t.shape[0] // HEAD_DIM
    cols_per_kv = o_ref.shape[2] // LANES // n_kv
    heads = 2 * cols_per_kv
    head_of_lane = lax.broadcasted_iota(jnp.int32, (1, heads * tq), 1) // tq
    sink = jnp.zeros((1, heads * tq), F32)
    for a in range(heads):
        sink = jnp.where(head_of_lane == a, sink_ref[heads * h + a] * LOG2E, sink)
    o_t = _softmax_pv_t(s_t, v_t, sink)[HEAD_DIM * h:HEAD_DIM * (h + 1)]
    for cc in range(cols_per_kv):
        c = cols_per_kv * h + cc
        tile = jnp.concatenate([o_t[:, tq * (2 * cc):tq * (2 * cc + 1)], o_t[:, tq * (2 * cc + 1):tq * (2 * cc + 2)]],
                               axis=0)
        o_ref[0, rs, LANES * c:LANES * (c + 1)] = tile.T.astype(BF16)


def _swa_kernel(sink_ref, q_ref, km_ref, k0_ref, kp_ref, vm_ref, v0_ref, vp_ref, kc_ref, vc_ref, *rest, n_step):
    o_ref = _ride_cast(rest)
    g = pl.program_id(1)
    tq = SWA_BLOCK
    nc = kc_ref.shape[1]
    n_kv = kc_ref.shape[2] // LANES
    heads = q_ref.shape[2] // HEAD_DIM // n_kv
    lane = lax.broadcasted_iota(jnp.int32, (tq, LANES), 1)
    jk = lax.broadcasted_iota(jnp.int32, (SWA_BLOCK, heads * tq), 0)
    iq = lax.broadcasted_iota(jnp.int32, (SWA_BLOCK, heads * tq), 1) % tq
    below, above = jk >= iq, jk <= iq
    nq = q_ref.shape[1] // SWA_BLOCK
    blk = lambda s: slice(SWA_BLOCK * s, SWA_BLOCK * (s + 1))

    def key_block(s, hs):
        return km_ref[0, :, hs] if s < 0 else kp_ref[0, :, hs] if s == nq else k0_ref[0, blk(s), hs]

    def value_block(s):
        return vm_ref[0] if s < 0 else vp_ref[0] if s == nq else v0_ref[0, :, blk(s)]

    def scores(sub, h):
        hs = slice(LANES * h, LANES * (h + 1))
        first = jnp.logical_and(below, g > 0) if sub == 0 else below
        last = jnp.logical_and(above, g < n_step - 1) if sub == nq - 1 else above
        masks = [(nc, None), (SWA_BLOCK, first), (SWA_BLOCK, None), (SWA_BLOCK, last)]
        kd = jnp.concatenate([kc_ref[0, :, hs]] + [key_block(s, hs) for s in (sub - 1, sub, sub + 1)], axis=0)
        return _gqa_scores(q_ref, h, kd, n_kv, masks, lane, SWA_BLOCK * sub, tq)

    groups = [(sub, h) for sub in range(nq) for h in range(n_kv)]
    ahead = 3
    pending = [scores(*grp) for grp in groups[:ahead]]
    for i, (sub, h) in enumerate(groups):
        s_t = pending.pop(0)
        if i + ahead < len(groups):
            pending.append(scores(*groups[i + ahead]))
        v_t = jnp.concatenate([vc_ref[0]] + [value_block(s) for s in (sub - 1, sub, sub + 1)], axis=1)
        _gqa_finish(sink_ref, o_ref, h, s_t, v_t, SWA_BLOCK * sub, tq)


def _cast_riders(ws, layer, total, flat):
    l, e, r, c = ws[0].shape
    rows = e * r // total
    ins = [pl.BlockSpec((rows, c), lambda i, j: (layer * total + flat(i, j), 0)) for _ in ws]
    outs = [pl.BlockSpec((len(ws), rows, c), lambda i, j: (0, flat(i, j), 0))]
    shapes = [jax.ShapeDtypeStruct((len(ws), e * r, c), BF16)]
    args = [w.reshape(l * e * r, c) for w in ws]
    return ins, outs, shapes, args


def _ride_cast(rest):
    n_cast = len(rest) - 2
    for k, w_ref in enumerate(rest[:n_cast]):
        rest[-1][k] = w_ref[...].astype(BF16)
    return rest[n_cast]


def _swa_call(sink, q, kd, vt, kdc, vtc, ws, layer):
    b, n, qw = q.shape
    nc = kdc.shape[1]
    kw = kd.shape[2]
    vw = vt.shape[1]
    n_blk = n // SWA_BLOCK
    nq = min(SWA_QBLK, n_blk)
    n_step = n_blk // nq
    prev = lambda j: jnp.maximum(nq * j - 1, 0)
    nxt = lambda j: jnp.minimum(nq * j + nq, n_blk - 1)
    kspec = lambda f: pl.BlockSpec((1, SWA_BLOCK, kw), lambda i, j: (i, f(j), 0))
    vspec = lambda f: pl.BlockSpec((1, vw, SWA_BLOCK), lambda i, j: (i, 0, f(j)))
    c_in, c_out, c_shape, c_args = _cast_riders(ws, layer, b * n_step, lambda i, j: i * n_step + j)
    return pl.pallas_call(
        functools.partial(_swa_kernel, n_step=n_step),
        grid=(b, n_step),
        in_specs=[
            pl.BlockSpec(memory_space=pltpu.SMEM),
            pl.BlockSpec((1, nq * SWA_BLOCK, qw), lambda i, j: (i, j, 0)),
            kspec(prev), pl.BlockSpec((1, nq * SWA_BLOCK, kw), lambda i, j: (i, j, 0)), kspec(nxt),
            vspec(prev), pl.BlockSpec((1, vw, nq * SWA_BLOCK), lambda i, j: (i, 0, j)), vspec(nxt),
            pl.BlockSpec((1, nc, kw), lambda i, j: (i, 0, 0)),
            pl.BlockSpec((1, vw, nc), lambda i, j: (i, 0, 0)),
        ] + c_in,
        out_specs=[pl.BlockSpec((1, nq * SWA_BLOCK, qw), lambda i, j: (i, j, 0))] + c_out,
        out_shape=[jax.ShapeDtypeStruct((b, n, qw), BF16)] + c_shape,
        compiler_params=_cparams(2, VMEM_LIMIT),
        name="swa",
    )(sink, q, kd, kd, kd, vt, vt, vt, kdc, vtc, *c_args)


def _ctx_attn_kernel(sink_ref, q_ref, kc_ref, vc_ref, o_ref):
    tq = q_ref.shape[1]
    lane = lax.broadcasted_iota(jnp.int32, (tq, LANES), 1)
    n_kv = kc_ref.shape[2] // LANES
    scores = [_gqa_scores(q_ref, h, kc_ref[0, :, LANES * h:LANES * (h + 1)], n_kv, [(kc_ref.shape[1], None)], lane,
                          0, tq) for h in range(n_kv)]
    for h, s_t in enumerate(scores):
        _gqa_finish(sink_ref, o_ref, h, s_t, vc_ref[0], 0, tq)


def _ctx_attn_call(sink, q, kd, vt):
    b, nc, qw = q.shape
    kw = kd.shape[2]
    vw = vt.shape[1]
    m = lambda i: (i, 0, 0)
    return pl.pallas_call(
        _ctx_attn_kernel,
        grid=(b,),
        in_specs=[
            pl.BlockSpec(memory_space=pltpu.SMEM),
            pl.BlockSpec((1, nc, qw), m),
            pl.BlockSpec((1, nc, kw), m),
            pl.BlockSpec((1, vw, nc), m),
        ],
        out_specs=pl.BlockSpec((1, nc, qw), m),
        out_shape=jax.ShapeDtypeStruct((b, nc, qw), BF16),
        compiler_params=_cparams(1, VMEM_LIMIT),
        name="ctx_attn",
    )(sink, q, kd, vt)


def _na_kernel(q_ref, k0_ref, k1_ref, k2_ref, k3_ref, v0_ref, v1_ref, v2_ref, v3_ref, kc_ref, vc_ref, tb_ref, *rest,
               n_j, rows):
    o_ref = _ride_cast(rest)
    g = pl.program_id(1)
    tq = NA_QROWS * GRID_W
    tk = NA_KROWS * GRID_W
    nc = kc_ref.shape[1]
    k_blocks = [k0_ref, k1_ref, k2_ref, k3_ref]
    v_blocks = [v0_ref, v1_ref, v2_ref, v3_ref]
    base = jnp.clip(NA_JBLK * g - 1, 0, n_j - (NA_JBLK + 2))
    lane = lax.broadcasted_iota(jnp.int32, (tq, LANES), 1)
    k_row = (lax.broadcasted_iota(jnp.int32, (nc + tk, LANES), 0) - nc) // GRID_W
    k_lane = lax.broadcasted_iota(jnp.int32, (nc + tk, LANES), 1)
    is_local = lax.broadcasted_iota(jnp.int32, (nc + tk, LANES), 0) >= nc
    q_grid_row = (lax.broadcasted_iota(jnp.int32, (2 * tq, LANES), 0) % tq) // GRID_W
    q_extra = jnp.where(lax.broadcasted_iota(jnp.int32, (2 * tq, LANES), 1) == q_grid_row, 1.0, 0.0).astype(BF16)
    n_dr = 2 * NA_ROWS_MAX - 1
    shifted, k_extras, deltas = [], [], []
    for u in range(NA_JBLK):
        j = NA_JBLK * g + u
        w0_blk = jnp.clip(j - 1, 0, n_j - 3)
        w0 = NA_QROWS * w0_blk
        lo = jnp.zeros((nc + tk, LANES), jnp.int32)
        for rr in range(NA_QROWS):
            r = NA_QROWS * j + rr
            lo_rr = jnp.clip(r - NA_ROWS_MAX // 2, 0, rows - NA_ROWS_MAX) - w0
            lo = jnp.where(k_lane == rr, lo_rr, lo)
        outside = jnp.logical_or(k_row < lo, k_row >= lo + NA_ROWS_MAX)
        k_extras.append(jnp.where(jnp.logical_and(jnp.logical_and(is_local, k_lane < NA_QROWS), outside), NEG, 0.0)
                        .astype(BF16))
        shifted.append(w0_blk - base == 1)
        deltas.append(w0 - NA_QROWS * j)

    def scores(u, c):
        cs = slice(LANES * c, LANES * (c + 1))
        qc = q_ref[0, tq * u:tq * (u + 1), cs]
        k_loc = [jnp.where(shifted[u], k_blocks[d + 1][0, :, cs], k_blocks[d][0, :, cs]) for d in range(3)]
        kl = jnp.concatenate([kc_ref[0, :, cs]] + k_loc, axis=0)
        qm = jnp.concatenate([_mask_half(qc, lane, 0), _mask_half(qc, lane, 1)], axis=0)
        return _dot_t(jnp.concatenate([kl, k_extras[u]], axis=1), jnp.concatenate([qm, q_extra], axis=1))

    groups = [(u, c) for c in range(q_ref.shape[2] // LANES) for u in range(NA_JBLK)]
    ahead = 2
    pending = [scores(*grp) for grp in groups[:ahead]]
    for i, (u, c) in enumerate(groups):
        cs = slice(LANES * c, LANES * (c + 1))
        s = pending.pop(0)
        if i + ahead < len(groups):
            pending.append(scores(*groups[i + ahead]))
        v_loc = [jnp.where(shifted[u], v_blocks[d + 1][0, cs, :], v_blocks[d][0, cs, :]) for d in range(3)]
        v_t = jnp.concatenate([vc_ref[0, cs, :]] + v_loc, axis=1)
        bias_rows = []
        for kr in range(NA_KROWS):
            pieces = []
            for half in range(2):
                for t in range(NA_QROWS // 2):
                    dr = deltas[u] + kr - 2 * t + NA_ROWS_MAX - 1
                    pieces.append(tb_ref[2 * c + half, jnp.clip(dr, 0, n_dr)])
            bias_rows.append(jnp.concatenate(pieces, axis=1))
        bias = jnp.concatenate(bias_rows, axis=0)
        s_t = jnp.concatenate([s[:nc], s[nc:] + bias], axis=0)
        o_t = _softmax_pv_t(s_t, v_t)
        tile = jnp.concatenate([o_t[:HEAD_DIM, :tq], o_t[HEAD_DIM:, tq:]], axis=0)
        o_ref[0, tq * u:tq * (u + 1), cs] = tile.T.astype(BF16)


def _na_call(q, k, vt, kc, vtc, tb, ws, layer):
    b, n, w = q.shape
    nc = kc.shape[1]
    rows = n // GRID_W
    n_j = rows // NA_QROWS
    n_g = n_j // NA_JBLK
    tq = NA_QROWS * GRID_W
    cur = lambda i, j: (i, j, 0)
    first = lambda j: jnp.clip(NA_JBLK * j - 1, 0, n_j - (NA_JBLK + 2))
    kspec = lambda d: pl.BlockSpec((1, tq, w), lambda i, j: (i, first(j) + d, 0))
    vspec = lambda d: pl.BlockSpec((1, w, tq), lambda i, j: (i, 0, first(j) + d))
    c_in, c_out, c_shape, c_args = _cast_riders(ws, layer, b * n_g, lambda i, j: i * n_g + j)
    return pl.pallas_call(
        functools.partial(_na_kernel, n_j=n_j, rows=rows),
        grid=(b, n_g),
        in_specs=[
            pl.BlockSpec((1, NA_JBLK * tq, w), cur),
            kspec(0), kspec(1), kspec(2), kspec(3),
            vspec(0), vspec(1), vspec(2), vspec(3),
            pl.BlockSpec((1, nc, w), lambda i, j: (i, 0, 0)),
            pl.BlockSpec((1, w, nc), lambda i, j: (i, 0, 0)),
            pl.BlockSpec(tb.shape, lambda i, j: (0, 0, 0, 0)),
        ] + c_in,
        out_specs=[pl.BlockSpec((1, NA_JBLK * tq, w), cur)] + c_out,
        out_shape=[jax.ShapeDtypeStruct((b, n, w), BF16)] + c_shape,
        compiler_params=_cparams(2, VMEM_LIMIT),
        name="na",
    )(q, k, k, k, k, vt, vt, vt, vt, kc, vtc, tb, *c_args)


def _na_bias_table(rpb):
    col_q = np.arange(GRID_W)[None, :]
    col_k = np.arange(GRID_W)[:, None]
    c_start = np.clip(col_q - NA_COLS // 2, 0, GRID_W - NA_COLS)
    col_valid = (col_k >= c_start) & (col_k < c_start + NA_COLS)
    dc_idx = np.clip(col_k - col_q + NA_COLS - 1, 0, 2 * NA_COLS - 2)
    t = jnp.where(col_valid[None, None], (rpb.astype(F32) * LOG2E)[:, :, dc_idx], NEG)
    pad = jnp.full_like(t[:, :1], NEG)
    t = jnp.concatenate([pad, t, pad], axis=1)
    return jnp.concatenate([t[:, 1:], t[:, :-1]], axis=-1)


def _fourier_kernel(ab_ref, ca_ref, sa_ref, cb_ref, sb_ref, rev_ref, o_ref, fold_ref, rev_scr):
    n = ab_ref.shape[1]
    nh = n // 2
    tm, fw = o_ref.shape[1], o_ref.shape[2]

    @pl.when(pl.program_id(1) == 0)
    def _():
        for m in range(nh // LANES):
            blk = ab_ref[0, n - LANES * (m + 1):n - LANES * m, :]
            rev_scr[LANES * m:LANES * (m + 1), :] = _dot(rev_ref[...], blk)
        ck = min(256, nh)
        row = lax.broadcasted_iota(jnp.int32, (ck, 1), 0)
        for r0 in range(0, nh, ck):
            prev = (r0 - 1) % nh
            shifted = jnp.where(row == 0, rev_scr[prev:prev + 1, :], pltpu.roll(rev_scr[r0:r0 + ck, :], 1, 0))
            lo = ab_ref[0, r0:r0 + ck, :].astype(F32)
            a2 = lo[:, :fw] + shifted[:, :fw]
            b2 = lo[:, fw:] - shifted[:, fw:]
            if r0 == 0:
                a2 = jnp.where(row == 0, lo[:, :fw], a2)
                b2 = jnp.where(row == 0, shifted[:, :fw], b2)
            fold_ref[r0:r0 + ck, :fw] = a2.astype(BF16)
            fold_ref[r0:r0 + ck, fw:] = b2.astype(BF16)

    cb = cb_ref[...]
    sb = sb_ref[...]
    sub = min(ROW_SUB, tm)
    for r0 in range(0, tm, sub):
        cm, sm = [], []
        for a in range(r0 // DFT_SUB, (r0 + sub) // DFT_SUB):
            ca = ca_ref[a:a + 1, :]
            sa = sa_ref[a:a + 1, :]
            cm.append((ca * cb - sa * sb).astype(BF16))
            sm.append((sa * cb + ca * sb).astype(BF16))
        y = _dot(jnp.concatenate(cm, axis=0), fold_ref[:, :fw]) - _dot(jnp.concatenate(sm, axis=0), fold_ref[:, fw:])
        o_ref[0, r0:r0 + sub] = y.astype(BF16)


def _fourier_call(ab, ca, sa, cb, sb, *, tm):
    b, n, w2 = ab.shape
    nh = n // 2
    fw = w2 // 2
    na = tm // DFT_SUB
    rev = jnp.asarray(np.eye(LANES, dtype=np.float32)[::-1], BF16)
    return pl.pallas_call(
        _fourier_kernel,
        grid=(b, n // tm),
        in_specs=[
            pl.BlockSpec((1, n, w2), lambda i, j: (i, 0, 0)),
            pl.BlockSpec((na, nh), lambda i, j: (j, 0)),
            pl.BlockSpec((na, nh), lambda i, j: (j, 0)),
            pl.BlockSpec((DFT_SUB, nh), lambda i, j: (0, 0)),
            pl.BlockSpec((DFT_SUB, nh), lambda i, j: (0, 0)),
            pl.BlockSpec((LANES, LANES), lambda i, j: (0, 0)),
        ],
        out_specs=pl.BlockSpec((1, tm, fw), lambda i, j: (i, j, 0)),
        out_shape=jax.ShapeDtypeStruct((b, n, fw), BF16),
        scratch_shapes=[pltpu.VMEM((nh, w2), BF16), pltpu.VMEM((nh, w2), F32)],
        compiler_params=_cparams(2, VMEM_LIMIT),
        name="fourier",
    )(ab, ca, sa, cb, sb, rev)


def _dft_tables(n):
    nh = n // 2
    k = jnp.arange(nh, dtype=jnp.int32)
    a = jnp.arange(n // DFT_SUB, dtype=jnp.int32)
    bb = jnp.arange(DFT_SUB, dtype=jnp.int32)
    ang_a = ((DFT_SUB * a[:, None] * k[None, :]) % n).astype(F32) * (2.0 * math.pi / n)
    ang_b = ((bb[:, None] * k[None, :]) % n).astype(F32) * (2.0 * math.pi / n)
    s = float(n) ** -0.5
    nyq = -jnp.cos(math.pi * bb.astype(F32)) * s
    sin_b = jnp.concatenate([nyq[:, None], (jnp.sin(ang_b) * s)[:, 1:]], axis=1)
    return jnp.cos(ang_a), jnp.sin(ang_a), jnp.cos(ang_b) * s, sin_b


def _channel_dft():
    c = jnp.arange(LANES, dtype=jnp.int32)
    ang = ((c[:, None] * c[None, :]) % LANES).astype(F32) * (2.0 * math.pi / LANES)
    s = float(LANES) ** -0.5
    return jnp.concatenate([jnp.cos(ang) * s, jnp.sin(ang) * s], axis=1)


def _postmix_kernel(*refs, n_in):
    a_refs = refs[:n_in]
    w_refs = refs[n_in:2 * n_in]
    x_ref, mod_ref, g_ref, rw_ref, x1_ref, h3_ref, aff_ref = refs[2 * n_in:]
    tm, d = x_ref.shape[1], x_ref.shape[2]
    sub = min(ROW_SUB, tm)
    p = d // 2 // LANES
    m = mod_ref[0]
    for r0 in range(0, tm, sub):
        rs = slice(r0, r0 + sub)
        y = None
        for a_ref, w_ref in zip(a_refs, w_refs):
            t = _dot(a_ref[0, rs], w_ref[...])
            y = t if y is None else y + t
        x1 = x_ref[0, rs] + m[2:3] * y
        x1_ref[0, rs] = x1
        h2 = _rms_mod(x1, g_ref[...], m[3:4], m[4:5])
        hh, hl = _split(h2)
        parts = _dot_t(rw_ref[...], jnp.concatenate([hh, hl], axis=1))
        lt = parts[:N_EXPERTS] + parts[N_EXPERTS:2 * N_EXPERTS]
        ex = jnp.exp(lt - jnp.max(lt, axis=0, keepdims=True))
        aff_ref[0, :, rs] = ex / jnp.sum(ex, axis=0, keepdims=True)
        bits = pltpu.bitcast(hh.astype(F32), jnp.uint32)
        packed = (bits[:, d // 2:] & jnp.uint32(0xFFFF0000)) | (bits[:, :d // 2] >> 16)
        for jj in range(p):
            h3_ref[0, pl.ds(r0 * p + jj, sub, stride=p), :] = packed[:, LANES * jj:LANES * (jj + 1)]


def _postmix_call(a_list, w_list, x, mod, gain, rw, *, tm):
    b, n, d = x.shape
    mb = mod.shape[0]
    n_in = len(a_list)
    p = d // 2 // LANES
    mod_map = (lambda i, j: (i, 0, 0)) if mb > 1 else (lambda i, j: (0, 0, 0))
    tok = lambda i, j: (i, j, 0)
    const2 = lambda i, j: (0, 0)
    in_specs = [pl.BlockSpec((1, tm, a.shape[2]), tok) for a in a_list]
    in_specs += [pl.BlockSpec(w.shape, const2) for w in w_list]
    in_specs += [
        pl.BlockSpec((1, tm, d), tok),
        pl.BlockSpec((1, 6, d), mod_map),
        pl.BlockSpec((1, d), const2),
        pl.BlockSpec((LANES, 2 * d), const2),
    ]
    return pl.pallas_call(
        functools.partial(_postmix_kernel, n_in=n_in),
        grid=(b, n // tm),
        in_specs=in_specs,
        out_specs=[
            pl.BlockSpec((1, tm, d), tok),
            pl.BlockSpec((1, tm * p, LANES), tok),
            pl.BlockSpec((1, N_EXPERTS, tm), lambda i, j: (i, 0, j)),
        ],
        out_shape=[
            jax.ShapeDtypeStruct((b, n, d), F32),
            jax.ShapeDtypeStruct((b, n * p, LANES), jnp.uint32),
            jax.ShapeDtypeStruct((b, N_EXPERTS, n), F32),
        ],
        compiler_params=_cparams(2, VMEM_LIMIT),
        name="postmix",
    )(*a_list, *w_list, x, mod, gain, rw)


def _prefix_incl(ones, tri):
    carry = jnp.zeros((ones.shape[0], 1), F32)
    outs = []
    for c in range(ones.shape[1] // LANES):
        blk = ones[:, LANES * c:LANES * (c + 1)]
        outs.append(_dot(blk.astype(BF16), tri) + carry)
        carry = carry + jnp.sum(blk, axis=1, keepdims=True)
    return jnp.concatenate(outs, axis=1)


def _topk_kernel(aff_ref, tri_ref, idx_ref, gate_ref, *, cap):
    g, ne, n = aff_ref.shape
    rows = g * ne
    aff = aff_ref[...].reshape(rows, n)
    bits = pltpu.bitcast(aff, jnp.int32)
    capf = jnp.float32(cap)

    def body(it, lo):
        sh = 28 - 2 * it
        new = lo
        for k in (1, 2, 3):
            cand = lo | jnp.left_shift(jnp.int32(k), sh)
            cnt = jnp.sum(jnp.where(bits >= cand, 1.0, 0.0), axis=1, keepdims=True)
            new = jnp.where(cnt >= capf, cand, new)
        return new

    thr = lax.fori_loop(0, 15, body, jnp.zeros((rows, 1), jnp.int32))
    gt = jnp.where(bits > thr, 1.0, 0.0)
    eq = jnp.where(bits == thr, 1.0, 0.0)
    need = capf - jnp.sum(gt, axis=1, keepdims=True)
    tri = tri_ref[...]
    eq_before = _prefix_incl(eq, tri) - eq
    sel = gt + eq * jnp.where(eq_before < need, 1.0, 0.0)
    slot = _prefix_incl(sel, tri).astype(jnp.int32) - 1
    tok = lax.broadcasted_iota(jnp.int32, (rows, n), 1)
    disp = jnp.where(sel > 0.0, tok - slot, -1)
    gate = aff
    for bit in range((n - 1).bit_length()):
        step = 1 << bit
        disp_in = pltpu.roll(disp, n - step, 1)
        gate_in = pltpu.roll(gate, n - step, 1)
        move_in = jnp.logical_and(disp_in >= 0, ((disp_in >> bit) & 1) == 1)
        stay = jnp.logical_and(disp >= 0, ((disp >> bit) & 1) == 0)
        gate = jnp.where(move_in, gate_in, gate)
        disp = jnp.where(move_in, disp_in, jnp.where(stay, disp, -1))
    idx_ref[...] = (tok + disp)[:, :cap].reshape(g, ne, cap)
    gate_ref[...] = gate[:, :cap].reshape(g, ne, cap)


def _topk_call(aff_t, tri, *, cap):
    b, e, n = aff_t.shape
    g = math.gcd(b, max(1, 4096 // n))
    return pl.pallas_call(
        functools.partial(_topk_kernel, cap=cap),
        grid=(b // g,),
        in_specs=[
            pl.BlockSpec((g, e, n), lambda i: (i, 0, 0)),
            pl.BlockSpec((LANES, LANES), lambda i: (0, 0)),
        ],
        out_specs=[pl.BlockSpec((g, e, cap), lambda i: (i, 0, 0))] * 2,
        out_shape=[jax.ShapeDtypeStruct((b, e, cap), jnp.int32), jax.ShapeDtypeStruct((b, e, cap), F32)],
        compiler_params=_cparams(1, VMEM_LIMIT),
        name="topk",
    )(aff_t, tri)


def _moe_kernel(idx_a_ref, idx_c_ref, gate_c_ref, h3_ref, w_ref, o_ref, xs_0, xs_1, y_0, y_1,
                *, cap, p, c8, ne, n_tiles):
    g = pl.program_id(0)
    valid_a = g < n_tiles
    valid_b = jnp.logical_and(g >= 1, g <= n_tiles)
    valid_c = g >= 2
    steady = jnp.logical_and(g >= 2, g < n_tiles)

    @pl.when(jnp.logical_and(valid_c, (g - 2) % ne == 0))
    def _():
        o_ref[...] = jnp.zeros(o_ref.shape, o_ref.dtype)

    unroll, group = 8, 4

    def gather_row(xs_a, s, dst):
        t = idx_a_ref[0, 0, s]
        xs_a[dst, :] = h3_ref[0, pl.ds(pl.multiple_of(t * p, p), p), :]

    def gather(xs_a, straight):
        if straight:
            for s in range(cap):
                gather_row(xs_a, s, pl.ds(s * p, p))
        else:
            def body(i, carry):
                for u in range(unroll):
                    s = i * unroll + u
                    gather_row(xs_a, s, pl.ds(pl.multiple_of(s * p, p), p))
                return carry
            lax.fori_loop(0, cap // unroll, body, 0)

    def scatter_rows(y_c, s_list, src_list):
        sums = []
        for s, src in zip(s_list, src_list):
            t = idx_c_ref[0, 0, s]
            rows = pl.ds(pl.multiple_of(t * c8, c8), c8)
            sums.append((rows, o_ref[0, rows, :] + y_c[src, :] * gate_c_ref[0, 0, s]))
        for rows, val in sums:
            o_ref[0, rows, :] = val

    def scatter(y_c, straight):
        if straight:
            for g0 in range(0, cap, group):
                ss = [g0 + u for u in range(group)]
                scatter_rows(y_c, ss, [pl.ds(s * c8, c8) for s in ss])
        else:
            def body(i, carry):
                ss = [i * group + u for u in range(group)]
                scatter_rows(y_c, ss, [pl.ds(pl.multiple_of(s * c8, c8), c8) for s in ss])
                return carry
            lax.fori_loop(0, cap // group, body, 0)

    def load_rows(xs_b):
        lo, hi = [], []
        for j in range(p):
            w = xs_b[pl.ds(j, cap, stride=p), :]
            lo.append(pltpu.bitcast(w << 16, F32).astype(BF16))
            hi.append(pltpu.bitcast(w & jnp.uint32(0xFFFF0000), F32).astype(BF16))
        return jnp.concatenate(lo + hi, axis=1)

    def expert(xs, y_b):
        rc = min(MOE_ROW_CHUNK, cap)
        chunks = range(0, cap, rc)
        au = [(_dot(xs[r0:r0 + rc], w_ref[0, 0]), _dot(xs[r0:r0 + rc], w_ref[1, 0])) for r0 in chunks]
        for r0, (a, u_) in zip(chunks, au):
            y = _dot((_silu(a) * u_).astype(BF16), w_ref[2, 0])
            for j in range(c8):
                y_b[pl.ds(r0 * c8 + j, rc, stride=c8), :] = y[:, LANES * j:LANES * (j + 1)]

    def step(xs_a, xs_b, y_b, y_c):
        @pl.when(steady)
        def _():
            xs = load_rows(xs_b)
            gather(xs_a, True)
            scatter(y_c, True)
            expert(xs, y_b)

        @pl.when(jnp.logical_not(steady))
        def _():
            @pl.when(valid_b)
            def _():
                expert(load_rows(xs_b), y_b)

            @pl.when(valid_a)
            def _():
                gather(xs_a, False)

            @pl.when(valid_c)
            def _():
                scatter(y_c, False)

    @pl.when(g % 2 == 0)
    def _():
        step(xs_0, xs_1, y_1, y_0)

    @pl.when(g % 2 == 1)
    def _():
        step(xs_1, xs_0, y_0, y_1)


def _moe_call(idx, h3, gate, w_all):
    bm, ne, cap = idx.shape
    d = w_all.shape[2]
    p = d // 2 // LANES
    c8 = d // LANES
    n = h3.shape[1] // p
    n_tiles = bm * ne
    idx3 = idx.reshape(n_tiles, 1, cap)
    gate3 = gate.reshape(n_tiles, 1, cap)
    tile_a = lambda i: jnp.minimum(i, n_tiles - 1)
    tile_b = lambda i: jnp.clip(i - 1, 0, n_tiles - 1)
    tile_c = lambda i: jnp.clip(i - 2, 0, n_tiles - 1)
    smem = lambda f: pl.BlockSpec((1, 1, cap), lambda i: (f(i), 0, 0), memory_space=pltpu.SMEM)
    return pl.pallas_call(
        functools.partial(_moe_kernel, cap=cap, p=p, c8=c8, ne=ne, n_tiles=n_tiles),
        grid=(n_tiles + 2,),
        in_specs=[
            smem(tile_a), smem(tile_c), smem(tile_c),
            pl.BlockSpec((1, n * p, LANES), lambda i: (tile_a(i) // ne, 0, 0)),
            pl.BlockSpec((w_all.shape[0], 1) + w_all.shape[2:], lambda i: (0, tile_b(i) % ne, 0, 0)),
        ],
        out_specs=pl.BlockSpec((1, n * c8, LANES), lambda i: (tile_c(i) // ne, 0, 0), pipeline_mode=pl.Buffered(1)),
        out_shape=jax.ShapeDtypeStruct((bm, n * c8, LANES), F32),
        scratch_shapes=[pltpu.VMEM((cap * p, LANES), jnp.uint32), pltpu.VMEM((cap * p, LANES), jnp.uint32),
                        pltpu.VMEM((cap * c8, LANES), F32), pltpu.VMEM((cap * c8, LANES), F32)],
        compiler_params=_cparams(1, VMEM_LIMIT),
        name="moe",
    )(idx3, idx3, gate3, h3, w_all)


def _resid_kernel(x_ref, moe_ref, mod_ref, o_ref):
    tm = x_ref.shape[1]
    sub = min(ROW_SUB, tm)
    for r0 in range(0, tm, sub):
        o_ref[0, r0:r0 + sub] = _gated_moe_residual(x_ref, moe_ref, mod_ref[0][5:6], r0, sub)


def _resid_call(x1, moe3, mod, *, tm):
    b, n, d = x1.shape
    mb = mod.shape[0]
    c8 = d // LANES
    mod_map = (lambda i, j: (i, 0, 0)) if mb > 1 else (lambda i, j: (0, 0, 0))
    tok = lambda i, j: (i, j, 0)
    return pl.pallas_call(
        _resid_kernel,
        grid=(b, n // tm),
        in_specs=[
            pl.BlockSpec((1, tm, d), tok),
            pl.BlockSpec((1, tm * c8, LANES), tok),
            pl.BlockSpec((1, 6, d), mod_map),
        ],
        out_specs=pl.BlockSpec((1, tm, d), tok),
        out_shape=jax.ShapeDtypeStruct((b, n, d), F32),
        compiler_params=_cparams(2, VMEM_LIMIT),
        name="resid",
    )(x1, moe3, mod)


def _rope_tables(n):
    t = jnp.arange(n)
    row = (t // GRID_W).astype(F32)
    col = (t % GRID_W).astype(F32)
    n_freq = HEAD_DIM // 4
    inv_freq = jnp.power(ROPE_BASE, -jnp.arange(n_freq, dtype=F32) / n_freq)
    ang = jnp.concatenate([row[:, None] * inv_freq, col[:, None] * inv_freq], axis=-1)
    cos, sin = jnp.cos(ang), jnp.sin(ang)
    cos_t = jnp.concatenate([cos, cos, cos, cos], axis=-1)
    sin_t = jnp.concatenate([-sin, sin, -sin, sin], axis=-1)
    return cos_t, sin_t


def _gain128(g):
    return jnp.concatenate([g, g]).reshape(1, LANES).astype(F32)


def _channel_mix(h_pack, aff_t, tri, w_all, *, flatten):
    b, _, n = aff_t.shape
    cap = EC_CAPACITY_FACTOR * n // N_EXPERTS
    idx, gate = _topk_call(aff_t, tri, cap=cap)
    if flatten:
        idx = (idx + (jnp.arange(b, dtype=jnp.int32) * n)[:, None, None]).transpose(1, 0, 2).reshape(1, N_EXPERTS, b * cap)
        gate = gate.transpose(1, 0, 2).reshape(1, N_EXPERTS, b * cap)
        h_pack = h_pack.reshape(1, -1, LANES)
    return _moe_call(idx, h_pack, gate, w_all).reshape(b, -1, LANES)


def kernel(x, c, ctx, c_ctx, ada_w, ada_b, norm1_g, norm2_g, router_w, exp_w_gate, exp_w_up, exp_w_down, ev_w_in,
           ev_w_out, ev_q_gain, ev_k_gain, ev_sink, od_w_in, od_w_out, od_q_gain, od_k_gain, od_rpb):
    b, n, d = x.shape
    nc = ctx.shape[1]
    tm = min(512, n)
    tmb = min(1024, n)
    tmc = min(512, nc)

    rows = -(-(b + 1) // 8) * 8
    cs = jnp.concatenate([c, c_ctx[None], jnp.zeros((rows - b - 1, d), F32)], axis=0)
    mods = _mod_call(cs, ada_w, ada_b).reshape(ada_w.shape[0], rows, 6, d)

    cos_t, sin_t = _rope_tables(n)
    cs_dft = _channel_dft()
    tri = jnp.asarray(np.triu(np.ones((LANES, LANES), np.float32)), BF16)
    fw = FOURIER_GROUPS * LANES

    def router_split(l):
        rw = router_w[l].T
        rh = rw.astype(BF16)
        rl = (rw - rh.astype(F32)).astype(BF16)
        left = jnp.pad(jnp.concatenate([rh, rl], axis=0), ((0, LANES - 2 * N_EXPERTS), (0, 0)))
        right = jnp.pad(rh, ((0, LANES - N_EXPERTS), (0, 0)))
        return jnp.concatenate([left, right], axis=1)

    w_stacks = (exp_w_gate, exp_w_up, exp_w_down)
    as_layer = lambda w: w.reshape((len(w_stacks),) + exp_w_gate.shape[1:])

    mod_x, mod_c = mods[0, :b], mods[0, b:b + 1]
    g1 = norm1_g[0].reshape(1, d)
    g2 = norm2_g[0].reshape(1, d)
    w_in = ev_w_in[0].astype(BF16)
    w_out = ev_w_out[0].astype(BF16)
    qg, kg = _gain128(ev_q_gain[0]), _gain128(ev_k_gain[0])
    sink = ev_sink[0].astype(F32)
    rw = router_split(0)

    ab_x, q_x, kd_x, vd_x = _premix_even_call(x, mod_x, g1, w_in, cs_dft, cos_t, sin_t, qg, kg, rope=True, tm=tmb)
    ab_c, q_c, kd_c, vd_c = _premix_even_call(ctx, mod_c, g1, w_in, cs_dft, cos_t[:nc], sin_t[:nc], qg, kg,
                                              rope=False, tm=tmc)
    a_x, wts = _swa_call(sink, q_x, kd_x, vd_x, kd_c, vd_c, w_stacks, 0)
    wts = as_layer(wts)
    a_c = _ctx_attn_call(sink, q_c, kd_c, vd_c)
    four_x = _fourier_call(ab_x, *_dft_tables(n), tm=tmb)
    four_c = _fourier_call(ab_c, *_dft_tables(nc), tm=tmc)
    w_out_parts = [w_out[:fw], w_out[fw:]]
    x1, h3_x, aff_x = _postmix_call([four_x, a_x], w_out_parts, x, mod_x, g2, rw, tm=tmb)
    c1, h3_c, aff_c = _postmix_call([four_c, a_c], w_out_parts, ctx, mod_c, g2, rw, tm=tmc)
    moe_x = _channel_mix(h3_x, aff_x, tri, wts, flatten=False)
    moe_c = _channel_mix(h3_c, aff_c, tri, wts, flatten=True)

    mod_x0, mod_c0 = mod_x, mod_c
    mod_x, mod_c = mods[1, :b], mods[1, b:b + 1]
    g1 = norm1_g[1].reshape(1, d)
    g2 = norm2_g[1].reshape(1, d)
    w_in = od_w_in[0].astype(BF16)
    w_out = od_w_out[0].astype(BF16)
    qg, kg = _gain128(od_q_gain[0]), _gain128(od_k_gain[0])
    rw = router_split(1)

    q_x, k_x, v_x, x = _premix_odd_call(x1, moe_x, mod_x0, mod_x, g1, w_in, qg, kg, emit_x=True, tm=tmb)
    _, k_c, v_c = _premix_odd_call(c1, moe_c, mod_c0, mod_c, g1, w_in, qg, kg, emit_x=False, tm=tmc)
    a_x, wts = _na_call(q_x, k_x, v_x, k_c, v_c, _na_bias_table(od_rpb[0]), w_stacks, 1)
    wts = as_layer(wts)
    x1, h3_x, aff_x = _postmix_call([a_x], [w_out], x, mod_x, g2, rw, tm=tmb)
    return _resid_call(x1, _channel_mix(h3_x, aff_x, tri, wts, flatten=False), mod_x, tm=tmb)
```

```python
import functools
import math

import numpy as np
import jax
import jax.numpy as jnp
from jax import lax
from jax.experimental import pallas as pl
from jax.experimental.pallas import tpu as pltpu

GRID_W = 64
HEAD_DIM = 64
FOURIER_GROUPS = 4
SWA_WINDOW = 128
SWA_BLOCK = 128
NA_ROWS_MAX = 8
NA_COLS = 16
N_EXPERTS = 16
EC_CAPACITY_FACTOR = 2
ROPE_BASE = 10000.0
EPS = 1e-6

LANES = 128
NEG = -1e30
NA_QROWS = 4
NA_JBLK = 2
NA_KROWS = 12
DFT_SUB = 64
MOE_ROW_CHUNK = 128
SWA_QBLK = 16
ROW_SUB = 256
ONES_ROWS = 16
LOG2E = math.log2(math.e)
VMEM_LIMIT = 60 * 1024 * 1024

F32 = jnp.float32
BF16 = jnp.bfloat16


def _cparams(n_axes, vmem=None):
    return pltpu.CompilerParams(dimension_semantics=("arbitrary",) * n_axes, vmem_limit_bytes=vmem)


def _dot(a, b):
    return jnp.dot(a, b, preferred_element_type=F32)


def _dot_t(a, b):
    return lax.dot_general(a, b, (((1,), (1,)), ((), ())), preferred_element_type=F32)


def _split(a):
    hi = a.astype(BF16)
    lo = (a - hi.astype(F32)).astype(BF16)
    return hi, lo


def _dot3(a, b):
    ah, al = _split(a)
    bh, bl = _split(b)
    return _dot(ah, bh) + _dot(al, bh) + _dot(ah, bl)


def _silu(a):
    return a / (1.0 + jnp.exp(-a))


def _rms_mod(x, gain, shift, scale):
    ms = jnp.mean(x * x, axis=-1, keepdims=True)
    y = x * lax.rsqrt(ms + EPS) * gain
    return y * (1.0 + scale) + shift


def _mod_kernel(cs_ref, w_ref, b_ref, o_ref):
    o_ref[0] = _dot3(_silu(cs_ref[...]), w_ref[0]) + b_ref[0]


def _mod_call(cs, ada_w, ada_b):
    depth, d, d6 = ada_w.shape
    r = cs.shape[0]
    tn = 1024
    return pl.pallas_call(
        _mod_kernel,
        grid=(depth, d6 // tn),
        in_specs=[
            pl.BlockSpec((r, d), lambda l, j: (0, 0)),
            pl.BlockSpec((1, d, tn), lambda l, j: (l, 0, j)),
            pl.BlockSpec((1, 1, tn), lambda l, j: (l, 0, j)),
        ],
        out_specs=pl.BlockSpec((1, r, tn), lambda l, j: (l, 0, j)),
        out_shape=jax.ShapeDtypeStruct((depth, r, d6), F32),
        compiler_params=_cparams(2),
        name="mod",
    )(cs, ada_w, ada_b.reshape(depth, 1, d6))


def _head_norm(t, gain, lane):
    t2 = t * t
    s_lo = jnp.sum(jnp.where(lane < HEAD_DIM, t2, 0.0), axis=-1, keepdims=True)
    s_all = jnp.sum(t2, axis=-1, keepdims=True)
    ms = jnp.where(lane < HEAD_DIM, s_lo, s_all - s_lo) * (1.0 / HEAD_DIM)
    return t * lax.rsqrt(ms + EPS) * gain


def _rope(t, cos_t, sin_t, lane):
    partner = jnp.where((lane % HEAD_DIM) < HEAD_DIM // 2, pltpu.roll(t, LANES - HEAD_DIM // 2, 1),
                        pltpu.roll(t, HEAD_DIM // 2, 1))
    return t * cos_t + partner * sin_t


def _dup_halves(t, lane):
    sw = pltpu.roll(t, HEAD_DIM, 1)
    return jnp.where(lane < HEAD_DIM, t, sw), jnp.where(lane < HEAD_DIM, sw, t)


def _premix_even_kernel(x_ref, mod_ref, g_ref, w_ref, cs_ref, cos_ref, sin_ref, qg_ref, kg_ref,
                        ab_ref, q_ref, kd_ref, vt_ref, *, rope):
    tm = x_ref.shape[1]
    sub = min(ROW_SUB, tm)
    m = mod_ref[0]
    lane = lax.broadcasted_iota(jnp.int32, (sub, LANES), 1)
    fw = FOURIER_GROUPS * LANES
    qw = q_ref.shape[2]
    csb = cs_ref[...].astype(BF16)
    for r0 in range(0, tm, sub):
        rs = slice(r0, r0 + sub)
        hb = _rms_mod(x_ref[0, rs], g_ref[...], m[0:1], m[1:2]).astype(BF16)
        pf = _dot(hb, w_ref[:, 0:fw])
        for g in range(FOURIER_GROUPS):
            ab = _dot(pf[:, LANES * g:LANES * (g + 1)].astype(BF16), csb)
            ab_ref[0, rs, LANES * g:LANES * (g + 1)] = ab[:, :LANES].astype(BF16)
            ab_ref[0, rs, fw + LANES * g:fw + LANES * (g + 1)] = ab[:, LANES:].astype(BF16)
        pq = _dot(hb, w_ref[:, fw:fw + qw])
        if rope:
            cos_t = cos_ref[rs]
            sin_t = sin_ref[rs]
        for c in range(qw // LANES):
            t = _head_norm(pq[:, LANES * c:LANES * (c + 1)], qg_ref[...], lane)
            if rope:
                t = _rope(t, cos_t, sin_t, lane)
            q_ref[0, rs, LANES * c:LANES * (c + 1)] = t.astype(BF16)
        pk = _dot(hb, w_ref[:, fw + qw:fw + qw + 2 * LANES])
        k = _head_norm(pk[:, :LANES], kg_ref[...], lane)
        if rope:
            k = _rope(k, cos_t, sin_t, lane)
        k0, k1 = _dup_halves(k, lane)
        kd_ref[0, rs, :LANES] = k0.astype(BF16)
        kd_ref[0, rs, LANES:] = k1.astype(BF16)
        vt_ref[0, :, rs] = pk[:, LANES:].T.astype(BF16)


def _premix_even_call(x, mod, gain, w_bf, cs, cos_t, sin_t, qg, kg, *, rope, tm):
    b, n, d = x.shape
    mb = mod.shape[0]
    wtot = w_bf.shape[1]
    fw = FOURIER_GROUPS * LANES
    qw = wtot - fw - 2 * LANES
    mod_map = (lambda i, j: (i, 0, 0)) if mb > 1 else (lambda i, j: (0, 0, 0))
    tok = lambda i, j: (i, j, 0)
    const2 = lambda i, j: (0, 0)
    return pl.pallas_call(
        functools.partial(_premix_even_kernel, rope=rope),
        grid=(b, n // tm),
        in_specs=[
            pl.BlockSpec((1, tm, d), tok),
            pl.BlockSpec((1, 6, d), mod_map),
            pl.BlockSpec((1, d), const2),
            pl.BlockSpec((d, wtot), const2),
            pl.BlockSpec((LANES, 2 * LANES), const2),
            pl.BlockSpec((tm, LANES), lambda i, j: (j, 0)),
            pl.BlockSpec((tm, LANES), lambda i, j: (j, 0)),
            pl.BlockSpec((1, LANES), const2),
            pl.BlockSpec((1, LANES), const2),
        ],
        out_specs=[
            pl.BlockSpec((1, tm, 2 * fw), tok),
            pl.BlockSpec((1, tm, qw), tok),
            pl.BlockSpec((1, tm, 2 * LANES), tok),
            pl.BlockSpec((1, LANES, tm), lambda i, j: (i, 0, j)),
        ],
        out_shape=[
            jax.ShapeDtypeStruct((b, n, 2 * fw), BF16),
            jax.ShapeDtypeStruct((b, n, qw), BF16),
            jax.ShapeDtypeStruct((b, n, 2 * LANES), BF16),
            jax.ShapeDtypeStruct((b, LANES, n), BF16),
        ],
        compiler_params=_cparams(2, VMEM_LIMIT),
        name="premix_even",
    )(x, mod, gain, w_bf, cs, cos_t, sin_t, qg, kg)


def _gated_moe_residual(x_ref, moe_ref, gate_row, r0, sub):
    c8 = x_ref.shape[2] // LANES
    cols = []
    for j in range(c8):
        cs = slice(LANES * j, LANES * (j + 1))
        cols.append(x_ref[0, r0:r0 + sub, cs] + gate_row[:, cs] * moe_ref[0, pl.ds(r0 * c8 + j, sub, stride=c8), :])
    return jnp.concatenate(cols, axis=1)


def _premix_odd_kernel(x_ref, moe_ref, modp_ref, mod_ref, g_ref, w_ref, qg_ref, kg_ref, q_ref, k_ref, v_ref,
                       *xo_ref):
    tm = x_ref.shape[1]
    sub = min(ROW_SUB, tm)
    m = mod_ref[0]
    gate_prev = modp_ref[0][5:6]
    lane = lax.broadcasted_iota(jnp.int32, (sub, LANES), 1)
    wq = q_ref.shape[2]
    chunk = 4 * LANES
    for r0 in range(0, tm, sub):
        rs = slice(r0, r0 + sub)
        x = _gated_moe_residual(x_ref, moe_ref, gate_prev, r0, sub)
        if xo_ref:
            xo_ref[0][0, rs] = x
        hb = _rms_mod(x, g_ref[...], m[0:1], m[1:2]).astype(BF16)
        for c0 in range(0, 3 * wq, chunk):
            p = _dot(hb, w_ref[:, c0:c0 + chunk])
            for cc in range(chunk // LANES):
                col = c0 + cc * LANES
                t = p[:, cc * LANES:(cc + 1) * LANES]
                if col < wq:
                    q_ref[0, rs, col:col + LANES] = _head_norm(t, qg_ref[...], lane).astype(BF16)
                elif col < 2 * wq:
                    k_ref[0, rs, col - wq:col - wq + LANES] = _head_norm(t, kg_ref[...], lane).astype(BF16)
                else:
                    v_ref[0, col - 2 * wq:col - 2 * wq + LANES, rs] = t.T.astype(BF16)


def _premix_odd_call(x1, moe3, mod_prev, mod, gain, w_bf, qg, kg, *, emit_x, tm):
    b, n, d = x1.shape
    mb = mod.shape[0]
    wq = w_bf.shape[1] // 3
    c8 = d // LANES
    mod_map = (lambda i, j: (i, 0, 0)) if mb > 1 else (lambda i, j: (0, 0, 0))
    tok = lambda i, j: (i, j, 0)
    const2 = lambda i, j: (0, 0)
    out_specs = [pl.BlockSpec((1, tm, wq), tok)] * 2 + [pl.BlockSpec((1, wq, tm), lambda i, j: (i, 0, j))]
    out_shape = [jax.ShapeDtypeStruct((b, n, wq), BF16)] * 2 + [jax.ShapeDtypeStruct((b, wq, n), BF16)]
    if emit_x:
        out_specs.append(pl.BlockSpec((1, tm, d), tok))
        out_shape.append(jax.ShapeDtypeStruct((b, n, d), F32))
    return pl.pallas_call(
        _premix_odd_kernel,
        grid=(b, n // tm),
        in_specs=[
            pl.BlockSpec((1, tm, d), tok),
            pl.BlockSpec((1, tm * c8, LANES), tok),
            pl.BlockSpec((1, 6, d), mod_map),
            pl.BlockSpec((1, 6, d), mod_map),
            pl.BlockSpec((1, d), const2),
            pl.BlockSpec((d, 3 * wq), const2),
            pl.BlockSpec((1, LANES), const2),
            pl.BlockSpec((1, LANES), const2),
        ],
        out_specs=out_specs,
        out_shape=out_shape,
        compiler_params=_cparams(2, VMEM_LIMIT),
        name="premix_odd",
    )(x1, moe3, mod_prev, mod, gain, w_bf, qg, kg)


def _col_reduce(x, op, slab=64):
    r = x.shape[0]
    if r > slab and r % slab == 0:
        x = op(x.reshape(r // slab, slab, x.shape[1]), axis=0)
    return op(x, axis=0, keepdims=True)


def _softmax_pv_t(s_t, v_t, sink=None):
    mx = _col_reduce(s_t, jnp.max)
    if sink is not None:
        mx = jnp.maximum(mx, sink)
    p = jnp.exp2((s_t - mx).astype(BF16))
    nd = v_t.shape[0]
    o = _dot(jnp.concatenate([v_t, jnp.ones((ONES_ROWS, v_t.shape[1]), BF16)], axis=0), p)
    den = o[nd:nd + 1]
    if sink is not None:
        den = den + jnp.exp2(sink - mx)
    return o[:nd] / den


def _mask_half(qc, lane, half):
    q32 = qc.astype(F32) * (HEAD_DIM ** -0.5 * LOG2E)
    keep = (lane < HEAD_DIM) if half == 0 else (lane >= HEAD_DIM)
    return jnp.where(keep, q32, 0.0).astype(BF16)


def _gqa_scores(q_ref, h, kd, n_kv, masks, lane, r0, tq):
    rs = slice(r0, r0 + tq)
    cols_per_kv = q_ref.shape[2] // LANES // n_kv
    q_rows = []
    for cc in range(cols_per_kv):
        c = cols_per_kv * h + cc
        for half in range(2):
            q_rows.append(_mask_half(q_ref[0, rs, LANES * c:LANES * (c + 1)], lane, half))
    qm = jnp.concatenate(q_rows, axis=0)
    s = _dot_t(kd, qm)
    pieces, k0 = [], 0
    for rows, mask in masks:
        blk = s[k0:k0 + rows]
        pieces.append(blk if mask is None else jnp.where(mask, blk, NEG))
        k0 += rows
    return jnp.concatenate(pieces, axis=0) if len(pieces) > 1 else pieces[0]


def _gqa_finish(sink_ref, o_ref, h, s_t, v_t, r0, tq):
    rs = slice(r0, r0 + tq)
    n_kv = v_t.shape[0] // HEAD_DIM
    cols_per_kv = o_ref.shape[2] // LANES // n_kv
    heads = 2 * cols_per_kv
    head_of_lane = lax.broadcasted_iota(jnp.int32, (1, heads * tq), 1) // tq
    sink = jnp.zeros((1, heads * tq), F32)
    for a in range(heads):
        sink = jnp.where(head_of_lane == a, sink_ref[heads * h + a] * LOG2E, sink)
    o_t = _softmax_pv_t(s_t, v_t, sink)[HEAD_DIM * h:HEAD_DIM * (h + 1)]
    for cc in range(cols_per_kv):
        c = cols_per_kv * h + cc
        tile = jnp.concatenate([o_t[:, tq * (2 * cc):tq * (2 * cc + 1)], o_t[:, tq * (2 * cc + 1):tq * (2 * cc + 2)]],
                               axis=0)
        o_ref[0, rs, LANES * c:LANES * (c + 1)] = tile.T.astype(BF16)


def _swa_kernel(sink_ref, q_ref, km_ref, k0_ref, kp_ref, vm_ref, v0_ref, vp_ref, kc_ref, vc_ref, *rest, n_step):
    o_ref = _ride_cast(rest)
    g = pl.program_id(1)
    tq = SWA_BLOCK
    nc = kc_ref.shape[1]
    n_kv = kc_ref.shape[2] // LANES
    heads = q_ref.shape[2] // HEAD_DIM // n_kv
    lane = lax.broadcasted_iota(jnp.int32, (tq, LANES), 1)
    jk = lax.broadcasted_iota(jnp.int32, (SWA_BLOCK, heads * tq), 0)
    iq = lax.broadcasted_iota(jnp.int32, (SWA_BLOCK, heads * tq), 1) % tq
    below, above = jk >= iq, jk <= iq
    nq = q_ref.shape[1] // SWA_BLOCK
    blk = lambda s: slice(SWA_BLOCK * s, SWA_BLOCK * (s + 1))

    def key_block(s, hs):
        return km_ref[0, :, hs] if s < 0 else kp_ref[0, :, hs] if s == nq else k0_ref[0, blk(s), hs]

    def value_block(s):
        return vm_ref[0] if s < 0 else vp_ref[0] if s == nq else v0_ref[0, :, blk(s)]

    def scores(sub, h):
        hs = slice(LANES * h, LANES * (h + 1))
        first = jnp.logical_and(below, g > 0) if sub == 0 else below
        last = jnp.logical_and(above, g < n_step - 1) if sub == nq - 1 else above
        masks = [(nc, None), (SWA_BLOCK, first), (SWA_BLOCK, None), (SWA_BLOCK, last)]
        kd = jnp.concatenate([kc_ref[0, :, hs]] + [key_block(s, hs) for s in (sub - 1, sub, sub + 1)], axis=0)
        return _gqa_scores(q_ref, h, kd, n_kv, masks, lane, SWA_BLOCK * sub, tq)

    groups = [(sub, h) for sub in range(nq) for h in range(n_kv)]
    ahead = 3
    pending = [scores(*grp) for grp in groups[:ahead]]
    for i, (sub, h) in enumerate(groups):
        s_t = pending.pop(0)
        if i + ahead < len(groups):
            pending.append(scores(*groups[i + ahead]))
        v_t = jnp.concatenate([vc_ref[0]] + [value_block(s) for s in (sub - 1, sub, sub + 1)], axis=1)
        _gqa_finish(sink_ref, o_ref, h, s_t, v_t, SWA_BLOCK * sub, tq)


def _cast_riders(ws, layer, total, flat):
    ins, outs, shapes, args = [], [], [], []
    for w in ws:
        l, e, r, c = w.shape
        rows = e * r // total
        ins.append(pl.BlockSpec((rows, c), lambda i, j: (layer * total + flat(i, j), 0)))
        outs.append(pl.BlockSpec((rows, c), lambda i, j: (flat(i, j), 0)))
        shapes.append(jax.ShapeDtypeStruct((e * r, c), BF16))
        args.append(w.reshape(l * e * r, c))
    return ins, outs, shapes, args


def _ride_cast(rest):
    n_cast = (len(rest) - 1) // 2
    for w_ref, c_ref in zip(rest[:n_cast], rest[n_cast + 1:]):
        c_ref[...] = w_ref[...].astype(BF16)
    return rest[n_cast]


def _swa_call(sink, q, kd, vt, kdc, vtc, ws, layer):
    b, n, qw = q.shape
    nc = kdc.shape[1]
    kw = kd.shape[2]
    vw = vt.shape[1]
    n_blk = n // SWA_BLOCK
    nq = min(SWA_QBLK, n_blk)
    n_step = n_blk // nq
    prev = lambda j: jnp.maximum(nq * j - 1, 0)
    nxt = lambda j: jnp.minimum(nq * j + nq, n_blk - 1)
    kspec = lambda f: pl.BlockSpec((1, SWA_BLOCK, kw), lambda i, j: (i, f(j), 0))
    vspec = lambda f: pl.BlockSpec((1, vw, SWA_BLOCK), lambda i, j: (i, 0, f(j)))
    c_in, c_out, c_shape, c_args = _cast_riders(ws, layer, b * n_step, lambda i, j: i * n_step + j)
    return pl.pallas_call(
        functools.partial(_swa_kernel, n_step=n_step),
        grid=(b, n_step),
        in_specs=[
            pl.BlockSpec(memory_space=pltpu.SMEM),
            pl.BlockSpec((1, nq * SWA_BLOCK, qw), lambda i, j: (i, j, 0)),
            kspec(prev), pl.BlockSpec((1, nq * SWA_BLOCK, kw), lambda i, j: (i, j, 0)), kspec(nxt),
            vspec(prev), pl.BlockSpec((1, vw, nq * SWA_BLOCK), lambda i, j: (i, 0, j)), vspec(nxt),
            pl.BlockSpec((1, nc, kw), lambda i, j: (i, 0, 0)),
            pl.BlockSpec((1, vw, nc), lambda i, j: (i, 0, 0)),
        ] + c_in,
        out_specs=[pl.BlockSpec((1, nq * SWA_BLOCK, qw), lambda i, j: (i, j, 0))] + c_out,
        out_shape=[jax.ShapeDtypeStruct((b, n, qw), BF16)] + c_shape,
        compiler_params=_cparams(2, VMEM_LIMIT),
        name="swa",
    )(sink, q, kd, kd, kd, vt, vt, vt, kdc, vtc, *c_args)


def _ctx_attn_kernel(sink_ref, q_ref, kc_ref, vc_ref, o_ref):
    tq = q_ref.shape[1]
    lane = lax.broadcasted_iota(jnp.int32, (tq, LANES), 1)
    n_kv = kc_ref.shape[2] // LANES
    scores = [_gqa_scores(q_ref, h, kc_ref[0, :, LANES * h:LANES * (h + 1)], n_kv, [(kc_ref.shape[1], None)], lane,
                          0, tq) for h in range(n_kv)]
    for h, s_t in enumerate(scores):
        _gqa_finish(sink_ref, o_ref, h, s_t, vc_ref[0], 0, tq)


def _ctx_attn_call(sink, q, kd, vt):
    b, nc, qw = q.shape
    kw = kd.shape[2]
    vw = vt.shape[1]
    m = lambda i: (i, 0, 0)
    return pl.pallas_call(
        _ctx_attn_kernel,
        grid=(b,),
        in_specs=[
            pl.BlockSpec(memory_space=pltpu.SMEM),
            pl.BlockSpec((1, nc, qw), m),
            pl.BlockSpec((1, nc, kw), m),
            pl.BlockSpec((1, vw, nc), m),
        ],
        out_specs=pl.BlockSpec((1, nc, qw), m),
        out_shape=jax.ShapeDtypeStruct((b, nc, qw), BF16),
        compiler_params=_cparams(1, VMEM_LIMIT),
        name="ctx_attn",
    )(sink, q, kd, vt)


def _na_kernel(q_ref, k0_ref, k1_ref, k2_ref, k3_ref, v0_ref, v1_ref, v2_ref, v3_ref, kc_ref, vc_ref, tb_ref, *rest,
               n_j, rows):
    o_ref = _ride_cast(rest)
    g = pl.program_id(1)
    tq = NA_QROWS * GRID_W
    tk = NA_KROWS * GRID_W
    nc = kc_ref.shape[1]
    k_blocks = [k0_ref, k1_ref, k2_ref, k3_ref]
    v_blocks = [v0_ref, v1_ref, v2_ref, v3_ref]
    base = jnp.clip(NA_JBLK * g - 1, 0, n_j - (NA_JBLK + 2))
    lane = lax.broadcasted_iota(jnp.int32, (tq, LANES), 1)
    k_row = (lax.broadcasted_iota(jnp.int32, (nc + tk, LANES), 0) - nc) // GRID_W
    k_lane = lax.broadcasted_iota(jnp.int32, (nc + tk, LANES), 1)
    is_local = lax.broadcasted_iota(jnp.int32, (nc + tk, LANES), 0) >= nc
    q_grid_row = (lax.broadcasted_iota(jnp.int32, (2 * tq, LANES), 0) % tq) // GRID_W
    q_extra = jnp.where(lax.broadcasted_iota(jnp.int32, (2 * tq, LANES), 1) == q_grid_row, 1.0, 0.0).astype(BF16)
    n_dr = 2 * NA_ROWS_MAX - 1
    shifted, k_extras, deltas = [], [], []
    for u in range(NA_JBLK):
        j = NA_JBLK * g + u
        w0_blk = jnp.clip(j - 1, 0, n_j - 3)
        w0 = NA_QROWS * w0_blk
        lo = jnp.zeros((nc + tk, LANES), jnp.int32)
        for rr in range(NA_QROWS):
            r = NA_QROWS * j + rr
            lo_rr = jnp.clip(r - NA_ROWS_MAX // 2, 0, rows - NA_ROWS_MAX) - w0
            lo = jnp.where(k_lane == rr, lo_rr, lo)
        outside = jnp.logical_or(k_row < lo, k_row >= lo + NA_ROWS_MAX)
        k_extras.append(jnp.where(jnp.logical_and(jnp.logical_and(is_local, k_lane < NA_QROWS), outside), NEG, 0.0)
                        .astype(BF16))
        shifted.append(w0_blk - base == 1)
        deltas.append(w0 - NA_QROWS * j)

    def scores(u, c):
        cs = slice(LANES * c, LANES * (c + 1))
        qc = q_ref[0, tq * u:tq * (u + 1), cs]
        k_loc = [jnp.where(shifted[u], k_blocks[d + 1][0, :, cs], k_blocks[d][0, :, cs]) for d in range(3)]
        kl = jnp.concatenate([kc_ref[0, :, cs]] + k_loc, axis=0)
        qm = jnp.concatenate([_mask_half(qc, lane, 0), _mask_half(qc, lane, 1)], axis=0)
        return _dot_t(jnp.concatenate([kl, k_extras[u]], axis=1), jnp.concatenate([qm, q_extra], axis=1))

    groups = [(u, c) for c in range(q_ref.shape[2] // LANES) for u in range(NA_JBLK)]
    ahead = 2
    pending = [scores(*grp) for grp in groups[:ahead]]
    for i, (u, c) in enumerate(groups):
        cs = slice(LANES * c, LANES * (c + 1))
        s = pending.pop(0)
        if i + ahead < len(groups):
            pending.append(scores(*groups[i + ahead]))
        v_loc = [jnp.where(shifted[u], v_blocks[d + 1][0, cs, :], v_blocks[d][0, cs, :]) for d in range(3)]
        v_t = jnp.concatenate([vc_ref[0, cs, :]] + v_loc, axis=1)
        bias_rows = []
        for kr in range(NA_KROWS):
            pieces = []
            for half in range(2):
                for t in range(NA_QROWS // 2):
                    dr = deltas[u] + kr - 2 * t + NA_ROWS_MAX - 1
                    pieces.append(tb_ref[2 * c + half, jnp.clip(dr, 0, n_dr)])
            bias_rows.append(jnp.concatenate(pieces, axis=1))
        bias = jnp.concatenate(bias_rows, axis=0)
        s_t = jnp.concatenate([s[:nc], s[nc:] + bias], axis=0)
        o_t = _softmax_pv_t(s_t, v_t)
        tile = jnp.concatenate([o_t[:HEAD_DIM, :tq], o_t[HEAD_DIM:, tq:]], axis=0)
        o_ref[0, tq * u:tq * (u + 1), cs] = tile.T.astype(BF16)


def _na_call(q, k, vt, kc, vtc, tb, ws, layer):
    b, n, w = q.shape
    nc = kc.shape[1]
    rows = n // GRID_W
    n_j = rows // NA_QROWS
    n_g = n_j // NA_JBLK
    tq = NA_QROWS * GRID_W
    cur = lambda i, j: (i, j, 0)
    first = lambda j: jnp.clip(NA_JBLK * j - 1, 0, n_j - (NA_JBLK + 2))
    kspec = lambda d: pl.BlockSpec((1, tq, w), lambda i, j: (i, first(j) + d, 0))
    vspec = lambda d: pl.BlockSpec((1, w, tq), lambda i, j: (i, 0, first(j) + d))
    c_in, c_out, c_shape, c_args = _cast_riders(ws, layer, b * n_g, lambda i, j: i * n_g + j)
    return pl.pallas_call(
        functools.partial(_na_kernel, n_j=n_j, rows=rows),
        grid=(b, n_g),
        in_specs=[
            pl.BlockSpec((1, NA_JBLK * tq, w), cur),
            kspec(0), kspec(1), kspec(2), kspec(3),
            vspec(0), vspec(1), vspec(2), vspec(3),
            pl.BlockSpec((1, nc, w), lambda i, j: (i, 0, 0)),
            pl.BlockSpec((1, w, nc), lambda i, j: (i, 0, 0)),
            pl.BlockSpec(tb.shape, lambda i, j: (0, 0, 0, 0)),
        ] + c_in,
        out_specs=[pl.BlockSpec((1, NA_JBLK * tq, w), cur)] + c_out,
        out_shape=[jax.ShapeDtypeStruct((b, n, w), BF16)] + c_shape,
        compiler_params=_cparams(2, VMEM_LIMIT),
        name="na",
    )(q, k, k, k, k, vt, vt, vt, vt, kc, vtc, tb, *c_args)


def _na_bias_table(rpb):
    col_q = np.arange(GRID_W)[None, :]
    col_k = np.arange(GRID_W)[:, None]
    c_start = np.clip(col_q - NA_COLS // 2, 0, GRID_W - NA_COLS)
    col_valid = (col_k >= c_start) & (col_k < c_start + NA_COLS)
    dc_idx = np.clip(col_k - col_q + NA_COLS - 1, 0, 2 * NA_COLS - 2)
    t = jnp.where(col_valid[None, None], (rpb.astype(F32) * LOG2E)[:, :, dc_idx], NEG)
    pad = jnp.full_like(t[:, :1], NEG)
    t = jnp.concatenate([pad, t, pad], axis=1)
    return jnp.concatenate([t[:, 1:], t[:, :-1]], axis=-1)


def _fourier_kernel(ab_ref, ca_ref, sa_ref, cb_ref, sb_ref, rev_ref, o_ref, fold_ref, rev_scr):
    n = ab_ref.shape[1]
    nh = n // 2
    tm, fw = o_ref.shape[1], o_ref.shape[2]

    @pl.when(pl.program_id(1) == 0)
    def _():
        for m in range(nh // LANES):
            blk = ab_ref[0, n - LANES * (m + 1):n - LANES * m, :]
            rev_scr[LANES * m:LANES * (m + 1), :] = _dot(rev_ref[...], blk)
        ck = min(256, nh)
        row = lax.broadcasted_iota(jnp.int32, (ck, 1), 0)
        for r0 in range(0, nh, ck):
            prev = (r0 - 1) % nh
            shifted = jnp.where(row == 0, rev_scr[prev:prev + 1, :], pltpu.roll(rev_scr[r0:r0 + ck, :], 1, 0))
            lo = ab_ref[0, r0:r0 + ck, :].astype(F32)
            a2 = lo[:, :fw] + shifted[:, :fw]
            b2 = lo[:, fw:] - shifted[:, fw:]
            if r0 == 0:
                a2 = jnp.where(row == 0, lo[:, :fw], a2)
                b2 = jnp.where(row == 0, shifted[:, :fw], b2)
            fold_ref[r0:r0 + ck, :fw] = a2.astype(BF16)
            fold_ref[r0:r0 + ck, fw:] = b2.astype(BF16)

    cb = cb_ref[...]
    sb = sb_ref[...]
    sub = min(ROW_SUB, tm)
    for r0 in range(0, tm, sub):
        cm, sm = [], []
        for a in range(r0 // DFT_SUB, (r0 + sub) // DFT_SUB):
            ca = ca_ref[a:a + 1, :]
            sa = sa_ref[a:a + 1, :]
            cm.append((ca * cb - sa * sb).astype(BF16))
            sm.append((sa * cb + ca * sb).astype(BF16))
        y = _dot(jnp.concatenate(cm, axis=0), fold_ref[:, :fw]) - _dot(jnp.concatenate(sm, axis=0), fold_ref[:, fw:])
        o_ref[0, r0:r0 + sub] = y.astype(BF16)


def _fourier_call(ab, ca, sa, cb, sb, *, tm):
    b, n, w2 = ab.shape
    nh = n // 2
    fw = w2 // 2
    na = tm // DFT_SUB
    rev = jnp.asarray(np.eye(LANES, dtype=np.float32)[::-1], BF16)
    return pl.pallas_call(
        _fourier_kernel,
        grid=(b, n // tm),
        in_specs=[
            pl.BlockSpec((1, n, w2), lambda i, j: (i, 0, 0)),
            pl.BlockSpec((na, nh), lambda i, j: (j, 0)),
            pl.BlockSpec((na, nh), lambda i, j: (j, 0)),
            pl.BlockSpec((DFT_SUB, nh), lambda i, j: (0, 0)),
            pl.BlockSpec((DFT_SUB, nh), lambda i, j: (0, 0)),
            pl.BlockSpec((LANES, LANES), lambda i, j: (0, 0)),
        ],
        out_specs=pl.BlockSpec((1, tm, fw), lambda i, j: (i, j, 0)),
        out_shape=jax.ShapeDtypeStruct((b, n, fw), BF16),
        scratch_shapes=[pltpu.VMEM((nh, w2), BF16), pltpu.VMEM((nh, w2), F32)],
        compiler_params=_cparams(2, VMEM_LIMIT),
        name="fourier",
    )(ab, ca, sa, cb, sb, rev)


def _dft_tables(n):
    nh = n // 2
    k = jnp.arange(nh, dtype=jnp.int32)
    a = jnp.arange(n // DFT_SUB, dtype=jnp.int32)
    bb = jnp.arange(DFT_SUB, dtype=jnp.int32)
    ang_a = ((DFT_SUB * a[:, None] * k[None, :]) % n).astype(F32) * (2.0 * math.pi / n)
    ang_b = ((bb[:, None] * k[None, :]) % n).astype(F32) * (2.0 * math.pi / n)
    s = float(n) ** -0.5
    nyq = -jnp.cos(math.pi * bb.astype(F32)) * s
    sin_b = jnp.concatenate([nyq[:, None], (jnp.sin(ang_b) * s)[:, 1:]], axis=1)
    return jnp.cos(ang_a), jnp.sin(ang_a), jnp.cos(ang_b) * s, sin_b


def _channel_dft():
    c = jnp.arange(LANES, dtype=jnp.int32)
    ang = ((c[:, None] * c[None, :]) % LANES).astype(F32) * (2.0 * math.pi / LANES)
    s = float(LANES) ** -0.5
    return jnp.concatenate([jnp.cos(ang) * s, jnp.sin(ang) * s], axis=1)


def _postmix_kernel(*refs, n_in):
    a_refs = refs[:n_in]
    w_refs = refs[n_in:2 * n_in]
    x_ref, mod_ref, g_ref, rw_ref, x1_ref, h3_ref, aff_ref = refs[2 * n_in:]
    tm, d = x_ref.shape[1], x_ref.shape[2]
    sub = min(ROW_SUB, tm)
    p = d // 2 // LANES
    m = mod_ref[0]
    for r0 in range(0, tm, sub):
        rs = slice(r0, r0 + sub)
        y = None
        for a_ref, w_ref in zip(a_refs, w_refs):
            t = _dot(a_ref[0, rs], w_ref[...])
            y = t if y is None else y + t
        x1 = x_ref[0, rs] + m[2:3] * y
        x1_ref[0, rs] = x1
        h2 = _rms_mod(x1, g_ref[...], m[3:4], m[4:5])
        hh, hl = _split(h2)
        parts = _dot_t(rw_ref[...], jnp.concatenate([hh, hl], axis=1))
        lt = parts[:N_EXPERTS] + parts[N_EXPERTS:2 * N_EXPERTS]
        ex = jnp.exp(lt - jnp.max(lt, axis=0, keepdims=True))
        aff_ref[0, :, rs] = ex / jnp.sum(ex, axis=0, keepdims=True)
        bits = pltpu.bitcast(hh.astype(F32), jnp.uint32)
        packed = (bits[:, d // 2:] & jnp.uint32(0xFFFF0000)) | (bits[:, :d // 2] >> 16)
        for jj in range(p):
            h3_ref[0, pl.ds(r0 * p + jj, sub, stride=p), :] = packed[:, LANES * jj:LANES * (jj + 1)]


def _postmix_call(a_list, w_list, x, mod, gain, rw, *, tm):
    b, n, d = x.shape
    mb = mod.shape[0]
    n_in = len(a_list)
    p = d // 2 // LANES
    mod_map = (lambda i, j: (i, 0, 0)) if mb > 1 else (lambda i, j: (0, 0, 0))
    tok = lambda i, j: (i, j, 0)
    const2 = lambda i, j: (0, 0)
    in_specs = [pl.BlockSpec((1, tm, a.shape[2]), tok) for a in a_list]
    in_specs += [pl.BlockSpec(w.shape, const2) for w in w_list]
    in_specs += [
        pl.BlockSpec((1, tm, d), tok),
        pl.BlockSpec((1, 6, d), mod_map),
        pl.BlockSpec((1, d), const2),
        pl.BlockSpec((LANES, 2 * d), const2),
    ]
    return pl.pallas_call(
        functools.partial(_postmix_kernel, n_in=n_in),
        grid=(b, n // tm),
        in_specs=in_specs,
        out_specs=[
            pl.BlockSpec((1, tm, d), tok),
            pl.BlockSpec((1, tm * p, LANES), tok),
            pl.BlockSpec((1, N_EXPERTS, tm), lambda i, j: (i, 0, j)),
        ],
        out_shape=[
            jax.ShapeDtypeStruct((b, n, d), F32),
            jax.ShapeDtypeStruct((b, n * p, LANES), jnp.uint32),
            jax.ShapeDtypeStruct((b, N_EXPERTS, n), F32),
        ],
        compiler_params=_cparams(2, VMEM_LIMIT),
        name="postmix",
    )(*a_list, *w_list, x, mod, gain, rw)


def _prefix_incl(ones, tri):
    carry = jnp.zeros((ones.shape[0], 1), F32)
    outs = []
    for c in range(ones.shape[1] // LANES):
        blk = ones[:, LANES * c:LANES * (c + 1)]
        outs.append(_dot(blk.astype(BF16), tri) + carry)
        carry = carry + jnp.sum(blk, axis=1, keepdims=True)
    return jnp.concatenate(outs, axis=1)


def _topk_kernel(aff_ref, tri_ref, idx_ref, gate_ref, *, cap):
    g, ne, n = aff_ref.shape
    rows = g * ne
    aff = aff_ref[...].reshape(rows, n)
    bits = pltpu.bitcast(aff, jnp.int32)
    capf = jnp.float32(cap)

    def body(it, lo):
        sh = 28 - 2 * it
        new = lo
        for k in (1, 2, 3):
            cand = lo | jnp.left_shift(jnp.int32(k), sh)
            cnt = jnp.sum(jnp.where(bits >= cand, 1.0, 0.0), axis=1, keepdims=True)
            new = jnp.where(cnt >= capf, cand, new)
        return new

    thr = lax.fori_loop(0, 15, body, jnp.zeros((rows, 1), jnp.int32))
    gt = jnp.where(bits > thr, 1.0, 0.0)
    eq = jnp.where(bits == thr, 1.0, 0.0)
    need = capf - jnp.sum(gt, axis=1, keepdims=True)
    tri = tri_ref[...]
    eq_before = _prefix_incl(eq, tri) - eq
    sel = gt + eq * jnp.where(eq_before < need, 1.0, 0.0)
    slot = _prefix_incl(sel, tri).astype(jnp.int32) - 1
    tok = lax.broadcasted_iota(jnp.int32, (rows, n), 1)
    disp = jnp.where(sel > 0.0, tok - slot, -1)
    gate = aff
    for bit in range((n - 1).bit_length()):
        step = 1 << bit
        disp_in = pltpu.roll(disp, n - step, 1)
        gate_in = pltpu.roll(gate, n - step, 1)
        move_in = jnp.logical_and(disp_in >= 0, ((disp_in >> bit) & 1) == 1)
        stay = jnp.logical_and(disp >= 0, ((disp >> bit) & 1) == 0)
        gate = jnp.where(move_in, gate_in, gate)
        disp = jnp.where(move_in, disp_in, jnp.where(stay, disp, -1))
    idx_ref[...] = (tok + disp)[:, :cap].reshape(g, ne, cap)
    gate_ref[...] = gate[:, :cap].reshape(g, ne, cap)


def _topk_call(aff_t, tri, *, cap):
    b, e, n = aff_t.shape
    g = math.gcd(b, max(1, 4096 // n))
    return pl.pallas_call(
        functools.partial(_topk_kernel, cap=cap),
        grid=(b // g,),
        in_specs=[
            pl.BlockSpec((g, e, n), lambda i: (i, 0, 0)),
            pl.BlockSpec((LANES, LANES), lambda i: (0, 0)),
        ],
        out_specs=[pl.BlockSpec((g, e, cap), lambda i: (i, 0, 0))] * 2,
        out_shape=[jax.ShapeDtypeStruct((b, e, cap), jnp.int32), jax.ShapeDtypeStruct((b, e, cap), F32)],
        compiler_params=_cparams(1, VMEM_LIMIT),
        name="topk",
    )(aff_t, tri)


def _moe_kernel(idx_a_ref, idx_c_ref, gate_c_ref, h3_ref, wg_ref, wu_ref, wd_ref, o_ref, xs_0, xs_1, y_0, y_1,
                *, cap, p, c8, ne, n_tiles):
    g = pl.program_id(0)
    valid_a = g < n_tiles
    valid_b = jnp.logical_and(g >= 1, g <= n_tiles)
    valid_c = g >= 2
    steady = jnp.logical_and(g >= 2, g < n_tiles)

    @pl.when(jnp.logical_and(valid_c, (g - 2) % ne == 0))
    def _():
        o_ref[...] = jnp.zeros(o_ref.shape, o_ref.dtype)

    unroll, group = 8, 4

    def gather_row(xs_a, s, dst):
        t = idx_a_ref[0, 0, s]
        xs_a[dst, :] = h3_ref[0, pl.ds(pl.multiple_of(t * p, p), p), :]

    def gather(xs_a, straight):
        if straight:
            for s in range(cap):
                gather_row(xs_a, s, pl.ds(s * p, p))
        else:
            def body(i, carry):
                for u in range(unroll):
                    s = i * unroll + u
                    gather_row(xs_a, s, pl.ds(pl.multiple_of(s * p, p), p))
                return carry
            lax.fori_loop(0, cap // unroll, body, 0)

    def scatter_rows(y_c, s_list, src_list):
        sums = []
        for s, src in zip(s_list, src_list):
            t = idx_c_ref[0, 0, s]
            rows = pl.ds(pl.multiple_of(t * c8, c8), c8)
            sums.append((rows, o_ref[0, rows, :] + y_c[src, :] * gate_c_ref[0, 0, s]))
        for rows, val in sums:
            o_ref[0, rows, :] = val

    def scatter(y_c, straight):
        if straight:
            for g0 in range(0, cap, group):
                ss = [g0 + u for u in range(group)]
                scatter_rows(y_c, ss, [pl.ds(s * c8, c8) for s in ss])
        else:
            def body(i, carry):
                ss = [i * group + u for u in range(group)]
                scatter_rows(y_c, ss, [pl.ds(pl.multiple_of(s * c8, c8), c8) for s in ss])
                return carry
            lax.fori_loop(0, cap // group, body, 0)

    def load_rows(xs_b):
        lo, hi = [], []
        for j in range(p):
            w = xs_b[pl.ds(j, cap, stride=p), :]
            lo.append(pltpu.bitcast(w << 16, F32).astype(BF16))
            hi.append(pltpu.bitcast(w & jnp.uint32(0xFFFF0000), F32).astype(BF16))
        return jnp.concatenate(lo + hi, axis=1)

    def expert(xs, y_b):
        rc = min(MOE_ROW_CHUNK, cap)
        chunks = range(0, cap, rc)
        au = [(_dot(xs[r0:r0 + rc], wg_ref[0, 0]), _dot(xs[r0:r0 + rc], wu_ref[0, 0])) for r0 in chunks]
        for r0, (a, u_) in zip(chunks, au):
            y = _dot((_silu(a) * u_).astype(BF16), wd_ref[0, 0])
            for j in range(c8):
                y_b[pl.ds(r0 * c8 + j, rc, stride=c8), :] = y[:, LANES * j:LANES * (j + 1)]

    def step(xs_a, xs_b, y_b, y_c):
        @pl.when(steady)
        def _():
            xs = load_rows(xs_b)
            gather(xs_a, True)
            scatter(y_c, True)
            expert(xs, y_b)

        @pl.when(jnp.logical_not(steady))
        def _():
            @pl.when(valid_b)
            def _():
                expert(load_rows(xs_b), y_b)

            @pl.when(valid_a)
            def _():
                gather(xs_a, False)

            @pl.when(valid_c)
            def _():
                scatter(y_c, False)

    @pl.when(g % 2 == 0)
    def _():
        step(xs_0, xs_1, y_1, y_0)

    @pl.when(g % 2 == 1)
    def _():
        step(xs_1, xs_0, y_0, y_1)


def _moe_call(idx, h3, gate, wg, wu, wd, layer):
    bm, ne, cap = idx.shape
    d = wg.shape[2]
    p = d // 2 // LANES
    c8 = d // LANES
    n = h3.shape[1] // p
    n_tiles = bm * ne
    idx3 = idx.reshape(n_tiles, 1, cap)
    gate3 = gate.reshape(n_tiles, 1, cap)
    tile_a = lambda i: jnp.minimum(i, n_tiles - 1)
    tile_b = lambda i: jnp.clip(i - 1, 0, n_tiles - 1)
    tile_c = lambda i: jnp.clip(i - 2, 0, n_tiles - 1)
    wmap = lambda i: (layer, tile_b(i) % ne, 0, 0)
    smem = lambda f: pl.BlockSpec((1, 1, cap), lambda i: (f(i), 0, 0), memory_space=pltpu.SMEM)
    return pl.pallas_call(
        functools.partial(_moe_kernel, cap=cap, p=p, c8=c8, ne=ne, n_tiles=n_tiles),
        grid=(n_tiles + 2,),
        in_specs=[
            smem(tile_a), smem(tile_c), smem(tile_c),
            pl.BlockSpec((1, n * p, LANES), lambda i: (tile_a(i) // ne, 0, 0)),
            pl.BlockSpec((1, 1, d, wg.shape[3]), wmap),
            pl.BlockSpec((1, 1, d, wu.shape[3]), wmap),
            pl.BlockSpec((1, 1, wd.shape[2], d), wmap),
        ],
        out_specs=pl.BlockSpec((1, n * c8, LANES), lambda i: (tile_c(i) // ne, 0, 0), pipeline_mode=pl.Buffered(1)),
        out_shape=jax.ShapeDtypeStruct((bm, n * c8, LANES), F32),
        scratch_shapes=[pltpu.VMEM((cap * p, LANES), jnp.uint32), pltpu.VMEM((cap * p, LANES), jnp.uint32),
                        pltpu.VMEM((cap * c8, LANES), F32), pltpu.VMEM((cap * c8, LANES), F32)],
        compiler_params=_cparams(1, VMEM_LIMIT),
        name="moe",
    )(idx3, idx3, gate3, h3, wg, wu, wd)


def _resid_kernel(x_ref, moe_ref, mod_ref, o_ref):
    tm = x_ref.shape[1]
    sub = min(ROW_SUB, tm)
    for r0 in range(0, tm, sub):
        o_ref[0, r0:r0 + sub] = _gated_moe_residual(x_ref, moe_ref, mod_ref[0][5:6], r0, sub)


def _resid_call(x1, moe3, mod, *, tm):
    b, n, d = x1.shape
    mb = mod.shape[0]
    c8 = d // LANES
    mod_map = (lambda i, j: (i, 0, 0)) if mb > 1 else (lambda i, j: (0, 0, 0))
    tok = lambda i, j: (i, j, 0)
    return pl.pallas_call(
        _resid_kernel,
        grid=(b, n // tm),
        in_specs=[
            pl.BlockSpec((1, tm, d), tok),
            pl.BlockSpec((1, tm * c8, LANES), tok),
            pl.BlockSpec((1, 6, d), mod_map),
        ],
        out_specs=pl.BlockSpec((1, tm, d), tok),
        out_shape=jax.ShapeDtypeStruct((b, n, d), F32),
        compiler_params=_cparams(2, VMEM_LIMIT),
        name="resid",
    )(x1, moe3, mod)


def _rope_tables(n):
    t = jnp.arange(n)
    row = (t // GRID_W).astype(F32)
    col = (t % GRID_W).astype(F32)
    n_freq = HEAD_DIM // 4
    inv_freq = jnp.power(ROPE_BASE, -jnp.arange(n_freq, dtype=F32) / n_freq)
    ang = jnp.concatenate([row[:, None] * inv_freq, col[:, None] * inv_freq], axis=-1)
    cos, sin = jnp.cos(ang), jnp.sin(ang)
    cos_t = jnp.concatenate([cos, cos, cos, cos], axis=-1)
    sin_t = jnp.concatenate([-sin, sin, -sin, sin], axis=-1)
    return cos_t, sin_t


def _gain128(g):
    return jnp.concatenate([g, g]).reshape(1, LANES).astype(F32)


def _channel_mix(h_pack, aff_t, tri, weights, layer, *, flatten):
    wg, wu, wd = weights
    b, _, n = aff_t.shape
    cap = EC_CAPACITY_FACTOR * n // N_EXPERTS
    idx, gate = _topk_call(aff_t, tri, cap=cap)
    if flatten:
        idx = (idx + (jnp.arange(b, dtype=jnp.int32) * n)[:, None, None]).transpose(1, 0, 2).reshape(1, N_EXPERTS, b * cap)
        gate = gate.transpose(1, 0, 2).reshape(1, N_EXPERTS, b * cap)
        h_pack = h_pack.reshape(1, -1, LANES)
    return _moe_call(idx, h_pack, gate, wg, wu, wd, layer).reshape(b, -1, LANES)


def kernel(x, c, ctx, c_ctx, ada_w, ada_b, norm1_g, norm2_g, router_w, exp_w_gate, exp_w_up, exp_w_down, ev_w_in,
           ev_w_out, ev_q_gain, ev_k_gain, ev_sink, od_w_in, od_w_out, od_q_gain, od_k_gain, od_rpb):
    b, n, d = x.shape
    nc = ctx.shape[1]
    tm = min(512, n)
    tmb = min(1024, n)
    tmc = min(512, nc)

    rows = -(-(b + 1) // 8) * 8
    cs = jnp.concatenate([c, c_ctx[None], jnp.zeros((rows - b - 1, d), F32)], axis=0)
    mods = _mod_call(cs, ada_w, ada_b).reshape(ada_w.shape[0], rows, 6, d)

    cos_t, sin_t = _rope_tables(n)
    cs_dft = _channel_dft()
    tri = jnp.asarray(np.triu(np.ones((LANES, LANES), np.float32)), BF16)
    fw = FOURIER_GROUPS * LANES

    def router_split(l):
        rw = router_w[l].T
        rh = rw.astype(BF16)
        rl = (rw - rh.astype(F32)).astype(BF16)
        left = jnp.pad(jnp.concatenate([rh, rl], axis=0), ((0, LANES - 2 * N_EXPERTS), (0, 0)))
        right = jnp.pad(rh, ((0, LANES - N_EXPERTS), (0, 0)))
        return jnp.concatenate([left, right], axis=1)

    w_stacks = (exp_w_gate, exp_w_up, exp_w_down)
    as_layer = lambda ws: [w.reshape((1,) + s.shape[1:]) for w, s in zip(ws, w_stacks)]

    mod_x, mod_c = mods[0, :b], mods[0, b:b + 1]
    g1 = norm1_g[0].reshape(1, d)
    g2 = norm2_g[0].reshape(1, d)
    w_in = ev_w_in[0].astype(BF16)
    w_out = ev_w_out[0].astype(BF16)
    qg, kg = _gain128(ev_q_gain[0]), _gain128(ev_k_gain[0])
    sink = ev_sink[0].astype(F32)
    rw = router_split(0)

    ab_x, q_x, kd_x, vd_x = _premix_even_call(x, mod_x, g1, w_in, cs_dft, cos_t, sin_t, qg, kg, rope=True, tm=tmb)
    ab_c, q_c, kd_c, vd_c = _premix_even_call(ctx, mod_c, g1, w_in, cs_dft, cos_t[:nc], sin_t[:nc], qg, kg,
                                              rope=False, tm=tmc)
    a_x, *wts = _swa_call(sink, q_x, kd_x, vd_x, kd_c, vd_c, w_stacks, 0)
    wts = as_layer(wts)
    a_c = _ctx_attn_call(sink, q_c, kd_c, vd_c)
    four_x = _fourier_call(ab_x, *_dft_tables(n), tm=tmb)
    four_c = _fourier_call(ab_c, *_dft_tables(nc), tm=tmc)
    w_out_parts = [w_out[:fw], w_out[fw:]]
    x1, h3_x, aff_x = _postmix_call([four_x, a_x], w_out_parts, x, mod_x, g2, rw, tm=tmb)
    c1, h3_c, aff_c = _postmix_call([four_c, a_c], w_out_parts, ctx, mod_c, g2, rw, tm=tmc)
    moe_x = _channel_mix(h3_x, aff_x, tri, wts, 0, flatten=False)
    moe_c = _channel_mix(h3_c, aff_c, tri, wts, 0, flatten=True)

    mod_x0, mod_c0 = mod_x, mod_c
    mod_x, mod_c = mods[1, :b], mods[1, b:b + 1]
    g1 = norm1_g[1].reshape(1, d)
    g2 = norm2_g[1].reshape(1, d)
    w_in = od_w_in[0].astype(BF16)
    w_out = od_w_out[0].astype(BF16)
    qg, kg = _gain128(od_q_gain[0]), _gain128(od_k_gain[0])
    rw = router_split(1)

    q_x, k_x, v_x, x = _premix_odd_call(x1, moe_x, mod_x0, mod_x, g1, w_in, qg, kg, emit_x=True, tm=tmb)
    _, k_c, v_c = _premix_odd_call(c1, moe_c, mod_c0, mod_c, g1, w_in, qg, kg, emit_x=False, tm=tmc)
    a_x, *wts = _na_call(q_x, k_x, v_x, k_c, v_c, _na_bias_table(od_rpb[0]), w_stacks, 1)
    wts = as_layer(wts)
    x1, h3_x, aff_x = _postmix_call([a_x], [w_out], x, mod_x, g2, rw, tm=tmb)
    return _resid_call(x1, _channel_mix(h3_x, aff_x, tri, wts, 0, flatten=False), mod_x, tm=min(2 * tmb, n))
```

```python
import functools
import math

import numpy as np
import jax
import jax.numpy as jnp
from jax import lax
from jax.experimental import pallas as pl
from jax.experimental.pallas import tpu as pltpu

GRID_W = 64
HEAD_DIM = 64
FOURIER_GROUPS = 4
SWA_WINDOW = 128
SWA_BLOCK = 128
NA_ROWS_MAX = 8
NA_COLS = 16
N_EXPERTS = 16
EC_CAPACITY_FACTOR = 2
ROPE_BASE = 10000.0
EPS = 1e-6

LANES = 128
NEG = -1e30
NA_QROWS = 4
NA_JBLK = 2
NA_KROWS = 12
DFT_SUB = 64
MOE_ROW_CHUNK = 128
SWA_QBLK = 8
ROW_SUB = 256
ONES_ROWS = 16
LOG2E = math.log2(math.e)
VMEM_LIMIT = 60 * 1024 * 1024

F32 = jnp.float32
BF16 = jnp.bfloat16


def _cparams(n_axes, vmem=None):
    return pltpu.CompilerParams(dimension_semantics=("arbitrary",) * n_axes, vmem_limit_bytes=vmem)


def _dot(a, b):
    return jnp.dot(a, b, preferred_element_type=F32)


def _dot_t(a, b):
    return lax.dot_general(a, b, (((1,), (1,)), ((), ())), preferred_element_type=F32)


def _split(a):
    hi = a.astype(BF16)
    lo = (a - hi.astype(F32)).astype(BF16)
    return hi, lo


def _dot3(a, b):
    ah, al = _split(a)
    bh, bl = _split(b)
    return _dot(ah, bh) + _dot(al, bh) + _dot(ah, bl)


def _silu(a):
    return a / (1.0 + jnp.exp(-a))


def _rms_mod(x, gain, shift, scale):
    ms = jnp.mean(x * x, axis=-1, keepdims=True)
    y = x * lax.rsqrt(ms + EPS) * gain
    return y * (1.0 + scale) + shift


def _mod_kernel(cs_ref, w_ref, b_ref, o_ref):
    o_ref[0] = _dot3(_silu(cs_ref[...]), w_ref[0]) + b_ref[0]


def _mod_call(cs, ada_w, ada_b):
    depth, d, d6 = ada_w.shape
    r = cs.shape[0]
    tn = 1024
    return pl.pallas_call(
        _mod_kernel,
        grid=(depth, d6 // tn),
        in_specs=[
            pl.BlockSpec((r, d), lambda l, j: (0, 0)),
            pl.BlockSpec((1, d, tn), lambda l, j: (l, 0, j)),
            pl.BlockSpec((1, 1, tn), lambda l, j: (l, 0, j)),
        ],
        out_specs=pl.BlockSpec((1, r, tn), lambda l, j: (l, 0, j)),
        out_shape=jax.ShapeDtypeStruct((depth, r, d6), F32),
        compiler_params=_cparams(2),
        name="mod",
    )(cs, ada_w, ada_b.reshape(depth, 1, d6))


def _head_norm(t, gain, lane):
    t2 = t * t
    s_lo = jnp.sum(jnp.where(lane < HEAD_DIM, t2, 0.0), axis=-1, keepdims=True)
    s_all = jnp.sum(t2, axis=-1, keepdims=True)
    ms = jnp.where(lane < HEAD_DIM, s_lo, s_all - s_lo) * (1.0 / HEAD_DIM)
    return t * lax.rsqrt(ms + EPS) * gain


def _rope(t, cos_t, sin_t, lane):
    partner = jnp.where((lane % HEAD_DIM) < HEAD_DIM // 2, pltpu.roll(t, LANES - HEAD_DIM // 2, 1),
                        pltpu.roll(t, HEAD_DIM // 2, 1))
    return t * cos_t + partner * sin_t


def _dup_halves(t, lane):
    sw = pltpu.roll(t, HEAD_DIM, 1)
    return jnp.where(lane < HEAD_DIM, t, sw), jnp.where(lane < HEAD_DIM, sw, t)


def _premix_even_kernel(x_ref, mod_ref, g_ref, w_ref, cs_ref, cos_ref, sin_ref, qg_ref, kg_ref,
                        ab_ref, q_ref, kd_ref, vt_ref, *, rope):
    tm = x_ref.shape[1]
    sub = min(ROW_SUB, tm)
    m = mod_ref[0]
    lane = lax.broadcasted_iota(jnp.int32, (sub, LANES), 1)
    fw = FOURIER_GROUPS * LANES
    qw = q_ref.shape[2]
    csb = cs_ref[...].astype(BF16)
    for r0 in range(0, tm, sub):
        rs = slice(r0, r0 + sub)
        hb = _rms_mod(x_ref[0, rs], g_ref[...], m[0:1], m[1:2]).astype(BF16)
        pf = _dot(hb, w_ref[:, 0:fw])
        for g in range(FOURIER_GROUPS):
            ab = _dot(pf[:, LANES * g:LANES * (g + 1)].astype(BF16), csb)
            ab_ref[0, rs, LANES * g:LANES * (g + 1)] = ab[:, :LANES].astype(BF16)
            ab_ref[0, rs, fw + LANES * g:fw + LANES * (g + 1)] = ab[:, LANES:].astype(BF16)
        pq = _dot(hb, w_ref[:, fw:fw + qw])
        if rope:
            cos_t = cos_ref[rs]
            sin_t = sin_ref[rs]
        for c in range(qw // LANES):
            t = _head_norm(pq[:, LANES * c:LANES * (c + 1)], qg_ref[...], lane)
            if rope:
                t = _rope(t, cos_t, sin_t, lane)
            q_ref[0, rs, LANES * c:LANES * (c + 1)] = t.astype(BF16)
        pk = _dot(hb, w_ref[:, fw + qw:fw + qw + 2 * LANES])
        k = _head_norm(pk[:, :LANES], kg_ref[...], lane)
        if rope:
            k = _rope(k, cos_t, sin_t, lane)
        k0, k1 = _dup_halves(k, lane)
        kd_ref[0, rs, :LANES] = k0.astype(BF16)
        kd_ref[0, rs, LANES:] = k1.astype(BF16)
        vt_ref[0, :, rs] = pk[:, LANES:].T.astype(BF16)


def _premix_even_call(x, mod, gain, w_bf, cs, cos_t, sin_t, qg, kg, *, rope, tm):
    b, n, d = x.shape
    mb = mod.shape[0]
    wtot = w_bf.shape[1]
    fw = FOURIER_GROUPS * LANES
    qw = wtot - fw - 2 * LANES
    mod_map = (lambda i, j: (i, 0, 0)) if mb > 1 else (lambda i, j: (0, 0, 0))
    tok = lambda i, j: (i, j, 0)
    const2 = lambda i, j: (0, 0)
    return pl.pallas_call(
        functools.partial(_premix_even_kernel, rope=rope),
        grid=(b, n // tm),
        in_specs=[
            pl.BlockSpec((1, tm, d), tok),
            pl.BlockSpec((1, 6, d), mod_map),
            pl.BlockSpec((1, d), const2),
            pl.BlockSpec((d, wtot), const2),
            pl.BlockSpec((LANES, 2 * LANES), const2),
            pl.BlockSpec((tm, LANES), lambda i, j: (j, 0)),
            pl.BlockSpec((tm, LANES), lambda i, j: (j, 0)),
            pl.BlockSpec((1, LANES), const2),
            pl.BlockSpec((1, LANES), const2),
        ],
        out_specs=[
            pl.BlockSpec((1, tm, 2 * fw), tok),
            pl.BlockSpec((1, tm, qw), tok),
            pl.BlockSpec((1, tm, 2 * LANES), tok),
            pl.BlockSpec((1, LANES, tm), lambda i, j: (i, 0, j)),
        ],
        out_shape=[
            jax.ShapeDtypeStruct((b, n, 2 * fw), BF16),
            jax.ShapeDtypeStruct((b, n, qw), BF16),
            jax.ShapeDtypeStruct((b, n, 2 * LANES), BF16),
            jax.ShapeDtypeStruct((b, LANES, n), BF16),
        ],
        compiler_params=_cparams(2, VMEM_LIMIT),
        name="premix_even",
    )(x, mod, gain, w_bf, cs, cos_t, sin_t, qg, kg)


def _gated_moe_residual(x_ref, moe_ref, gate_row, r0, sub):
    c8 = x_ref.shape[2] // LANES
    cols = []
    for j in range(c8):
        cs = slice(LANES * j, LANES * (j + 1))
        cols.append(x_ref[0, r0:r0 + sub, cs] + gate_row[:, cs] * moe_ref[0, pl.ds(r0 * c8 + j, sub, stride=c8), :])
    return jnp.concatenate(cols, axis=1)


def _premix_odd_kernel(x_ref, moe_ref, modp_ref, mod_ref, g_ref, w_ref, qg_ref, kg_ref, q_ref, k_ref, v_ref,
                       *xo_ref):
    tm = x_ref.shape[1]
    sub = min(ROW_SUB, tm)
    m = mod_ref[0]
    gate_prev = modp_ref[0][5:6]
    lane = lax.broadcasted_iota(jnp.int32, (sub, LANES), 1)
    wq = q_ref.shape[2]
    chunk = 4 * LANES
    for r0 in range(0, tm, sub):
        rs = slice(r0, r0 + sub)
        x = _gated_moe_residual(x_ref, moe_ref, gate_prev, r0, sub)
        if xo_ref:
            xo_ref[0][0, rs] = x
        hb = _rms_mod(x, g_ref[...], m[0:1], m[1:2]).astype(BF16)
        for c0 in range(0, 3 * wq, chunk):
            p = _dot(hb, w_ref[:, c0:c0 + chunk])
            for cc in range(chunk // LANES):
                col = c0 + cc * LANES
                t = p[:, cc * LANES:(cc + 1) * LANES]
                if col < wq:
                    q_ref[0, rs, col:col + LANES] = _head_norm(t, qg_ref[...], lane).astype(BF16)
                elif col < 2 * wq:
                    k_ref[0, rs, col - wq:col - wq + LANES] = _head_norm(t, kg_ref[...], lane).astype(BF16)
                else:
                    v_ref[0, col - 2 * wq:col - 2 * wq + LANES, rs] = t.T.astype(BF16)


def _premix_odd_call(x1, moe3, mod_prev, mod, gain, w_bf, qg, kg, *, emit_x, tm):
    b, n, d = x1.shape
    mb = mod.shape[0]
    wq = w_bf.shape[1] // 3
    c8 = d // LANES
    mod_map = (lambda i, j: (i, 0, 0)) if mb > 1 else (lambda i, j: (0, 0, 0))
    tok = lambda i, j: (i, j, 0)
    const2 = lambda i, j: (0, 0)
    out_specs = [pl.BlockSpec((1, tm, wq), tok)] * 2 + [pl.BlockSpec((1, wq, tm), lambda i, j: (i, 0, j))]
    out_shape = [jax.ShapeDtypeStruct((b, n, wq), BF16)] * 2 + [jax.ShapeDtypeStruct((b, wq, n), BF16)]
    if emit_x:
        out_specs.append(pl.BlockSpec((1, tm, d), tok))
        out_shape.append(jax.ShapeDtypeStruct((b, n, d), F32))
    return pl.pallas_call(
        _premix_odd_kernel,
        grid=(b, n // tm),
        in_specs=[
            pl.BlockSpec((1, tm, d), tok),
            pl.BlockSpec((1, tm * c8, LANES), tok),
            pl.BlockSpec((1, 6, d), mod_map),
            pl.BlockSpec((1, 6, d), mod_map),
            pl.BlockSpec((1, d), const2),
            pl.BlockSpec((d, 3 * wq), const2),
            pl.BlockSpec((1, LANES), const2),
            pl.BlockSpec((1, LANES), const2),
        ],
        out_specs=out_specs,
        out_shape=out_shape,
        compiler_params=_cparams(2, VMEM_LIMIT),
        name="premix_odd",
    )(x1, moe3, mod_prev, mod, gain, w_bf, qg, kg)


def _col_reduce(x, op, slab=64):
    r = x.shape[0]
    if r > slab and r % slab == 0:
        x = op(x.reshape(r // slab, slab, x.shape[1]), axis=0)
    return op(x, axis=0, keepdims=True)


def _softmax_pv_t(s_t, v_t, sink=None):
    mx = _col_reduce(s_t, jnp.max)
    if sink is not None:
        mx = jnp.maximum(mx, sink)
    p = jnp.exp2((s_t - mx).astype(BF16))
    nd = v_t.shape[0]
    o = _dot(jnp.concatenate([v_t, jnp.ones((ONES_ROWS, v_t.shape[1]), BF16)], axis=0), p)
    den = o[nd:nd + 1]
    if sink is not None:
        den = den + jnp.exp2(sink - mx)
    return o[:nd] / den


def _mask_half(qc, lane, half):
    q32 = qc.astype(F32) * (HEAD_DIM ** -0.5 * LOG2E)
    keep = (lane < HEAD_DIM) if half == 0 else (lane >= HEAD_DIM)
    return jnp.where(keep, q32, 0.0).astype(BF16)


def _gqa_scores(q_ref, h, kd, n_kv, masks, lane, r0, tq):
    rs = slice(r0, r0 + tq)
    cols_per_kv = q_ref.shape[2] // LANES // n_kv
    q_rows = []
    for cc in range(cols_per_kv):
        c = cols_per_kv * h + cc
        for half in range(2):
            q_rows.append(_mask_half(q_ref[0, rs, LANES * c:LANES * (c + 1)], lane, half))
    qm = jnp.concatenate(q_rows, axis=0)
    s = _dot_t(kd, qm)
    pieces, k0 = [], 0
    for rows, mask in masks:
        blk = s[k0:k0 + rows]
        pieces.append(blk if mask is None else jnp.where(mask, blk, NEG))
        k0 += rows
    return jnp.concatenate(pieces, axis=0) if len(pieces) > 1 else pieces[0]


def _gqa_finish(sink_ref, o_ref, h, s_t, v_t, r0, tq):
    rs = slice(r0, r0 + tq)
    n_kv = v_t.shape[0] // HEAD_DIM
    cols_per_kv = o_ref.shape[2] // LANES // n_kv
    heads = 2 * cols_per_kv
    head_of_lane = lax.broadcasted_iota(jnp.int32, (1, heads * tq), 1) // tq
    sink = jnp.zeros((1, heads * tq), F32)
    for a in range(heads):
        sink = jnp.where(head_of_lane == a, sink_ref[heads * h + a] * LOG2E, sink)
    o_t = _softmax_pv_t(s_t, v_t, sink)[HEAD_DIM * h:HEAD_DIM * (h + 1)]
    for cc in range(cols_per_kv):
        c = cols_per_kv * h + cc
        tile = jnp.concatenate([o_t[:, tq * (2 * cc):tq * (2 * cc + 1)], o_t[:, tq * (2 * cc + 1):tq * (2 * cc + 2)]],
                               axis=0)
        o_ref[0, rs, LANES * c:LANES * (c + 1)] = tile.T.astype(BF16)


def _swa_kernel(sink_ref, q_ref, km_ref, k0_ref, kp_ref, vm_ref, v0_ref, vp_ref, kc_ref, vc_ref, *rest, n_step):
    o_ref = _ride_cast(rest)
    g = pl.program_id(1)
    tq = SWA_BLOCK
    nc = kc_ref.shape[1]
    n_kv = kc_ref.shape[2] // LANES
    heads = q_ref.shape[2] // HEAD_DIM // n_kv
    lane = lax.broadcasted_iota(jnp.int32, (tq, LANES), 1)
    jk = lax.broadcasted_iota(jnp.int32, (SWA_BLOCK, heads * tq), 0)
    iq = lax.broadcasted_iota(jnp.int32, (SWA_BLOCK, heads * tq), 1) % tq
    below, above = jk >= iq, jk <= iq
    nq = q_ref.shape[1] // SWA_BLOCK
    blk = lambda s: slice(SWA_BLOCK * s, SWA_BLOCK * (s + 1))

    def key_block(s, hs):
        return km_ref[0, :, hs] if s < 0 else kp_ref[0, :, hs] if s == nq else k0_ref[0, blk(s), hs]

    def value_block(s):
        return vm_ref[0] if s < 0 else vp_ref[0] if s == nq else v0_ref[0, :, blk(s)]

    def scores(sub, h):
        hs = slice(LANES * h, LANES * (h + 1))
        first = jnp.logical_and(below, g > 0) if sub == 0 else below
        last = jnp.logical_and(above, g < n_step - 1) if sub == nq - 1 else above
        masks = [(nc, None), (SWA_BLOCK, first), (SWA_BLOCK, None), (SWA_BLOCK, last)]
        kd = jnp.concatenate([kc_ref[0, :, hs]] + [key_block(s, hs) for s in (sub - 1, sub, sub + 1)], axis=0)
        return _gqa_scores(q_ref, h, kd, n_kv, masks, lane, SWA_BLOCK * sub, tq)

    groups = [(sub, h) for sub in range(nq) for h in range(n_kv)]
    ahead = 3
    pending = [scores(*grp) for grp in groups[:ahead]]
    for i, (sub, h) in enumerate(groups):
        s_t = pending.pop(0)
        if i + ahead < len(groups):
            pending.append(scores(*groups[i + ahead]))
        v_t = jnp.concatenate([vc_ref[0]] + [value_block(s) for s in (sub - 1, sub, sub + 1)], axis=1)
        _gqa_finish(sink_ref, o_ref, h, s_t, v_t, SWA_BLOCK * sub, tq)


def _cast_riders(ws, layer, total, flat):
    ins, outs, shapes, args = [], [], [], []
    for w in ws:
        l, e, r, c = w.shape
        rows = e * r // total
        ins.append(pl.BlockSpec((rows, c), lambda i, j: (layer * total + flat(i, j), 0)))
        outs.append(pl.BlockSpec((rows, c), lambda i, j: (flat(i, j), 0)))
        shapes.append(jax.ShapeDtypeStruct((e * r, c), BF16))
        args.append(w.reshape(l * e * r, c))
    return ins, outs, shapes, args


def _ride_cast(rest):
    n_cast = (len(rest) - 1) // 2
    for w_ref, c_ref in zip(rest[:n_cast], rest[n_cast + 1:]):
        c_ref[...] = w_ref[...].astype(BF16)
    return rest[n_cast]


def _swa_call(sink, q, kd, vt, kdc, vtc, ws, layer):
    b, n, qw = q.shape
    nc = kdc.shape[1]
    kw = kd.shape[2]
    vw = vt.shape[1]
    n_blk = n // SWA_BLOCK
    nq = min(SWA_QBLK, n_blk)
    n_step = n_blk // nq
    prev = lambda j: jnp.maximum(nq * j - 1, 0)
    nxt = lambda j: jnp.minimum(nq * j + nq, n_blk - 1)
    kspec = lambda f: pl.BlockSpec((1, SWA_BLOCK, kw), lambda i, j: (i, f(j), 0))
    vspec = lambda f: pl.BlockSpec((1, vw, SWA_BLOCK), lambda i, j: (i, 0, f(j)))
    c_in, c_out, c_shape, c_args = _cast_riders(ws, layer, b * n_step, lambda i, j: i * n_step + j)
    return pl.pallas_call(
        functools.partial(_swa_kernel, n_step=n_step),
        grid=(b, n_step),
        in_specs=[
            pl.BlockSpec(memory_space=pltpu.SMEM),
            pl.BlockSpec((1, nq * SWA_BLOCK, qw), lambda i, j: (i, j, 0)),
            kspec(prev), pl.BlockSpec((1, nq * SWA_BLOCK, kw), lambda i, j: (i, j, 0)), kspec(nxt),
            vspec(prev), pl.BlockSpec((1, vw, nq * SWA_BLOCK), lambda i, j: (i, 0, j)), vspec(nxt),
            pl.BlockSpec((1, nc, kw), lambda i, j: (i, 0, 0)),
            pl.BlockSpec((1, vw, nc), lambda i, j: (i, 0, 0)),
        ] + c_in,
        out_specs=[pl.BlockSpec((1, nq * SWA_BLOCK, qw), lambda i, j: (i, j, 0))] + c_out,
        out_shape=[jax.ShapeDtypeStruct((b, n, qw), BF16)] + c_shape,
        compiler_params=_cparams(2, VMEM_LIMIT),
        name="swa",
    )(sink, q, kd, kd, kd, vt, vt, vt, kdc, vtc, *c_args)


def _ctx_attn_kernel(sink_ref, q_ref, kc_ref, vc_ref, o_ref):
    tq = q_ref.shape[1]
    lane = lax.broadcasted_iota(jnp.int32, (tq, LANES), 1)
    n_kv = kc_ref.shape[2] // LANES
    scores = [_gqa_scores(q_ref, h, kc_ref[0, :, LANES * h:LANES * (h + 1)], n_kv, [(kc_ref.shape[1], None)], lane,
                          0, tq) for h in range(n_kv)]
    for h, s_t in enumerate(scores):
        _gqa_finish(sink_ref, o_ref, h, s_t, vc_ref[0], 0, tq)


def _ctx_attn_call(sink, q, kd, vt):
    b, nc, qw = q.shape
    kw = kd.shape[2]
    vw = vt.shape[1]
    m = lambda i: (i, 0, 0)
    return pl.pallas_call(
        _ctx_attn_kernel,
        grid=(b,),
        in_specs=[
            pl.BlockSpec(memory_space=pltpu.SMEM),
            pl.BlockSpec((1, nc, qw), m),
            pl.BlockSpec((1, nc, kw), m),
            pl.BlockSpec((1, vw, nc), m),
        ],
        out_specs=pl.BlockSpec((1, nc, qw), m),
        out_shape=jax.ShapeDtypeStruct((b, nc, qw), BF16),
        compiler_params=_cparams(1, VMEM_LIMIT),
        name="ctx_attn",
    )(sink, q, kd, vt)


def _na_kernel(q_ref, k0_ref, k1_ref, k2_ref, k3_ref, v0_ref, v1_ref, v2_ref, v3_ref, kc_ref, vc_ref, tb_ref, *rest,
               n_j, rows):
    o_ref = _ride_cast(rest)
    g = pl.program_id(1)
    tq = NA_QROWS * GRID_W
    tk = NA_KROWS * GRID_W
    nc = kc_ref.shape[1]
    k_blocks = [k0_ref, k1_ref, k2_ref, k3_ref]
    v_blocks = [v0_ref, v1_ref, v2_ref, v3_ref]
    base = jnp.clip(NA_JBLK * g - 1, 0, n_j - (NA_JBLK + 2))
    lane = lax.broadcasted_iota(jnp.int32, (tq, LANES), 1)
    k_row = (lax.broadcasted_iota(jnp.int32, (nc + tk, LANES), 0) - nc) // GRID_W
    k_lane = lax.broadcasted_iota(jnp.int32, (nc + tk, LANES), 1)
    is_local = lax.broadcasted_iota(jnp.int32, (nc + tk, LANES), 0) >= nc
    q_grid_row = (lax.broadcasted_iota(jnp.int32, (2 * tq, LANES), 0) % tq) // GRID_W
    q_extra = jnp.where(lax.broadcasted_iota(jnp.int32, (2 * tq, LANES), 1) == q_grid_row, 1.0, 0.0).astype(BF16)
    n_dr = 2 * NA_ROWS_MAX - 1
    shifted, k_extras, deltas = [], [], []
    for u in range(NA_JBLK):
        j = NA_JBLK * g + u
        w0_blk = jnp.clip(j - 1, 0, n_j - 3)
        w0 = NA_QROWS * w0_blk
        lo = jnp.zeros((nc + tk, LANES), jnp.int32)
        for rr in range(NA_QROWS):
            r = NA_QROWS * j + rr
            lo_rr = jnp.clip(r - NA_ROWS_MAX // 2, 0, rows - NA_ROWS_MAX) - w0
            lo = jnp.where(k_lane == rr, lo_rr, lo)
        outside = jnp.logical_or(k_row < lo, k_row >= lo + NA_ROWS_MAX)
        k_extras.append(jnp.where(jnp.logical_and(jnp.logical_and(is_local, k_lane < NA_QROWS), outside), NEG, 0.0)
                        .astype(BF16))
        shifted.append(w0_blk - base == 1)
        deltas.append(w0 - NA_QROWS * j)

    def scores(u, c):
        cs = slice(LANES * c, LANES * (c + 1))
        qc = q_ref[0, tq * u:tq * (u + 1), cs]
        k_loc = [jnp.where(shifted[u], k_blocks[d + 1][0, :, cs], k_blocks[d][0, :, cs]) for d in range(3)]
        kl = jnp.concatenate([kc_ref[0, :, cs]] + k_loc, axis=0)
        qm = jnp.concatenate([_mask_half(qc, lane, 0), _mask_half(qc, lane, 1)], axis=0)
        return _dot_t(jnp.concatenate([kl, k_extras[u]], axis=1), jnp.concatenate([qm, q_extra], axis=1))

    groups = [(u, c) for c in range(q_ref.shape[2] // LANES) for u in range(NA_JBLK)]
    ahead = 2
    pending = [scores(*grp) for grp in groups[:ahead]]
    for i, (u, c) in enumerate(groups):
        cs = slice(LANES * c, LANES * (c + 1))
        s = pending.pop(0)
        if i + ahead < len(groups):
            pending.append(scores(*groups[i + ahead]))
        v_loc = [jnp.where(shifted[u], v_blocks[d + 1][0, cs, :], v_blocks[d][0, cs, :]) for d in range(3)]
        v_t = jnp.concatenate([vc_ref[0, cs, :]] + v_loc, axis=1)
        bias_rows = []
        for kr in range(NA_KROWS):
            pieces = []
            for half in range(2):
                for t in range(NA_QROWS // 2):
                    dr = deltas[u] + kr - 2 * t + NA_ROWS_MAX - 1
                    pieces.append(tb_ref[2 * c + half, jnp.clip(dr, 0, n_dr)])
            bias_rows.append(jnp.concatenate(pieces, axis=1))
        bias = jnp.concatenate(bias_rows, axis=0)
        s_t = jnp.concatenate([s[:nc], s[nc:] + bias], axis=0)
        o_t = _softmax_pv_t(s_t, v_t)
        tile = jnp.concatenate([o_t[:HEAD_DIM, :tq], o_t[HEAD_DIM:, tq:]], axis=0)
        o_ref[0, tq * u:tq * (u + 1), cs] = tile.T.astype(BF16)


def _na_call(q, k, vt, kc, vtc, tb, ws, layer):
    b, n, w = q.shape
    nc = kc.shape[1]
    rows = n // GRID_W
    n_j = rows // NA_QROWS
    n_g = n_j // NA_JBLK
    tq = NA_QROWS * GRID_W
    cur = lambda i, j: (i, j, 0)
    first = lambda j: jnp.clip(NA_JBLK * j - 1, 0, n_j - (NA_JBLK + 2))
    kspec = lambda d: pl.BlockSpec((1, tq, w), lambda i, j: (i, first(j) + d, 0))
    vspec = lambda d: pl.BlockSpec((1, w, tq), lambda i, j: (i, 0, first(j) + d))
    c_in, c_out, c_shape, c_args = _cast_riders(ws, layer, b * n_g, lambda i, j: i * n_g + j)
    return pl.pallas_call(
        functools.partial(_na_kernel, n_j=n_j, rows=rows),
        grid=(b, n_g),
        in_specs=[
            pl.BlockSpec((1, NA_JBLK * tq, w), cur),
            kspec(0), kspec(1), kspec(2), kspec(3),
            vspec(0), vspec(1), vspec(2), vspec(3),
            pl.BlockSpec((1, nc, w), lambda i, j: (i, 0, 0)),
            pl.BlockSpec((1, w, nc), lambda i, j: (i, 0, 0)),
            pl.BlockSpec(tb.shape, lambda i, j: (0, 0, 0, 0)),
        ] + c_in,
        out_specs=[pl.BlockSpec((1, NA_JBLK * tq, w), cur)] + c_out,
        out_shape=[jax.ShapeDtypeStruct((b, n, w), BF16)] + c_shape,
        compiler_params=_cparams(2, VMEM_LIMIT),
        name="na",
    )(q, k, k, k, k, vt, vt, vt, vt, kc, vtc, tb, *c_args)


def _na_bias_table(rpb):
    col_q = np.arange(GRID_W)[None, :]
    col_k = np.arange(GRID_W)[:, None]
    c_start = np.clip(col_q - NA_COLS // 2, 0, GRID_W - NA_COLS)
    col_valid = (col_k >= c_start) & (col_k < c_start + NA_COLS)
    dc_idx = np.clip(col_k - col_q + NA_COLS - 1, 0, 2 * NA_COLS - 2)
    t = jnp.where(col_valid[None, None], (rpb.astype(F32) * LOG2E)[:, :, dc_idx], NEG)
    pad = jnp.full_like(t[:, :1], NEG)
    t = jnp.concatenate([pad, t, pad], axis=1)
    return jnp.concatenate([t[:, 1:], t[:, :-1]], axis=-1)


def _fourier_kernel(ab_ref, ca_ref, sa_ref, cb_ref, sb_ref, rev_ref, o_ref, fold_ref, rev_scr):
    n = ab_ref.shape[1]
    nh = n // 2
    tm, fw = o_ref.shape[1], o_ref.shape[2]

    @pl.when(pl.program_id(1) == 0)
    def _():
        for m in range(nh // LANES):
            blk = ab_ref[0, n - LANES * (m + 1):n - LANES * m, :]
            rev_scr[LANES * m:LANES * (m + 1), :] = _dot(rev_ref[...], blk)
        ck = min(256, nh)
        row = lax.broadcasted_iota(jnp.int32, (ck, 1), 0)
        for r0 in range(0, nh, ck):
            prev = (r0 - 1) % nh
            shifted = jnp.where(row == 0, rev_scr[prev:prev + 1, :], pltpu.roll(rev_scr[r0:r0 + ck, :], 1, 0))
            lo = ab_ref[0, r0:r0 + ck, :].astype(F32)
            a2 = lo[:, :fw] + shifted[:, :fw]
            b2 = lo[:, fw:] - shifted[:, fw:]
            if r0 == 0:
                a2 = jnp.where(row == 0, lo[:, :fw], a2)
                b2 = jnp.where(row == 0, shifted[:, :fw], b2)
            fold_ref[r0:r0 + ck, :fw] = a2.astype(BF16)
            fold_ref[r0:r0 + ck, fw:] = b2.astype(BF16)

    cb = cb_ref[...]
    sb = sb_ref[...]
    sub = min(ROW_SUB, tm)
    for r0 in range(0, tm, sub):
        cm, sm = [], []
        for a in range(r0 // DFT_SUB, (r0 + sub) // DFT_SUB):
            ca = ca_ref[a:a + 1, :]
            sa = sa_ref[a:a + 1, :]
            cm.append((ca * cb - sa * sb).astype(BF16))
            sm.append((sa * cb + ca * sb).astype(BF16))
        y = _dot(jnp.concatenate(cm, axis=0), fold_ref[:, :fw]) - _dot(jnp.concatenate(sm, axis=0), fold_ref[:, fw:])
        o_ref[0, r0:r0 + sub] = y.astype(BF16)


def _fourier_call(ab, ca, sa, cb, sb, *, tm):
    b, n, w2 = ab.shape
    nh = n // 2
    fw = w2 // 2
    na = tm // DFT_SUB
    rev = jnp.asarray(np.eye(LANES, dtype=np.float32)[::-1], BF16)
    return pl.pallas_call(
        _fourier_kernel,
        grid=(b, n // tm),
        in_specs=[
            pl.BlockSpec((1, n, w2), lambda i, j: (i, 0, 0)),
            pl.BlockSpec((na, nh), lambda i, j: (j, 0)),
            pl.BlockSpec((na, nh), lambda i, j: (j, 0)),
            pl.BlockSpec((DFT_SUB, nh), lambda i, j: (0, 0)),
            pl.BlockSpec((DFT_SUB, nh), lambda i, j: (0, 0)),
            pl.BlockSpec((LANES, LANES), lambda i, j: (0, 0)),
        ],
        out_specs=pl.BlockSpec((1, tm, fw), lambda i, j: (i, j, 0)),
        out_shape=jax.ShapeDtypeStruct((b, n, fw), BF16),
        scratch_shapes=[pltpu.VMEM((nh, w2), BF16), pltpu.VMEM((nh, w2), F32)],
        compiler_params=_cparams(2, VMEM_LIMIT),
        name="fourier",
    )(ab, ca, sa, cb, sb, rev)


def _dft_tables(n):
    nh = n // 2
    k = jnp.arange(nh, dtype=jnp.int32)
    a = jnp.arange(n // DFT_SUB, dtype=jnp.int32)
    bb = jnp.arange(DFT_SUB, dtype=jnp.int32)
    ang_a = ((DFT_SUB * a[:, None] * k[None, :]) % n).astype(F32) * (2.0 * math.pi / n)
    ang_b = ((bb[:, None] * k[None, :]) % n).astype(F32) * (2.0 * math.pi / n)
    s = float(n) ** -0.5
    nyq = -jnp.cos(math.pi * bb.astype(F32)) * s
    sin_b = jnp.concatenate([nyq[:, None], (jnp.sin(ang_b) * s)[:, 1:]], axis=1)
    return jnp.cos(ang_a), jnp.sin(ang_a), jnp.cos(ang_b) * s, sin_b


def _channel_dft():
    c = jnp.arange(LANES, dtype=jnp.int32)
    ang = ((c[:, None] * c[None, :]) % LANES).astype(F32) * (2.0 * math.pi / LANES)
    s = float(LANES) ** -0.5
    return jnp.concatenate([jnp.cos(ang) * s, jnp.sin(ang) * s], axis=1)


def _postmix_kernel(*refs, n_in):
    a_refs = refs[:n_in]
    w_refs = refs[n_in:2 * n_in]
    x_ref, mod_ref, g_ref, rw_ref, x1_ref, h3_ref, aff_ref = refs[2 * n_in:]
    tm, d = x_ref.shape[1], x_ref.shape[2]
    sub = min(ROW_SUB, tm)
    p = d // 2 // LANES
    m = mod_ref[0]
    for r0 in range(0, tm, sub):
        rs = slice(r0, r0 + sub)
        y = None
        for a_ref, w_ref in zip(a_refs, w_refs):
            t = _dot(a_ref[0, rs], w_ref[...])
            y = t if y is None else y + t
        x1 = x_ref[0, rs] + m[2:3] * y
        x1_ref[0, rs] = x1
        h2 = _rms_mod(x1, g_ref[...], m[3:4], m[4:5])
        hh, hl = _split(h2)
        parts = _dot_t(rw_ref[...], jnp.concatenate([hh, hl], axis=1))
        lt = parts[:N_EXPERTS] + parts[N_EXPERTS:2 * N_EXPERTS]
        ex = jnp.exp(lt - jnp.max(lt, axis=0, keepdims=True))
        aff_ref[0, :, rs] = ex / jnp.sum(ex, axis=0, keepdims=True)
        bits = pltpu.bitcast(hh.astype(F32), jnp.uint32)
        packed = (bits[:, d // 2:] & jnp.uint32(0xFFFF0000)) | (bits[:, :d // 2] >> 16)
        for jj in range(p):
            h3_ref[0, pl.ds(r0 * p + jj, sub, stride=p), :] = packed[:, LANES * jj:LANES * (jj + 1)]


def _postmix_call(a_list, w_list, x, mod, gain, rw, *, tm):
    b, n, d = x.shape
    mb = mod.shape[0]
    n_in = len(a_list)
    p = d // 2 // LANES
    mod_map = (lambda i, j: (i, 0, 0)) if mb > 1 else (lambda i, j: (0, 0, 0))
    tok = lambda i, j: (i, j, 0)
    const2 = lambda i, j: (0, 0)
    in_specs = [pl.BlockSpec((1, tm, a.shape[2]), tok) for a in a_list]
    in_specs += [pl.BlockSpec(w.shape, const2) for w in w_list]
    in_specs += [
        pl.BlockSpec((1, tm, d), tok),
        pl.BlockSpec((1, 6, d), mod_map),
        pl.BlockSpec((1, d), const2),
        pl.BlockSpec((LANES, 2 * d), const2),
    ]
    return pl.pallas_call(
        functools.partial(_postmix_kernel, n_in=n_in),
        grid=(b, n // tm),
        in_specs=in_specs,
        out_specs=[
            pl.BlockSpec((1, tm, d), tok),
            pl.BlockSpec((1, tm * p, LANES), tok),
            pl.BlockSpec((1, N_EXPERTS, tm), lambda i, j: (i, 0, j)),
        ],
        out_shape=[
            jax.ShapeDtypeStruct((b, n, d), F32),
            jax.ShapeDtypeStruct((b, n * p, LANES), jnp.uint32),
            jax.ShapeDtypeStruct((b, N_EXPERTS, n), F32),
        ],
        compiler_params=_cparams(2, VMEM_LIMIT),
        name="postmix",
    )(*a_list, *w_list, x, mod, gain, rw)


def _prefix_incl(ones, tri):
    carry = jnp.zeros((ones.shape[0], 1), F32)
    outs = []
    for c in range(ones.shape[1] // LANES):
        blk = ones[:, LANES * c:LANES * (c + 1)]
        outs.append(_dot(blk.astype(BF16), tri) + carry)
        carry = carry + jnp.sum(blk, axis=1, keepdims=True)
    return jnp.concatenate(outs, axis=1)


def _topk_kernel(aff_ref, tri_ref, idx_ref, gate_ref, *, cap):
    g, ne, n = aff_ref.shape
    rows = g * ne
    aff = aff_ref[...].reshape(rows, n)
    bits = pltpu.bitcast(aff, jnp.int32)
    capf = jnp.float32(cap)

    def body(it, lo):
        sh = 28 - 2 * it
        new = lo
        for k in (1, 2, 3):
            cand = lo | jnp.left_shift(jnp.int32(k), sh)
            cnt = jnp.sum(jnp.where(bits >= cand, 1.0, 0.0), axis=1, keepdims=True)
            new = jnp.where(cnt >= capf, cand, new)
        return new

    thr = lax.fori_loop(0, 15, body, jnp.zeros((rows, 1), jnp.int32))
    gt = jnp.where(bits > thr, 1.0, 0.0)
    eq = jnp.where(bits == thr, 1.0, 0.0)
    need = capf - jnp.sum(gt, axis=1, keepdims=True)
    tri = tri_ref[...]
    eq_before = _prefix_incl(eq, tri) - eq
    sel = gt + eq * jnp.where(eq_before < need, 1.0, 0.0)
    slot = _prefix_incl(sel, tri).astype(jnp.int32) - 1
    tok = lax.broadcasted_iota(jnp.int32, (rows, n), 1)
    disp = jnp.where(sel > 0.0, tok - slot, -1)
    gate = aff
    for bit in range((n - 1).bit_length()):
        step = 1 << bit
        disp_in = pltpu.roll(disp, n - step, 1)
        gate_in = pltpu.roll(gate, n - step, 1)
        move_in = jnp.logical_and(disp_in >= 0, ((disp_in >> bit) & 1) == 1)
        stay = jnp.logical_and(disp >= 0, ((disp >> bit) & 1) == 0)
        gate = jnp.where(move_in, gate_in, gate)
        disp = jnp.where(move_in, disp_in, jnp.where(stay, disp, -1))
    idx_ref[...] = (tok + disp)[:, :cap].reshape(g, ne, cap)
    gate_ref[...] = gate[:, :cap].reshape(g, ne, cap)


def _topk_call(aff_t, tri, *, cap):
    b, e, n = aff_t.shape
    g = math.gcd(b, max(1, 16384 // n))
    return pl.pallas_call(
        functools.partial(_topk_kernel, cap=cap),
        grid=(b // g,),
        in_specs=[
            pl.BlockSpec((g, e, n), lambda i: (i, 0, 0)),
            pl.BlockSpec((LANES, LANES), lambda i: (0, 0)),
        ],
        out_specs=[pl.BlockSpec((g, e, cap), lambda i: (i, 0, 0))] * 2,
        out_shape=[jax.ShapeDtypeStruct((b, e, cap), jnp.int32), jax.ShapeDtypeStruct((b, e, cap), F32)],
        compiler_params=_cparams(1, VMEM_LIMIT),
        name="topk",
    )(aff_t, tri)


def _moe_kernel(idx_a_ref, idx_c_ref, gate_c_ref, h3_ref, wg_ref, wu_ref, wd_ref, o_ref, xs_0, xs_1, y_0, y_1,
                *, cap, p, c8, ne, n_tiles):
    g = pl.program_id(0)
    valid_a = g < n_tiles
    valid_b = jnp.logical_and(g >= 1, g <= n_tiles)
    valid_c = g >= 2
    steady = jnp.logical_and(g >= 2, g < n_tiles)

    @pl.when(jnp.logical_and(valid_c, (g - 2) % ne == 0))
    def _():
        o_ref[...] = jnp.zeros(o_ref.shape, o_ref.dtype)

    unroll, group = 8, 4

    def gather_row(xs_a, s, dst):
        t = idx_a_ref[0, 0, s]
        xs_a[dst, :] = h3_ref[0, pl.ds(pl.multiple_of(t * p, p), p), :]

    def gather(xs_a, straight):
        if straight:
            for s in range(cap):
                gather_row(xs_a, s, pl.ds(s * p, p))
        else:
            def body(i, carry):
                for u in range(unroll):
                    s = i * unroll + u
                    gather_row(xs_a, s, pl.ds(pl.multiple_of(s * p, p), p))
                return carry
            lax.fori_loop(0, cap // unroll, body, 0)

    def scatter_rows(y_c, s_list, src_list):
        sums = []
        for s, src in zip(s_list, src_list):
            t = idx_c_ref[0, 0, s]
            rows = pl.ds(pl.multiple_of(t * c8, c8), c8)
            sums.append((rows, o_ref[0, rows, :] + y_c[src, :] * gate_c_ref[0, 0, s]))
        for rows, val in sums:
            o_ref[0, rows, :] = val

    def scatter(y_c, straight):
        if straight:
            for g0 in range(0, cap, group):
                ss = [g0 + u for u in range(group)]
                scatter_rows(y_c, ss, [pl.ds(s * c8, c8) for s in ss])
        else:
            def body(i, carry):
                ss = [i * group + u for u in range(group)]
                scatter_rows(y_c, ss, [pl.ds(pl.multiple_of(s * c8, c8), c8) for s in ss])
                return carry
            lax.fori_loop(0, cap // group, body, 0)

    def load_rows(xs_b):
        lo, hi = [], []
        for j in range(p):
            w = xs_b[pl.ds(j, cap, stride=p), :]
            lo.append(pltpu.bitcast(w << 16, F32).astype(BF16))
            hi.append(pltpu.bitcast(w & jnp.uint32(0xFFFF0000), F32).astype(BF16))
        return jnp.concatenate(lo + hi, axis=1)

    def expert(xs, y_b):
        rc = min(MOE_ROW_CHUNK, cap)
        chunks = range(0, cap, rc)
        au = [(_dot(xs[r0:r0 + rc], wg_ref[0, 0]), _dot(xs[r0:r0 + rc], wu_ref[0, 0])) for r0 in chunks]
        for r0, (a, u_) in zip(chunks, au):
            y = _dot((_silu(a) * u_).astype(BF16), wd_ref[0, 0])
            for j in range(c8):
                y_b[pl.ds(r0 * c8 + j, rc, stride=c8), :] = y[:, LANES * j:LANES * (j + 1)]

    def step(xs_a, xs_b, y_b, y_c):
        @pl.when(steady)
        def _():
            xs = load_rows(xs_b)
            gather(xs_a, True)
            scatter(y_c, True)
            expert(xs, y_b)

        @pl.when(jnp.logical_not(steady))
        def _():
            @pl.when(valid_b)
            def _():
                expert(load_rows(xs_b), y_b)

            @pl.when(valid_a)
            def _():
                gather(xs_a, False)

            @pl.when(valid_c)
            def _():
                scatter(y_c, False)

    @pl.when(g % 2 == 0)
    def _():
        step(xs_0, xs_1, y_1, y_0)

    @pl.when(g % 2 == 1)
    def _():
        step(xs_1, xs_0, y_0, y_1)


def _moe_call(idx, h3, gate, wg, wu, wd, layer):
    bm, ne, cap = idx.shape
    d = wg.shape[2]
    p = d // 2 // LANES
    c8 = d // LANES
    n = h3.shape[1] // p
    n_tiles = bm * ne
    idx3 = idx.reshape(n_tiles, 1, cap)
    gate3 = gate.reshape(n_tiles, 1, cap)
    tile_a = lambda i: jnp.minimum(i, n_tiles - 1)
    tile_b = lambda i: jnp.clip(i - 1, 0, n_tiles - 1)
    tile_c = lambda i: jnp.clip(i - 2, 0, n_tiles - 1)
    wmap = lambda i: (layer, tile_b(i) % ne, 0, 0)
    smem = lambda f: pl.BlockSpec((1, 1, cap), lambda i: (f(i), 0, 0), memory_space=pltpu.SMEM)
    return pl.pallas_call(
        functools.partial(_moe_kernel, cap=cap, p=p, c8=c8, ne=ne, n_tiles=n_tiles),
        grid=(n_tiles + 2,),
        in_specs=[
            smem(tile_a), smem(tile_c), smem(tile_c),
            pl.BlockSpec((1, n * p, LANES), lambda i: (tile_a(i) // ne, 0, 0)),
            pl.BlockSpec((1, 1, d, wg.shape[3]), wmap),
            pl.BlockSpec((1, 1, d, wu.shape[3]), wmap),
            pl.BlockSpec((1, 1, wd.shape[2], d), wmap),
        ],
        out_specs=pl.BlockSpec((1, n * c8, LANES), lambda i: (tile_c(i) // ne, 0, 0), pipeline_mode=pl.Buffered(1)),
        out_shape=jax.ShapeDtypeStruct((bm, n * c8, LANES), F32),
        scratch_shapes=[pltpu.VMEM((cap * p, LANES), jnp.uint32), pltpu.VMEM((cap * p, LANES), jnp.uint32),
                        pltpu.VMEM((cap * c8, LANES), F32), pltpu.VMEM((cap * c8, LANES), F32)],
        compiler_params=_cparams(1, VMEM_LIMIT),
        name="moe",
    )(idx3, idx3, gate3, h3, wg, wu, wd)


def _resid_kernel(x_ref, moe_ref, mod_ref, o_ref):
    tm = x_ref.shape[1]
    sub = min(ROW_SUB, tm)
    for r0 in range(0, tm, sub):
        o_ref[0, r0:r0 + sub] = _gated_moe_residual(x_ref, moe_ref, mod_ref[0][5:6], r0, sub)


def _resid_call(x1, moe3, mod, *, tm):
    b, n, d = x1.shape
    mb = mod.shape[0]
    c8 = d // LANES
    mod_map = (lambda i, j: (i, 0, 0)) if mb > 1 else (lambda i, j: (0, 0, 0))
    tok = lambda i, j: (i, j, 0)
    return pl.pallas_call(
        _resid_kernel,
        grid=(b, n // tm),
        in_specs=[
            pl.BlockSpec((1, tm, d), tok),
            pl.BlockSpec((1, tm * c8, LANES), tok),
            pl.BlockSpec((1, 6, d), mod_map),
        ],
        out_specs=pl.BlockSpec((1, tm, d), tok),
        out_shape=jax.ShapeDtypeStruct((b, n, d), F32),
        compiler_params=_cparams(2, VMEM_LIMIT),
        name="resid",
    )(x1, moe3, mod)


def _rope_tables(n):
    t = jnp.arange(n)
    row = (t // GRID_W).astype(F32)
    col = (t % GRID_W).astype(F32)
    n_freq = HEAD_DIM // 4
    inv_freq = jnp.power(ROPE_BASE, -jnp.arange(n_freq, dtype=F32) / n_freq)
    ang = jnp.concatenate([row[:, None] * inv_freq, col[:, None] * inv_freq], axis=-1)
    cos, sin = jnp.cos(ang), jnp.sin(ang)
    cos_t = jnp.concatenate([cos, cos, cos, cos], axis=-1)
    sin_t = jnp.concatenate([-sin, sin, -sin, sin], axis=-1)
    return cos_t, sin_t


def _gain128(g):
    return jnp.concatenate([g, g]).reshape(1, LANES).astype(F32)


def _channel_mix(h_pack, aff_t, tri, weights, layer, *, flatten):
    wg, wu, wd = weights
    b, _, n = aff_t.shape
    cap = EC_CAPACITY_FACTOR * n // N_EXPERTS
    idx, gate = _topk_call(aff_t, tri, cap=cap)
    if flatten:
        idx = (idx + (jnp.arange(b, dtype=jnp.int32) * n)[:, None, None]).transpose(1, 0, 2).reshape(1, N_EXPERTS, b * cap)
        gate = gate.transpose(1, 0, 2).reshape(1, N_EXPERTS, b * cap)
        h_pack = h_pack.reshape(1, -1, LANES)
    return _moe_call(idx, h_pack, gate, wg, wu, wd, layer).reshape(b, -1, LANES)


def kernel(x, c, ctx, c_ctx, ada_w, ada_b, norm1_g, norm2_g, router_w, exp_w_gate, exp_w_up, exp_w_down, ev_w_in,
           ev_w_out, ev_q_gain, ev_k_gain, ev_sink, od_w_in, od_w_out, od_q_gain, od_k_gain, od_rpb):
    b, n, d = x.shape
    nc = ctx.shape[1]
    tm = min(512, n)
    tmb = min(1024, n)
    tmc = min(512, nc)

    rows = -(-(b + 1) // 8) * 8
    cs = jnp.concatenate([c, c_ctx[None], jnp.zeros((rows - b - 1, d), F32)], axis=0)
    mods = _mod_call(cs, ada_w, ada_b).reshape(ada_w.shape[0], rows, 6, d)

    cos_t, sin_t = _rope_tables(n)
    cs_dft = _channel_dft()
    tri = jnp.asarray(np.triu(np.ones((LANES, LANES), np.float32)), BF16)
    fw = FOURIER_GROUPS * LANES

    def router_split(l):
        rw = router_w[l].T
        rh = rw.astype(BF16)
        rl = (rw - rh.astype(F32)).astype(BF16)
        left = jnp.pad(jnp.concatenate([rh, rl], axis=0), ((0, LANES - 2 * N_EXPERTS), (0, 0)))
        right = jnp.pad(rh, ((0, LANES - N_EXPERTS), (0, 0)))
        return jnp.concatenate([left, right], axis=1)

    w_stacks = (exp_w_gate, exp_w_up, exp_w_down)
    as_layer = lambda ws: [w.reshape((1,) + s.shape[1:]) for w, s in zip(ws, w_stacks)]

    mod_x, mod_c = mods[0, :b], mods[0, b:b + 1]
    g1 = norm1_g[0].reshape(1, d)
    g2 = norm2_g[0].reshape(1, d)
    w_in = ev_w_in[0].astype(BF16)
    w_out = ev_w_out[0].astype(BF16)
    qg, kg = _gain128(ev_q_gain[0]), _gain128(ev_k_gain[0])
    sink = ev_sink[0].astype(F32)
    rw = router_split(0)

    ab_x, q_x, kd_x, vd_x = _premix_even_call(x, mod_x, g1, w_in, cs_dft, cos_t, sin_t, qg, kg, rope=True, tm=tmb)
    ab_c, q_c, kd_c, vd_c = _premix_even_call(ctx, mod_c, g1, w_in, cs_dft, cos_t[:nc], sin_t[:nc], qg, kg,
                                              rope=False, tm=tmc)
    a_x, *wts = _swa_call(sink, q_x, kd_x, vd_x, kd_c, vd_c, w_stacks, 0)
    wts = as_layer(wts)
    a_c = _ctx_attn_call(sink, q_c, kd_c, vd_c)
    four_x = _fourier_call(ab_x, *_dft_tables(n), tm=tmb)
    four_c = _fourier_call(ab_c, *_dft_tables(nc), tm=tmc)
    w_out_parts = [w_out[:fw], w_out[fw:]]
    x1, h3_x, aff_x = _postmix_call([four_x, a_x], w_out_parts, x, mod_x, g2, rw, tm=tmb)
    c1, h3_c, aff_c = _postmix_call([four_c, a_c], w_out_parts, ctx, mod_c, g2, rw, tm=tmc)
    moe_x = _channel_mix(h3_x, aff_x, tri, wts, 0, flatten=False)
    moe_c = _channel_mix(h3_c, aff_c, tri, wts, 0, flatten=True)

    mod_x0, mod_c0 = mod_x, mod_c
    mod_x, mod_c = mods[1, :b], mods[1, b:b + 1]
    g1 = norm1_g[1].reshape(1, d)
    g2 = norm2_g[1].reshape(1, d)
    w_in = od_w_in[0].astype(BF16)
    w_out = od_w_out[0].astype(BF16)
    qg, kg = _gain128(od_q_gain[0]), _gain128(od_k_gain[0])
    rw = router_split(1)

    q_x, k_x, v_x, x = _premix_odd_call(x1, moe_x, mod_x0, mod_x, g1, w_in, qg, kg, emit_x=True, tm=tmb)
    _, k_c, v_c = _premix_odd_call(c1, moe_c, mod_c0, mod_c, g1, w_in, qg, kg, emit_x=False, tm=tmc)
    a_x, *wts = _na_call(q_x, k_x, v_x, k_c, v_c, _na_bias_table(od_rpb[0]), w_stacks, 1)
    wts = as_layer(wts)
    x1, h3_x, aff_x = _postmix_call([a_x], [w_out], x, mod_x, g2, rw, tm=tmb)
    return _resid_call(x1, _channel_mix(h3_x, aff_x, tri, wts, 0, flatten=False), mod_x, tm=tmb)
```

```python
import functools
import math

import numpy as np
import jax
import jax.numpy as jnp
from jax import lax
from jax.experimental import pallas as pl
from jax.experimental.pallas import tpu as pltpu

GRID_W = 64
HEAD_DIM = 64
FOURIER_GROUPS = 4
SWA_WINDOW = 128
SWA_BLOCK = 128
NA_ROWS_MAX = 8
NA_COLS = 16
N_EXPERTS = 16
EC_CAPACITY_FACTOR = 2
ROPE_BASE = 10000.0
EPS = 1e-6

LANES = 128
NEG = -1e30
NA_QROWS = 4
NA_JBLK = 2
NA_KROWS = 12
DFT_SUB = 64
MOE_ROW_CHUNK = 128
SWA_QBLK = 8
ROW_SUB = 256
ONES_ROWS = 16
LOG2E = math.log2(math.e)
VMEM_LIMIT = 60 * 1024 * 1024

F32 = jnp.float32
BF16 = jnp.bfloat16


def _cparams(n_axes, vmem=None):
    return pltpu.CompilerParams(dimension_semantics=("arbitrary",) * n_axes, vmem_limit_bytes=vmem)


def _dot(a, b):
    return jnp.dot(a, b, preferred_element_type=F32)


def _dot_t(a, b):
    return lax.dot_general(a, b, (((1,), (1,)), ((), ())), preferred_element_type=F32)


def _split(a):
    hi = a.astype(BF16)
    lo = (a - hi.astype(F32)).astype(BF16)
    return hi, lo


def _dot3(a, b):
    ah, al = _split(a)
    bh, bl = _split(b)
    return _dot(ah, bh) + _dot(al, bh) + _dot(ah, bl)


def _silu(a):
    return a / (1.0 + jnp.exp(-a))


def _rms_mod(x, gain, shift, scale):
    ms = jnp.mean(x * x, axis=-1, keepdims=True)
    y = x * lax.rsqrt(ms + EPS) * gain
    return y * (1.0 + scale) + shift


def _mod_kernel(cs_ref, w_ref, b_ref, o_ref):
    o_ref[0] = _dot3(_silu(cs_ref[...]), w_ref[0]) + b_ref[0]


def _mod_call(cs, ada_w, ada_b):
    depth, d, d6 = ada_w.shape
    r = cs.shape[0]
    tn = 1024
    return pl.pallas_call(
        _mod_kernel,
        grid=(depth, d6 // tn),
        in_specs=[
            pl.BlockSpec((r, d), lambda l, j: (0, 0)),
            pl.BlockSpec((1, d, tn), lambda l, j: (l, 0, j)),
            pl.BlockSpec((1, 1, tn), lambda l, j: (l, 0, j)),
        ],
        out_specs=pl.BlockSpec((1, r, tn), lambda l, j: (l, 0, j)),
        out_shape=jax.ShapeDtypeStruct((depth, r, d6), F32),
        compiler_params=_cparams(2),
        name="mod",
    )(cs, ada_w, ada_b.reshape(depth, 1, d6))


def _head_norm(t, gain, lane):
    t2 = t * t
    s_lo = jnp.sum(jnp.where(lane < HEAD_DIM, t2, 0.0), axis=-1, keepdims=True)
    s_all = jnp.sum(t2, axis=-1, keepdims=True)
    ms = jnp.where(lane < HEAD_DIM, s_lo, s_all - s_lo) * (1.0 / HEAD_DIM)
    return t * lax.rsqrt(ms + EPS) * gain


def _rope(t, cos_t, sin_t, lane):
    partner = jnp.where((lane % HEAD_DIM) < HEAD_DIM // 2, pltpu.roll(t, LANES - HEAD_DIM // 2, 1),
                        pltpu.roll(t, HEAD_DIM // 2, 1))
    return t * cos_t + partner * sin_t


def _dup_halves(t, lane):
    sw = pltpu.roll(t, HEAD_DIM, 1)
    return jnp.where(lane < HEAD_DIM, t, sw), jnp.where(lane < HEAD_DIM, sw, t)


def _premix_even_kernel(x_ref, mod_ref, g_ref, w_ref, cs_ref, cos_ref, sin_ref, qg_ref, kg_ref,
                        ab_ref, q_ref, kd_ref, vt_ref, *, rope):
    tm = x_ref.shape[1]
    sub = min(ROW_SUB, tm)
    m = mod_ref[0]
    lane = lax.broadcasted_iota(jnp.int32, (sub, LANES), 1)
    fw = FOURIER_GROUPS * LANES
    qw = q_ref.shape[2]
    csb = cs_ref[...].astype(BF16)
    for r0 in range(0, tm, sub):
        rs = slice(r0, r0 + sub)
        hb = _rms_mod(x_ref[0, rs], g_ref[...], m[0:1], m[1:2]).astype(BF16)
        pf = _dot(hb, w_ref[:, 0:fw])
        for g in range(FOURIER_GROUPS):
            ab = _dot(pf[:, LANES * g:LANES * (g + 1)].astype(BF16), csb)
            ab_ref[0, rs, LANES * g:LANES * (g + 1)] = ab[:, :LANES].astype(BF16)
            ab_ref[0, rs, fw + LANES * g:fw + LANES * (g + 1)] = ab[:, LANES:].astype(BF16)
        pq = _dot(hb, w_ref[:, fw:fw + qw])
        if rope:
            cos_t = cos_ref[rs]
            sin_t = sin_ref[rs]
        for c in range(qw // LANES):
            t = _head_norm(pq[:, LANES * c:LANES * (c + 1)], qg_ref[...], lane)
            if rope:
                t = _rope(t, cos_t, sin_t, lane)
            q_ref[0, rs, LANES * c:LANES * (c + 1)] = t.astype(BF16)
        pk = _dot(hb, w_ref[:, fw + qw:fw + qw + 2 * LANES])
        k = _head_norm(pk[:, :LANES], kg_ref[...], lane)
        if rope:
            k = _rope(k, cos_t, sin_t, lane)
        k0, k1 = _dup_halves(k, lane)
        kd_ref[0, rs, :LANES] = k0.astype(BF16)
        kd_ref[0, rs, LANES:] = k1.astype(BF16)
        vt_ref[0, :, rs] = pk[:, LANES:].T.astype(BF16)


def _premix_even_call(x, mod, gain, w_bf, cs, cos_t, sin_t, qg, kg, *, rope, tm):
    b, n, d = x.shape
    mb = mod.shape[0]
    wtot = w_bf.shape[1]
    fw = FOURIER_GROUPS * LANES
    qw = wtot - fw - 2 * LANES
    mod_map = (lambda i, j: (i, 0, 0)) if mb > 1 else (lambda i, j: (0, 0, 0))
    tok = lambda i, j: (i, j, 0)
    const2 = lambda i, j: (0, 0)
    return pl.pallas_call(
        functools.partial(_premix_even_kernel, rope=rope),
        grid=(b, n // tm),
        in_specs=[
            pl.BlockSpec((1, tm, d), tok),
            pl.BlockSpec((1, 6, d), mod_map),
            pl.BlockSpec((1, d), const2),
            pl.BlockSpec((d, wtot), const2),
            pl.BlockSpec((LANES, 2 * LANES), const2),
            pl.BlockSpec((tm, LANES), lambda i, j: (j, 0)),
            pl.BlockSpec((tm, LANES), lambda i, j: (j, 0)),
            pl.BlockSpec((1, LANES), const2),
            pl.BlockSpec((1, LANES), const2),
        ],
        out_specs=[
            pl.BlockSpec((1, tm, 2 * fw), tok),
            pl.BlockSpec((1, tm, qw), tok),
            pl.BlockSpec((1, tm, 2 * LANES), tok),
            pl.BlockSpec((1, LANES, tm), lambda i, j: (i, 0, j)),
        ],
        out_shape=[
            jax.ShapeDtypeStruct((b, n, 2 * fw), BF16),
            jax.ShapeDtypeStruct((b, n, qw), BF16),
            jax.ShapeDtypeStruct((b, n, 2 * LANES), BF16),
            jax.ShapeDtypeStruct((b, LANES, n), BF16),
        ],
        compiler_params=_cparams(2, VMEM_LIMIT),
        name="premix_even",
    )(x, mod, gain, w_bf, cs, cos_t, sin_t, qg, kg)


def _gated_moe_residual(x_ref, moe_ref, gate_row, r0, sub):
    c8 = x_ref.shape[2] // LANES
    cols = []
    for j in range(c8):
        cs = slice(LANES * j, LANES * (j + 1))
        cols.append(x_ref[0, r0:r0 + sub, cs] + gate_row[:, cs] * moe_ref[0, pl.ds(r0 * c8 + j, sub, stride=c8), :])
    return jnp.concatenate(cols, axis=1)


def _premix_odd_kernel(x_ref, moe_ref, modp_ref, mod_ref, g_ref, w_ref, qg_ref, kg_ref, q_ref, k_ref, v_ref,
                       *xo_ref):
    tm = x_ref.shape[1]
    sub = min(ROW_SUB, tm)
    m = mod_ref[0]
    gate_prev = modp_ref[0][5:6]
    lane = lax.broadcasted_iota(jnp.int32, (sub, LANES), 1)
    wq = q_ref.shape[2]
    chunk = 4 * LANES
    for r0 in range(0, tm, sub):
        rs = slice(r0, r0 + sub)
        x = _gated_moe_residual(x_ref, moe_ref, gate_prev, r0, sub)
        if xo_ref:
            xo_ref[0][0, rs] = x
        hb = _rms_mod(x, g_ref[...], m[0:1], m[1:2]).astype(BF16)
        for c0 in range(0, 3 * wq, chunk):
            p = _dot(hb, w_ref[:, c0:c0 + chunk])
            for cc in range(chunk // LANES):
                col = c0 + cc * LANES
                t = p[:, cc * LANES:(cc + 1) * LANES]
                if col < wq:
                    q_ref[0, rs, col:col + LANES] = _head_norm(t, qg_ref[...], lane).astype(BF16)
                elif col < 2 * wq:
                    k_ref[0, rs, col - wq:col - wq + LANES] = _head_norm(t, kg_ref[...], lane).astype(BF16)
                else:
                    v_ref[0, col - 2 * wq:col - 2 * wq + LANES, rs] = t.T.astype(BF16)


def _premix_odd_call(x1, moe3, mod_prev, mod, gain, w_bf, qg, kg, *, emit_x, tm):
    b, n, d = x1.shape
    mb = mod.shape[0]
    wq = w_bf.shape[1] // 3
    c8 = d // LANES
    mod_map = (lambda i, j: (i, 0, 0)) if mb > 1 else (lambda i, j: (0, 0, 0))
    tok = lambda i, j: (i, j, 0)
    const2 = lambda i, j: (0, 0)
    out_specs = [pl.BlockSpec((1, tm, wq), tok)] * 2 + [pl.BlockSpec((1, wq, tm), lambda i, j: (i, 0, j))]
    out_shape = [jax.ShapeDtypeStruct((b, n, wq), BF16)] * 2 + [jax.ShapeDtypeStruct((b, wq, n), BF16)]
    if emit_x:
        out_specs.append(pl.BlockSpec((1, tm, d), tok))
        out_shape.append(jax.ShapeDtypeStruct((b, n, d), F32))
    return pl.pallas_call(
        _premix_odd_kernel,
        grid=(b, n // tm),
        in_specs=[
            pl.BlockSpec((1, tm, d), tok),
            pl.BlockSpec((1, tm * c8, LANES), tok),
            pl.BlockSpec((1, 6, d), mod_map),
            pl.BlockSpec((1, 6, d), mod_map),
            pl.BlockSpec((1, d), const2),
            pl.BlockSpec((d, 3 * wq), const2),
            pl.BlockSpec((1, LANES), const2),
            pl.BlockSpec((1, LANES), const2),
        ],
        out_specs=out_specs,
        out_shape=out_shape,
        compiler_params=_cparams(2, VMEM_LIMIT),
        name="premix_odd",
    )(x1, moe3, mod_prev, mod, gain, w_bf, qg, kg)


def _col_reduce(x, op, slab=64):
    r = x.shape[0]
    if r > slab and r % slab == 0:
        x = op(x.reshape(r // slab, slab, x.shape[1]), axis=0)
    return op(x, axis=0, keepdims=True)


def _softmax_pv_t(s_t, v_t, sink=None):
    mx = _col_reduce(s_t, jnp.max)
    if sink is not None:
        mx = jnp.maximum(mx, sink)
    p = jnp.exp2((s_t - mx).astype(BF16))
    nd = v_t.shape[0]
    o = _dot(jnp.concatenate([v_t, jnp.ones((ONES_ROWS, v_t.shape[1]), BF16)], axis=0), p)
    den = o[nd:nd + 1]
    if sink is not None:
        den = den + jnp.exp2(sink - mx)
    return o[:nd] / den


def _mask_half(qc, lane, half):
    q32 = qc.astype(F32) * (HEAD_DIM ** -0.5 * LOG2E)
    keep = (lane < HEAD_DIM) if half == 0 else (lane >= HEAD_DIM)
    return jnp.where(keep, q32, 0.0).astype(BF16)


def _gqa_scores(q_ref, h, kd, n_kv, masks, lane, r0, tq):
    rs = slice(r0, r0 + tq)
    cols_per_kv = q_ref.shape[2] // LANES // n_kv
    q_rows = []
    for cc in range(cols_per_kv):
        c = cols_per_kv * h + cc
        for half in range(2):
            q_rows.append(_mask_half(q_ref[0, rs, LANES * c:LANES * (c + 1)], lane, half))
    qm = jnp.concatenate(q_rows, axis=0)
    s = _dot_t(kd, qm)
    pieces, k0 = [], 0
    for rows, mask in masks:
        blk = s[k0:k0 + rows]
        pieces.append(blk if mask is None else jnp.where(mask, blk, NEG))
        k0 += rows
    return jnp.concatenate(pieces, axis=0) if len(pieces) > 1 else pieces[0]


def _gqa_finish(sink_ref, o_ref, h, s_t, v_t, r0, tq):
    rs = slice(r0, r0 + tq)
    n_kv = v_t.shape[0] // HEAD_DIM
    cols_per_kv = o_ref.shape[2] // LANES // n_kv
    heads = 2 * cols_per_kv
    head_of_lane = lax.broadcasted_iota(jnp.int32, (1, heads * tq), 1) // tq
    sink = jnp.zeros((1, heads * tq), F32)
    for a in range(heads):
        sink = jnp.where(head_of_lane == a, sink_ref[heads * h + a] * LOG2E, sink)
    o_t = _softmax_pv_t(s_t, v_t, sink)[HEAD_DIM * h:HEAD_DIM * (h + 1)]
    for cc in range(cols_per_kv):
        c = cols_per_kv * h + cc
        tile = jnp.concatenate([o_t[:, tq * (2 * cc):tq * (2 * cc + 1)], o_t[:, tq * (2 * cc + 1):tq * (2 * cc + 2)]],
                               axis=0)
        o_ref[0, rs, LANES * c:LANES * (c + 1)] = tile.T.astype(BF16)


def _swa_kernel(sink_ref, q_ref, km_ref, k0_ref, kp_ref, vm_ref, v0_ref, vp_ref, kc_ref, vc_ref, *rest, n_step):
    o_ref = _ride_cast(rest)
    g = pl.program_id(1)
    tq = SWA_BLOCK
    nc = kc_ref.shape[1]
    n_kv = kc_ref.shape[2] // LANES
    heads = q_ref.shape[2] // HEAD_DIM // n_kv
    lane = lax.broadcasted_iota(jnp.int32, (tq, LANES), 1)
    jk = lax.broadcasted_iota(jnp.int32, (SWA_BLOCK, heads * tq), 0)
    iq = lax.broadcasted_iota(jnp.int32, (SWA_BLOCK, heads * tq), 1) % tq
    below, above = jk >= iq, jk <= iq
    nq = q_ref.shape[1] // SWA_BLOCK
    blk = lambda s: slice(SWA_BLOCK * s, SWA_BLOCK * (s + 1))

    def key_block(s, hs):
        return km_ref[0, :, hs] if s < 0 else kp_ref[0, :, hs] if s == nq else k0_ref[0, blk(s), hs]

    def value_block(s):
        return vm_ref[0] if s < 0 else vp_ref[0] if s == nq else v0_ref[0, :, blk(s)]

    def scores(sub, h):
        hs = slice(LANES * h, LANES * (h + 1))
        first = jnp.logical_and(below, g > 0) if sub == 0 else below
        last = jnp.logical_and(above, g < n_step - 1) if sub == nq - 1 else above
        masks = [(nc, None), (SWA_BLOCK, first), (SWA_BLOCK, None), (SWA_BLOCK, last)]
        kd = jnp.concatenate([kc_ref[0, :, hs]] + [key_block(s, hs) for s in (sub - 1, sub, sub + 1)], axis=0)
        return _gqa_scores(q_ref, h, kd, n_kv, masks, lane, SWA_BLOCK * sub, tq)

    groups = [(sub, h) for sub in range(nq) for h in range(n_kv)]
    ahead = 3
    pending = [scores(*grp) for grp in groups[:ahead]]
    for i, (sub, h) in enumerate(groups):
        s_t = pending.pop(0)
        if i + ahead < len(groups):
            pending.append(scores(*groups[i + ahead]))
        v_t = jnp.concatenate([vc_ref[0]] + [value_block(s) for s in (sub - 1, sub, sub + 1)], axis=1)
        _gqa_finish(sink_ref, o_ref, h, s_t, v_t, SWA_BLOCK * sub, tq)


def _cast_riders(ws, layer, total, flat):
    ins, outs, shapes, args = [], [], [], []
    for w in ws:
        l, e, r, c = w.shape
        rows = e * r // total
        ins.append(pl.BlockSpec((rows, c), lambda i, j: (layer * total + flat(i, j), 0)))
        outs.append(pl.BlockSpec((rows, c), lambda i, j: (flat(i, j), 0)))
        shapes.append(jax.ShapeDtypeStruct((e * r, c), BF16))
        args.append(w.reshape(l * e * r, c))
    return ins, outs, shapes, args


def _ride_cast(rest):
    n_cast = (len(rest) - 1) // 2
    for w_ref, c_ref in zip(rest[:n_cast], rest[n_cast + 1:]):
        c_ref[...] = w_ref[...].astype(BF16)
    return rest[n_cast]


def _swa_call(sink, q, kd, vt, kdc, vtc, ws, layer):
    b, n, qw = q.shape
    nc = kdc.shape[1]
    kw = kd.shape[2]
    vw = vt.shape[1]
    n_blk = n // SWA_BLOCK
    nq = min(SWA_QBLK, n_blk)
    n_step = n_blk // nq
    prev = lambda j: jnp.maximum(nq * j - 1, 0)
    nxt = lambda j: jnp.minimum(nq * j + nq, n_blk - 1)
    kspec = lambda f: pl.BlockSpec((1, SWA_BLOCK, kw), lambda i, j: (i, f(j), 0))
    vspec = lambda f: pl.BlockSpec((1, vw, SWA_BLOCK), lambda i, j: (i, 0, f(j)))
    c_in, c_out, c_shape, c_args = _cast_riders(ws, layer, b * n_step, lambda i, j: i * n_step + j)
    return pl.pallas_call(
        functools.partial(_swa_kernel, n_step=n_step),
        grid=(b, n_step),
        in_specs=[
            pl.BlockSpec(memory_space=pltpu.SMEM),
            pl.BlockSpec((1, nq * SWA_BLOCK, qw), lambda i, j: (i, j, 0)),
            kspec(prev), pl.BlockSpec((1, nq * SWA_BLOCK, kw), lambda i, j: (i, j, 0)), kspec(nxt),
            vspec(prev), pl.BlockSpec((1, vw, nq * SWA_BLOCK), lambda i, j: (i, 0, j)), vspec(nxt),
            pl.BlockSpec((1, nc, kw), lambda i, j: (i, 0, 0)),
            pl.BlockSpec((1, vw, nc), lambda i, j: (i, 0, 0)),
        ] + c_in,
        out_specs=[pl.BlockSpec((1, nq * SWA_BLOCK, qw), lambda i, j: (i, j, 0))] + c_out,
        out_shape=[jax.ShapeDtypeStruct((b, n, qw), BF16)] + c_shape,
        compiler_params=_cparams(2, VMEM_LIMIT),
        name="swa",
    )(sink, q, kd, kd, kd, vt, vt, vt, kdc, vtc, *c_args)


def _ctx_attn_kernel(sink_ref, q_ref, kc_ref, vc_ref, o_ref):
    tq = q_ref.shape[1]
    lane = lax.broadcasted_iota(jnp.int32, (tq, LANES), 1)
    n_kv = kc_ref.shape[2] // LANES
    scores = [_gqa_scores(q_ref, h, kc_ref[0, :, LANES * h:LANES * (h + 1)], n_kv, [(kc_ref.shape[1], None)], lane,
                          0, tq) for h in range(n_kv)]
    for h, s_t in enumerate(scores):
        _gqa_finish(sink_ref, o_ref, h, s_t, vc_ref[0], 0, tq)


def _ctx_attn_call(sink, q, kd, vt):
    b, nc, qw = q.shape
    kw = kd.shape[2]
    vw = vt.shape[1]
    m = lambda i: (i, 0, 0)
    return pl.pallas_call(
        _ctx_attn_kernel,
        grid=(b,),
        in_specs=[
            pl.BlockSpec(memory_space=pltpu.SMEM),
            pl.BlockSpec((1, nc, qw), m),
            pl.BlockSpec((1, nc, kw), m),
            pl.BlockSpec((1, vw, nc), m),
        ],
        out_specs=pl.BlockSpec((1, nc, qw), m),
        out_shape=jax.ShapeDtypeStruct((b, nc, qw), BF16),
        compiler_params=_cparams(1, VMEM_LIMIT),
        name="ctx_attn",
    )(sink, q, kd, vt)


def _na_kernel(q_ref, k0_ref, k1_ref, k2_ref, k3_ref, v0_ref, v1_ref, v2_ref, v3_ref, kc_ref, vc_ref, tb_ref, *rest,
               n_j, rows):
    o_ref = _ride_cast(rest)
    g = pl.program_id(1)
    tq = NA_QROWS * GRID_W
    tk = NA_KROWS * GRID_W
    nc = kc_ref.shape[1]
    k_blocks = [k0_ref, k1_ref, k2_ref, k3_ref]
    v_blocks = [v0_ref, v1_ref, v2_ref, v3_ref]
    base = jnp.clip(NA_JBLK * g - 1, 0, n_j - (NA_JBLK + 2))
    lane = lax.broadcasted_iota(jnp.int32, (tq, LANES), 1)
    k_row = (lax.broadcasted_iota(jnp.int32, (nc + tk, LANES), 0) - nc) // GRID_W
    k_lane = lax.broadcasted_iota(jnp.int32, (nc + tk, LANES), 1)
    is_local = lax.broadcasted_iota(jnp.int32, (nc + tk, LANES), 0) >= nc
    q_grid_row = (lax.broadcasted_iota(jnp.int32, (2 * tq, LANES), 0) % tq) // GRID_W
    q_extra = jnp.where(lax.broadcasted_iota(jnp.int32, (2 * tq, LANES), 1) == q_grid_row, 1.0, 0.0).astype(BF16)
    n_dr = 2 * NA_ROWS_MAX - 1
    shifted, k_extras, deltas = [], [], []
    for u in range(NA_JBLK):
        j = NA_JBLK * g + u
        w0_blk = jnp.clip(j - 1, 0, n_j - 3)
        w0 = NA_QROWS * w0_blk
        lo = jnp.zeros((nc + tk, LANES), jnp.int32)
        for rr in range(NA_QROWS):
            r = NA_QROWS * j + rr
            lo_rr = jnp.clip(r - NA_ROWS_MAX // 2, 0, rows - NA_ROWS_MAX) - w0
            lo = jnp.where(k_lane == rr, lo_rr, lo)
        outside = jnp.logical_or(k_row < lo, k_row >= lo + NA_ROWS_MAX)
        k_extras.append(jnp.where(jnp.logical_and(jnp.logical_and(is_local, k_lane < NA_QROWS), outside), NEG, 0.0)
                        .astype(BF16))
        shifted.append(w0_blk - base == 1)
        deltas.append(w0 - NA_QROWS * j)

    def scores(u, c):
        cs = slice(LANES * c, LANES * (c + 1))
        qc = q_ref[0, tq * u:tq * (u + 1), cs]
        k_loc = [jnp.where(shifted[u], k_blocks[d + 1][0, :, cs], k_blocks[d][0, :, cs]) for d in range(3)]
        kl = jnp.concatenate([kc_ref[0, :, cs]] + k_loc, axis=0)
        qm = jnp.concatenate([_mask_half(qc, lane, 0), _mask_half(qc, lane, 1)], axis=0)
        return _dot_t(jnp.concatenate([kl, k_extras[u]], axis=1), jnp.concatenate([qm, q_extra], axis=1))

    groups = [(u, c) for c in range(q_ref.shape[2] // LANES) for u in range(NA_JBLK)]
    ahead = 2
    pending = [scores(*grp) for grp in groups[:ahead]]
    for i, (u, c) in enumerate(groups):
        cs = slice(LANES * c, LANES * (c + 1))
        s = pending.pop(0)
        if i + ahead < len(groups):
            pending.append(scores(*groups[i + ahead]))
        v_loc = [jnp.where(shifted[u], v_blocks[d + 1][0, cs, :], v_blocks[d][0, cs, :]) for d in range(3)]
        v_t = jnp.concatenate([vc_ref[0, cs, :]] + v_loc, axis=1)
        bias_rows = []
        for kr in range(NA_KROWS):
            pieces = []
            for half in range(2):
                for t in range(NA_QROWS // 2):
                    dr = deltas[u] + kr - 2 * t + NA_ROWS_MAX - 1
                    pieces.append(tb_ref[2 * c + half, jnp.clip(dr, 0, n_dr)])
            bias_rows.append(jnp.concatenate(pieces, axis=1))
        bias = jnp.concatenate(bias_rows, axis=0)
        s_t = jnp.concatenate([s[:nc], s[nc:] + bias], axis=0)
        o_t = _softmax_pv_t(s_t, v_t)
        tile = jnp.concatenate([o_t[:HEAD_DIM, :tq], o_t[HEAD_DIM:, tq:]], axis=0)
        o_ref[0, tq * u:tq * (u + 1), cs] = tile.T.astype(BF16)


def _na_call(q, k, vt, kc, vtc, tb, ws, layer):
    b, n, w = q.shape
    nc = kc.shape[1]
    rows = n // GRID_W
    n_j = rows // NA_QROWS
    n_g = n_j // NA_JBLK
    tq = NA_QROWS * GRID_W
    cur = lambda i, j: (i, j, 0)
    first = lambda j: jnp.clip(NA_JBLK * j - 1, 0, n_j - (NA_JBLK + 2))
    kspec = lambda d: pl.BlockSpec((1, tq, w), lambda i, j: (i, first(j) + d, 0))
    vspec = lambda d: pl.BlockSpec((1, w, tq), lambda i, j: (i, 0, first(j) + d))
    c_in, c_out, c_shape, c_args = _cast_riders(ws, layer, b * n_g, lambda i, j: i * n_g + j)
    return pl.pallas_call(
        functools.partial(_na_kernel, n_j=n_j, rows=rows),
        grid=(b, n_g),
        in_specs=[
            pl.BlockSpec((1, NA_JBLK * tq, w), cur),
            kspec(0), kspec(1), kspec(2), kspec(3),
            vspec(0), vspec(1), vspec(2), vspec(3),
            pl.BlockSpec((1, nc, w), lambda i, j: (i, 0, 0)),
            pl.BlockSpec((1, w, nc), lambda i, j: (i, 0, 0)),
            pl.BlockSpec(tb.shape, lambda i, j: (0, 0, 0, 0)),
        ] + c_in,
        out_specs=[pl.BlockSpec((1, NA_JBLK * tq, w), cur)] + c_out,
        out_shape=[jax.ShapeDtypeStruct((b, n, w), BF16)] + c_shape,
        compiler_params=_cparams(2, VMEM_LIMIT),
        name="na",
    )(q, k, k, k, k, vt, vt, vt, vt, kc, vtc, tb, *c_args)


def _na_bias_table(rpb):
    col_q = np.arange(GRID_W)[None, :]
    col_k = np.arange(GRID_W)[:, None]
    c_start = np.clip(col_q - NA_COLS // 2, 0, GRID_W - NA_COLS)
    col_valid = (col_k >= c_start) & (col_k < c_start + NA_COLS)
    dc_idx = np.clip(col_k - col_q + NA_COLS - 1, 0, 2 * NA_COLS - 2)
    t = jnp.where(col_valid[None, None], (rpb.astype(F32) * LOG2E)[:, :, dc_idx], NEG)
    pad = jnp.full_like(t[:, :1], NEG)
    t = jnp.concatenate([pad, t, pad], axis=1)
    return jnp.concatenate([t[:, 1:], t[:, :-1]], axis=-1)


def _fourier_kernel(ab_ref, ca_ref, sa_ref, cb_ref, sb_ref, rev_ref, o_ref, fold_ref, rev_scr):
    n = ab_ref.shape[1]
    nh = n // 2
    tm, fw = o_ref.shape[1], o_ref.shape[2]

    @pl.when(pl.program_id(1) == 0)
    def _():
        for m in range(nh // LANES):
            blk = ab_ref[0, n - LANES * (m + 1):n - LANES * m, :]
            rev_scr[LANES * m:LANES * (m + 1), :] = _dot(rev_ref[...], blk)
        ck = min(256, nh)
        row = lax.broadcasted_iota(jnp.int32, (ck, 1), 0)
        for r0 in range(0, nh, ck):
            prev = (r0 - 1) % nh
            shifted = jnp.where(row == 0, rev_scr[prev:prev + 1, :], pltpu.roll(rev_scr[r0:r0 + ck, :], 1, 0))
            lo = ab_ref[0, r0:r0 + ck, :].astype(F32)
            a2 = lo[:, :fw] + shifted[:, :fw]
            b2 = lo[:, fw:] - shifted[:, fw:]
            if r0 == 0:
                a2 = jnp.where(row == 0, lo[:, :fw], a2)
                b2 = jnp.where(row == 0, shifted[:, :fw], b2)
            fold_ref[r0:r0 + ck, :fw] = a2.astype(BF16)
            fold_ref[r0:r0 + ck, fw:] = b2.astype(BF16)

    cb = cb_ref[...]
    sb = sb_ref[...]
    sub = min(ROW_SUB, tm)
    for r0 in range(0, tm, sub):
        cm, sm = [], []
        for a in range(r0 // DFT_SUB, (r0 + sub) // DFT_SUB):
            ca = ca_ref[a:a + 1, :]
            sa = sa_ref[a:a + 1, :]
            cm.append((ca * cb - sa * sb).astype(BF16))
            sm.append((sa * cb + ca * sb).astype(BF16))
        y = _dot(jnp.concatenate(cm, axis=0), fold_ref[:, :fw]) - _dot(jnp.concatenate(sm, axis=0), fold_ref[:, fw:])
        o_ref[0, r0:r0 + sub] = y.astype(BF16)


def _fourier_call(ab, ca, sa, cb, sb, *, tm):
    b, n, w2 = ab.shape
    nh = n // 2
    fw = w2 // 2
    na = tm // DFT_SUB
    rev = jnp.asarray(np.eye(LANES, dtype=np.float32)[::-1], BF16)
    return pl.pallas_call(
        _fourier_kernel,
        grid=(b, n // tm),
        in_specs=[
            pl.BlockSpec((1, n, w2), lambda i, j: (i, 0, 0)),
            pl.BlockSpec((na, nh), lambda i, j: (j, 0)),
            pl.BlockSpec((na, nh), lambda i, j: (j, 0)),
            pl.BlockSpec((DFT_SUB, nh), lambda i, j: (0, 0)),
            pl.BlockSpec((DFT_SUB, nh), lambda i, j: (0, 0)),
            pl.BlockSpec((LANES, LANES), lambda i, j: (0, 0)),
        ],
        out_specs=pl.BlockSpec((1, tm, fw), lambda i, j: (i, j, 0)),
        out_shape=jax.ShapeDtypeStruct((b, n, fw), BF16),
        scratch_shapes=[pltpu.VMEM((nh, w2), BF16), pltpu.VMEM((nh, w2), F32)],
        compiler_params=_cparams(2, VMEM_LIMIT),
        name="fourier",
    )(ab, ca, sa, cb, sb, rev)


def _dft_tables(n):
    nh = n // 2
    k = jnp.arange(nh, dtype=jnp.int32)
    a = jnp.arange(n // DFT_SUB, dtype=jnp.int32)
    bb = jnp.arange(DFT_SUB, dtype=jnp.int32)
    ang_a = ((DFT_SUB * a[:, None] * k[None, :]) % n).astype(F32) * (2.0 * math.pi / n)
    ang_b = ((bb[:, None] * k[None, :]) % n).astype(F32) * (2.0 * math.pi / n)
    s = float(n) ** -0.5
    nyq = -jnp.cos(math.pi * bb.astype(F32)) * s
    sin_b = jnp.concatenate([nyq[:, None], (jnp.sin(ang_b) * s)[:, 1:]], axis=1)
    return jnp.cos(ang_a), jnp.sin(ang_a), jnp.cos(ang_b) * s, sin_b


def _channel_dft():
    c = jnp.arange(LANES, dtype=jnp.int32)
    ang = ((c[:, None] * c[None, :]) % LANES).astype(F32) * (2.0 * math.pi / LANES)
    s = float(LANES) ** -0.5
    return jnp.concatenate([jnp.cos(ang) * s, jnp.sin(ang) * s], axis=1)


def _postmix_kernel(*refs, n_in):
    a_refs = refs[:n_in]
    w_refs = refs[n_in:2 * n_in]
    x_ref, mod_ref, g_ref, rw_ref, x1_ref, h3_ref, aff_ref = refs[2 * n_in:]
    tm, d = x_ref.shape[1], x_ref.shape[2]
    sub = min(ROW_SUB, tm)
    p = d // 2 // LANES
    m = mod_ref[0]
    for r0 in range(0, tm, sub):
        rs = slice(r0, r0 + sub)
        y = None
        for a_ref, w_ref in zip(a_refs, w_refs):
            t = _dot(a_ref[0, rs], w_ref[...])
            y = t if y is None else y + t
        x1 = x_ref[0, rs] + m[2:3] * y
        x1_ref[0, rs] = x1
        h2 = _rms_mod(x1, g_ref[...], m[3:4], m[4:5])
        hh, hl = _split(h2)
        parts = _dot_t(rw_ref[...], jnp.concatenate([hh, hl], axis=1))
        lt = parts[:N_EXPERTS] + parts[N_EXPERTS:2 * N_EXPERTS]
        ex = jnp.exp(lt - jnp.max(lt, axis=0, keepdims=True))
        aff_ref[0, :, rs] = ex / jnp.sum(ex, axis=0, keepdims=True)
        bits = pltpu.bitcast(hh.astype(F32), jnp.uint32)
        packed = (bits[:, d // 2:] & jnp.uint32(0xFFFF0000)) | (bits[:, :d // 2] >> 16)
        for jj in range(p):
            h3_ref[0, pl.ds(r0 * p + jj, sub, stride=p), :] = packed[:, LANES * jj:LANES * (jj + 1)]


def _postmix_call(a_list, w_list, x, mod, gain, rw, *, tm):
    b, n, d = x.shape
    mb = mod.shape[0]
    n_in = len(a_list)
    p = d // 2 // LANES
    mod_map = (lambda i, j: (i, 0, 0)) if mb > 1 else (lambda i, j: (0, 0, 0))
    tok = lambda i, j: (i, j, 0)
    const2 = lambda i, j: (0, 0)
    in_specs = [pl.BlockSpec((1, tm, a.shape[2]), tok) for a in a_list]
    in_specs += [pl.BlockSpec(w.shape, const2) for w in w_list]
    in_specs += [
        pl.BlockSpec((1, tm, d), tok),
        pl.BlockSpec((1, 6, d), mod_map),
        pl.BlockSpec((1, d), const2),
        pl.BlockSpec((LANES, 2 * d), const2),
    ]
    return pl.pallas_call(
        functools.partial(_postmix_kernel, n_in=n_in),
        grid=(b, n // tm),
        in_specs=in_specs,
        out_specs=[
            pl.BlockSpec((1, tm, d), tok),
            pl.BlockSpec((1, tm * p, LANES), tok),
            pl.BlockSpec((1, N_EXPERTS, tm), lambda i, j: (i, 0, j)),
        ],
        out_shape=[
            jax.ShapeDtypeStruct((b, n, d), F32),
            jax.ShapeDtypeStruct((b, n * p, LANES), jnp.uint32),
            jax.ShapeDtypeStruct((b, N_EXPERTS, n), F32),
        ],
        compiler_params=_cparams(2, VMEM_LIMIT),
        name="postmix",
    )(*a_list, *w_list, x, mod, gain, rw)


def _prefix_incl(ones, tri):
    carry = jnp.zeros((ones.shape[0], 1), F32)
    outs = []
    for c in range(ones.shape[1] // LANES):
        blk = ones[:, LANES * c:LANES * (c + 1)]
        outs.append(_dot(blk.astype(BF16), tri) + carry)
        carry = carry + jnp.sum(blk, axis=1, keepdims=True)
    return jnp.concatenate(outs, axis=1)


def _topk_kernel(aff_ref, tri_ref, idx_ref, gate_ref, *, cap):
    g, ne, n = aff_ref.shape
    rows = g * ne
    aff = aff_ref[...].reshape(rows, n)
    bits = pltpu.bitcast(aff, jnp.int32)
    capf = jnp.float32(cap)

    def body(it, lo):
        sh = 28 - 2 * it
        new = lo
        for k in (1, 2, 3):
            cand = lo | jnp.left_shift(jnp.int32(k), sh)
            cnt = jnp.sum(jnp.where(bits >= cand, 1.0, 0.0), axis=1, keepdims=True)
            new = jnp.where(cnt >= capf, cand, new)
        return new

    thr = lax.fori_loop(0, 15, body, jnp.zeros((rows, 1), jnp.int32))
    gt = jnp.where(bits > thr, 1.0, 0.0)
    eq = jnp.where(bits == thr, 1.0, 0.0)
    need = capf - jnp.sum(gt, axis=1, keepdims=True)
    tri = tri_ref[...]
    eq_before = _prefix_incl(eq, tri) - eq
    sel = gt + eq * jnp.where(eq_before < need, 1.0, 0.0)
    slot = _prefix_incl(sel, tri).astype(jnp.int32) - 1
    tok = lax.broadcasted_iota(jnp.int32, (rows, n), 1)
    disp = jnp.where(sel > 0.0, tok - slot, -1)
    gate = aff
    for bit in range((n - 1).bit_length()):
        step = 1 << bit
        disp_in = pltpu.roll(disp, n - step, 1)
        gate_in = pltpu.roll(gate, n - step, 1)
        move_in = jnp.logical_and(disp_in >= 0, ((disp_in >> bit) & 1) == 1)
        stay = jnp.logical_and(disp >= 0, ((disp >> bit) & 1) == 0)
        gate = jnp.where(move_in, gate_in, gate)
        disp = jnp.where(move_in, disp_in, jnp.where(stay, disp, -1))
    idx_ref[...] = (tok + disp)[:, :cap].reshape(g, ne, cap)
    gate_ref[...] = gate[:, :cap].reshape(g, ne, cap)


def _topk_call(aff_t, tri, *, cap):
    b, e, n = aff_t.shape
    g = math.gcd(b, max(1, 32768 // n))
    return pl.pallas_call(
        functools.partial(_topk_kernel, cap=cap),
        grid=(b // g,),
        in_specs=[
            pl.BlockSpec((g, e, n), lambda i: (i, 0, 0)),
            pl.BlockSpec((LANES, LANES), lambda i: (0, 0)),
        ],
        out_specs=[pl.BlockSpec((g, e, cap), lambda i: (i, 0, 0))] * 2,
        out_shape=[jax.ShapeDtypeStruct((b, e, cap), jnp.int32), jax.ShapeDtypeStruct((b, e, cap), F32)],
        compiler_params=_cparams(1, VMEM_LIMIT),
        name="topk",
    )(aff_t, tri)


def _moe_kernel(idx_a_ref, idx_c_ref, gate_c_ref, h3_ref, wg_ref, wu_ref, wd_ref, o_ref, xs_0, xs_1, y_0, y_1,
                *, cap, p, c8, ne, n_tiles):
    g = pl.program_id(0)
    valid_a = g < n_tiles
    valid_b = jnp.logical_and(g >= 1, g <= n_tiles)
    valid_c = g >= 2
    steady = jnp.logical_and(g >= 2, g < n_tiles)

    @pl.when(jnp.logical_and(valid_c, (g - 2) % ne == 0))
    def _():
        o_ref[...] = jnp.zeros(o_ref.shape, o_ref.dtype)

    unroll, group = 8, 4

    def gather_row(xs_a, s, dst):
        t = idx_a_ref[0, 0, s]
        xs_a[dst, :] = h3_ref[0, pl.ds(pl.multiple_of(t * p, p), p), :]

    def gather(xs_a, straight):
        if straight:
            for s in range(cap):
                gather_row(xs_a, s, pl.ds(s * p, p))
        else:
            def body(i, carry):
                for u in range(unroll):
                    s = i * unroll + u
                    gather_row(xs_a, s, pl.ds(pl.multiple_of(s * p, p), p))
                return carry
            lax.fori_loop(0, cap // unroll, body, 0)

    def scatter_rows(y_c, s_list, src_list):
        sums = []
        for s, src in zip(s_list, src_list):
            t = idx_c_ref[0, 0, s]
            rows = pl.ds(pl.multiple_of(t * c8, c8), c8)
            sums.append((rows, o_ref[0, rows, :] + y_c[src, :] * gate_c_ref[0, 0, s]))
        for rows, val in sums:
            o_ref[0, rows, :] = val

    def scatter(y_c, straight):
        if straight:
            for g0 in range(0, cap, group):
                ss = [g0 + u for u in range(group)]
                scatter_rows(y_c, ss, [pl.ds(s * c8, c8) for s in ss])
        else:
            def body(i, carry):
                ss = [i * group + u for u in range(group)]
                scatter_rows(y_c, ss, [pl.ds(pl.multiple_of(s * c8, c8), c8) for s in ss])
                return carry
            lax.fori_loop(0, cap // group, body, 0)

    def load_rows(xs_b):
        lo, hi = [], []
        for j in range(p):
            w = xs_b[pl.ds(j, cap, stride=p), :]
            lo.append(pltpu.bitcast(w << 16, F32).astype(BF16))
            hi.append(pltpu.bitcast(w & jnp.uint32(0xFFFF0000), F32).astype(BF16))
        return jnp.concatenate(lo + hi, axis=1)

    def expert(xs, y_b):
        rc = min(MOE_ROW_CHUNK, cap)
        chunks = range(0, cap, rc)
        au = [(_dot(xs[r0:r0 + rc], wg_ref[0, 0]), _dot(xs[r0:r0 + rc], wu_ref[0, 0])) for r0 in chunks]
        for r0, (a, u_) in zip(chunks, au):
            y = _dot((_silu(a) * u_).astype(BF16), wd_ref[0, 0])
            for j in range(c8):
                y_b[pl.ds(r0 * c8 + j, rc, stride=c8), :] = y[:, LANES * j:LANES * (j + 1)]

    def step(xs_a, xs_b, y_b, y_c):
        @pl.when(steady)
        def _():
            xs = load_rows(xs_b)
            gather(xs_a, True)
            scatter(y_c, True)
            expert(xs, y_b)

        @pl.when(jnp.logical_not(steady))
        def _():
            @pl.when(valid_b)
            def _():
                expert(load_rows(xs_b), y_b)

            @pl.when(valid_a)
            def _():
                gather(xs_a, False)

            @pl.when(valid_c)
            def _():
                scatter(y_c, False)

    @pl.when(g % 2 == 0)
    def _():
        step(xs_0, xs_1, y_1, y_0)

    @pl.when(g % 2 == 1)
    def _():
        step(xs_1, xs_0, y_0, y_1)


def _moe_call(idx, h3, gate, wg, wu, wd, layer):
    bm, ne, cap = idx.shape
    d = wg.shape[2]
    p = d // 2 // LANES
    c8 = d // LANES
    n = h3.shape[1] // p
    n_tiles = bm * ne
    idx3 = idx.reshape(n_tiles, 1, cap)
    gate3 = gate.reshape(n_tiles, 1, cap)
    tile_a = lambda i: jnp.minimum(i, n_tiles - 1)
    tile_b = lambda i: jnp.clip(i - 1, 0, n_tiles - 1)
    tile_c = lambda i: jnp.clip(i - 2, 0, n_tiles - 1)
    wmap = lambda i: (layer, tile_b(i) % ne, 0, 0)
    smem = lambda f: pl.BlockSpec((1, 1, cap), lambda i: (f(i), 0, 0), memory_space=pltpu.SMEM)
    return pl.pallas_call(
        functools.partial(_moe_kernel, cap=cap, p=p, c8=c8, ne=ne, n_tiles=n_tiles),
        grid=(n_tiles + 2,),
        in_specs=[
            smem(tile_a), smem(tile_c), smem(tile_c),
            pl.BlockSpec((1, n * p, LANES), lambda i: (tile_a(i) // ne, 0, 0)),
            pl.BlockSpec((1, 1, d, wg.shape[3]), wmap),
            pl.BlockSpec((1, 1, d, wu.shape[3]), wmap),
            pl.BlockSpec((1, 1, wd.shape[2], d), wmap),
        ],
        out_specs=pl.BlockSpec((1, n * c8, LANES), lambda i: (tile_c(i) // ne, 0, 0), pipeline_mode=pl.Buffered(1)),
        out_shape=jax.ShapeDtypeStruct((bm, n * c8, LANES), F32),
        scratch_shapes=[pltpu.VMEM((cap * p, LANES), jnp.uint32), pltpu.VMEM((cap * p, LANES), jnp.uint32),
                        pltpu.VMEM((cap * c8, LANES), F32), pltpu.VMEM((cap * c8, LANES), F32)],
        compiler_params=_cparams(1, VMEM_LIMIT),
        name="moe",
    )(idx3, idx3, gate3, h3, wg, wu, wd)


def _resid_kernel(x_ref, moe_ref, mod_ref, o_ref):
    tm = x_ref.shape[1]
    sub = min(ROW_SUB, tm)
    for r0 in range(0, tm, sub):
        o_ref[0, r0:r0 + sub] = _gated_moe_residual(x_ref, moe_ref, mod_ref[0][5:6], r0, sub)


def _resid_call(x1, moe3, mod, *, tm):
    b, n, d = x1.shape
    mb = mod.shape[0]
    c8 = d // LANES
    mod_map = (lambda i, j: (i, 0, 0)) if mb > 1 else (lambda i, j: (0, 0, 0))
    tok = lambda i, j: (i, j, 0)
    return pl.pallas_call(
        _resid_kernel,
        grid=(b, n // tm),
        in_specs=[
            pl.BlockSpec((1, tm, d), tok),
            pl.BlockSpec((1, tm * c8, LANES), tok),
            pl.BlockSpec((1, 6, d), mod_map),
        ],
        out_specs=pl.BlockSpec((1, tm, d), tok),
        out_shape=jax.ShapeDtypeStruct((b, n, d), F32),
        compiler_params=_cparams(2, VMEM_LIMIT),
        name="resid",
    )(x1, moe3, mod)


def _rope_tables(n):
    t = jnp.arange(n)
    row = (t // GRID_W).astype(F32)
    col = (t % GRID_W).astype(F32)
    n_freq = HEAD_DIM // 4
    inv_freq = jnp.power(ROPE_BASE, -jnp.arange(n_freq, dtype=F32) / n_freq)
    ang = jnp.concatenate([row[:, None] * inv_freq, col[:, None] * inv_freq], axis=-1)
    cos, sin = jnp.cos(ang), jnp.sin(ang)
    cos_t = jnp.concatenate([cos, cos, cos, cos], axis=-1)
    sin_t = jnp.concatenate([-sin, sin, -sin, sin], axis=-1)
    return cos_t, sin_t


def _gain128(g):
    return jnp.concatenate([g, g]).reshape(1, LANES).astype(F32)


def _channel_mix(h_pack, aff_t, tri, weights, layer, *, flatten):
    wg, wu, wd = weights
    b, _, n = aff_t.shape
    cap = EC_CAPACITY_FACTOR * n // N_EXPERTS
    idx, gate = _topk_call(aff_t, tri, cap=cap)
    if flatten:
        idx = (idx + (jnp.arange(b, dtype=jnp.int32) * n)[:, None, None]).transpose(1, 0, 2).reshape(1, N_EXPERTS, b * cap)
        gate = gate.transpose(1, 0, 2).reshape(1, N_EXPERTS, b * cap)
        h_pack = h_pack.reshape(1, -1, LANES)
    return _moe_call(idx, h_pack, gate, wg, wu, wd, layer).reshape(b, -1, LANES)


def kernel(x, c, ctx, c_ctx, ada_w, ada_b, norm1_g, norm2_g, router_w, exp_w_gate, exp_w_up, exp_w_down, ev_w_in,
           ev_w_out, ev_q_gain, ev_k_gain, ev_sink, od_w_in, od_w_out, od_q_gain, od_k_gain, od_rpb):
    b, n, d = x.shape
    nc = ctx.shape[1]
    tm = min(512, n)
    tmb = min(1024, n)
    tmc = min(512, nc)

    rows = -(-(b + 1) // 8) * 8
    cs = jnp.concatenate([c, c_ctx[None], jnp.zeros((rows - b - 1, d), F32)], axis=0)
    mods = _mod_call(cs, ada_w, ada_b).reshape(ada_w.shape[0], rows, 6, d)

    cos_t, sin_t = _rope_tables(n)
    cs_dft = _channel_dft()
    tri = jnp.asarray(np.triu(np.ones((LANES, LANES), np.float32)), BF16)
    fw = FOURIER_GROUPS * LANES

    def router_split(l):
        rw = router_w[l].T
        rh = rw.astype(BF16)
        rl = (rw - rh.astype(F32)).astype(BF16)
        left = jnp.pad(jnp.concatenate([rh, rl], axis=0), ((0, LANES - 2 * N_EXPERTS), (0, 0)))
        right = jnp.pad(rh, ((0, LANES - N_EXPERTS), (0, 0)))
        return jnp.concatenate([left, right], axis=1)

    w_stacks = (exp_w_gate, exp_w_up, exp_w_down)
    as_layer = lambda ws: [w.reshape((1,) + s.shape[1:]) for w, s in zip(ws, w_stacks)]

    mod_x, mod_c = mods[0, :b], mods[0, b:b + 1]
    g1 = norm1_g[0].reshape(1, d)
    g2 = norm2_g[0].reshape(1, d)
    w_in = ev_w_in[0].astype(BF16)
    w_out = ev_w_out[0].astype(BF16)
    qg, kg = _gain128(ev_q_gain[0]), _gain128(ev_k_gain[0])
    sink = ev_sink[0].astype(F32)
    rw = router_split(0)

    ab_x, q_x, kd_x, vd_x = _premix_even_call(x, mod_x, g1, w_in, cs_dft, cos_t, sin_t, qg, kg, rope=True, tm=tmb)
    ab_c, q_c, kd_c, vd_c = _premix_even_call(ctx, mod_c, g1, w_in, cs_dft, cos_t[:nc], sin_t[:nc], qg, kg,
                                              rope=False, tm=tmc)
    a_x, *wts = _swa_call(sink, q_x, kd_x, vd_x, kd_c, vd_c, w_stacks, 0)
    wts = as_layer(wts)
    a_c = _ctx_attn_call(sink, q_c, kd_c, vd_c)
    four_x = _fourier_call(ab_x, *_dft_tables(n), tm=tmb)
    four_c = _fourier_call(ab_c, *_dft_tables(nc), tm=tmc)
    w_out_parts = [w_out[:fw], w_out[fw:]]
    x1, h3_x, aff_x = _postmix_call([four_x, a_x], w_out_parts, x, mod_x, g2, rw, tm=tmb)
    c1, h3_c, aff_c = _postmix_call([four_c, a_c], w_out_parts, ctx, mod_c, g2, rw, tm=tmc)
    moe_x = _channel_mix(h3_x, aff_x, tri, wts, 0, flatten=False)
    moe_c = _channel_mix(h3_c, aff_c, tri, wts, 0, flatten=True)

    mod_x0, mod_c0 = mod_x, mod_c
    mod_x, mod_c = mods[1, :b], mods[1, b:b + 1]
    g1 = norm1_g[1].reshape(1, d)
    g2 = norm2_g[1].reshape(1, d)
    w_in = od_w_in[0].astype(BF16)
    w_out = od_w_out[0].astype(BF16)
    qg, kg = _gain128(od_q_gain[0]), _gain128(od_k_gain[0])
    rw = router_split(1)

    q_x, k_x, v_x, x = _premix_odd_call(x1, moe_x, mod_x0, mod_x, g1, w_in, qg, kg, emit_x=True, tm=tmb)
    _, k_c, v_c = _premix_odd_call(c1, moe_c, mod_c0, mod_c, g1, w_in, qg, kg, emit_x=False, tm=tmc)
    a_x, *wts = _na_call(q_x, k_x, v_x, k_c, v_c, _na_bias_table(od_rpb[0]), w_stacks, 1)
    wts = as_layer(wts)
    x1, h3_x, aff_x = _postmix_call([a_x], [w_out], x, mod_x, g2, rw, tm=tmb)
    return _resid_call(x1, _channel_mix(h3_x, aff_x, tri, wts, 0, flatten=False), mod_x, tm=tmb)
```
